```python
import math
import jax, jax.numpy as jnp
from jax import lax
import numpy as np

D_MODEL = 1024
BATCH = 4
SEQ = 4096
DEPTH = 1

HEAD_DIM = 64
N_HEADS_FOX = 8
N_HEADS_DIL = 8
FOX_WIDTH = N_HEADS_FOX * HEAD_DIM
DIL_WIDTH = N_HEADS_DIL * HEAD_DIM
DIL_PATTERNS = ((128, 1), (512, 4), (2048, 16))
ROPE_DIM = HEAD_DIM // 4
ROPE_THETA = 500000.0
Q_BLOCK = 128
D_FF = 2816
CONV_WIDTH = 3
RMS_EPS = 1e-6
NEG_INF = -1e30
IN_SPLITS = (FOX_WIDTH, FOX_WIDTH, FOX_WIDTH, N_HEADS_FOX,
             DIL_WIDTH, DIL_WIDTH, DIL_WIDTH, D_MODEL, D_MODEL)
IN_WIDTH = sum(IN_SPLITS)

kernel_name = "hybrid_fox_dilated_gated_convffn"


def rmsnorm(x, g):
    xf = x.astype(jnp.float32)
    inv = lax.rsqrt(jnp.mean(xf * xf, axis=-1, keepdims=True) + RMS_EPS)
    return (xf * inv * g.astype(jnp.float32)).astype(x.dtype)


def partial_rope(t):
    S = t.shape[1]
    half = ROPE_DIM // 2
    inv_freq = ROPE_THETA ** (-jnp.arange(half, dtype=jnp.float32) * 2.0 / ROPE_DIM)
    ang = jnp.arange(S, dtype=jnp.float32)[:, None] * inv_freq[None, :]
    cos = jnp.cos(ang)[:, None, :]
    sin = jnp.sin(ang)[:, None, :]
    tf = t.astype(jnp.float32)
    t1, t2, rest = tf[..., :half], tf[..., half:ROPE_DIM], tf[..., ROPE_DIM:]
    out = jnp.concatenate([t1 * cos - t2 * sin, t2 * cos + t1 * sin, rest], axis=-1)
    return out.astype(t.dtype)


def split_heads(t, n_heads):
    B, S, _ = t.shape
    return t.reshape(B, S, n_heads, HEAD_DIM)


def fox_attention(q, k, v, log_f):
    B, S, H, dh = q.shape
    nb = S // Q_BLOCK
    scale = 1.0 / math.sqrt(dh)
    F = jnp.cumsum(log_f, axis=1).transpose(0, 2, 1)
    kt = k.transpose(0, 2, 1, 3)
    vt = v.transpose(0, 2, 1, 3)
    q_blocks = q.transpose(0, 2, 1, 3).reshape(B, H, nb, Q_BLOCK, dh).transpose(2, 0, 1, 3, 4)
    f_blocks = F.reshape(B, H, nb, Q_BLOCK).transpose(2, 0, 1, 3)
    starts = jnp.arange(nb, dtype=jnp.int32) * Q_BLOCK
    kpos = jnp.arange(S, dtype=jnp.int32)

    def one_block(args):
        qb, fqb, start = args
        s = jnp.einsum('bhqd,bhkd->bhqk', qb, kt).astype(jnp.float32) * scale
        s = s + fqb[..., None] - F[:, :, None, :]
        qpos = start + jnp.arange(Q_BLOCK, dtype=jnp.int32)
        causal = kpos[None, :] <= qpos[:, None]
        s = jnp.where(causal[None, None], s, NEG_INF)
        p = jax.nn.softmax(s, axis=-1)
        return jnp.einsum('bhqk,bhkd->bhqd', p.astype(vt.dtype), vt)

    out = lax.map(one_block, (q_blocks, f_blocks, starts))
    return out.transpose(1, 0, 3, 2, 4).reshape(B, S, H, dh)


def dilated_branch(q, k, v, window, dilation):
    B, S, H, dh = q.shape
    L = S // dilation
    w_sub = window // dilation
    blk = w_sub
    nb = -(-L // blk)
    Lp = nb * blk
    scale = 1.0 / math.sqrt(dh)

    def prep(t):
        t = t.reshape(B, L, dilation, H, dh)
        t = jnp.pad(t, ((0, 0), (0, Lp - L), (0, 0), (0, 0), (0, 0)))
        return t.reshape(B, nb, blk, dilation, H, dh)

    def with_prev(t):
        prev = jnp.pad(t[:, :-1], ((0, 0), (1, 0), (0, 0), (0, 0), (0, 0), (0, 0)))
        return jnp.concatenate([prev, t], axis=2)

    qs = prep(q)
    kk = with_prev(prep(k))
    vv = with_prev(prep(v))
    s = jnp.einsum('bnqrhd,bnkrhd->bnrhqk', qs, kk).astype(jnp.float32) * scale
    qi = jnp.arange(blk)[:, None]
    ki = jnp.arange(2 * blk)[None, :]
    dist = qi + blk - ki
    band = (dist >= 0) & (dist <= w_sub)
    exists = (jnp.arange(nb)[:, None, None] > 0) | (ki[None] >= blk)
    valid = band[None] & exists
    s = jnp.where(valid[None, :, None, None], s, NEG_INF)
    lse = jax.nn.logsumexp(s, axis=-1)
    p = jnp.exp(s - lse[..., None])
    o = jnp.einsum('bnrhqk,bnkrhd->bnqrhd', p.astype(vv.dtype), vv)
    o = o.reshape(B, Lp, dilation, H, dh)[:, :L].reshape(B, S, H, dh)
    lse = lse.transpose(0, 1, 4, 2, 3).reshape(B, Lp, dilation, H)[:, :L].reshape(B, S, H)
    return o, lse


def dilated_attention(q, k, v):
    outs, lses = [], []
    for window, dilation in DIL_PATTERNS:
        o, l = dilated_branch(q, k, v, window, dilation)
        outs.append(o)
        lses.append(l)
    lse = jnp.stack(lses, axis=0)
    alpha = jax.nn.softmax(lse, axis=0)
    out = jnp.stack(outs, axis=0).astype(jnp.float32)
    return jnp.sum(alpha[..., None] * out, axis=0).astype(q.dtype)


def causal_dwconv(u, w, b):
    S = u.shape[1]
    up = jnp.pad(u, ((0, 0), (CONV_WIDTH - 1, 0), (0, 0)))
    y = sum(up[:, i:i + S] * w[i] for i in range(CONV_WIDTH))
    return y + b


def setup_inputs(seed: int = 0) -> dict:
    key = jax.random.key(seed)
    ks = jax.random.split(key, 20)
    f32 = jnp.float32

    def nrm(k, shape, fan_in):
        return jax.random.normal(k, shape, f32) * (fan_in ** -0.5)

    def gain(k):
        return 1.0 + 0.05 * jax.random.normal(k, (DEPTH, D_MODEL), f32)

    return {
        "x": jax.random.normal(ks[0], (BATCH, SEQ, D_MODEL), f32),
        "g_pre_mix": gain(ks[1]),
        "w_in": nrm(ks[2], (DEPTH, D_MODEL, IN_WIDTH), D_MODEL),
        "b_forget": 2.0 + 0.5 * jax.random.normal(ks[3], (DEPTH, N_HEADS_FOX), f32),
        "w_o_fox": nrm(ks[4], (DEPTH, FOX_WIDTH, D_MODEL), FOX_WIDTH),
        "w_o_dil": nrm(ks[5], (DEPTH, DIL_WIDTH, D_MODEL), DIL_WIDTH),
        "w_out": nrm(ks[6], (DEPTH, D_MODEL, D_MODEL), D_MODEL),
        "g_post_mix": gain(ks[7]),
        "g_pre_ffn": gain(ks[8]),
        "w_up": nrm(ks[9], (DEPTH, D_MODEL, 2 * D_FF), D_MODEL),
        "conv_w": nrm(ks[10], (DEPTH, CONV_WIDTH, 2 * D_FF), CONV_WIDTH),
        "conv_b": 0.02 * jax.random.normal(ks[11], (DEPTH, 2 * D_FF), f32),
        "w_down": nrm(ks[12], (DEPTH, D_FF, D_MODEL), D_FF),
        "g_post_ffn": gain(ks[13]),
    }


def reference(x, g_pre_mix, w_in, b_forget, w_o_fox, w_o_dil, w_out, g_post_mix,
              g_pre_ffn, w_up, conv_w, conv_b, w_down, g_post_ffn):
    B, S, _ = x.shape
    offsets = np.cumsum((0,) + IN_SPLITS)
    for l in range(DEPTH):
        h = rmsnorm(x, g_pre_mix[l])
        z = h @ w_in[l]
        qa, ka, va, fa, qb, kb, vb, ga, gb = [z[..., offsets[i]:offsets[i + 1]]
                                              for i in range(len(IN_SPLITS))]
        log_f = jax.nn.log_sigmoid((fa + b_forget[l]).astype(jnp.float32))
        ya = fox_attention(split_heads(qa, N_HEADS_FOX), split_heads(ka, N_HEADS_FOX),
                           split_heads(va, N_HEADS_FOX), log_f)
        ya = ya.reshape(B, S, FOX_WIDTH) @ w_o_fox[l]
        qd = partial_rope(split_heads(qb, N_HEADS_DIL))
        kd = partial_rope(split_heads(kb, N_HEADS_DIL))
        yb = dilated_attention(qd, kd, split_heads(vb, N_HEADS_DIL))
        yb = yb.reshape(B, S, DIL_WIDTH) @ w_o_dil[l]
        mixed = jax.nn.sigmoid(ga) * ya + jax.nn.sigmoid(gb) * yb
        x = x + rmsnorm(mixed @ w_out[l], g_post_mix[l])
        h = rmsnorm(x, g_pre_ffn[l])
        u = causal_dwconv(h @ w_up[l], conv_w[l], conv_b[l])
        a, b = u[..., :D_FF], u[..., D_FF:]
        m = jax.nn.gelu(a, approximate=True) * b
        x = x + rmsnorm(m @ w_down[l], g_post_ffn[l])
    return x
```

```python
import functools
import math

import numpy as np
import jax
import jax.numpy as jnp
from jax import lax
from jax.experimental import pallas as pl
from jax.experimental.pallas import tpu as pltpu

D_MODEL = 1024
HEAD_DIM = 64
N_HEADS = 8
ATT_WIDTH = N_HEADS * HEAD_DIM
N_HEAD_PAIRS = N_HEADS // 2
DIL_PATTERNS = ((128, 1), (512, 4), (2048, 16))
ROPE_DIM = HEAD_DIM // 4
ROPE_HALF = ROPE_DIM // 2
ROPE_THETA = 500000.0
D_FF = 2816
CONV_WIDTH = 3
RMS_EPS = 1e-6
NEG_INF = -1e30
Q_SCALE = 1.0 / math.sqrt(HEAD_DIM)

LANES = 128
BF16_ROWS = 16
Z_WIDTH = 3 * ATT_WIDTH * 2 + 2 * D_MODEL
Z_BLK = 512
N_PIECES = 3
VMEM_LIMIT = 56 * 1024 * 1024

BF = jnp.bfloat16
F32 = jnp.float32


def _cparams(sem):
    return pltpu.CompilerParams(dimension_semantics=sem, vmem_limit_bytes=VMEM_LIMIT)


def _rms(xf, g):
    inv = lax.rsqrt(jnp.mean(xf * xf, axis=-1, keepdims=True) + RMS_EPS)
    return xf * inv * g


def _split3(x):
    hi = x.astype(BF)
    r1 = x - hi.astype(F32)
    mid = r1.astype(BF)
    lo = (r1 - mid.astype(F32)).astype(BF)
    return hi, mid, lo


def _dot(a, b):
    return jnp.dot(a, b, preferred_element_type=F32)


def _dot_nt(a, b):
    return lax.dot_general(a, b, (((1,), (1,)), ((), ())), preferred_element_type=F32)


def _in_proj_kernel(x_ref, g_ref, w_ref, wf_ref, freq_ref, z_ref, fa_ref,
                    h_scr, cos_scr, sneg_scr, spos_scr, *, tm):
    p = pl.program_id(0)
    b = pl.program_id(1)
    j = pl.program_id(2)

    @pl.when((b == 0) & (j == 0))
    def _tables():
        pos = (p * tm + lax.broadcasted_iota(jnp.int32, (tm, LANES), 0)).astype(F32)
        lane = lax.broadcasted_iota(jnp.int32, (tm, LANES), 1)
        c = lane % HEAD_DIM
        ang = pos * freq_ref[...]
        cs = jnp.cos(ang)
        sn = jnp.sin(ang)
        cos_scr[...] = jnp.where(c < ROPE_DIM, cs, 1.0)
        sneg_scr[...] = jnp.where(c < ROPE_HALF, -sn, 0.0)
        spos_scr[...] = jnp.where((c >= ROPE_HALF) & (c < ROPE_DIM), sn, 0.0)

    @pl.when(j == 0)
    def _norm():
        h = _rms(x_ref[...], g_ref[...]).astype(BF)
        h_scr[...] = h
        fa_ref[...] = _dot(h, wf_ref[...])

    acc = _dot(h_scr[...], w_ref[...])

    def rope(a):
        outs = []
        for s in range(Z_BLK // LANES):
            t = a[:, s * LANES:(s + 1) * LANES]
            up = pltpu.roll(t, LANES - ROPE_HALF, 1)
            dn = pltpu.roll(t, ROPE_HALF, 1)
            outs.append(t * cos_scr[...] + up * sneg_scr[...] + dn * spos_scr[...])
        return jnp.concatenate(outs, axis=1)

    is_q_fox = j == 0
    is_q_dil = j == 3
    is_k_dil = j == 4

    @pl.when(is_q_fox)
    def _():
        z_ref[...] = (acc * Q_SCALE).astype(BF)

    @pl.when(is_q_dil)
    def _():
        z_ref[...] = (rope(acc) * Q_SCALE).astype(BF)

    @pl.when(is_k_dil)
    def _():
        z_ref[...] = rope(acc).astype(BF)

    @pl.when(jnp.logical_not(is_q_fox | is_q_dil | is_k_dil))
    def _():
        z_ref[...] = acc.astype(BF)


def _in_proj(x2d, g, w_main, w_f, freq_lanes, *, batch, seq, tm):
    n_p = seq // tm
    n_j = Z_WIDTH // Z_BLK
    tokens = batch * seq
    row = lambda p, b, j: (b * n_p + p, 0)
    return pl.pallas_call(
        functools.partial(_in_proj_kernel, tm=tm),
        grid=(n_p, batch, n_j),
        in_specs=[
            pl.BlockSpec((tm, D_MODEL), row),
            pl.BlockSpec((1, D_MODEL), lambda p, b, j: (0, 0)),
            pl.BlockSpec((D_MODEL, Z_BLK), lambda p, b, j: (0, j)),
            pl.BlockSpec((D_MODEL, LANES), lambda p, b, j: (0, 0)),
            pl.BlockSpec((1, LANES), lambda p, b, j: (0, 0)),
        ],
        out_specs=[
            pl.BlockSpec((tm, Z_BLK), lambda p, b, j: (b * n_p + p, j)),
            pl.BlockSpec((tm, LANES), row),
        ],
        out_shape=[
            jax.ShapeDtypeStruct((tokens, Z_WIDTH), BF),
            jax.ShapeDtypeStruct((tokens, LANES), F32),
        ],
        scratch_shapes=[
            pltpu.VMEM((tm, D_MODEL), BF),
            pltpu.VMEM((tm, LANES), F32),
            pltpu.VMEM((tm, LANES), F32),
            pltpu.VMEM((tm, LANES), F32),
        ],
        compiler_params=_cparams(("arbitrary", "arbitrary", "arbitrary")),
        name="in_proj",
    )(x2d, g, w_main, w_f, freq_lanes)


def _extras_base(head):
    return (HEAD_DIM if head % 2 == 0 else 0) + 2 * N_PIECES * (head // 2)


def _forget_scan_kernel(fa_ref, bias_ref, tri_ref, pq_ref, pk_ref, oq_ref, ok_ref,
                        gq_ref, gk_ref, carry_scr):
    c = pl.program_id(1)

    @pl.when(c == 0)
    def _():
        carry_scr[...] = jnp.zeros_like(carry_scr)

    t = fa_ref[...] + bias_ref[...]
    log_f = jnp.minimum(t, 0.0) - jnp.log1p(jnp.exp(-jnp.abs(t)))
    tri = tri_ref[...]
    run = carry_scr[...]
    for piece in _split3(log_f):
        run = run + _dot(tri, piece)
    rows = run.shape[0]
    carry_scr[...] = run[rows - 1:rows, :]
    gq = oq_ref[...].astype(F32)
    gk = ok_ref[...].astype(F32)
    for i, piece in enumerate(_split3(run)):
        gq = gq + _dot(piece, pq_ref[i])
        gk = gk - _dot(piece, pk_ref[i])
    gq_ref[0] = gq.astype(BF)
    gk_ref[0] = gk.astype(BF)


def _forget_scan_constants(chunk):
    tri = np.tril(np.ones((chunk, chunk), np.float32))
    pq = np.zeros((N_PIECES, LANES, LANES), np.float32)
    pk = np.zeros((N_PIECES, LANES, LANES), np.float32)
    oq = np.zeros((1, LANES), np.float32)
    ok = np.zeros((1, LANES), np.float32)
    for h in range(N_HEADS):
        base = _extras_base(h)
        for i in range(N_PIECES):
            pq[i, h, base + i] = 1.0
            oq[0, base + N_PIECES + i] = 1.0
            ok[0, base + i] = 1.0
            pk[i, h, base + N_PIECES + i] = 1.0
    as_bf = lambda a: jnp.asarray(a, BF)
    return as_bf(tri), as_bf(pq), as_bf(pk), as_bf(oq), as_bf(ok)


def _forget_scan(fa, bias_lanes, *, batch, seq, chunk):
    n_c = seq // chunk
    tri, pq, pk, oq, ok = _forget_scan_constants(chunk)
    const2 = lambda b, c: (0, 0)
    const3 = lambda b, c: (0, 0, 0)
    return pl.pallas_call(
        _forget_scan_kernel,
        grid=(batch, n_c),
        in_specs=[
            pl.BlockSpec((chunk, LANES), lambda b, c: (b * n_c + c, 0)),
            pl.BlockSpec((1, LANES), const2),
            pl.BlockSpec((chunk, chunk), const2),
            pl.BlockSpec((N_PIECES, LANES, LANES), const3),
            pl.BlockSpec((N_PIECES, LANES, LANES), const3),
            pl.BlockSpec((1, LANES), const2),
            pl.BlockSpec((1, LANES), const2),
        ],
        out_specs=[
            pl.BlockSpec((1, chunk, LANES), lambda b, c: (b, c, 0)),
            pl.BlockSpec((1, chunk, LANES), lambda b, c: (b, c, 0)),
        ],
        out_shape=[
            jax.ShapeDtypeStruct((batch, seq, LANES), BF),
            jax.ShapeDtypeStruct((batch, seq, LANES), BF),
        ],
        scratch_shapes=[pltpu.VMEM((1, LANES), F32)],
        compiler_params=_cparams(("arbitrary", "arbitrary")),
        name="forget_scan",
    )(fa, bias_lanes, tri, pq, pk, oq, ok)


def _with_bias_lanes(slab, extras, hp, parity):
    lane = lax.broadcasted_iota(jnp.int32, slab.shape, 1)
    own = (lane < HEAD_DIM) if parity == 0 else (lane >= HEAD_DIM)
    base = (HEAD_DIM if parity == 0 else 0) + 2 * N_PIECES * hp
    in_extras = (lane >= base) & (lane < base + 2 * N_PIECES)
    return jnp.where(own, slab, jnp.where(in_extras, extras, jnp.zeros_like(extras)))


def _fox_kernel(q_ref, k_ref, v_ref, gq_ref, gk_ref, o_ref,
                ka_scr, kb_scr, vt_scr, acc_scr, ot_scr, *, tq):
    hp = pl.program_id(1)
    qi = pl.program_id(2)
    k_scr = (ka_scr, kb_scr)

    @pl.when(qi == 0)
    def _prep():
        k2 = k_ref[...]
        gk = gk_ref[0]
        ka_scr[...] = _with_bias_lanes(k2, gk, hp, 0)
        kb_scr[...] = _with_bias_lanes(k2, gk, hp, 1)
        vt_scr[...] = v_ref[...].astype(F32).T.astype(BF)

    q2 = q_ref[...]
    gq = gq_ref[0]
    row = lax.broadcasted_iota(jnp.int32, (tq, tq), 0)
    col = lax.broadcasted_iota(jnp.int32, (tq, tq), 1)
    causal = row <= col

    for parity in range(2):
        qp = _with_bias_lanes(q2, gq, hp, parity)
        kp_scr = k_scr[parity]
        acc_scr[...] = jnp.zeros_like(acc_scr)

        def step(j, carry, masked):
            m, l = carry
            start = pl.multiple_of(j * tq, tq)
            kt = kp_scr[pl.ds(start, tq), :]
            st = _dot_nt(kt, qp)
            if masked:
                st = jnp.where(causal, st, NEG_INF)
            m_new = jnp.maximum(m, jnp.max(st, axis=0, keepdims=True))
            alpha = jnp.exp(m - m_new)
            pt = jnp.exp(st - m_new)
            l_new = alpha * l + jnp.sum(pt, axis=0, keepdims=True)
            vt = vt_scr[parity * HEAD_DIM:(parity + 1) * HEAD_DIM, pl.ds(start, tq)]
            acc_scr[...] = alpha * acc_scr[...] + _dot(vt, pt.astype(BF))
            return m_new, l_new

        init = (jnp.full((1, tq), NEG_INF, F32), jnp.zeros((1, tq), F32))
        carry = lax.fori_loop(0, qi, functools.partial(step, masked=False), init)
        _, l = step(qi, carry, True)
        ot_scr[parity * HEAD_DIM:(parity + 1) * HEAD_DIM, :] = acc_scr[...] / l

    o_ref[...] = ot_scr[...].T.astype(BF)


def _fox_attention(z, gq, gk, *, batch, seq, tq):
    n_q = seq // tq
    tokens = batch * seq
    return pl.pallas_call(
        functools.partial(_fox_kernel, tq=tq),
        grid=(batch, N_HEAD_PAIRS, n_q),
        in_specs=[
            pl.BlockSpec((tq, LANES), lambda b, hp, qi: (b * n_q + qi, hp)),
            pl.BlockSpec((seq, LANES), lambda b, hp, qi: (b, N_HEAD_PAIRS + hp)),
            pl.BlockSpec((seq, LANES), lambda b, hp, qi: (b, 2 * N_HEAD_PAIRS + hp)),
            pl.BlockSpec((1, tq, LANES), lambda b, hp, qi: (b, qi, 0)),
            pl.BlockSpec((1, seq, LANES), lambda b, hp, qi: (b, 0, 0)),
        ],
        out_specs=pl.BlockSpec((tq, LANES), lambda b, hp, qi: (b * n_q + qi, hp)),
        out_shape=jax.ShapeDtypeStruct((tokens, ATT_WIDTH), BF),
        scratch_shapes=[
            pltpu.VMEM((seq, LANES), BF),
            pltpu.VMEM((seq, LANES), BF),
            pltpu.VMEM((LANES, seq), BF),
            pltpu.VMEM((HEAD_DIM, tq), F32),
            pltpu.VMEM((LANES, tq), F32),
        ],
        compiler_params=_cparams(("arbitrary", "arbitrary", "arbitrary")),
        name="fox_attn",
    )(z, z, z, gq, gk)


def _dilated_kernel(*refs, tq, blk, has_prev, emit_lse):
    q_ref, kc_ref, kp_ref, vc_ref, vp_ref = refs[:5]
    pos = 5
    if has_prev:
        op_ref, lp_ref = refs[pos:pos + 2]
        pos += 2
    o_ref = refs[pos]
    pos += 1
    if emit_lse:
        l_ref = refs[pos]
        pos += 1
    kw_scr, vw_scr = refs[pos:pos + 2]

    qi = pl.program_id(2)
    kw_scr[0:blk, :] = kp_ref[0]
    kw_scr[blk:, :] = kc_ref[0]
    vw_scr[0:blk, :] = vp_ref[0]
    vw_scr[blk:, :] = vc_ref[0]

    lane = lax.broadcasted_iota(jnp.int32, (blk, LANES), 1)
    low = lane < HEAD_DIM
    qrow = lax.broadcasted_iota(jnp.int32, (2 * blk, 2 * blk), 0) % blk
    kcol = lax.broadcasted_iota(jnp.int32, (2 * blk, 2 * blk), 1)
    dist = qrow + blk - kcol
    band = (dist >= 0) & (dist <= blk)
    first_band = band & ((kcol >= blk) | (qi > 0))

    for bi in range(tq // blk):
        valid = first_band if bi == 0 else band
        for hp in range(N_HEAD_PAIRS):
            cols = slice(hp * LANES, (hp + 1) * LANES)
            q2 = q_ref[0, bi * blk:(bi + 1) * blk, cols]
            zero = jnp.zeros_like(q2)
            qq = jnp.concatenate([jnp.where(low, q2, zero), jnp.where(low, zero, q2)], axis=0)
            kwin = kw_scr[bi * blk:(bi + 2) * blk, cols]
            vwin = vw_scr[bi * blk:(bi + 2) * blk, cols]
            s = _dot_nt(qq, kwin)
            s = jnp.where(valid, s, NEG_INF)
            m = jnp.max(s, axis=1, keepdims=True)
            p = jnp.exp(s - m)
            l = jnp.sum(p, axis=1, keepdims=True)
            pv = _dot(p.astype(BF), vwin) / l
            lse = m + jnp.log(l)
            o2 = jnp.where(low, pv[:blk], pv[blk:])
            lse2 = jnp.where(low, jnp.broadcast_to(lse[:blk], (blk, LANES)),
                             jnp.broadcast_to(lse[blk:], (blk, LANES)))
            rows = slice(bi * blk, (bi + 1) * blk)
            if has_prev:
                o_prev = op_ref[0, rows, cols].astype(F32)
                l_prev = lp_ref[0, rows, cols]
                l_max = jnp.maximum(l_prev, lse2)
                l_new = l_max + jnp.log(jnp.exp(l_prev - l_max) + jnp.exp(lse2 - l_max))
                o2 = o_prev * jnp.exp(l_prev - l_new) + o2 * jnp.exp(lse2 - l_new)
                lse2 = l_new
            o_ref[0, rows, cols] = o2.astype(BF)
            if emit_lse:
                l_ref[0, rows, cols] = lse2


def _dilated_pattern(z, prev, *, batch, seq, dilation, blk, emit_lse):
    sub_len = seq // dilation
    tq = min(512, sub_len)
    n_q = sub_len // tq
    per = tq // blk
    zr = z.reshape(batch, sub_len, dilation * Z_WIDTH)
    zcols = Z_WIDTH // ATT_WIDTH
    cur = lambda off: pl.BlockSpec(
        (1, tq, ATT_WIDTH), lambda b, j, qi: (b, qi, j * zcols + off))
    halo = lambda off: pl.BlockSpec(
        (1, blk, ATT_WIDTH),
        lambda b, j, qi: (b, jnp.maximum(qi * per - 1, 0), j * zcols + off))
    io = pl.BlockSpec((1, tq, ATT_WIDTH), lambda b, j, qi: (b, qi, j))
    q_off, k_off, v_off = 3, 4, 5
    in_specs = [cur(q_off), cur(k_off), halo(k_off), cur(v_off), halo(v_off)]
    args = [zr, zr, zr, zr, zr]
    has_prev = prev is not None
    view = (batch, sub_len, dilation * ATT_WIDTH)
    if has_prev:
        in_specs += [io, io]
        args += [prev[0].reshape(view), prev[1].reshape(view)]
    out_specs = [io]
    out_shape = [jax.ShapeDtypeStruct(view, BF)]
    if emit_lse:
        out_specs.append(io)
        out_shape.append(jax.ShapeDtypeStruct(view, F32))
    outs = pl.pallas_call(
        functools.partial(_dilated_kernel, tq=tq, blk=blk, has_prev=has_prev,
                          emit_lse=emit_lse),
        grid=(batch, dilation, n_q),
        in_specs=in_specs,
        out_specs=out_specs,
        out_shape=out_shape,
        scratch_shapes=[
            pltpu.VMEM((tq + blk, ATT_WIDTH), BF),
            pltpu.VMEM((tq + blk, ATT_WIDTH), BF),
        ],
        compiler_params=_cparams(("arbitrary", "arbitrary", "arbitrary")),
        name=f"dilated_attn_r{dilation}",
    )(*args)
    canon = (batch, seq, ATT_WIDTH)
    return [o.reshape(canon) for o in outs]


def _dilated_attention(z, *, batch, seq):
    prev = None
    for idx, (window, dilation) in enumerate(DIL_PATTERNS):
        last = idx == len(DIL_PATTERNS) - 1
        prev = _dilated_pattern(z, prev, batch=batch, seq=seq, dilation=dilation,
                                blk=window // dilation, emit_lse=not last)
    return prev[0].reshape(batch * seq, ATT_WIDTH)


def _mix_kernel(ya_ref, yb_ref, ga_ref, gb_ref, x_ref, woa_ref, wob_ref, wout_ref,
                g_ref, o_ref):
    pa = _dot(ya_ref[...], woa_ref[...])
    pb = _dot(yb_ref[...], wob_ref[...])
    mixed = (jax.nn.sigmoid(ga_ref[...].astype(F32)) * pa
             + jax.nn.sigmoid(gb_ref[...].astype(F32)) * pb)
    y = _dot(mixed.astype(BF), wout_ref[...])
    o_ref[...] = x_ref[...] + _rms(y, g_ref[...])


def _mix(ya, yb, z, x2d, woa, wob, wout, g, *, tm):
    tokens = x2d.shape[0]
    gate_blk = lambda off: pl.BlockSpec((tm, D_MODEL), lambda i: (i, off))
    const = lambda i: (0, 0)
    return pl.pallas_call(
        _mix_kernel,
        grid=(tokens // tm,),
        in_specs=[
            pl.BlockSpec((tm, ATT_WIDTH), lambda i: (i, 0)),
            pl.BlockSpec((tm, ATT_WIDTH), lambda i: (i, 0)),
            gate_blk(3),
            gate_blk(4),
            pl.BlockSpec((tm, D_MODEL), lambda i: (i, 0)),
            pl.BlockSpec((ATT_WIDTH, D_MODEL), const),
            pl.BlockSpec((ATT_WIDTH, D_MODEL), const),
            pl.BlockSpec((D_MODEL, D_MODEL), const),
            pl.BlockSpec((1, D_MODEL), const),
        ],
        out_specs=pl.BlockSpec((tm, D_MODEL), lambda i: (i, 0)),
        out_shape=jax.ShapeDtypeStruct((tokens, D_MODEL), F32),
        compiler_params=_cparams(("arbitrary",)),
        name="mix",
    )(ya, yb, z, z, x2d, woa, wob, wout, g)


def _ffn_kernel(x_ref, halo_ref, gpre_ref, wa_ref, wb_ref, cwa_ref, cwb_ref,
                cba_ref, cbb_ref, wd_ref, gpost_ref, o_ref, h_scr, acc_scr,
                *, tm, tiles_per_seq):
    i = pl.program_id(0)
    f = pl.program_id(1)
    n_f = pl.num_programs(1)
    halo = BF16_ROWS

    @pl.when(f == 0)
    def _norm():
        g = gpre_ref[...]
        hh = _rms(halo_ref[...], g)
        hh = jnp.where(i % tiles_per_seq == 0, jnp.zeros_like(hh), hh)
        h_scr[0:halo, :] = hh.astype(BF)
        h_scr[halo:, :] = _rms(x_ref[...], g).astype(BF)
        acc_scr[...] = jnp.zeros_like(acc_scr)

    h = h_scr[...]

    def conv(w_ref, cw_ref, cb_ref):
        u = _dot(h, w_ref[...])
        out = cb_ref[...]
        for tap in range(CONV_WIDTH):
            lo = halo - (CONV_WIDTH - 1) + tap
            out = out + cw_ref[tap:tap + 1, :] * u[lo:lo + tm, :]
        return out

    a = conv(wa_ref, cwa_ref, cba_ref)
    bgate = conv(wb_ref, cwb_ref, cbb_ref)
    c0 = math.sqrt(2.0 / math.pi)
    gelu = 0.5 * a * (1.0 + jnp.tanh(c0 * (a + 0.044715 * (a * a * a))))
    acc_scr[...] += _dot((gelu * bgate).astype(BF), wd_ref[...])

    @pl.when(f == n_f - 1)
    def _out():
        o_ref[...] = x_ref[...] + _rms(acc_scr[...], gpost_ref[...])


def _ffn(x1, g_pre, w_up, conv_w, conv_b, w_down, g_post, *, seq, tm, tf):
    tokens = x1.shape[0]
    n_f = D_FF // tf
    halo = BF16_ROWS
    per = tm // halo
    const = lambda i, f: (0, 0)
    col_a = lambda i, f: (0, f)
    col_b = lambda i, f: (0, n_f + f)
    return pl.pallas_call(
        functools.partial(_ffn_kernel, tm=tm, tiles_per_seq=seq // tm),
        grid=(tokens // tm, n_f),
        in_specs=[
            pl.BlockSpec((tm, D_MODEL), lambda i, f: (i, 0)),
            pl.BlockSpec((halo, D_MODEL), lambda i, f: (jnp.maximum(i * per - 1, 0), 0)),
            pl.BlockSpec((1, D_MODEL), const),
            pl.BlockSpec((D_MODEL, tf), col_a),
            pl.BlockSpec((D_MODEL, tf), col_b),
            pl.BlockSpec((CONV_WIDTH, tf), col_a),
            pl.BlockSpec((CONV_WIDTH, tf), col_b),
            pl.BlockSpec((1, tf), col_a),
            pl.BlockSpec((1, tf), col_b),
            pl.BlockSpec((tf, D_MODEL), lambda i, f: (f, 0)),
            pl.BlockSpec((1, D_MODEL), const),
        ],
        out_specs=pl.BlockSpec((tm, D_MODEL), lambda i, f: (i, 0)),
        out_shape=jax.ShapeDtypeStruct((tokens, D_MODEL), F32),
        scratch_shapes=[
            pltpu.VMEM((tm + halo, D_MODEL), BF),
            pltpu.VMEM((tm, D_MODEL), F32),
        ],
        compiler_params=_cparams(("arbitrary", "arbitrary")),
        name="ffn",
    )(x1, x1, g_pre, w_up, w_up, conv_w, conv_w, conv_b, conv_b, w_down, g_post)


def _rope_freq_lanes():
    inv_freq = ROPE_THETA ** (-jnp.arange(ROPE_HALF, dtype=F32) * 2.0 / ROPE_DIM)
    lane = np.arange(LANES) % HEAD_DIM
    return inv_freq[lane % ROPE_HALF].reshape(1, LANES)


def kernel(x, g_pre_mix, w_in, b_forget, w_o_fox, w_o_dil, w_out, g_post_mix,
           g_pre_ffn, w_up, conv_w, conv_b, w_down, g_post_ffn):
    batch, seq, d_model = x.shape
    assert d_model == D_MODEL and seq % 1024 == 0
    depth = w_in.shape[0]
    fox_end = 3 * ATT_WIDTH
    freq_lanes = _rope_freq_lanes()
    x2d = x.reshape(batch * seq, D_MODEL)
    row = lambda v: v.reshape(1, -1)
    for l in range(depth):
        w_main = jnp.concatenate(
            [w_in[l][:, :fox_end], w_in[l][:, fox_end + N_HEADS:]], axis=1).astype(BF)
        w_f = jnp.pad(w_in[l][:, fox_end:fox_end + N_HEADS],
                      ((0, 0), (0, LANES - N_HEADS))).astype(BF)
        bias_lanes = jnp.pad(b_forget[l], (0, LANES - N_HEADS)).reshape(1, LANES)

        z, fa = _in_proj(x2d, row(g_pre_mix[l]), w_main, w_f, freq_lanes,
                         batch=batch, seq=seq, tm=1024)
        gq, gk = _forget_scan(fa, bias_lanes, batch=batch, seq=seq, chunk=512)
        ya = _fox_attention(z, gq, gk, batch=batch, seq=seq, tq=512)
        yb = _dilated_attention(z, batch=batch, seq=seq)
        x2d = _mix(ya, yb, z, x2d, w_o_fox[l].astype(BF), w_o_dil[l].astype(BF),
                   w_out[l].astype(BF), row(g_post_mix[l]), tm=512)
        x2d = _ffn(x2d, row(g_pre_ffn[l]), w_up[l].astype(BF), conv_w[l],
                   row(conv_b[l]), w_down[l].astype(BF), row(g_post_ffn[l]),
                   seq=seq, tm=1024, tf=256)
    return x2d.reshape(batch, seq, D_MODEL)
```

```python
import functools
import math

import numpy as np
import jax
import jax.numpy as jnp
from jax import lax
from jax.experimental import pallas as pl
from jax.experimental.pallas import tpu as pltpu

D_MODEL = 1024
HEAD_DIM = 64
N_HEADS = 8
ATT_WIDTH = N_HEADS * HEAD_DIM
N_HEAD_PAIRS = N_HEADS // 2
DIL_PATTERNS = ((128, 1), (512, 4), (2048, 16))
ROPE_DIM = HEAD_DIM // 4
ROPE_HALF = ROPE_DIM // 2
ROPE_THETA = 500000.0
D_FF = 2816
CONV_WIDTH = 3
RMS_EPS = 1e-6
NEG_INF = -1e30
Q_SCALE = 1.0 / math.sqrt(HEAD_DIM)
LOG2_E = math.log2(math.e)

LANES = 128
BF16_ROWS = 16
Z_WIDTH = 3 * ATT_WIDTH * 2 + 2 * D_MODEL
Z_BLK = 512
DIL_SECTION = 3
DIL_CHUNK = 1024
N_PIECES = 3
VMEM_LIMIT = 56 * 1024 * 1024

BF = jnp.bfloat16
F32 = jnp.float32


def _cparams(sem):
    return pltpu.CompilerParams(dimension_semantics=sem, vmem_limit_bytes=VMEM_LIMIT)


def _rms(xf, g):
    inv = lax.rsqrt(jnp.mean(xf * xf, axis=-1, keepdims=True) + RMS_EPS)
    return xf * inv * g


def _split3(x):
    hi = x.astype(BF)
    r1 = x - hi.astype(F32)
    mid = r1.astype(BF)
    lo = (r1 - mid.astype(F32)).astype(BF)
    return hi, mid, lo


def _dot(a, b):
    return jnp.dot(a, b, preferred_element_type=F32)


def _dot_nt(a, b):
    return lax.dot_general(a, b, (((1,), (1,)), ((), ())), preferred_element_type=F32)


def _in_proj_kernel(x_ref, g_ref, w_ref, wf_ref, freq_ref, z_ref, fa_ref, zd4_ref, zd16_ref,
                    h_scr, cos_scr, sneg_scr, spos_scr, stage_scr, *, tm):
    p = pl.program_id(0)
    b = pl.program_id(1)
    j = pl.program_id(2)
    zd_refs = {4: zd4_ref, 16: zd16_ref}

    def emit_dilated(val):
        z_ref[...] = val.astype(BF)
        for s in range(Z_BLK // LANES):
            stage_scr[s] = val[:, s * LANES:(s + 1) * LANES]
        for _, r in DIL_PATTERNS[1:]:
            seg = tm // r
            for jj in range(r):
                for s in range(Z_BLK // LANES):
                    zd_refs[r][jj * seg:(jj + 1) * seg, s * LANES:(s + 1) * LANES] = (
                        stage_scr[s, pl.ds(jj, seg, stride=r), :].astype(BF))

    @pl.when((b == 0) & (j == 0))
    def _tables():
        pos = (p * tm + lax.broadcasted_iota(jnp.int32, (tm, LANES), 0)).astype(F32)
        lane = lax.broadcasted_iota(jnp.int32, (tm, LANES), 1)
        c = lane % HEAD_DIM
        ang = pos * freq_ref[...]
        cs = jnp.cos(ang)
        sn = jnp.sin(ang)
        cos_scr[...] = jnp.where(c < ROPE_DIM, cs, 1.0)
        sneg_scr[...] = jnp.where(c < ROPE_HALF, -sn, 0.0)
        spos_scr[...] = jnp.where((c >= ROPE_HALF) & (c < ROPE_DIM), sn, 0.0)

    @pl.when(j == 0)
    def _norm():
        h = _rms(x_ref[...], g_ref[...]).astype(BF)
        h_scr[...] = h
        fa_ref[...] = _dot(h, wf_ref[...])

    acc = _dot(h_scr[...], w_ref[...])

    def rope(a):
        outs = []
        for s in range(Z_BLK // LANES):
            t = a[:, s * LANES:(s + 1) * LANES]
            up = pltpu.roll(t, LANES - ROPE_HALF, 1)
            dn = pltpu.roll(t, ROPE_HALF, 1)
            outs.append(t * cos_scr[...] + up * sneg_scr[...] + dn * spos_scr[...])
        return jnp.concatenate(outs, axis=1)

    is_q_fox = j == 0
    is_q_dil = j == DIL_SECTION
    is_k_dil = j == DIL_SECTION + 1
    is_v_dil = j == DIL_SECTION + 2

    @pl.when(is_q_fox)
    def _():
        z_ref[...] = (acc * Q_SCALE).astype(BF)

    @pl.when(is_q_dil)
    def _():
        emit_dilated(rope(acc) * (Q_SCALE * LOG2_E))

    @pl.when(is_k_dil)
    def _():
        emit_dilated(rope(acc))

    @pl.when(is_v_dil)
    def _():
        emit_dilated(acc)

    @pl.when(jnp.logical_not(is_q_fox | is_q_dil | is_k_dil | is_v_dil))
    def _():
        z_ref[...] = acc.astype(BF)


def _in_proj(x2d, g, w_main, w_f, freq_lanes, *, batch, seq, tm):
    n_p = seq // tm
    n_j = Z_WIDTH // Z_BLK
    tokens = batch * seq
    row = lambda p, b, j: (b * n_p + p, 0)
    dil_blk = pl.BlockSpec(
        (tm, Z_BLK), lambda p, b, j: (b * n_p + p, jnp.clip(j - DIL_SECTION, 0, 2)))
    dil_shape = jax.ShapeDtypeStruct((tokens, 3 * ATT_WIDTH), BF)
    return pl.pallas_call(
        functools.partial(_in_proj_kernel, tm=tm),
        grid=(n_p, batch, n_j),
        in_specs=[
            pl.BlockSpec((tm, D_MODEL), row),
            pl.BlockSpec((1, D_MODEL), lambda p, b, j: (0, 0)),
            pl.BlockSpec((D_MODEL, Z_BLK), lambda p, b, j: (0, j)),
            pl.BlockSpec((D_MODEL, LANES), lambda p, b, j: (0, 0)),
            pl.BlockSpec((1, LANES), lambda p, b, j: (0, 0)),
        ],
        out_specs=[
            pl.BlockSpec((tm, Z_BLK), lambda p, b, j: (b * n_p + p, j)),
            pl.BlockSpec((tm, LANES), row),
            dil_blk,
            dil_blk,
        ],
        out_shape=[
            jax.ShapeDtypeStruct((tokens, Z_WIDTH), BF),
            jax.ShapeDtypeStruct((tokens, LANES), F32),
            dil_shape,
            dil_shape,
        ],
        scratch_shapes=[
            pltpu.VMEM((tm, D_MODEL), BF),
            pltpu.VMEM((tm, LANES), F32),
            pltpu.VMEM((tm, LANES), F32),
            pltpu.VMEM((tm, LANES), F32),
            pltpu.VMEM((Z_BLK // LANES, tm, LANES), F32),
        ],
        compiler_params=_cparams(("arbitrary", "arbitrary", "arbitrary")),
        name="in_proj",
    )(x2d, g, w_main, w_f, freq_lanes)


def _extras_base(head):
    return (HEAD_DIM if head % 2 == 0 else 0) + 2 * N_PIECES * (head // 2)


def _forget_scan_kernel(fa_ref, bias_ref, tri_ref, pq_ref, pk_ref, oq_ref, ok_ref,
                        gq_ref, gk_ref, carry_scr):
    c = pl.program_id(1)

    @pl.when(c == 0)
    def _():
        carry_scr[...] = jnp.zeros_like(carry_scr)

    t = fa_ref[...] + bias_ref[...]
    log_f = jnp.minimum(t, 0.0) - jnp.log1p(jnp.exp(-jnp.abs(t)))
    tri = tri_ref[...]
    run = carry_scr[...]
    for piece in _split3(log_f):
        run = run + _dot(tri, piece)
    rows = run.shape[0]
    carry_scr[...] = run[rows - 1:rows, :]
    gq = oq_ref[...].astype(F32)
    gk = ok_ref[...].astype(F32)
    for i, piece in enumerate(_split3(run)):
        gq = gq + _dot(piece, pq_ref[i])
        gk = gk - _dot(piece, pk_ref[i])
    gq_ref[0] = gq.astype(BF)
    gk_ref[0] = gk.astype(BF)


def _forget_scan_constants(chunk):
    tri = np.tril(np.ones((chunk, chunk), np.float32))
    pq = np.zeros((N_PIECES, LANES, LANES), np.float32)
    pk = np.zeros((N_PIECES, LANES, LANES), np.float32)
    oq = np.zeros((1, LANES), np.float32)
    ok = np.zeros((1, LANES), np.float32)
    for h in range(N_HEADS):
        base = _extras_base(h)
        for i in range(N_PIECES):
            pq[i, h, base + i] = 1.0
            oq[0, base + N_PIECES + i] = 1.0
            ok[0, base + i] = 1.0
            pk[i, h, base + N_PIECES + i] = 1.0
    as_bf = lambda a: jnp.asarray(a, BF)
    return as_bf(tri), as_bf(pq), as_bf(pk), as_bf(oq), as_bf(ok)


def _forget_scan(fa, bias_lanes, *, batch, seq, chunk):
    n_c = seq // chunk
    tri, pq, pk, oq, ok = _forget_scan_constants(chunk)
    const2 = lambda b, c: (0, 0)
    const3 = lambda b, c: (0, 0, 0)
    return pl.pallas_call(
        _forget_scan_kernel,
        grid=(batch, n_c),
        in_specs=[
            pl.BlockSpec((chunk, LANES), lambda b, c: (b * n_c + c, 0)),
            pl.BlockSpec((1, LANES), const2),
            pl.BlockSpec((chunk, chunk), const2),
            pl.BlockSpec((N_PIECES, LANES, LANES), const3),
            pl.BlockSpec((N_PIECES, LANES, LANES), const3),
            pl.BlockSpec((1, LANES), const2),
            pl.BlockSpec((1, LANES), const2),
        ],
        out_specs=[
            pl.BlockSpec((1, chunk, LANES), lambda b, c: (b, c, 0)),
            pl.BlockSpec((1, chunk, LANES), lambda b, c: (b, c, 0)),
        ],
        out_shape=[
            jax.ShapeDtypeStruct((batch, seq, LANES), BF),
            jax.ShapeDtypeStruct((batch, seq, LANES), BF),
        ],
        scratch_shapes=[pltpu.VMEM((1, LANES), F32)],
        compiler_params=_cparams(("arbitrary", "arbitrary")),
        name="forget_scan",
    )(fa, bias_lanes, tri, pq, pk, oq, ok)


def _with_bias_lanes(slab, extras, hp, parity):
    lane = lax.broadcasted_iota(jnp.int32, slab.shape, 1)
    own = (lane < HEAD_DIM) if parity == 0 else (lane >= HEAD_DIM)
    base = (HEAD_DIM if parity == 0 else 0) + 2 * N_PIECES * hp
    in_extras = (lane >= base) & (lane < base + 2 * N_PIECES)
    return jnp.where(own, slab, jnp.where(in_extras, extras, jnp.zeros_like(extras)))


def _fox_kernel(q_ref, k_ref, v_ref, gq_ref, gk_ref, o_ref,
                ka_scr, kb_scr, vt_scr, acc_scr, ot_scr, *, tq):
    hp = pl.program_id(1)
    qi = pl.program_id(2)
    k_scr = (ka_scr, kb_scr)

    @pl.when(qi == 0)
    def _prep():
        k2 = k_ref[...]
        gk = gk_ref[0]
        ka_scr[...] = _with_bias_lanes(k2, gk, hp, 0)
        kb_scr[...] = _with_bias_lanes(k2, gk, hp, 1)
        vt_scr[...] = v_ref[...].astype(F32).T.astype(BF)

    q2 = q_ref[...]
    gq = gq_ref[0]
    row = lax.broadcasted_iota(jnp.int32, (tq, tq), 0)
    col = lax.broadcasted_iota(jnp.int32, (tq, tq), 1)
    causal = row <= col

    for parity in range(2):
        qp = _with_bias_lanes(q2, gq, hp, parity)
        kp_scr = k_scr[parity]
        acc_scr[...] = jnp.zeros_like(acc_scr)

        def step(j, carry, masked):
            m, l = carry
            start = pl.multiple_of(j * tq, tq)
            kt = kp_scr[pl.ds(start, tq), :]
            st = _dot_nt(kt, qp)
            if masked:
                st = jnp.where(causal, st, NEG_INF)
            m_new = jnp.maximum(m, jnp.max(st, axis=0, keepdims=True))
            alpha = jnp.exp(m - m_new)
            pt = jnp.exp(st - m_new)
            l_new = alpha * l + jnp.sum(pt, axis=0, keepdims=True)
            vt = vt_scr[parity * HEAD_DIM:(parity + 1) * HEAD_DIM, pl.ds(start, tq)]
            acc_scr[...] = alpha * acc_scr[...] + _dot(vt, pt.astype(BF))
            return m_new, l_new

        init = (jnp.full((1, tq), NEG_INF, F32), jnp.zeros((1, tq), F32))
        carry = lax.fori_loop(0, qi, functools.partial(step, masked=False), init)
        _, l = step(qi, carry, True)
        ot_scr[parity * HEAD_DIM:(parity + 1) * HEAD_DIM, :] = acc_scr[...] / l

    o_ref[...] = ot_scr[...].T.astype(BF)


def _fox_attention(z, gq, gk, *, batch, seq, tq):
    n_q = seq // tq
    tokens = batch * seq
    return pl.pallas_call(
        functools.partial(_fox_kernel, tq=tq),
        grid=(batch, N_HEAD_PAIRS, n_q),
        in_specs=[
            pl.BlockSpec((tq, LANES), lambda b, hp, qi: (b * n_q + qi, hp)),
            pl.BlockSpec((seq, LANES), lambda b, hp, qi: (b, N_HEAD_PAIRS + hp)),
            pl.BlockSpec((seq, LANES), lambda b, hp, qi: (b, 2 * N_HEAD_PAIRS + hp)),
            pl.BlockSpec((1, tq, LANES), lambda b, hp, qi: (b, qi, 0)),
            pl.BlockSpec((1, seq, LANES), lambda b, hp, qi: (b, 0, 0)),
        ],
        out_specs=pl.BlockSpec((tq, LANES), lambda b, hp, qi: (b * n_q + qi, hp)),
        out_shape=jax.ShapeDtypeStruct((tokens, ATT_WIDTH), BF),
        scratch_shapes=[
            pltpu.VMEM((seq, LANES), BF),
            pltpu.VMEM((seq, LANES), BF),
            pltpu.VMEM((LANES, seq), BF),
            pltpu.VMEM((HEAD_DIM, tq), F32),
            pltpu.VMEM((LANES, tq), F32),
        ],
        compiler_params=_cparams(("arbitrary", "arbitrary", "arbitrary")),
        name="fox_attn",
    )(z, z, z, gq, gk)


def _log2(n):
    assert n > 0 and n & (n - 1) == 0, n
    return n.bit_length() - 1


def _dilated_kernel(q1, k1, v1, q4, k4, v4, q16, k16, v16, o_ref,
                    num1, num4, num16, den1, den4, den16, max1, max4, max16, bias_scr,
                    *, seq, blk, chunk, group, merge_rows):
    sources = ((q1, k1, v1), (q4, k4, v4), (q16, k16, v16))
    num_scrs = (num1, num4, num16)
    den_scrs = (den1, den4, den16)
    max_scrs = (max1, max4, max16)
    n_blocks = seq // blk

    low = lax.broadcasted_iota(jnp.int32, (blk, LANES), 1) < HEAD_DIM
    qrow = lax.broadcasted_iota(jnp.int32, (2 * blk, 2 * blk), 0) % blk
    kcol = lax.broadcasted_iota(jnp.int32, (2 * blk, 2 * blk), 1)
    dist = qrow + blk - kcol
    band = (dist >= 0) & (dist <= blk)
    bias_scr[0] = jnp.where(band & (kcol >= blk), 0.0, NEG_INF)
    bias_scr[1] = jnp.where(band, 0.0, NEG_INF)
    ones = jnp.ones((2 * blk, LANES), BF)

    for idx, (window, dilation) in enumerate(DIL_PATTERNS):
        assert window // dilation == blk
        q_ref, k_ref, v_ref = sources[idx]
        num_scr, den_scr, max_scr = num_scrs[idx], den_scrs[idx], max_scrs[idx]
        sub_shift = _log2(seq // dilation // blk)
        seg_rows = chunk // dilation
        seg = min(blk, seg_rows)

        def load(ref, j, l0):
            parts = []
            for s in range(blk // seg):
                l = l0 + s * seg
                p = lax.shift_right_logical(l, _log2(seg_rows))
                i = l & (seg_rows - 1)
                start = pl.multiple_of(p * chunk + j * seg_rows + i, seg)
                parts.append(ref[pl.ds(start, seg), :])
            return parts[0] if len(parts) == 1 else jnp.concatenate(parts, axis=0)

        def one_block(g):
            j = lax.shift_right_logical(g, sub_shift)
            gs = g & ((1 << sub_shift) - 1)
            l0 = gs * blk
            has_prev = gs > 0
            lp = jnp.maximum(l0 - blk, 0)
            q2 = load(q_ref, j, l0)
            zero = jnp.zeros_like(q2)
            qq = jnp.concatenate([jnp.where(low, q2, zero), jnp.where(low, zero, q2)], axis=0)
            kwin = jnp.concatenate([load(k_ref, j, lp), load(k_ref, j, l0)], axis=0)
            vwin = jnp.concatenate([load(v_ref, j, lp), load(v_ref, j, l0)], axis=0)
            s = _dot_nt(qq, kwin) + bias_scr[has_prev.astype(jnp.int32)]
            m = jnp.max(s, axis=1, keepdims=True)
            p = jnp.exp2(s - m).astype(BF)
            pv = _dot(p, jnp.concatenate([vwin, ones], axis=1))
            if dilation == 1:
                dst = pl.ds(pl.multiple_of(l0, blk), blk)
            else:
                dst = pl.ds(l0 * dilation + j, blk, stride=dilation)
            num_scr[dst, :] = jnp.where(low, pv[:blk, :LANES], pv[blk:, :LANES])
            den_scr[dst, :] = jnp.where(low, pv[:blk, LANES:], pv[blk:, LANES:])
            max_scr[dst, :] = jnp.where(low, jnp.broadcast_to(m[:blk], (blk, LANES)),
                                        jnp.broadcast_to(m[blk:], (blk, LANES)))

        def blocks(it, carry):
            for u in range(group):
                one_block(it * group + u)
            return carry

        lax.fori_loop(0, n_blocks // group, blocks, 0)

    def merge(c, carry):
        rows = pl.ds(pl.multiple_of(c * merge_rows, merge_rows), merge_rows)
        maxes = [max_scr[rows, :] for max_scr in max_scrs]
        top = functools.reduce(jnp.maximum, maxes)
        weights = [jnp.exp2(m - top) for m in maxes]
        num = sum(w * num_scr[rows, :] for w, num_scr in zip(weights, num_scrs))
        den = sum(w * den_scr[rows, :] for w, den_scr in zip(weights, den_scrs))
        o_ref[rows, :] = (num / den).astype(BF)
        return carry

    lax.fori_loop(0, seq // merge_rows, merge, 0)


def _dilated_attention(z, zd4, zd16, *, batch, seq):
    blk = DIL_PATTERNS[0][0] // DIL_PATTERNS[0][1]
    tokens = batch * seq
    per_section = ATT_WIDTH // LANES
    col = lambda section: pl.BlockSpec(
        (seq, LANES), lambda b, hp: (b, section * per_section + hp))
    f32_buf = pltpu.VMEM((seq, LANES), F32)
    return pl.pallas_call(
        functools.partial(_dilated_kernel, seq=seq, blk=blk, chunk=DIL_CHUNK, group=4,
                          merge_rows=256),
        grid=(batch, N_HEAD_PAIRS),
        in_specs=[col(DIL_SECTION), col(DIL_SECTION + 1), col(DIL_SECTION + 2),
                  col(0), col(1), col(2), col(0), col(1), col(2)],
        out_specs=pl.BlockSpec((seq, LANES), lambda b, hp: (b, hp)),
        out_shape=jax.ShapeDtypeStruct((tokens, ATT_WIDTH), BF),
        scratch_shapes=[f32_buf] * 9 + [pltpu.VMEM((2, 2 * blk, 2 * blk), F32)],
        compiler_params=_cparams(("arbitrary", "arbitrary")),
        name="dilated_attn",
    )(z, z, z, zd4, zd4, zd4, zd16, zd16, zd16)


def _mix_kernel(ya_ref, yb_ref, ga_ref, gb_ref, x_ref, woa_ref, wob_ref, wout_ref,
                g_ref, o_ref):
    pa = _dot(ya_ref[...], woa_ref[...])
    pb = _dot(yb_ref[...], wob_ref[...])
    mixed = (jax.nn.sigmoid(ga_ref[...].astype(F32)) * pa
             + jax.nn.sigmoid(gb_ref[...].astype(F32)) * pb)
    y = _dot(mixed.astype(BF), wout_ref[...])
    o_ref[...] = x_ref[...] + _rms(y, g_ref[...])


def _mix(ya, yb, z, x2d, woa, wob, wout, g, *, tm):
    tokens = x2d.shape[0]
    gate_blk = lambda off: pl.BlockSpec((tm, D_MODEL), lambda i: (i, off))
    const = lambda i: (0, 0)
    return pl.pallas_call(
        _mix_kernel,
        grid=(tokens // tm,),
        in_specs=[
            pl.BlockSpec((tm, ATT_WIDTH), lambda i: (i, 0)),
            pl.BlockSpec((tm, ATT_WIDTH), lambda i: (i, 0)),
            gate_blk(3),
            gate_blk(4),
            pl.BlockSpec((tm, D_MODEL), lambda i: (i, 0)),
            pl.BlockSpec((ATT_WIDTH, D_MODEL), const),
            pl.BlockSpec((ATT_WIDTH, D_MODEL), const),
            pl.BlockSpec((D_MODEL, D_MODEL), const),
            pl.BlockSpec((1, D_MODEL), const),
        ],
        out_specs=pl.BlockSpec((tm, D_MODEL), lambda i: (i, 0)),
        out_shape=jax.ShapeDtypeStruct((tokens, D_MODEL), F32),
        compiler_params=_cparams(("arbitrary",)),
        name="mix",
    )(ya, yb, z, z, x2d, woa, wob, wout, g)


def _ffn_kernel(x_ref, halo_ref, gpre_ref, wa_ref, wb_ref, cwa_ref, cwb_ref,
                cba_ref, cbb_ref, wd_ref, gpost_ref, o_ref, h_scr, acc_scr,
                *, tm, tiles_per_seq):
    i = pl.program_id(0)
    f = pl.program_id(1)
    n_f = pl.num_programs(1)
    halo = BF16_ROWS

    @pl.when(f == 0)
    def _norm():
        g = gpre_ref[...]
        hh = _rms(halo_ref[...], g)
        hh = jnp.where(i % tiles_per_seq == 0, jnp.zeros_like(hh), hh)
        h_scr[0:halo, :] = hh.astype(BF)
        h_scr[halo:, :] = _rms(x_ref[...], g).astype(BF)
        acc_scr[...] = jnp.zeros_like(acc_scr)

    h = h_scr[...]

    def conv(w_ref, cw_ref, cb_ref):
        u = _dot(h, w_ref[...])
        out = cb_ref[...]
        for tap in range(CONV_WIDTH):
            lo = halo - (CONV_WIDTH - 1) + tap
            out = out + cw_ref[tap:tap + 1, :] * u[lo:lo + tm, :]
        return out

    a = conv(wa_ref, cwa_ref, cba_ref)
    bgate = conv(wb_ref, cwb_ref, cbb_ref)
    c0 = math.sqrt(2.0 / math.pi)
    gelu = 0.5 * a * (1.0 + jnp.tanh(c0 * (a + 0.044715 * (a * a * a))))
    acc_scr[...] += _dot((gelu * bgate).astype(BF), wd_ref[...])

    @pl.when(f == n_f - 1)
    def _out():
        o_ref[...] = x_ref[...] + _rms(acc_scr[...], gpost_ref[...])


def _ffn(x1, g_pre, w_up, conv_w, conv_b, w_down, g_post, *, seq, tm, tf):
    tokens = x1.shape[0]
    n_f = D_FF // tf
    halo = BF16_ROWS
    per = tm // halo
    const = lambda i, f: (0, 0)
    col_a = lambda i, f: (0, f)
    col_b = lambda i, f: (0, n_f + f)
    return pl.pallas_call(
        functools.partial(_ffn_kernel, tm=tm, tiles_per_seq=seq // tm),
        grid=(tokens // tm, n_f),
        in_specs=[
            pl.BlockSpec((tm, D_MODEL), lambda i, f: (i, 0)),
            pl.BlockSpec((halo, D_MODEL), lambda i, f: (jnp.maximum(i * per - 1, 0), 0)),
            pl.BlockSpec((1, D_MODEL), const),
            pl.BlockSpec((D_MODEL, tf), col_a),
            pl.BlockSpec((D_MODEL, tf), col_b),
            pl.BlockSpec((CONV_WIDTH, tf), col_a),
            pl.BlockSpec((CONV_WIDTH, tf), col_b),
            pl.BlockSpec((1, tf), col_a),
            pl.BlockSpec((1, tf), col_b),
            pl.BlockSpec((tf, D_MODEL), lambda i, f: (f, 0)),
            pl.BlockSpec((1, D_MODEL), const),
        ],
        out_specs=pl.BlockSpec((tm, D_MODEL), lambda i, f: (i, 0)),
        out_shape=jax.ShapeDtypeStruct((tokens, D_MODEL), F32),
        scratch_shapes=[
            pltpu.VMEM((tm + halo, D_MODEL), BF),
            pltpu.VMEM((tm, D_MODEL), F32),
        ],
        compiler_params=_cparams(("arbitrary", "arbitrary")),
        name="ffn",
    )(x1, x1, g_pre, w_up, w_up, conv_w, conv_w, conv_b, conv_b, w_down, g_post)


def _rope_freq_lanes():
    inv_freq = ROPE_THETA ** (-jnp.arange(ROPE_HALF, dtype=F32) * 2.0 / ROPE_DIM)
    lane = np.arange(LANES) % HEAD_DIM
    return inv_freq[lane % ROPE_HALF].reshape(1, LANES)


def kernel(x, g_pre_mix, w_in, b_forget, w_o_fox, w_o_dil, w_out, g_post_mix,
           g_pre_ffn, w_up, conv_w, conv_b, w_down, g_post_ffn):
    batch, seq, d_model = x.shape
    assert d_model == D_MODEL and seq % 1024 == 0
    depth = w_in.shape[0]
    fox_end = 3 * ATT_WIDTH
    freq_lanes = _rope_freq_lanes()
    x2d = x.reshape(batch * seq, D_MODEL)
    row = lambda v: v.reshape(1, -1)
    for l in range(depth):
        w_main = jnp.concatenate(
            [w_in[l][:, :fox_end], w_in[l][:, fox_end + N_HEADS:]], axis=1).astype(BF)
        w_f = jnp.pad(w_in[l][:, fox_end:fox_end + N_HEADS],
                      ((0, 0), (0, LANES - N_HEADS))).astype(BF)
        bias_lanes = jnp.pad(b_forget[l], (0, LANES - N_HEADS)).reshape(1, LANES)

        z, fa, zd4, zd16 = _in_proj(x2d, row(g_pre_mix[l]), w_main, w_f, freq_lanes,
                                    batch=batch, seq=seq, tm=DIL_CHUNK)
        gq, gk = _forget_scan(fa, bias_lanes, batch=batch, seq=seq, chunk=512)
        ya = _fox_attention(z, gq, gk, batch=batch, seq=seq, tq=512)
        yb = _dilated_attention(z, zd4, zd16, batch=batch, seq=seq)
        x2d = _mix(ya, yb, z, x2d, w_o_fox[l].astype(BF), w_o_dil[l].astype(BF),
                   w_out[l].astype(BF), row(g_post_mix[l]), tm=512)
        x2d = _ffn(x2d, row(g_pre_ffn[l]), w_up[l].astype(BF), conv_w[l],
                   row(conv_b[l]), w_down[l].astype(BF), row(g_post_ffn[l]),
                   seq=seq, tm=1024, tf=256)
    return x2d.reshape(batch, seq, D_MODEL)
```

```python
import functools
import math

import numpy as np
import jax
import jax.numpy as jnp
from jax import lax
from jax.experimental import pallas as pl
from jax.experimental.pallas import tpu as pltpu

D_MODEL = 1024
HEAD_DIM = 64
N_HEADS = 8
ATT_WIDTH = N_HEADS * HEAD_DIM
N_HEAD_PAIRS = N_HEADS // 2
DIL_PATTERNS = ((128, 1), (512, 4), (2048, 16))
ROPE_DIM = HEAD_DIM // 4
ROPE_HALF = ROPE_DIM // 2
ROPE_THETA = 500000.0
D_FF = 2816
CONV_WIDTH = 3
RMS_EPS = 1e-6
NEG_INF = -1e30
Q_SCALE = 1.0 / math.sqrt(HEAD_DIM)
LOG2_E = math.log2(math.e)

LANES = 128
BF16_ROWS = 16
Z_WIDTH = 3 * ATT_WIDTH * 2 + 2 * D_MODEL
Z_BLK = 512
DIL_SECTION = 3
DIL_CHUNK = 1024
N_PIECES = 3
VMEM_LIMIT = 56 * 1024 * 1024

BF = jnp.bfloat16
F32 = jnp.float32


def _cparams(sem):
    return pltpu.CompilerParams(dimension_semantics=sem, vmem_limit_bytes=VMEM_LIMIT)


def _rms(xf, g):
    inv = lax.rsqrt(jnp.mean(xf * xf, axis=-1, keepdims=True) + RMS_EPS)
    return xf * inv * g


def _split3(x):
    hi = x.astype(BF)
    r1 = x - hi.astype(F32)
    mid = r1.astype(BF)
    lo = (r1 - mid.astype(F32)).astype(BF)
    return hi, mid, lo


def _dot(a, b):
    return jnp.dot(a, b, preferred_element_type=F32)


def _dot_nt(a, b):
    return lax.dot_general(a, b, (((1,), (1,)), ((), ())), preferred_element_type=F32)


def _in_proj_kernel(x_ref, g_ref, w_ref, wf_ref, freq_ref, z_ref, fa_ref, zd4_ref, zd16_ref,
                    h_scr, cos_scr, sneg_scr, spos_scr, stage_scr, *, tm):
    p = pl.program_id(0)
    b = pl.program_id(1)
    j = pl.program_id(2)
    zd_refs = {4: zd4_ref, 16: zd16_ref}

    def emit_dilated(val):
        z_ref[...] = val.astype(BF)
        for s in range(Z_BLK // LANES):
            stage_scr[s] = val[:, s * LANES:(s + 1) * LANES]
        for _, r in DIL_PATTERNS[1:]:
            seg = tm // r
            for jj in range(r):
                for s in range(Z_BLK // LANES):
                    zd_refs[r][jj * seg:(jj + 1) * seg, s * LANES:(s + 1) * LANES] = (
                        stage_scr[s, pl.ds(jj, seg, stride=r), :].astype(BF))

    @pl.when((b == 0) & (j == 0))
    def _tables():
        pos = (p * tm + lax.broadcasted_iota(jnp.int32, (tm, LANES), 0)).astype(F32)
        lane = lax.broadcasted_iota(jnp.int32, (tm, LANES), 1)
        c = lane % HEAD_DIM
        ang = pos * freq_ref[...]
        cs = jnp.cos(ang)
        sn = jnp.sin(ang)
        cos_scr[...] = jnp.where(c < ROPE_DIM, cs, 1.0)
        sneg_scr[...] = jnp.where(c < ROPE_HALF, -sn, 0.0)
        spos_scr[...] = jnp.where((c >= ROPE_HALF) & (c < ROPE_DIM), sn, 0.0)

    @pl.when(j == 0)
    def _norm():
        h = _rms(x_ref[...], g_ref[...]).astype(BF)
        h_scr[...] = h
        fa_ref[...] = _dot(h, wf_ref[...])

    acc = _dot(h_scr[...], w_ref[...])

    def rope(a):
        outs = []
        for s in range(Z_BLK // LANES):
            t = a[:, s * LANES:(s + 1) * LANES]
            up = pltpu.roll(t, LANES - ROPE_HALF, 1)
            dn = pltpu.roll(t, ROPE_HALF, 1)
            outs.append(t * cos_scr[...] + up * sneg_scr[...] + dn * spos_scr[...])
        return jnp.concatenate(outs, axis=1)

    is_q_fox = j == 0
    is_q_dil = j == DIL_SECTION
    is_k_dil = j == DIL_SECTION + 1
    is_v_dil = j == DIL_SECTION + 2

    @pl.when(is_q_fox)
    def _():
        z_ref[...] = (acc * (Q_SCALE * LOG2_E)).astype(BF)

    @pl.when(is_q_dil)
    def _():
        emit_dilated(rope(acc) * (Q_SCALE * LOG2_E))

    @pl.when(is_k_dil)
    def _():
        emit_dilated(rope(acc))

    @pl.when(is_v_dil)
    def _():
        emit_dilated(acc)

    @pl.when(jnp.logical_not(is_q_fox | is_q_dil | is_k_dil | is_v_dil))
    def _():
        z_ref[...] = acc.astype(BF)


def _in_proj(x2d, g, w_main, w_f, freq_lanes, *, batch, seq, tm):
    n_p = seq // tm
    n_j = Z_WIDTH // Z_BLK
    tokens = batch * seq
    row = lambda p, b, j: (b * n_p + p, 0)
    dil_blk = pl.BlockSpec(
        (tm, Z_BLK), lambda p, b, j: (b * n_p + p, jnp.clip(j - DIL_SECTION, 0, 2)))
    dil_shape = jax.ShapeDtypeStruct((tokens, 3 * ATT_WIDTH), BF)
    return pl.pallas_call(
        functools.partial(_in_proj_kernel, tm=tm),
        grid=(n_p, batch, n_j),
        in_specs=[
            pl.BlockSpec((tm, D_MODEL), row),
            pl.BlockSpec((1, D_MODEL), lambda p, b, j: (0, 0)),
            pl.BlockSpec((D_MODEL, Z_BLK), lambda p, b, j: (0, j)),
            pl.BlockSpec((D_MODEL, LANES), lambda p, b, j: (0, 0)),
            pl.BlockSpec((1, LANES), lambda p, b, j: (0, 0)),
        ],
        out_specs=[
            pl.BlockSpec((tm, Z_BLK), lambda p, b, j: (b * n_p + p, j)),
            pl.BlockSpec((tm, LANES), row),
            dil_blk,
            dil_blk,
        ],
        out_shape=[
            jax.ShapeDtypeStruct((tokens, Z_WIDTH), BF),
            jax.ShapeDtypeStruct((tokens, LANES), F32),
            dil_shape,
            dil_shape,
        ],
        scratch_shapes=[
            pltpu.VMEM((tm, D_MODEL), BF),
            pltpu.VMEM((tm, LANES), F32),
            pltpu.VMEM((tm, LANES), F32),
            pltpu.VMEM((tm, LANES), F32),
            pltpu.VMEM((Z_BLK // LANES, tm, LANES), F32),
        ],
        compiler_params=_cparams(("arbitrary", "arbitrary", "arbitrary")),
        name="in_proj",
    )(x2d, g, w_main, w_f, freq_lanes)


def _extras_base(head):
    return (HEAD_DIM if head % 2 == 0 else 0) + 2 * N_PIECES * (head // 2)


def _forget_scan_kernel(fa_ref, bias_ref, tri_ref, pq_ref, pk_ref, oq_ref, ok_ref,
                        gq_ref, gk_ref, carry_scr):
    c = pl.program_id(1)

    @pl.when(c == 0)
    def _():
        carry_scr[...] = jnp.zeros_like(carry_scr)

    t = fa_ref[...] + bias_ref[...]
    log_f = jnp.minimum(t, 0.0) - jnp.log1p(jnp.exp(-jnp.abs(t)))
    tri = tri_ref[...]
    run = carry_scr[...]
    for piece in _split3(log_f):
        run = run + _dot(tri, piece)
    rows = run.shape[0]
    carry_scr[...] = run[rows - 1:rows, :]
    gq = oq_ref[...].astype(F32)
    gk = ok_ref[...].astype(F32)
    for i, piece in enumerate(_split3(run * LOG2_E)):
        gq = gq + _dot(piece, pq_ref[i])
        gk = gk - _dot(piece, pk_ref[i])
    gq_ref[0] = gq.astype(BF)
    gk_ref[0] = gk.astype(BF)


def _forget_scan_constants(chunk):
    tri = np.tril(np.ones((chunk, chunk), np.float32))
    pq = np.zeros((N_PIECES, LANES, LANES), np.float32)
    pk = np.zeros((N_PIECES, LANES, LANES), np.float32)
    oq = np.zeros((1, LANES), np.float32)
    ok = np.zeros((1, LANES), np.float32)
    for h in range(N_HEADS):
        base = _extras_base(h)
        for i in range(N_PIECES):
            pq[i, h, base + i] = 1.0
            oq[0, base + N_PIECES + i] = 1.0
            ok[0, base + i] = 1.0
            pk[i, h, base + N_PIECES + i] = 1.0
    as_bf = lambda a: jnp.asarray(a, BF)
    return as_bf(tri), as_bf(pq), as_bf(pk), as_bf(oq), as_bf(ok)


def _forget_scan(fa, bias_lanes, *, batch, seq, chunk):
    n_c = seq // chunk
    tri, pq, pk, oq, ok = _forget_scan_constants(chunk)
    const2 = lambda b, c: (0, 0)
    const3 = lambda b, c: (0, 0, 0)
    return pl.pallas_call(
        _forget_scan_kernel,
        grid=(batch, n_c),
        in_specs=[
            pl.BlockSpec((chunk, LANES), lambda b, c: (b * n_c + c, 0)),
            pl.BlockSpec((1, LANES), const2),
            pl.BlockSpec((chunk, chunk), const2),
            pl.BlockSpec((N_PIECES, LANES, LANES), const3),
            pl.BlockSpec((N_PIECES, LANES, LANES), const3),
            pl.BlockSpec((1, LANES), const2),
            pl.BlockSpec((1, LANES), const2),
        ],
        out_specs=[
            pl.BlockSpec((1, chunk, LANES), lambda b, c: (b, c, 0)),
            pl.BlockSpec((1, chunk, LANES), lambda b, c: (b, c, 0)),
        ],
        out_shape=[
            jax.ShapeDtypeStruct((batch, seq, LANES), BF),
            jax.ShapeDtypeStruct((batch, seq, LANES), BF),
        ],
        scratch_shapes=[pltpu.VMEM((1, LANES), F32)],
        compiler_params=_cparams(("arbitrary", "arbitrary")),
        name="forget_scan",
    )(fa, bias_lanes, tri, pq, pk, oq, ok)


def _with_bias_lanes(slab, extras, hp, parity):
    lane = lax.broadcasted_iota(jnp.int32, slab.shape, 1)
    own = (lane < HEAD_DIM) if parity == 0 else (lane >= HEAD_DIM)
    base = (HEAD_DIM if parity == 0 else 0) + 2 * N_PIECES * hp
    in_extras = (lane >= base) & (lane < base + 2 * N_PIECES)
    return jnp.where(own, slab, jnp.where(in_extras, extras, jnp.zeros_like(extras)))


def _fox_kernel(q_ref, k_ref, v_ref, gq_ref, gk_ref, o_ref,
                ka_scr, kb_scr, vta_scr, vtb_scr, bias_scr, s_scr, acc_scr, ot_scr, *, tq):
    hp = pl.program_id(1)
    qi = pl.program_id(2)
    k_scrs = (ka_scr, kb_scr)
    vt_scrs = (vta_scr, vtb_scr)
    seq = k_ref.shape[0]

    @pl.when(qi == 0)
    def _prep():
        k2 = k_ref[...]
        gk = gk_ref[0]
        ka_scr[...] = _with_bias_lanes(k2, gk, hp, 0)
        kb_scr[...] = _with_bias_lanes(k2, gk, hp, 1)
        vt = v_ref[...].astype(F32).T.astype(BF)
        ones = jnp.ones((BF16_ROWS, seq), BF)
        for head, vt_scr in enumerate(vt_scrs):
            vt_scr[0:HEAD_DIM, :] = vt[head * HEAD_DIM:(head + 1) * HEAD_DIM, :]
            vt_scr[HEAD_DIM:, :] = ones
        row = lax.broadcasted_iota(jnp.int32, (tq, tq), 0)
        col = lax.broadcasted_iota(jnp.int32, (tq, tq), 1)
        bias_scr[...] = jnp.where(row <= col, 0.0, NEG_INF)

    q2 = q_ref[...]
    gq = gq_ref[0]
    qps = [_with_bias_lanes(q2, gq, hp, head) for head in range(2)]

    def scores(head, start, masked):
        kt = k_scrs[head][pl.ds(start, tq), :]
        st = _dot_nt(kt, qps[head])
        if masked:
            st = st + bias_scr[...]
        s_scr[head] = st
        return jnp.max(st, axis=0, keepdims=True)

    def softmax_pv(head, st, tile, tile_max, m):
        start = pl.multiple_of(tile * tq, tq)
        m_new = jnp.maximum(m, tile_max)
        alpha = jnp.exp2(m - m_new)
        pt = jnp.exp2(st - m_new).astype(BF)
        vt = vt_scrs[head][:, pl.ds(start, tq)]
        acc_scr[head] = alpha * acc_scr[head] + _dot(vt, pt)
        return m_new

    acc_scr[...] = jnp.zeros_like(acc_scr)
    diag = pl.multiple_of(qi * tq, tq)
    first_max = [scores(head, diag, True) for head in range(2)]
    m_init = jnp.full((1, tq), NEG_INF, F32)

    def body(j, carry):
        max_a, max_b, m_a, m_b, prev_tile = carry
        prev = [s_scr[head] for head in range(2)]
        start = pl.multiple_of(j * tq, tq)
        next_max = [scores(head, start, False) for head in range(2)]
        m_a = softmax_pv(0, prev[0], prev_tile, max_a, m_a)
        m_b = softmax_pv(1, prev[1], prev_tile, max_b, m_b)
        return next_max[0], next_max[1], m_a, m_b, j

    max_a, max_b, m_a, m_b, prev_tile = lax.fori_loop(
        0, qi, body, (first_max[0], first_max[1], m_init, m_init, qi))
    softmax_pv(0, s_scr[0], prev_tile, max_a, m_a)
    softmax_pv(1, s_scr[1], prev_tile, max_b, m_b)

    for head in range(2):
        acc = acc_scr[head]
        ot_scr[head * HEAD_DIM:(head + 1) * HEAD_DIM, :] = (
            acc[0:HEAD_DIM, :] / acc[HEAD_DIM:HEAD_DIM + 1, :])
    o_ref[...] = ot_scr[...].T.astype(BF)


def _fox_attention(z, gq, gk, *, batch, seq, tq):
    n_q = seq // tq
    tokens = batch * seq
    return pl.pallas_call(
        functools.partial(_fox_kernel, tq=tq),
        grid=(batch, N_HEAD_PAIRS, n_q),
        in_specs=[
            pl.BlockSpec((tq, LANES), lambda b, hp, qi: (b * n_q + qi, hp)),
            pl.BlockSpec((seq, LANES), lambda b, hp, qi: (b, N_HEAD_PAIRS + hp)),
            pl.BlockSpec((seq, LANES), lambda b, hp, qi: (b, 2 * N_HEAD_PAIRS + hp)),
            pl.BlockSpec((1, tq, LANES), lambda b, hp, qi: (b, qi, 0)),
            pl.BlockSpec((1, seq, LANES), lambda b, hp, qi: (b, 0, 0)),
        ],
        out_specs=pl.BlockSpec((tq, LANES), lambda b, hp, qi: (b * n_q + qi, hp)),
        out_shape=jax.ShapeDtypeStruct((tokens, ATT_WIDTH), BF),
        scratch_shapes=[
            pltpu.VMEM((seq, LANES), BF),
            pltpu.VMEM((seq, LANES), BF),
            pltpu.VMEM((HEAD_DIM + BF16_ROWS, seq), BF),
            pltpu.VMEM((HEAD_DIM + BF16_ROWS, seq), BF),
            pltpu.VMEM((tq, tq), F32),
            pltpu.VMEM((2, tq, tq), F32),
            pltpu.VMEM((2, HEAD_DIM + BF16_ROWS, tq), F32),
            pltpu.VMEM((LANES, tq), F32),
        ],
        compiler_params=_cparams(("arbitrary", "arbitrary", "arbitrary")),
        name="fox_attn",
    )(z, z, z, gq, gk)


def _log2(n):
    assert n > 0 and n & (n - 1) == 0, n
    return n.bit_length() - 1


def _dilated_kernel(q1, k1, v1, q4, k4, v4, q16, k16, v16, o_ref,
                    num1, num4, num16, den1, den4, den16, max1, max4, max16, bias_scr,
                    *, seq, blk, chunk, group, merge_rows):
    sources = ((q1, k1, v1), (q4, k4, v4), (q16, k16, v16))
    num_scrs = (num1, num4, num16)
    den_scrs = (den1, den4, den16)
    max_scrs = (max1, max4, max16)
    n_blocks = seq // blk

    low = lax.broadcasted_iota(jnp.int32, (blk, LANES), 1) < HEAD_DIM
    qrow = lax.broadcasted_iota(jnp.int32, (2 * blk, 2 * blk), 0) % blk
    kcol = lax.broadcasted_iota(jnp.int32, (2 * blk, 2 * blk), 1)
    dist = qrow + blk - kcol
    band = (dist >= 0) & (dist <= blk)
    bias_scr[0] = jnp.where(band & (kcol >= blk), 0.0, NEG_INF)
    bias_scr[1] = jnp.where(band, 0.0, NEG_INF)
    ones = jnp.ones((2 * blk, LANES), BF)

    for idx, (window, dilation) in enumerate(DIL_PATTERNS):
        assert window // dilation == blk
        q_ref, k_ref, v_ref = sources[idx]
        num_scr, den_scr, max_scr = num_scrs[idx], den_scrs[idx], max_scrs[idx]
        sub_shift = _log2(seq // dilation // blk)
        seg_rows = chunk // dilation
        seg = min(blk, seg_rows)

        def load(ref, j, l0):
            parts = []
            for s in range(blk // seg):
                l = l0 + s * seg
                p = lax.shift_right_logical(l, _log2(seg_rows))
                i = l & (seg_rows - 1)
                start = pl.multiple_of(p * chunk + j * seg_rows + i, seg)
                parts.append(ref[pl.ds(start, seg), :])
            return parts[0] if len(parts) == 1 else jnp.concatenate(parts, axis=0)

        def one_block(g):
            j = lax.shift_right_logical(g, sub_shift)
            gs = g & ((1 << sub_shift) - 1)
            l0 = gs * blk
            lp = jnp.maximum(l0 - blk, 0)
            q2 = load(q_ref, j, l0)
            zero = jnp.zeros_like(q2)
            qq = jnp.concatenate([jnp.where(low, q2, zero), jnp.where(low, zero, q2)], axis=0)
            kwin = jnp.concatenate([load(k_ref, j, lp), load(k_ref, j, l0)], axis=0)
            vwin = jnp.concatenate([load(v_ref, j, lp), load(v_ref, j, l0)], axis=0)
            s = _dot_nt(qq, kwin) + bias_scr[jnp.minimum(gs, 1)]
            m = jnp.max(s, axis=1, keepdims=True)
            p = jnp.exp2(s - m).astype(BF)
            pv = _dot(p, jnp.concatenate([vwin, ones], axis=1))
            if dilation == 1:
                dst = pl.ds(pl.multiple_of(l0, blk), blk)
            else:
                dst = pl.ds(l0 * dilation + j, blk, stride=dilation)
            num_scr[dst, :] = jnp.where(low, pv[:blk, :LANES], pv[blk:, :LANES])
            den_scr[dst, :] = jnp.where(low, pv[:blk, LANES:], pv[blk:, LANES:])
            max_scr[dst, :] = jnp.where(low, jnp.broadcast_to(m[:blk], (blk, LANES)),
                                        jnp.broadcast_to(m[blk:], (blk, LANES)))

        def blocks(it, carry):
            for u in range(group):
                one_block(it * group + u)
            return carry

        lax.fori_loop(0, n_blocks // group, blocks, 0)

    def merge(c, carry):
        rows = pl.ds(pl.multiple_of(c * merge_rows, merge_rows), merge_rows)
        maxes = [max_scr[rows, :] for max_scr in max_scrs]
        top = functools.reduce(jnp.maximum, maxes)
        weights = [jnp.exp2(m - top) for m in maxes]
        num = sum(w * num_scr[rows, :] for w, num_scr in zip(weights, num_scrs))
        den = sum(w * den_scr[rows, :] for w, den_scr in zip(weights, den_scrs))
        o_ref[rows, :] = (num / den).astype(BF)
        return carry

    lax.fori_loop(0, seq // merge_rows, merge, 0)


def _dilated_attention(z, zd4, zd16, *, batch, seq):
    blk = DIL_PATTERNS[0][0] // DIL_PATTERNS[0][1]
    tokens = batch * seq
    per_section = ATT_WIDTH // LANES
    col = lambda section: pl.BlockSpec(
        (seq, LANES), lambda b, hp: (b, section * per_section + hp))
    f32_buf = pltpu.VMEM((seq, LANES), F32)
    return pl.pallas_call(
        functools.partial(_dilated_kernel, seq=seq, blk=blk, chunk=DIL_CHUNK, group=4,
                          merge_rows=256),
        grid=(batch, N_HEAD_PAIRS),
        in_specs=[col(DIL_SECTION), col(DIL_SECTION + 1), col(DIL_SECTION + 2),
                  col(0), col(1), col(2), col(0), col(1), col(2)],
        out_specs=pl.BlockSpec((seq, LANES), lambda b, hp: (b, hp)),
        out_shape=jax.ShapeDtypeStruct((tokens, ATT_WIDTH), BF),
        scratch_shapes=[f32_buf] * 9 + [pltpu.VMEM((2, 2 * blk, 2 * blk), F32)],
        compiler_params=_cparams(("arbitrary", "arbitrary")),
        name="dilated_attn",
    )(z, z, z, zd4, zd4, zd4, zd16, zd16, zd16)


def _mix_kernel(ya_ref, yb_ref, ga_ref, gb_ref, x_ref, woa_ref, wob_ref, wout_ref,
                g_ref, o_ref):
    pa = _dot(ya_ref[...], woa_ref[...])
    pb = _dot(yb_ref[...], wob_ref[...])
    mixed = (jax.nn.sigmoid(ga_ref[...].astype(F32)) * pa
             + jax.nn.sigmoid(gb_ref[...].astype(F32)) * pb)
    y = _dot(mixed.astype(BF), wout_ref[...])
    o_ref[...] = x_ref[...] + _rms(y, g_ref[...])


def _mix(ya, yb, z, x2d, woa, wob, wout, g, *, tm):
    tokens = x2d.shape[0]
    gate_blk = lambda off: pl.BlockSpec((tm, D_MODEL), lambda i: (i, off))
    const = lambda i: (0, 0)
    return pl.pallas_call(
        _mix_kernel,
        grid=(tokens // tm,),
        in_specs=[
            pl.BlockSpec((tm, ATT_WIDTH), lambda i: (i, 0)),
            pl.BlockSpec((tm, ATT_WIDTH), lambda i: (i, 0)),
            gate_blk(3),
            gate_blk(4),
            pl.BlockSpec((tm, D_MODEL), lambda i: (i, 0)),
            pl.BlockSpec((ATT_WIDTH, D_MODEL), const),
            pl.BlockSpec((ATT_WIDTH, D_MODEL), const),
            pl.BlockSpec((D_MODEL, D_MODEL), const),
            pl.BlockSpec((1, D_MODEL), const),
        ],
        out_specs=pl.BlockSpec((tm, D_MODEL), lambda i: (i, 0)),
        out_shape=jax.ShapeDtypeStruct((tokens, D_MODEL), F32),
        compiler_params=_cparams(("arbitrary",)),
        name="mix",
    )(ya, yb, z, z, x2d, woa, wob, wout, g)


def _ffn_kernel(x_ref, halo_ref, gpre_ref, wa_ref, wb_ref, cwa_ref, cwb_ref,
                cba_ref, cbb_ref, wd_ref, gpost_ref, o_ref, h_scr, acc_scr,
                *, tm, tiles_per_seq):
    i = pl.program_id(0)
    f = pl.program_id(1)
    n_f = pl.num_programs(1)
    halo = BF16_ROWS

    @pl.when(f == 0)
    def _norm():
        g = gpre_ref[...]
        hh = _rms(halo_ref[...], g)
        hh = jnp.where(i % tiles_per_seq == 0, jnp.zeros_like(hh), hh)
        h_scr[0:halo, :] = hh.astype(BF)
        h_scr[halo:, :] = _rms(x_ref[...], g).astype(BF)
        acc_scr[...] = jnp.zeros_like(acc_scr)

    h = h_scr[...]

    def conv(w_ref, cw_ref, cb_ref):
        u = _dot(h, w_ref[...])
        out = cb_ref[...]
        for tap in range(CONV_WIDTH):
            lo = halo - (CONV_WIDTH - 1) + tap
            out = out + cw_ref[tap:tap + 1, :] * u[lo:lo + tm, :]
        return out

    a = conv(wa_ref, cwa_ref, cba_ref)
    bgate = conv(wb_ref, cwb_ref, cbb_ref)
    c0 = math.sqrt(2.0 / math.pi)
    gelu = 0.5 * a * (1.0 + jnp.tanh(c0 * (a + 0.044715 * (a * a * a))))
    acc_scr[...] += _dot((gelu * bgate).astype(BF), wd_ref[...])

    @pl.when(f == n_f - 1)
    def _out():
        o_ref[...] = x_ref[...] + _rms(acc_scr[...], gpost_ref[...])


def _ffn(x1, g_pre, w_up, conv_w, conv_b, w_down, g_post, *, seq, tm, tf):
    tokens = x1.shape[0]
    n_f = D_FF // tf
    halo = BF16_ROWS
    per = tm // halo
    const = lambda i, f: (0, 0)
    col_a = lambda i, f: (0, f)
    col_b = lambda i, f: (0, n_f + f)
    return pl.pallas_call(
        functools.partial(_ffn_kernel, tm=tm, tiles_per_seq=seq // tm),
        grid=(tokens // tm, n_f),
        in_specs=[
            pl.BlockSpec((tm, D_MODEL), lambda i, f: (i, 0)),
            pl.BlockSpec((halo, D_MODEL), lambda i, f: (jnp.maximum(i * per - 1, 0), 0)),
            pl.BlockSpec((1, D_MODEL), const),
            pl.BlockSpec((D_MODEL, tf), col_a),
            pl.BlockSpec((D_MODEL, tf), col_b),
            pl.BlockSpec((CONV_WIDTH, tf), col_a),
            pl.BlockSpec((CONV_WIDTH, tf), col_b),
            pl.BlockSpec((1, tf), col_a),
            pl.BlockSpec((1, tf), col_b),
            pl.BlockSpec((tf, D_MODEL), lambda i, f: (f, 0)),
            pl.BlockSpec((1, D_MODEL), const),
        ],
        out_specs=pl.BlockSpec((tm, D_MODEL), lambda i, f: (i, 0)),
        out_shape=jax.ShapeDtypeStruct((tokens, D_MODEL), F32),
        scratch_shapes=[
            pltpu.VMEM((tm + halo, D_MODEL), BF),
            pltpu.VMEM((tm, D_MODEL), F32),
        ],
        compiler_params=_cparams(("arbitrary", "arbitrary")),
        name="ffn",
    )(x1, x1, g_pre, w_up, w_up, conv_w, conv_w, conv_b, conv_b, w_down, g_post)


def _rope_freq_lanes():
    inv_freq = ROPE_THETA ** (-jnp.arange(ROPE_HALF, dtype=F32) * 2.0 / ROPE_DIM)
    lane = np.arange(LANES) % HEAD_DIM
    return inv_freq[lane % ROPE_HALF].reshape(1, LANES)


def kernel(x, g_pre_mix, w_in, b_forget, w_o_fox, w_o_dil, w_out, g_post_mix,
           g_pre_ffn, w_up, conv_w, conv_b, w_down, g_post_ffn):
    batch, seq, d_model = x.shape
    assert d_model == D_MODEL and seq % 1024 == 0
    depth = w_in.shape[0]
    fox_end = 3 * ATT_WIDTH
    freq_lanes = _rope_freq_lanes()
    x2d = x.reshape(batch * seq, D_MODEL)
    row = lambda v: v.reshape(1, -1)
    for l in range(depth):
        w_main = jnp.concatenate(
            [w_in[l][:, :fox_end], w_in[l][:, fox_end + N_HEADS:]], axis=1).astype(BF)
        w_f = jnp.pad(w_in[l][:, fox_end:fox_end + N_HEADS],
                      ((0, 0), (0, LANES - N_HEADS))).astype(BF)
        bias_lanes = jnp.pad(b_forget[l], (0, LANES - N_HEADS)).reshape(1, LANES)

        z, fa, zd4, zd16 = _in_proj(x2d, row(g_pre_mix[l]), w_main, w_f, freq_lanes,
                                    batch=batch, seq=seq, tm=DIL_CHUNK)
        gq, gk = _forget_scan(fa, bias_lanes, batch=batch, seq=seq, chunk=512)
        ya = _fox_attention(z, gq, gk, batch=batch, seq=seq, tq=512)
        yb = _dilated_attention(z, zd4, zd16, batch=batch, seq=seq)
        x2d = _mix(ya, yb, z, x2d, w_o_fox[l].astype(BF), w_o_dil[l].astype(BF),
                   w_out[l].astype(BF), row(g_post_mix[l]), tm=512)
        x2d = _ffn(x2d, row(g_pre_ffn[l]), w_up[l].astype(BF), conv_w[l],
                   row(conv_b[l]), w_down[l].astype(BF), row(g_post_ffn[l]),
                   seq=seq, tm=1024, tf=256)
    return x2d.reshape(batch, seq, D_MODEL)
```

```python
import functools
import math

import numpy as np
import jax
import jax.numpy as jnp
from jax import lax
from jax.experimental import pallas as pl
from jax.experimental.pallas import tpu as pltpu

D_MODEL = 1024
HEAD_DIM = 64
N_HEADS = 8
ATT_WIDTH = N_HEADS * HEAD_DIM
N_HEAD_PAIRS = N_HEADS // 2
DIL_PATTERNS = ((128, 1), (512, 4), (2048, 16))
ROPE_DIM = HEAD_DIM // 4
ROPE_HALF = ROPE_DIM // 2
ROPE_THETA = 500000.0
D_FF = 2816
CONV_WIDTH = 3
RMS_EPS = 1e-6
NEG_INF = -1e30
Q_SCALE = 1.0 / math.sqrt(HEAD_DIM)
LOG2_E = math.log2(math.e)

LANES = 128
BF16_ROWS = 16
Z_WIDTH = 3 * ATT_WIDTH * 2 + 2 * D_MODEL
Z_BLK = 512
DIL_SECTION = 3
DIL_CHUNK = 1024
N_PIECES = 3
VMEM_LIMIT = 56 * 1024 * 1024

BF = jnp.bfloat16
F32 = jnp.float32


def _cparams(sem, flags=None):
    return pltpu.CompilerParams(dimension_semantics=sem, vmem_limit_bytes=VMEM_LIMIT,
                                flags=flags)


def _rms(xf, g):
    inv = lax.rsqrt(jnp.mean(xf * xf, axis=-1, keepdims=True) + RMS_EPS)
    return xf * inv * g


def _split3(x):
    hi = x.astype(BF)
    r1 = x - hi.astype(F32)
    mid = r1.astype(BF)
    lo = (r1 - mid.astype(F32)).astype(BF)
    return hi, mid, lo


def _dot(a, b):
    return jnp.dot(a, b, preferred_element_type=F32)


def _dot_nt(a, b):
    return lax.dot_general(a, b, (((1,), (1,)), ((), ())), preferred_element_type=F32)


def _in_proj_kernel(x_ref, g_ref, w_ref, wf_ref, freq_ref, z_ref, fa_ref, zd4_ref, zd16_ref,
                    h_scr, cos_scr, sneg_scr, spos_scr, stage_scr, stage4_scr, *, tm):
    p = pl.program_id(0)
    b = pl.program_id(1)
    j = pl.program_id(2)
    zd_refs = {4: zd4_ref, 16: zd16_ref}

    def emit_dilated(val):
        z_ref[...] = val.astype(BF)
        for s in range(Z_BLK // LANES):
            lanes = slice(s * LANES, (s + 1) * LANES)
            stage_scr[s] = val[:, lanes]
            seg4 = tm // 4
            for c in range(4):
                part = stage_scr[s, pl.ds(c, seg4, stride=4), :]
                stage4_scr[s, c * seg4:(c + 1) * seg4, :] = part
                zd_refs[4][c * seg4:(c + 1) * seg4, lanes] = part.astype(BF)
            seg16 = tm // 16
            for jj in range(16):
                c, a = jj % 4, jj // 4
                part = stage4_scr[s, pl.ds(c * seg4 + a, seg16, stride=4), :]
                zd_refs[16][jj * seg16:(jj + 1) * seg16, lanes] = part.astype(BF)

    @pl.when((b == 0) & (j == 0))
    def _tables():
        pos = (p * tm + lax.broadcasted_iota(jnp.int32, (tm, LANES), 0)).astype(F32)
        lane = lax.broadcasted_iota(jnp.int32, (tm, LANES), 1)
        c = lane % HEAD_DIM
        ang = pos * freq_ref[...]
        cs = jnp.cos(ang)
        sn = jnp.sin(ang)
        cos_scr[...] = jnp.where(c < ROPE_DIM, cs, 1.0)
        sneg_scr[...] = jnp.where(c < ROPE_HALF, -sn, 0.0)
        spos_scr[...] = jnp.where((c >= ROPE_HALF) & (c < ROPE_DIM), sn, 0.0)

    @pl.when(j == 0)
    def _norm():
        h = _rms(x_ref[...], g_ref[...]).astype(BF)
        h_scr[...] = h
        fa_ref[...] = _dot(h, wf_ref[...])

    acc = _dot(h_scr[...], w_ref[...])

    def rope(a):
        outs = []
        for s in range(Z_BLK // LANES):
            t = a[:, s * LANES:(s + 1) * LANES]
            up = pltpu.roll(t, LANES - ROPE_HALF, 1)
            dn = pltpu.roll(t, ROPE_HALF, 1)
            outs.append(t * cos_scr[...] + up * sneg_scr[...] + dn * spos_scr[...])
        return jnp.concatenate(outs, axis=1)

    is_q_fox = j == 0
    is_q_dil = j == DIL_SECTION
    is_k_dil = j == DIL_SECTION + 1
    is_v_dil = j == DIL_SECTION + 2

    @pl.when(is_q_fox)
    def _():
        z_ref[...] = (acc * (Q_SCALE * LOG2_E)).astype(BF)

    @pl.when(is_q_dil)
    def _():
        emit_dilated(rope(acc) * (Q_SCALE * LOG2_E))

    @pl.when(is_k_dil)
    def _():
        emit_dilated(rope(acc))

    @pl.when(is_v_dil)
    def _():
        emit_dilated(acc)

    @pl.when(jnp.logical_not(is_q_fox | is_q_dil | is_k_dil | is_v_dil))
    def _():
        z_ref[...] = acc.astype(BF)


def _in_proj(x2d, g, w_main, w_f, freq_lanes, *, batch, seq, tm):
    n_p = seq // tm
    n_j = Z_WIDTH // Z_BLK
    tokens = batch * seq
    row = lambda p, b, j: (b * n_p + p, 0)
    dil_blk = pl.BlockSpec(
        (tm, Z_BLK), lambda p, b, j: (b * n_p + p, jnp.clip(j - DIL_SECTION, 0, 2)))
    dil_shape = jax.ShapeDtypeStruct((tokens, 3 * ATT_WIDTH), BF)
    return pl.pallas_call(
        functools.partial(_in_proj_kernel, tm=tm),
        grid=(n_p, batch, n_j),
        in_specs=[
            pl.BlockSpec((tm, D_MODEL), row),
            pl.BlockSpec((1, D_MODEL), lambda p, b, j: (0, 0)),
            pl.BlockSpec((D_MODEL, Z_BLK), lambda p, b, j: (0, j)),
            pl.BlockSpec((D_MODEL, LANES), lambda p, b, j: (0, 0)),
            pl.BlockSpec((1, LANES), lambda p, b, j: (0, 0)),
        ],
        out_specs=[
            pl.BlockSpec((tm, Z_BLK), lambda p, b, j: (b * n_p + p, j)),
            pl.BlockSpec((tm, LANES), row),
            dil_blk,
            dil_blk,
        ],
        out_shape=[
            jax.ShapeDtypeStruct((tokens, Z_WIDTH), BF),
            jax.ShapeDtypeStruct((tokens, LANES), F32),
            dil_shape,
            dil_shape,
        ],
        scratch_shapes=[
            pltpu.VMEM((tm, D_MODEL), BF),
            pltpu.VMEM((tm, LANES), F32),
            pltpu.VMEM((tm, LANES), F32),
            pltpu.VMEM((tm, LANES), F32),
            pltpu.VMEM((Z_BLK // LANES, tm, LANES), F32),
            pltpu.VMEM((Z_BLK // LANES, tm, LANES), F32),
        ],
        compiler_params=_cparams(("arbitrary", "arbitrary", "arbitrary")),
        name="in_proj",
    )(x2d, g, w_main, w_f, freq_lanes)


def _extras_base(head):
    return (HEAD_DIM if head % 2 == 0 else 0) + 2 * N_PIECES * (head // 2)


def _forget_scan_kernel(fa_ref, bias_ref, tri_ref, pq_ref, pk_ref, oq_ref, ok_ref,
                        gq_ref, gk_ref, carry_scr):
    c = pl.program_id(1)

    @pl.when(c == 0)
    def _():
        carry_scr[...] = jnp.zeros_like(carry_scr)

    t = fa_ref[...] + bias_ref[...]
    log_f = jnp.minimum(t, 0.0) - jnp.log1p(jnp.exp(-jnp.abs(t)))
    tri = tri_ref[...]
    run = carry_scr[...]
    for piece in _split3(log_f):
        run = run + _dot(tri, piece)
    rows = run.shape[0]
    carry_scr[...] = run[rows - 1:rows, :]
    gq = oq_ref[...].astype(F32)
    gk = ok_ref[...].astype(F32)
    for i, piece in enumerate(_split3(run * LOG2_E)):
        gq = gq + _dot(piece, pq_ref[i])
        gk = gk - _dot(piece, pk_ref[i])
    gq_ref[0] = gq.astype(BF)
    gk_ref[0] = gk.astype(BF)


def _forget_scan_constants(chunk):
    tri = np.tril(np.ones((chunk, chunk), np.float32))
    pq = np.zeros((N_PIECES, LANES, LANES), np.float32)
    pk = np.zeros((N_PIECES, LANES, LANES), np.float32)
    oq = np.zeros((1, LANES), np.float32)
    ok = np.zeros((1, LANES), np.float32)
    for h in range(N_HEADS):
        base = _extras_base(h)
        for i in range(N_PIECES):
            pq[i, h, base + i] = 1.0
            oq[0, base + N_PIECES + i] = 1.0
            ok[0, base + i] = 1.0
            pk[i, h, base + N_PIECES + i] = 1.0
    as_bf = lambda a: jnp.asarray(a, BF)
    return as_bf(tri), as_bf(pq), as_bf(pk), as_bf(oq), as_bf(ok)


def _forget_scan(fa, bias_lanes, *, batch, seq, chunk):
    n_c = seq // chunk
    tri, pq, pk, oq, ok = _forget_scan_constants(chunk)
    const2 = lambda b, c: (0, 0)
    const3 = lambda b, c: (0, 0, 0)
    return pl.pallas_call(
        _forget_scan_kernel,
        grid=(batch, n_c),
        in_specs=[
            pl.BlockSpec((chunk, LANES), lambda b, c: (b * n_c + c, 0)),
            pl.BlockSpec((1, LANES), const2),
            pl.BlockSpec((chunk, chunk), const2),
            pl.BlockSpec((N_PIECES, LANES, LANES), const3),
            pl.BlockSpec((N_PIECES, LANES, LANES), const3),
            pl.BlockSpec((1, LANES), const2),
            pl.BlockSpec((1, LANES), const2),
        ],
        out_specs=[
            pl.BlockSpec((1, chunk, LANES), lambda b, c: (b, c, 0)),
            pl.BlockSpec((1, chunk, LANES), lambda b, c: (b, c, 0)),
        ],
        out_shape=[
            jax.ShapeDtypeStruct((batch, seq, LANES), BF),
            jax.ShapeDtypeStruct((batch, seq, LANES), BF),
        ],
        scratch_shapes=[pltpu.VMEM((1, LANES), F32)],
        compiler_params=_cparams(("arbitrary", "arbitrary")),
        name="forget_scan",
    )(fa, bias_lanes, tri, pq, pk, oq, ok)


def _with_bias_lanes(slab, extras, hp, parity):
    lane = lax.broadcasted_iota(jnp.int32, slab.shape, 1)
    own = (lane < HEAD_DIM) if parity == 0 else (lane >= HEAD_DIM)
    base = (HEAD_DIM if parity == 0 else 0) + 2 * N_PIECES * hp
    in_extras = (lane >= base) & (lane < base + 2 * N_PIECES)
    return jnp.where(own, slab, jnp.where(in_extras, extras, jnp.zeros_like(extras)))


def _fox_kernel(q_ref, k_ref, v_ref, gq_ref, gk_ref, o_ref,
                ka_scr, kb_scr, vta_scr, vtb_scr, bias_scr, s_scr, acc_scr, ot_scr, *, tq):
    hp = pl.program_id(1)
    qi = pl.program_id(2)
    k_scrs = (ka_scr, kb_scr)
    vt_scrs = (vta_scr, vtb_scr)
    seq = k_ref.shape[0]

    @pl.when(qi == 0)
    def _prep():
        k2 = k_ref[...]
        gk = gk_ref[0]
        ka_scr[...] = _with_bias_lanes(k2, gk, hp, 0)
        kb_scr[...] = _with_bias_lanes(k2, gk, hp, 1)
        vt = v_ref[...].astype(F32).T.astype(BF)
        ones = jnp.ones((BF16_ROWS, seq), BF)
        for head, vt_scr in enumerate(vt_scrs):
            vt_scr[0:HEAD_DIM, :] = vt[head * HEAD_DIM:(head + 1) * HEAD_DIM, :]
            vt_scr[HEAD_DIM:, :] = ones
        row = lax.broadcasted_iota(jnp.int32, (tq, tq), 0)
        col = lax.broadcasted_iota(jnp.int32, (tq, tq), 1)
        bias_scr[...] = jnp.where(row <= col, 0.0, NEG_INF)

    q2 = q_ref[...]
    gq = gq_ref[0]
    qps = [_with_bias_lanes(q2, gq, hp, head) for head in range(2)]

    def scores(head, start, masked):
        kt = k_scrs[head][pl.ds(start, tq), :]
        st = _dot_nt(kt, qps[head])
        if masked:
            st = st + bias_scr[...]
        s_scr[head] = st
        return jnp.max(st, axis=0, keepdims=True)

    def softmax_pv(head, st, tile, tile_max, m):
        start = pl.multiple_of(tile * tq, tq)
        m_new = jnp.maximum(m, tile_max)
        alpha = jnp.exp2(m - m_new)
        pt = jnp.exp2(st - m_new).astype(BF)
        vt = vt_scrs[head][:, pl.ds(start, tq)]
        acc_scr[head] = alpha * acc_scr[head] + _dot(vt, pt)
        return m_new

    acc_scr[...] = jnp.zeros_like(acc_scr)
    diag = pl.multiple_of(qi * tq, tq)
    first_max = [scores(head, diag, True) for head in range(2)]
    m_init = jnp.full((1, tq), NEG_INF, F32)

    def body(j, carry):
        max_a, max_b, m_a, m_b, prev_tile = carry
        prev = [s_scr[head] for head in range(2)]
        start = pl.multiple_of(j * tq, tq)
        next_max = [scores(head, start, False) for head in range(2)]
        m_a = softmax_pv(0, prev[0], prev_tile, max_a, m_a)
        m_b = softmax_pv(1, prev[1], prev_tile, max_b, m_b)
        return next_max[0], next_max[1], m_a, m_b, j

    max_a, max_b, m_a, m_b, prev_tile = lax.fori_loop(
        0, qi, body, (first_max[0], first_max[1], m_init, m_init, qi))
    softmax_pv(0, s_scr[0], prev_tile, max_a, m_a)
    softmax_pv(1, s_scr[1], prev_tile, max_b, m_b)

    for head in range(2):
        acc = acc_scr[head]
        ot_scr[head * HEAD_DIM:(head + 1) * HEAD_DIM, :] = (
            acc[0:HEAD_DIM, :] / acc[HEAD_DIM:HEAD_DIM + 1, :])
    o_ref[...] = ot_scr[...].T.astype(BF)


def _fox_attention(z, gq, gk, *, batch, seq, tq):
    n_q = seq // tq
    tokens = batch * seq
    return pl.pallas_call(
        functools.partial(_fox_kernel, tq=tq),
        grid=(batch, N_HEAD_PAIRS, n_q),
        in_specs=[
            pl.BlockSpec((tq, LANES), lambda b, hp, qi: (b * n_q + qi, hp)),
            pl.BlockSpec((seq, LANES), lambda b, hp, qi: (b, N_HEAD_PAIRS + hp)),
            pl.BlockSpec((seq, LANES), lambda b, hp, qi: (b, 2 * N_HEAD_PAIRS + hp)),
            pl.BlockSpec((1, tq, LANES), lambda b, hp, qi: (b, qi, 0)),
            pl.BlockSpec((1, seq, LANES), lambda b, hp, qi: (b, 0, 0)),
        ],
        out_specs=pl.BlockSpec((tq, LANES), lambda b, hp, qi: (b * n_q + qi, hp)),
        out_shape=jax.ShapeDtypeStruct((tokens, ATT_WIDTH), BF),
        scratch_shapes=[
            pltpu.VMEM((seq, LANES), BF),
            pltpu.VMEM((seq, LANES), BF),
            pltpu.VMEM((HEAD_DIM + BF16_ROWS, seq), BF),
            pltpu.VMEM((HEAD_DIM + BF16_ROWS, seq), BF),
            pltpu.VMEM((tq, tq), F32),
            pltpu.VMEM((2, tq, tq), F32),
            pltpu.VMEM((2, HEAD_DIM + BF16_ROWS, tq), F32),
            pltpu.VMEM((LANES, tq), F32),
        ],
        compiler_params=_cparams(("arbitrary", "arbitrary", "arbitrary")),
        name="fox_attn",
    )(z, z, z, gq, gk)


def _log2(n):
    assert n > 0 and n & (n - 1) == 0, n
    return n.bit_length() - 1


def _dilated_kernel(q1, k1, v1, q4, k4, v4, q16, k16, v16, o_ref,
                    num1, num4, num16, den1, den4, den16, max1, max4, max16, bias_scr,
                    *, seq, blk, chunk, group, merge_rows):
    sources = ((q1, k1, v1), (q4, k4, v4), (q16, k16, v16))
    num_scrs = (num1, num4, num16)
    den_scrs = (den1, den4, den16)
    max_scrs = (max1, max4, max16)
    n_blocks = seq // blk

    low = lax.broadcasted_iota(jnp.int32, (blk, LANES), 1) < HEAD_DIM
    qrow = lax.broadcasted_iota(jnp.int32, (2 * blk, 2 * blk), 0) % blk
    kcol = lax.broadcasted_iota(jnp.int32, (2 * blk, 2 * blk), 1)
    dist = qrow + blk - kcol
    band = (dist >= 0) & (dist <= blk)
    bias_scr[0] = jnp.where(band & (kcol >= blk), 0.0, NEG_INF)
    bias_scr[1] = jnp.where(band, 0.0, NEG_INF)
    ones = jnp.ones((2 * blk, LANES), BF)

    for idx, (window, dilation) in enumerate(DIL_PATTERNS):
        assert window // dilation == blk
        q_ref, k_ref, v_ref = sources[idx]
        num_scr, den_scr, max_scr = num_scrs[idx], den_scrs[idx], max_scrs[idx]
        sub_shift = _log2(seq // dilation // blk)
        seg_rows = chunk // dilation
        seg = min(blk, seg_rows)

        def load(ref, j, l0):
            parts = []
            for s in range(blk // seg):
                l = l0 + s * seg
                p = lax.shift_right_logical(l, _log2(seg_rows))
                i = l & (seg_rows - 1)
                start = pl.multiple_of(p * chunk + j * seg_rows + i, seg)
                parts.append(ref[pl.ds(start, seg), :])
            return parts[0] if len(parts) == 1 else jnp.concatenate(parts, axis=0)

        def one_block(g):
            j = lax.shift_right_logical(g, sub_shift)
            gs = g & ((1 << sub_shift) - 1)
            l0 = gs * blk
            lp = jnp.maximum(l0 - blk, 0)
            q2 = load(q_ref, j, l0)
            zero = jnp.zeros_like(q2)
            qq = jnp.concatenate([jnp.where(low, q2, zero), jnp.where(low, zero, q2)], axis=0)
            kwin = jnp.concatenate([load(k_ref, j, lp), load(k_ref, j, l0)], axis=0)
            vwin = jnp.concatenate([load(v_ref, j, lp), load(v_ref, j, l0)], axis=0)
            s = _dot_nt(qq, kwin) + bias_scr[jnp.minimum(gs, 1)]
            m = jnp.max(s, axis=1, keepdims=True)
            p = jnp.exp2(s - m).astype(BF)
            pv = _dot(p, jnp.concatenate([vwin, ones], axis=1))
            if dilation == 1:
                dst = pl.ds(pl.multiple_of(l0, blk), blk)
            else:
                dst = pl.ds(l0 * dilation + j, blk, stride=dilation)
            num_scr[dst, :] = jnp.where(low, pv[:blk, :LANES], pv[blk:, :LANES])
            den_scr[dst, :] = jnp.where(low, pv[:blk, LANES:], pv[blk:, LANES:])
            max_scr[dst, :] = jnp.where(low, jnp.broadcast_to(m[:blk], (blk, LANES)),
                                        jnp.broadcast_to(m[blk:], (blk, LANES)))

        def blocks(it, carry):
            for u in range(group):
                one_block(it * group + u)
            return carry

        lax.fori_loop(0, n_blocks // group, blocks, 0)

    def merge(c, carry):
        rows = pl.ds(pl.multiple_of(c * merge_rows, merge_rows), merge_rows)
        maxes = [max_scr[rows, :] for max_scr in max_scrs]
        top = functools.reduce(jnp.maximum, maxes)
        weights = [jnp.exp2(m - top) for m in maxes]
        num = sum(w * num_scr[rows, :] for w, num_scr in zip(weights, num_scrs))
        den = sum(w * den_scr[rows, :] for w, den_scr in zip(weights, den_scrs))
        o_ref[rows, :] = (num / den).astype(BF)
        return carry

    lax.fori_loop(0, seq // merge_rows, merge, 0)


def _dilated_attention(z, zd4, zd16, *, batch, seq):
    blk = DIL_PATTERNS[0][0] // DIL_PATTERNS[0][1]
    tokens = batch * seq
    per_section = ATT_WIDTH // LANES
    col = lambda section: pl.BlockSpec(
        (seq, LANES), lambda b, hp: (b, section * per_section + hp))
    f32_buf = pltpu.VMEM((seq, LANES), F32)
    return pl.pallas_call(
        functools.partial(_dilated_kernel, seq=seq, blk=blk, chunk=DIL_CHUNK, group=4,
                          merge_rows=256),
        grid=(batch, N_HEAD_PAIRS),
        in_specs=[col(DIL_SECTION), col(DIL_SECTION + 1), col(DIL_SECTION + 2),
                  col(0), col(1), col(2), col(0), col(1), col(2)],
        out_specs=pl.BlockSpec((seq, LANES), lambda b, hp: (b, hp)),
        out_shape=jax.ShapeDtypeStruct((tokens, ATT_WIDTH), BF),
        scratch_shapes=[f32_buf] * 9 + [pltpu.VMEM((2, 2 * blk, 2 * blk), F32)],
        compiler_params=_cparams(("arbitrary", "arbitrary")),
        name="dilated_attn",
    )(z, z, z, zd4, zd4, zd4, zd16, zd16, zd16)


def _mix_kernel(ya_ref, yb_ref, ga_ref, gb_ref, x_ref, woa_ref, wob_ref, wout_ref,
                g_ref, o_ref):
    pa = _dot(ya_ref[...], woa_ref[...])
    pb = _dot(yb_ref[...], wob_ref[...])
    mixed = (jax.nn.sigmoid(ga_ref[...].astype(F32)) * pa
             + jax.nn.sigmoid(gb_ref[...].astype(F32)) * pb)
    y = _dot(mixed.astype(BF), wout_ref[...])
    o_ref[...] = x_ref[...] + _rms(y, g_ref[...])


def _mix(ya, yb, z, x2d, woa, wob, wout, g, *, tm):
    tokens = x2d.shape[0]
    gate_blk = lambda off: pl.BlockSpec((tm, D_MODEL), lambda i: (i, off))
    const = lambda i: (0, 0)
    return pl.pallas_call(
        _mix_kernel,
        grid=(tokens // tm,),
        in_specs=[
            pl.BlockSpec((tm, ATT_WIDTH), lambda i: (i, 0)),
            pl.BlockSpec((tm, ATT_WIDTH), lambda i: (i, 0)),
            gate_blk(3),
            gate_blk(4),
            pl.BlockSpec((tm, D_MODEL), lambda i: (i, 0)),
            pl.BlockSpec((ATT_WIDTH, D_MODEL), const),
            pl.BlockSpec((ATT_WIDTH, D_MODEL), const),
            pl.BlockSpec((D_MODEL, D_MODEL), const),
            pl.BlockSpec((1, D_MODEL), const),
        ],
        out_specs=pl.BlockSpec((tm, D_MODEL), lambda i: (i, 0)),
        out_shape=jax.ShapeDtypeStruct((tokens, D_MODEL), F32),
        compiler_params=_cparams(("arbitrary",)),
        name="mix",
    )(ya, yb, z, z, x2d, woa, wob, wout, g)


def _ffn_kernel(x_ref, halo_ref, gpre_ref, wa_ref, wb_ref, cwa_ref, cwb_ref,
                cba_ref, cbb_ref, wd_ref, gpost_ref, o_ref,
                h_scr, ua0_scr, ub0_scr, ua1_scr, ub1_scr, acc_scr,
                *, tm, tiles_per_seq, n_tiles, n_chunks):
    i = pl.program_id(0)
    f = pl.program_id(1)
    halo = BF16_ROWS
    rows = tm // n_chunks
    u_sets = ((ua0_scr, ub0_scr), (ua1_scr, ub1_scr))

    def up_proj(dst, lo, hi):
        h = h_scr[lo:hi, :]
        dst[0][lo:hi, :] = _dot(h, wa_ref[...])
        dst[1][lo:hi, :] = _dot(h, wb_ref[...])

    def conv(u_scr, cw_ref, cb_ref, post_scale, first, n):
        out = cb_ref[...] * post_scale
        for tap in range(CONV_WIDTH):
            lo = halo + first - (CONV_WIDTH - 1) + tap
            out = out + (cw_ref[tap:tap + 1, :] * post_scale) * u_scr[lo:lo + n, :]
        return out

    def hidden(src, first, n):
        a = conv(src[0], cwa_ref, cba_ref, 1.0, first, n)
        half_b = conv(src[1], cwb_ref, cbb_ref, 0.5, first, n)
        c0 = math.sqrt(2.0 / math.pi)
        inner = a * (c0 + (c0 * 0.044715) * (a * a))
        return ((a * half_b) * (1.0 + jnp.tanh(inner))).astype(BF)

    @pl.when(f == 0)
    def _first():
        g = gpre_ref[...]
        hh = _rms(halo_ref[...], g)
        hh = jnp.where(i % tiles_per_seq == 0, jnp.zeros_like(hh), hh)
        h_scr[0:halo, :] = hh.astype(BF)
        h_scr[halo:, :] = _rms(x_ref[...], g).astype(BF)
        acc_scr[...] = jnp.zeros_like(acc_scr)
        up_proj(u_sets[0], 0, tm + halo)

    for parity in range(2):
        @pl.when((f > 0) & (f < n_tiles) & (f % 2 == parity))
        def _steady(parity=parity):
            src, dst = u_sets[1 - parity], u_sets[parity]
            for c in range(n_chunks):
                lo = 0 if c == 0 else halo + c * rows
                up_proj(dst, lo, halo + (c + 1) * rows)
                out_rows = slice(c * rows, (c + 1) * rows)
                acc_scr[out_rows, :] += _dot(hidden(src, c * rows, rows), wd_ref[...])

    @pl.when(f == n_tiles)
    def _last():
        src = u_sets[(n_tiles - 1) % 2]
        y = acc_scr[...] + _dot(hidden(src, 0, tm), wd_ref[...])
        o_ref[...] = x_ref[...] + _rms(y, gpost_ref[...])


def _ffn(x1, g_pre, w_up, conv_w, conv_b, w_down, g_post, *, seq, tm, tf):
    tokens = x1.shape[0]
    n_f = D_FF // tf
    halo = BF16_ROWS
    per = tm // halo
    const = lambda i, f: (0, 0)
    cur = lambda f: jnp.minimum(f, n_f - 1)
    prev = lambda f: jnp.maximum(f - 1, 0)
    up_a = lambda i, f: (0, cur(f))
    up_b = lambda i, f: (0, n_f + cur(f))
    conv_a = lambda i, f: (0, prev(f))
    conv_b_idx = lambda i, f: (0, n_f + prev(f))
    return pl.pallas_call(
        functools.partial(_ffn_kernel, tm=tm, tiles_per_seq=seq // tm, n_tiles=n_f,
                          n_chunks=4),
        grid=(tokens // tm, n_f + 1),
        in_specs=[
            pl.BlockSpec((tm, D_MODEL), lambda i, f: (i, 0)),
            pl.BlockSpec((halo, D_MODEL), lambda i, f: (jnp.maximum(i * per - 1, 0), 0)),
            pl.BlockSpec((1, D_MODEL), const),
            pl.BlockSpec((D_MODEL, tf), up_a),
            pl.BlockSpec((D_MODEL, tf), up_b),
            pl.BlockSpec((CONV_WIDTH, tf), conv_a),
            pl.BlockSpec((CONV_WIDTH, tf), conv_b_idx),
            pl.BlockSpec((1, tf), conv_a),
            pl.BlockSpec((1, tf), conv_b_idx),
            pl.BlockSpec((tf, D_MODEL), lambda i, f: (prev(f), 0)),
            pl.BlockSpec((1, D_MODEL), const),
        ],
        out_specs=pl.BlockSpec((tm, D_MODEL), lambda i, f: (i, 0)),
        out_shape=jax.ShapeDtypeStruct((tokens, D_MODEL), F32),
        scratch_shapes=[
            pltpu.VMEM((tm + halo, D_MODEL), BF),
            pltpu.VMEM((tm + halo, tf), F32),
            pltpu.VMEM((tm + halo, tf), F32),
            pltpu.VMEM((tm + halo, tf), F32),
            pltpu.VMEM((tm + halo, tf), F32),
            pltpu.VMEM((tm, D_MODEL), F32),
        ],
        compiler_params=_cparams(("arbitrary", "arbitrary")),
        name="ffn",
    )(x1, x1, g_pre, w_up, w_up, conv_w, conv_w, conv_b, conv_b, w_down, g_post)


def _rope_freq_lanes():
    inv_freq = ROPE_THETA ** (-jnp.arange(ROPE_HALF, dtype=F32) * 2.0 / ROPE_DIM)
    lane = np.arange(LANES) % HEAD_DIM
    return inv_freq[lane % ROPE_HALF].reshape(1, LANES)


def kernel(x, g_pre_mix, w_in, b_forget, w_o_fox, w_o_dil, w_out, g_post_mix,
           g_pre_ffn, w_up, conv_w, conv_b, w_down, g_post_ffn):
    batch, seq, d_model = x.shape
    assert d_model == D_MODEL and seq % 1024 == 0
    depth = w_in.shape[0]
    fox_end = 3 * ATT_WIDTH
    freq_lanes = _rope_freq_lanes()
    x2d = x.reshape(batch * seq, D_MODEL)
    row = lambda v: v.reshape(1, -1)
    for l in range(depth):
        w_main = jnp.concatenate(
            [w_in[l][:, :fox_end], w_in[l][:, fox_end + N_HEADS:]], axis=1).astype(BF)
        w_f = jnp.pad(w_in[l][:, fox_end:fox_end + N_HEADS],
                      ((0, 0), (0, LANES - N_HEADS))).astype(BF)
        bias_lanes = jnp.pad(b_forget[l], (0, LANES - N_HEADS)).reshape(1, LANES)

        z, fa, zd4, zd16 = _in_proj(x2d, row(g_pre_mix[l]), w_main, w_f, freq_lanes,
                                    batch=batch, seq=seq, tm=DIL_CHUNK)
        gq, gk = _forget_scan(fa, bias_lanes, batch=batch, seq=seq, chunk=512)
        ya = _fox_attention(z, gq, gk, batch=batch, seq=seq, tq=512)
        yb = _dilated_attention(z, zd4, zd16, batch=batch, seq=seq)
        x2d = _mix(ya, yb, z, x2d, w_o_fox[l].astype(BF), w_o_dil[l].astype(BF),
                   w_out[l].astype(BF), row(g_post_mix[l]), tm=512)
        x2d = _ffn(x2d, row(g_pre_ffn[l]), w_up[l].astype(BF), conv_w[l],
                   row(conv_b[l]), w_down[l].astype(BF), row(g_post_ffn[l]),
                   seq=seq, tm=1024, tf=256)
    return x2d.reshape(batch, seq, D_MODEL)
```

```python
import functools
import math

import numpy as np
import jax
import jax.numpy as jnp
from jax import lax
from jax.experimental import pallas as pl
from jax.experimental.pallas import tpu as pltpu

D_MODEL = 1024
HEAD_DIM = 64
N_HEADS = 8
ATT_WIDTH = N_HEADS * HEAD_DIM
N_HEAD_PAIRS = N_HEADS // 2
DIL_PATTERNS = ((128, 1), (512, 4), (2048, 16))
ROPE_DIM = HEAD_DIM // 4
ROPE_HALF = ROPE_DIM // 2
ROPE_THETA = 500000.0
D_FF = 2816
CONV_WIDTH = 3
RMS_EPS = 1e-6
NEG_INF = -1e30
Q_SCALE = 1.0 / math.sqrt(HEAD_DIM)
LOG2_E = math.log2(math.e)

LANES = 128
BF16_ROWS = 16
Z_WIDTH = 3 * ATT_WIDTH * 2 + 2 * D_MODEL
Z_BLK = 512
DIL_SECTION = 3
DIL_CHUNK = 1024
N_PIECES = 3
VMEM_LIMIT = 56 * 1024 * 1024

BF = jnp.bfloat16
F32 = jnp.float32


def _cparams(sem, flags=None):
    return pltpu.CompilerParams(dimension_semantics=sem, vmem_limit_bytes=VMEM_LIMIT,
                                flags=flags)


def _rms(xf, g):
    inv = lax.rsqrt(jnp.mean(xf * xf, axis=-1, keepdims=True) + RMS_EPS)
    return xf * inv * g


def _split3(x):
    hi = x.astype(BF)
    r1 = x - hi.astype(F32)
    mid = r1.astype(BF)
    lo = (r1 - mid.astype(F32)).astype(BF)
    return hi, mid, lo


def _dot(a, b):
    return jnp.dot(a, b, preferred_element_type=F32)


def _dot_nt(a, b):
    return lax.dot_general(a, b, (((1,), (1,)), ((), ())), preferred_element_type=F32)


def _in_proj_kernel(x_ref, g_ref, w_ref, wf_ref, freq_ref, z_ref, fa_ref, zd4_ref, zd16_ref,
                    h_scr, cos_scr, sneg_scr, spos_scr, stage_scr, stage4_scr, *, tm):
    p = pl.program_id(0)
    b = pl.program_id(1)
    j = pl.program_id(2)
    zd_refs = {4: zd4_ref, 16: zd16_ref}

    def emit_dilated(val):
        z_ref[...] = val.astype(BF)
        for s in range(Z_BLK // LANES):
            lanes = slice(s * LANES, (s + 1) * LANES)
            stage_scr[s] = val[:, lanes]
            seg4 = tm // 4
            for c in range(4):
                part = stage_scr[s, pl.ds(c, seg4, stride=4), :]
                stage4_scr[s, c * seg4:(c + 1) * seg4, :] = part
                zd_refs[4][c * seg4:(c + 1) * seg4, lanes] = part.astype(BF)
            seg16 = tm // 16
            for jj in range(16):
                c, a = jj % 4, jj // 4
                part = stage4_scr[s, pl.ds(c * seg4 + a, seg16, stride=4), :]
                zd_refs[16][jj * seg16:(jj + 1) * seg16, lanes] = part.astype(BF)

    @pl.when((b == 0) & (j == 0))
    def _tables():
        pos = (p * tm + lax.broadcasted_iota(jnp.int32, (tm, LANES), 0)).astype(F32)
        lane = lax.broadcasted_iota(jnp.int32, (tm, LANES), 1)
        c = lane % HEAD_DIM
        ang = pos * freq_ref[...]
        cs = jnp.cos(ang)
        sn = jnp.sin(ang)
        cos_scr[...] = jnp.where(c < ROPE_DIM, cs, 1.0)
        sneg_scr[...] = jnp.where(c < ROPE_HALF, -sn, 0.0)
        spos_scr[...] = jnp.where((c >= ROPE_HALF) & (c < ROPE_DIM), sn, 0.0)

    @pl.when(j == 0)
    def _norm():
        h = _rms(x_ref[...], g_ref[...]).astype(BF)
        h_scr[...] = h
        fa_ref[...] = _dot(h, wf_ref[...])

    acc = _dot(h_scr[...], w_ref[...])

    def rope(a):
        outs = []
        for s in range(Z_BLK // LANES):
            t = a[:, s * LANES:(s + 1) * LANES]
            up = pltpu.roll(t, LANES - ROPE_HALF, 1)
            dn = pltpu.roll(t, ROPE_HALF, 1)
            outs.append(t * cos_scr[...] + up * sneg_scr[...] + dn * spos_scr[...])
        return jnp.concatenate(outs, axis=1)

    is_q_fox = j == 0
    is_q_dil = j == DIL_SECTION
    is_k_dil = j == DIL_SECTION + 1
    is_v_dil = j == DIL_SECTION + 2

    @pl.when(is_q_fox)
    def _():
        z_ref[...] = (acc * (Q_SCALE * LOG2_E)).astype(BF)

    @pl.when(is_q_dil)
    def _():
        emit_dilated(rope(acc) * (Q_SCALE * LOG2_E))

    @pl.when(is_k_dil)
    def _():
        emit_dilated(rope(acc))

    @pl.when(is_v_dil)
    def _():
        emit_dilated(acc)

    @pl.when(jnp.logical_not(is_q_fox | is_q_dil | is_k_dil | is_v_dil))
    def _():
        z_ref[...] = acc.astype(BF)


def _in_proj(x2d, g, w_main, w_f, freq_lanes, *, batch, seq, tm):
    n_p = seq // tm
    n_j = Z_WIDTH // Z_BLK
    tokens = batch * seq
    row = lambda p, b, j: (b * n_p + p, 0)
    dil_blk = pl.BlockSpec(
        (tm, Z_BLK), lambda p, b, j: (b * n_p + p, jnp.clip(j - DIL_SECTION, 0, 2)))
    dil_shape = jax.ShapeDtypeStruct((tokens, 3 * ATT_WIDTH), BF)
    return pl.pallas_call(
        functools.partial(_in_proj_kernel, tm=tm),
        grid=(n_p, batch, n_j),
        in_specs=[
            pl.BlockSpec((tm, D_MODEL), row),
            pl.BlockSpec((1, D_MODEL), lambda p, b, j: (0, 0)),
            pl.BlockSpec((D_MODEL, Z_BLK), lambda p, b, j: (0, j)),
            pl.BlockSpec((D_MODEL, LANES), lambda p, b, j: (0, 0)),
            pl.BlockSpec((1, LANES), lambda p, b, j: (0, 0)),
        ],
        out_specs=[
            pl.BlockSpec((tm, Z_BLK), lambda p, b, j: (b * n_p + p, j)),
            pl.BlockSpec((tm, LANES), row),
            dil_blk,
            dil_blk,
        ],
        out_shape=[
            jax.ShapeDtypeStruct((tokens, Z_WIDTH), BF),
            jax.ShapeDtypeStruct((tokens, LANES), F32),
            dil_shape,
            dil_shape,
        ],
        scratch_shapes=[
            pltpu.VMEM((tm, D_MODEL), BF),
            pltpu.VMEM((tm, LANES), F32),
            pltpu.VMEM((tm, LANES), F32),
            pltpu.VMEM((tm, LANES), F32),
            pltpu.VMEM((Z_BLK // LANES, tm, LANES), F32),
            pltpu.VMEM((Z_BLK // LANES, tm, LANES), F32),
        ],
        compiler_params=_cparams(("arbitrary", "arbitrary", "arbitrary")),
        name="in_proj",
    )(x2d, g, w_main, w_f, freq_lanes)


def _extras_base(head):
    return (HEAD_DIM if head % 2 == 0 else 0) + 2 * N_PIECES * (head // 2)


def _forget_scan_kernel(fa_ref, bias_ref, tri_ref, pq_ref, pk_ref, oq_ref, ok_ref,
                        gq_ref, gk_ref, carry_scr):
    c = pl.program_id(1)

    @pl.when(c == 0)
    def _():
        carry_scr[...] = jnp.zeros_like(carry_scr)

    t = fa_ref[...] + bias_ref[...]
    log_f = jnp.minimum(t, 0.0) - jnp.log1p(jnp.exp(-jnp.abs(t)))
    tri = tri_ref[...]
    run = carry_scr[...]
    for piece in _split3(log_f):
        run = run + _dot(tri, piece)
    rows = run.shape[0]
    carry_scr[...] = run[rows - 1:rows, :]
    gq = oq_ref[...].astype(F32)
    gk = ok_ref[...].astype(F32)
    for i, piece in enumerate(_split3(run * LOG2_E)):
        gq = gq + _dot(piece, pq_ref[i])
        gk = gk - _dot(piece, pk_ref[i])
    gq_ref[0] = gq.astype(BF)
    gk_ref[0] = gk.astype(BF)


def _forget_scan_constants(chunk):
    tri = np.tril(np.ones((chunk, chunk), np.float32))
    pq = np.zeros((N_PIECES, LANES, LANES), np.float32)
    pk = np.zeros((N_PIECES, LANES, LANES), np.float32)
    oq = np.zeros((1, LANES), np.float32)
    ok = np.zeros((1, LANES), np.float32)
    for h in range(N_HEADS):
        base = _extras_base(h)
        for i in range(N_PIECES):
            pq[i, h, base + i] = 1.0
            oq[0, base + N_PIECES + i] = 1.0
            ok[0, base + i] = 1.0
            pk[i, h, base + N_PIECES + i] = 1.0
    as_bf = lambda a: jnp.asarray(a, BF)
    return as_bf(tri), as_bf(pq), as_bf(pk), as_bf(oq), as_bf(ok)


def _forget_scan(fa, bias_lanes, *, batch, seq, chunk):
    n_c = seq // chunk
    tri, pq, pk, oq, ok = _forget_scan_constants(chunk)
    const2 = lambda b, c: (0, 0)
    const3 = lambda b, c: (0, 0, 0)
    return pl.pallas_call(
        _forget_scan_kernel,
        grid=(batch, n_c),
        in_specs=[
            pl.BlockSpec((chunk, LANES), lambda b, c: (b * n_c + c, 0)),
            pl.BlockSpec((1, LANES), const2),
            pl.BlockSpec((chunk, chunk), const2),
            pl.BlockSpec((N_PIECES, LANES, LANES), const3),
            pl.BlockSpec((N_PIECES, LANES, LANES), const3),
            pl.BlockSpec((1, LANES), const2),
            pl.BlockSpec((1, LANES), const2),
        ],
        out_specs=[
            pl.BlockSpec((1, chunk, LANES), lambda b, c: (b, c, 0)),
            pl.BlockSpec((1, chunk, LANES), lambda b, c: (b, c, 0)),
        ],
        out_shape=[
            jax.ShapeDtypeStruct((batch, seq, LANES), BF),
            jax.ShapeDtypeStruct((batch, seq, LANES), BF),
        ],
        scratch_shapes=[pltpu.VMEM((1, LANES), F32)],
        compiler_params=_cparams(("arbitrary", "arbitrary")),
        name="forget_scan",
    )(fa, bias_lanes, tri, pq, pk, oq, ok)


def _with_bias_lanes(slab, extras, hp, parity):
    lane = lax.broadcasted_iota(jnp.int32, slab.shape, 1)
    own = (lane < HEAD_DIM) if parity == 0 else (lane >= HEAD_DIM)
    base = (HEAD_DIM if parity == 0 else 0) + 2 * N_PIECES * hp
    in_extras = (lane >= base) & (lane < base + 2 * N_PIECES)
    return jnp.where(own, slab, jnp.where(in_extras, extras, jnp.zeros_like(extras)))


def _fox_kernel(q_ref, k_ref, v_ref, gq_ref, gk_ref, o_ref,
                ka_scr, kb_scr, vta_scr, vtb_scr, bias_scr, s_scr, acc_scr, ot_scr, *, tq):
    hp = pl.program_id(1)
    qi = pl.program_id(2)
    k_scrs = (ka_scr, kb_scr)
    vt_scrs = (vta_scr, vtb_scr)
    seq = k_ref.shape[0]

    @pl.when(qi == 0)
    def _prep():
        k2 = k_ref[...]
        gk = gk_ref[0]
        ka_scr[...] = _with_bias_lanes(k2, gk, hp, 0)
        kb_scr[...] = _with_bias_lanes(k2, gk, hp, 1)
        vt = v_ref[...].astype(F32).T.astype(BF)
        ones = jnp.ones((BF16_ROWS, seq), BF)
        for head, vt_scr in enumerate(vt_scrs):
            vt_scr[0:HEAD_DIM, :] = vt[head * HEAD_DIM:(head + 1) * HEAD_DIM, :]
            vt_scr[HEAD_DIM:, :] = ones
        row = lax.broadcasted_iota(jnp.int32, (tq, tq), 0)
        col = lax.broadcasted_iota(jnp.int32, (tq, tq), 1)
        bias_scr[...] = jnp.where(row <= col, 0.0, NEG_INF)

    q2 = q_ref[...]
    gq = gq_ref[0]
    qps = [_with_bias_lanes(q2, gq, hp, head) for head in range(2)]

    def scores(head, start, masked):
        kt = k_scrs[head][pl.ds(start, tq), :]
        st = _dot_nt(kt, qps[head])
        if masked:
            st = st + bias_scr[...]
        s_scr[head] = st
        return jnp.max(st, axis=0, keepdims=True)

    def softmax_pv(head, st, tile, tile_max, m):
        start = pl.multiple_of(tile * tq, tq)
        m_new = jnp.maximum(m, tile_max)
        alpha = jnp.exp2(m - m_new)
        pt = jnp.exp2(st - m_new).astype(BF)
        vt = vt_scrs[head][:, pl.ds(start, tq)]
        acc_scr[head] = alpha * acc_scr[head] + _dot(vt, pt)
        return m_new

    acc_scr[...] = jnp.zeros_like(acc_scr)
    diag = pl.multiple_of(qi * tq, tq)
    first_max = [scores(head, diag, True) for head in range(2)]
    m_init = jnp.full((1, tq), NEG_INF, F32)

    def body(j, carry):
        max_a, max_b, m_a, m_b, prev_tile = carry
        prev = [s_scr[head] for head in range(2)]
        start = pl.multiple_of(j * tq, tq)
        next_max = [scores(head, start, False) for head in range(2)]
        m_a = softmax_pv(0, prev[0], prev_tile, max_a, m_a)
        m_b = softmax_pv(1, prev[1], prev_tile, max_b, m_b)
        return next_max[0], next_max[1], m_a, m_b, j

    max_a, max_b, m_a, m_b, prev_tile = lax.fori_loop(
        0, qi, body, (first_max[0], first_max[1], m_init, m_init, qi))
    softmax_pv(0, s_scr[0], prev_tile, max_a, m_a)
    softmax_pv(1, s_scr[1], prev_tile, max_b, m_b)

    for head in range(2):
        acc = acc_scr[head]
        ot_scr[head * HEAD_DIM:(head + 1) * HEAD_DIM, :] = (
            acc[0:HEAD_DIM, :] / acc[HEAD_DIM:HEAD_DIM + 1, :])
    o_ref[...] = ot_scr[...].T.astype(BF)


def _fox_attention(z, gq, gk, *, batch, seq, tq):
    n_q = seq // tq
    tokens = batch * seq
    return pl.pallas_call(
        functools.partial(_fox_kernel, tq=tq),
        grid=(batch, N_HEAD_PAIRS, n_q),
        in_specs=[
            pl.BlockSpec((tq, LANES), lambda b, hp, qi: (b * n_q + qi, hp)),
            pl.BlockSpec((seq, LANES), lambda b, hp, qi: (b, N_HEAD_PAIRS + hp)),
            pl.BlockSpec((seq, LANES), lambda b, hp, qi: (b, 2 * N_HEAD_PAIRS + hp)),
            pl.BlockSpec((1, tq, LANES), lambda b, hp, qi: (b, qi, 0)),
            pl.BlockSpec((1, seq, LANES), lambda b, hp, qi: (b, 0, 0)),
        ],
        out_specs=pl.BlockSpec((tq, LANES), lambda b, hp, qi: (b * n_q + qi, hp)),
        out_shape=jax.ShapeDtypeStruct((tokens, ATT_WIDTH), BF),
        scratch_shapes=[
            pltpu.VMEM((seq, LANES), BF),
            pltpu.VMEM((seq, LANES), BF),
            pltpu.VMEM((HEAD_DIM + BF16_ROWS, seq), BF),
            pltpu.VMEM((HEAD_DIM + BF16_ROWS, seq), BF),
            pltpu.VMEM((tq, tq), F32),
            pltpu.VMEM((2, tq, tq), F32),
            pltpu.VMEM((2, HEAD_DIM + BF16_ROWS, tq), F32),
            pltpu.VMEM((LANES, tq), F32),
        ],
        compiler_params=_cparams(("arbitrary", "arbitrary", "arbitrary")),
        name="fox_attn",
    )(z, z, z, gq, gk)


def _log2(n):
    assert n > 0 and n & (n - 1) == 0, n
    return n.bit_length() - 1


def _dilated_kernel(q1, k1, v1, q4, k4, v4, q16, k16, v16, o_ref,
                    num1, num4, num16, den1, den4, den16, max1, max4, max16, bias_scr,
                    *, seq, blk, chunk, group, merge_rows):
    sources = ((q1, k1, v1), (q4, k4, v4), (q16, k16, v16))
    num_scrs = (num1, num4, num16)
    den_scrs = (den1, den4, den16)
    max_scrs = (max1, max4, max16)
    n_blocks = seq // blk

    low = lax.broadcasted_iota(jnp.int32, (blk, LANES), 1) < HEAD_DIM
    qrow = lax.broadcasted_iota(jnp.int32, (2 * blk, 2 * blk), 0) % blk
    kcol = lax.broadcasted_iota(jnp.int32, (2 * blk, 2 * blk), 1)
    dist = qrow + blk - kcol
    band = (dist >= 0) & (dist <= blk)
    bias_scr[0] = jnp.where(band & (kcol >= blk), 0.0, NEG_INF)
    bias_scr[1] = jnp.where(band, 0.0, NEG_INF)
    ones = jnp.ones((2 * blk, LANES), BF)

    for idx, (window, dilation) in enumerate(DIL_PATTERNS):
        assert window // dilation == blk
        q_ref, k_ref, v_ref = sources[idx]
        num_scr, den_scr, max_scr = num_scrs[idx], den_scrs[idx], max_scrs[idx]
        sub_shift = _log2(seq // dilation // blk)
        seg_rows = chunk // dilation
        seg = min(blk, seg_rows)

        def load(ref, j, l0):
            parts = []
            for s in range(blk // seg):
                l = l0 + s * seg
                p = lax.shift_right_logical(l, _log2(seg_rows))
                i = l & (seg_rows - 1)
                start = pl.multiple_of(p * chunk + j * seg_rows + i, seg)
                parts.append(ref[pl.ds(start, seg), :])
            return parts[0] if len(parts) == 1 else jnp.concatenate(parts, axis=0)

        def one_block(g):
            j = lax.shift_right_logical(g, sub_shift)
            gs = g & ((1 << sub_shift) - 1)
            l0 = gs * blk
            lp = jnp.maximum(l0 - blk, 0)
            q2 = load(q_ref, j, l0)
            zero = jnp.zeros_like(q2)
            qq = jnp.concatenate([jnp.where(low, q2, zero), jnp.where(low, zero, q2)], axis=0)
            kwin = jnp.concatenate([load(k_ref, j, lp), load(k_ref, j, l0)], axis=0)
            vwin = jnp.concatenate([load(v_ref, j, lp), load(v_ref, j, l0)], axis=0)
            s = _dot_nt(qq, kwin) + bias_scr[jnp.minimum(gs, 1)]
            m = jnp.max(s, axis=1, keepdims=True)
            p = jnp.exp2(s - m).astype(BF)
            pv = _dot(p, jnp.concatenate([vwin, ones], axis=1))
            if dilation == 1:
                dst = pl.ds(pl.multiple_of(l0, blk), blk)
            else:
                dst = pl.ds(l0 * dilation + j, blk, stride=dilation)
            num_scr[dst, :] = jnp.where(low, pv[:blk, :LANES], pv[blk:, :LANES])
            den_scr[dst, :] = jnp.where(low, pv[:blk, LANES:], pv[blk:, LANES:])
            max_scr[dst, :] = jnp.where(low, jnp.broadcast_to(m[:blk], (blk, LANES)),
                                        jnp.broadcast_to(m[blk:], (blk, LANES)))

        def blocks(it, carry):
            for u in range(group):
                one_block(it * group + u)
            return carry

        lax.fori_loop(0, n_blocks // group, blocks, 0)

    def merge(c, carry):
        rows = pl.ds(pl.multiple_of(c * merge_rows, merge_rows), merge_rows)
        maxes = [max_scr[rows, :] for max_scr in max_scrs]
        top = functools.reduce(jnp.maximum, maxes)
        weights = [jnp.exp2(m - top) for m in maxes]
        num = sum(w * num_scr[rows, :] for w, num_scr in zip(weights, num_scrs))
        den = sum(w * den_scr[rows, :] for w, den_scr in zip(weights, den_scrs))
        o_ref[rows, :] = (num / den).astype(BF)
        return carry

    lax.fori_loop(0, seq // merge_rows, merge, 0)


def _dilated_attention(z, zd4, zd16, *, batch, seq):
    blk = DIL_PATTERNS[0][0] // DIL_PATTERNS[0][1]
    tokens = batch * seq
    per_section = ATT_WIDTH // LANES
    col = lambda section: pl.BlockSpec(
        (seq, LANES), lambda b, hp: (b, section * per_section + hp))
    f32_buf = pltpu.VMEM((seq, LANES), F32)
    return pl.pallas_call(
        functools.partial(_dilated_kernel, seq=seq, blk=blk, chunk=DIL_CHUNK, group=4,
                          merge_rows=256),
        grid=(batch, N_HEAD_PAIRS),
        in_specs=[col(DIL_SECTION), col(DIL_SECTION + 1), col(DIL_SECTION + 2),
                  col(0), col(1), col(2), col(0), col(1), col(2)],
        out_specs=pl.BlockSpec((seq, LANES), lambda b, hp: (b, hp)),
        out_shape=jax.ShapeDtypeStruct((tokens, ATT_WIDTH), BF),
        scratch_shapes=[f32_buf] * 9 + [pltpu.VMEM((2, 2 * blk, 2 * blk), F32)],
        compiler_params=_cparams(("arbitrary", "arbitrary")),
        name="dilated_attn",
    )(z, z, z, zd4, zd4, zd4, zd16, zd16, zd16)


def _mix_kernel(ya_ref, yb_ref, ga_ref, gb_ref, x_ref, woa_ref, wob_ref, wout_ref,
                g_ref, o_ref):
    pa = _dot(ya_ref[...], woa_ref[...])
    pb = _dot(yb_ref[...], wob_ref[...])
    mixed = (jax.nn.sigmoid(ga_ref[...].astype(F32)) * pa
             + jax.nn.sigmoid(gb_ref[...].astype(F32)) * pb)
    y = _dot(mixed.astype(BF), wout_ref[...])
    o_ref[...] = x_ref[...] + _rms(y, g_ref[...])


def _mix(ya, yb, z, x2d, woa, wob, wout, g, *, tm):
    tokens = x2d.shape[0]
    gate_blk = lambda off: pl.BlockSpec((tm, D_MODEL), lambda i: (i, off))
    const = lambda i: (0, 0)
    return pl.pallas_call(
        _mix_kernel,
        grid=(tokens // tm,),
        in_specs=[
            pl.BlockSpec((tm, ATT_WIDTH), lambda i: (i, 0)),
            pl.BlockSpec((tm, ATT_WIDTH), lambda i: (i, 0)),
            gate_blk(3),
            gate_blk(4),
            pl.BlockSpec((tm, D_MODEL), lambda i: (i, 0)),
            pl.BlockSpec((ATT_WIDTH, D_MODEL), const),
            pl.BlockSpec((ATT_WIDTH, D_MODEL), const),
            pl.BlockSpec((D_MODEL, D_MODEL), const),
            pl.BlockSpec((1, D_MODEL), const),
        ],
        out_specs=pl.BlockSpec((tm, D_MODEL), lambda i: (i, 0)),
        out_shape=jax.ShapeDtypeStruct((tokens, D_MODEL), F32),
        compiler_params=_cparams(("arbitrary",)),
        name="mix",
    )(ya, yb, z, z, x2d, woa, wob, wout, g)


def _ffn_kernel(x_ref, halo_ref, gpre_ref, wup_ref, cw_ref, cb_ref, wd_ref, gpost_ref, o_ref,
                h_scr, ua0_scr, ub0_scr, ua1_scr, ub1_scr, acc_scr,
                *, tm, tf, tiles_per_seq, n_chunks):
    i = pl.program_id(0)
    halo = BF16_ROWS
    rows = tm // n_chunks
    n_tiles = D_FF // tf
    u_sets = ((ua0_scr, ub0_scr), (ua1_scr, ub1_scr))

    def cols(f, gate):
        return pl.ds(pl.multiple_of(f * tf + (D_FF if gate else 0), LANES), tf)

    def up_proj(f, dst, lo, hi):
        h = h_scr[lo:hi, :]
        dst[0][lo:hi, :] = _dot(h, wup_ref[:, cols(f, False)])
        dst[1][lo:hi, :] = _dot(h, wup_ref[:, cols(f, True)])

    def conv(u_scr, window, post_scale, first, n):
        out = cb_ref[:, window] * post_scale
        for tap in range(CONV_WIDTH):
            lo = halo + first - (CONV_WIDTH - 1) + tap
            out = out + (cw_ref[tap:tap + 1, window] * post_scale) * u_scr[lo:lo + n, :]
        return out

    def down_proj(f, src, first, n):
        a = conv(src[0], cols(f, False), 1.0, first, n)
        half_b = conv(src[1], cols(f, True), 0.5, first, n)
        c0 = math.sqrt(2.0 / math.pi)
        inner = a * (c0 + (c0 * 0.044715) * (a * a))
        hidden = ((a * half_b) * (1.0 + jnp.tanh(inner))).astype(BF)
        return _dot(hidden, wd_ref[pl.ds(pl.multiple_of(f * tf, tf), tf), :])

    def stage(f, parity):
        src, dst = u_sets[1 - parity], u_sets[parity]
        for c in range(n_chunks):
            lo = 0 if c == 0 else halo + c * rows
            up_proj(f, dst, lo, halo + (c + 1) * rows)
            out_rows = slice(c * rows, (c + 1) * rows)
            acc_scr[out_rows, :] += down_proj(f - 1, src, c * rows, rows)

    g = gpre_ref[...]
    hh = _rms(halo_ref[...], g)
    hh = jnp.where(i % tiles_per_seq == 0, jnp.zeros_like(hh), hh)
    h_scr[0:halo, :] = hh.astype(BF)
    h_scr[halo:, :] = _rms(x_ref[...], g).astype(BF)
    acc_scr[...] = jnp.zeros_like(acc_scr)
    up_proj(0, u_sets[0], 0, tm + halo)

    def pair(k, carry):
        stage(2 * k + 1, 1)
        stage(2 * k + 2, 0)
        return carry

    assert n_tiles % 2 == 1
    lax.fori_loop(0, (n_tiles - 1) // 2, pair, 0)
    y = acc_scr[...] + down_proj(n_tiles - 1, u_sets[0], 0, tm)
    o_ref[...] = x_ref[...] + _rms(y, gpost_ref[...])


def _ffn(x1, g_pre, w_up, conv_w, conv_b, w_down, g_post, *, seq, tm, tf):
    tokens = x1.shape[0]
    halo = BF16_ROWS
    per = tm // halo
    resident = lambda shape: pl.BlockSpec(shape, lambda i: (0, 0),
                                          pipeline_mode=pl.Buffered(1))
    u_buf = pltpu.VMEM((tm + halo, tf), F32)
    return pl.pallas_call(
        functools.partial(_ffn_kernel, tm=tm, tf=tf, tiles_per_seq=seq // tm, n_chunks=4),
        grid=(tokens // tm,),
        in_specs=[
            pl.BlockSpec((tm, D_MODEL), lambda i: (i, 0)),
            pl.BlockSpec((halo, D_MODEL), lambda i: (jnp.maximum(i * per - 1, 0), 0)),
            resident((1, D_MODEL)),
            resident((D_MODEL, 2 * D_FF)),
            resident((CONV_WIDTH, 2 * D_FF)),
            resident((1, 2 * D_FF)),
            resident((D_FF, D_MODEL)),
            resident((1, D_MODEL)),
        ],
        out_specs=pl.BlockSpec((tm, D_MODEL), lambda i: (i, 0)),
        out_shape=jax.ShapeDtypeStruct((tokens, D_MODEL), F32),
        scratch_shapes=[
            pltpu.VMEM((tm + halo, D_MODEL), BF),
            u_buf, u_buf, u_buf, u_buf,
            pltpu.VMEM((tm, D_MODEL), F32),
        ],
        compiler_params=_cparams(("arbitrary",)),
        name="ffn",
    )(x1, x1, g_pre, w_up, conv_w, conv_b, w_down, g_post)


def _rope_freq_lanes():
    inv_freq = ROPE_THETA ** (-jnp.arange(ROPE_HALF, dtype=F32) * 2.0 / ROPE_DIM)
    lane = np.arange(LANES) % HEAD_DIM
    return inv_freq[lane % ROPE_HALF].reshape(1, LANES)


def kernel(x, g_pre_mix, w_in, b_forget, w_o_fox, w_o_dil, w_out, g_post_mix,
           g_pre_ffn, w_up, conv_w, conv_b, w_down, g_post_ffn):
    batch, seq, d_model = x.shape
    assert d_model == D_MODEL and seq % 1024 == 0
    depth = w_in.shape[0]
    fox_end = 3 * ATT_WIDTH
    freq_lanes = _rope_freq_lanes()
    x2d = x.reshape(batch * seq, D_MODEL)
    row = lambda v: v.reshape(1, -1)
    for l in range(depth):
        w_main = jnp.concatenate(
            [w_in[l][:, :fox_end], w_in[l][:, fox_end + N_HEADS:]], axis=1).astype(BF)
        w_f = jnp.pad(w_in[l][:, fox_end:fox_end + N_HEADS],
                      ((0, 0), (0, LANES - N_HEADS))).astype(BF)
        bias_lanes = jnp.pad(b_forget[l], (0, LANES - N_HEADS)).reshape(1, LANES)

        z, fa, zd4, zd16 = _in_proj(x2d, row(g_pre_mix[l]), w_main, w_f, freq_lanes,
                                    batch=batch, seq=seq, tm=DIL_CHUNK)
        gq, gk = _forget_scan(fa, bias_lanes, batch=batch, seq=seq, chunk=512)
        ya = _fox_attention(z, gq, gk, batch=batch, seq=seq, tq=512)
        yb = _dilated_attention(z, zd4, zd16, batch=batch, seq=seq)
        x2d = _mix(ya, yb, z, x2d, w_o_fox[l].astype(BF), w_o_dil[l].astype(BF),
                   w_out[l].astype(BF), row(g_post_mix[l]), tm=512)
        x2d = _ffn(x2d, row(g_pre_ffn[l]), w_up[l].astype(BF), conv_w[l],
                   row(conv_b[l]), w_down[l].astype(BF), row(g_post_ffn[l]),
                   seq=seq, tm=1024, tf=256)
    return x2d.reshape(batch, seq, D_MODEL)
```

```python
import functools
import math

import numpy as np
import jax
import jax.numpy as jnp
from jax import lax
from jax.experimental import pallas as pl
from jax.experimental.pallas import tpu as pltpu

D_MODEL = 1024
HEAD_DIM = 64
N_HEADS = 8
ATT_WIDTH = N_HEADS * HEAD_DIM
N_HEAD_PAIRS = N_HEADS // 2
DIL_PATTERNS = ((128, 1), (512, 4), (2048, 16))
ROPE_DIM = HEAD_DIM // 4
ROPE_HALF = ROPE_DIM // 2
ROPE_THETA = 500000.0
D_FF = 2816
CONV_WIDTH = 3
RMS_EPS = 1e-6
NEG_INF = -1e30
Q_SCALE = 1.0 / math.sqrt(HEAD_DIM)
LOG2_E = math.log2(math.e)

LANES = 128
BF16_ROWS = 16
Z_WIDTH = 3 * ATT_WIDTH * 2 + 2 * D_MODEL
Z_BLK = 512
DIL_SECTION = 3
DIL_CHUNK = 512
N_PIECES = 3
VMEM_LIMIT = 56 * 1024 * 1024

BF = jnp.bfloat16
F32 = jnp.float32


def _cparams(sem, flags=None):
    return pltpu.CompilerParams(dimension_semantics=sem, vmem_limit_bytes=VMEM_LIMIT,
                                flags=flags)


def _rms(xf, g):
    inv = lax.rsqrt(jnp.mean(xf * xf, axis=-1, keepdims=True) + RMS_EPS)
    return xf * inv * g


def _split3(x):
    hi = x.astype(BF)
    r1 = x - hi.astype(F32)
    mid = r1.astype(BF)
    lo = (r1 - mid.astype(F32)).astype(BF)
    return hi, mid, lo


def _dot(a, b):
    return jnp.dot(a, b, preferred_element_type=F32)


def _dot_nt(a, b):
    return lax.dot_general(a, b, (((1,), (1,)), ((), ())), preferred_element_type=F32)


def _in_proj_kernel(x_ref, g_ref, w_ref, wf_ref, freq_ref, z_ref, fa_ref, zd4_ref, zd16_ref,
                    h_scr, cos_scr, sneg_scr, spos_scr, stage_scr, stage4_scr, *, tm):
    p = pl.program_id(0)
    b = pl.program_id(1)
    zd_refs = {4: zd4_ref, 16: zd16_ref}
    n_slabs = Z_BLK // LANES

    @pl.when(b == 0)
    def _tables():
        pos = (p * tm + lax.broadcasted_iota(jnp.int32, (tm, LANES), 0)).astype(F32)
        lane = lax.broadcasted_iota(jnp.int32, (tm, LANES), 1)
        c = lane % HEAD_DIM
        ang = pos * freq_ref[...]
        cs = jnp.cos(ang)
        sn = jnp.sin(ang)
        cos_scr[...] = jnp.where(c < ROPE_DIM, cs, 1.0)
        sneg_scr[...] = jnp.where(c < ROPE_HALF, -sn, 0.0)
        spos_scr[...] = jnp.where((c >= ROPE_HALF) & (c < ROPE_DIM), sn, 0.0)

    def rope(t):
        up = pltpu.roll(t, LANES - ROPE_HALF, 1)
        dn = pltpu.roll(t, ROPE_HALF, 1)
        return t * cos_scr[...] + up * sneg_scr[...] + dn * spos_scr[...]

    def emit_dilated(section, slab, val):
        out_lanes = slice(section * Z_BLK + slab * LANES, section * Z_BLK + (slab + 1) * LANES)
        z_ref[:, (DIL_SECTION + section) * Z_BLK + slab * LANES:
              (DIL_SECTION + section) * Z_BLK + (slab + 1) * LANES] = val.astype(BF)
        stage, stage4 = stage_scr.at[section, slab], stage4_scr.at[section, slab]
        stage[...] = val
        seg4 = tm // 4
        for c in range(4):
            part = stage[pl.ds(c, seg4, stride=4), :]
            stage4[c * seg4:(c + 1) * seg4, :] = part
            zd_refs[4][c * seg4:(c + 1) * seg4, out_lanes] = part.astype(BF)
        seg16 = tm // 16
        for jj in range(16):
            c, a = jj % 4, jj // 4
            part = stage4[pl.ds(c * seg4 + a, seg16, stride=4), :]
            zd_refs[16][jj * seg16:(jj + 1) * seg16, out_lanes] = part.astype(BF)

    h = _rms(x_ref[...], g_ref[...]).astype(BF)
    h_scr[...] = h
    fa_ref[...] = _dot(h, wf_ref[...])

    for j in range(Z_WIDTH // Z_BLK):
        cols = slice(j * Z_BLK, (j + 1) * Z_BLK)
        acc = _dot(h_scr[...], w_ref[:, cols])
        section = j - DIL_SECTION
        if j == 0:
            z_ref[:, cols] = (acc * (Q_SCALE * LOG2_E)).astype(BF)
        elif 0 <= section < 3:
            for slab in range(n_slabs):
                t = acc[:, slab * LANES:(slab + 1) * LANES]
                if section == 0:
                    t = rope(t) * (Q_SCALE * LOG2_E)
                elif section == 1:
                    t = rope(t)
                emit_dilated(section, slab, t)
        else:
            z_ref[:, cols] = acc.astype(BF)


def _in_proj(x2d, g, w_main, w_f, freq_lanes, *, batch, seq, tm):
    n_p = seq // tm
    tokens = batch * seq
    row = lambda p, b: (b * n_p + p, 0)
    resident = lambda shape: pl.BlockSpec(shape, lambda p, b: (0, 0),
                                          pipeline_mode=pl.Buffered(1))
    dil_shape = jax.ShapeDtypeStruct((tokens, 3 * ATT_WIDTH), BF)
    stage_buf = pltpu.VMEM((3, Z_BLK // LANES, tm, LANES), F32)
    return pl.pallas_call(
        functools.partial(_in_proj_kernel, tm=tm),
        grid=(n_p, batch),
        in_specs=[
            pl.BlockSpec((tm, D_MODEL), row),
            resident((1, D_MODEL)),
            resident((D_MODEL, Z_WIDTH)),
            resident((D_MODEL, LANES)),
            resident((1, LANES)),
        ],
        out_specs=[
            pl.BlockSpec((tm, Z_WIDTH), row),
            pl.BlockSpec((tm, LANES), row),
            pl.BlockSpec((tm, 3 * ATT_WIDTH), row),
            pl.BlockSpec((tm, 3 * ATT_WIDTH), row),
        ],
        out_shape=[
            jax.ShapeDtypeStruct((tokens, Z_WIDTH), BF),
            jax.ShapeDtypeStruct((tokens, LANES), F32),
            dil_shape,
            dil_shape,
        ],
        scratch_shapes=[
            pltpu.VMEM((tm, D_MODEL), BF),
            pltpu.VMEM((tm, LANES), F32),
            pltpu.VMEM((tm, LANES), F32),
            pltpu.VMEM((tm, LANES), F32),
            stage_buf,
            stage_buf,
        ],
        compiler_params=_cparams(("arbitrary", "arbitrary")),
        name="in_proj",
    )(x2d, g, w_main, w_f, freq_lanes)


def _extras_base(head):
    return (HEAD_DIM if head % 2 == 0 else 0) + 2 * N_PIECES * (head // 2)


def _forget_scan_kernel(fa_ref, bias_ref, tri_ref, pq_ref, pk_ref, oq_ref, ok_ref,
                        gq_ref, gk_ref, carry_scr):
    c = pl.program_id(1)

    @pl.when(c == 0)
    def _():
        carry_scr[...] = jnp.zeros_like(carry_scr)

    t = fa_ref[...] + bias_ref[...]
    log_f = jnp.minimum(t, 0.0) - jnp.log1p(jnp.exp(-jnp.abs(t)))
    tri = tri_ref[...]
    run = carry_scr[...]
    for piece in _split3(log_f):
        run = run + _dot(tri, piece)
    rows = run.shape[0]
    carry_scr[...] = run[rows - 1:rows, :]
    gq = oq_ref[...].astype(F32)
    gk = ok_ref[...].astype(F32)
    for i, piece in enumerate(_split3(run * LOG2_E)):
        gq = gq + _dot(piece, pq_ref[i])
        gk = gk - _dot(piece, pk_ref[i])
    gq_ref[0] = gq.astype(BF)
    gk_ref[0] = gk.astype(BF)


def _forget_scan_constants(chunk):
    tri = np.tril(np.ones((chunk, chunk), np.float32))
    pq = np.zeros((N_PIECES, LANES, LANES), np.float32)
    pk = np.zeros((N_PIECES, LANES, LANES), np.float32)
    oq = np.zeros((1, LANES), np.float32)
    ok = np.zeros((1, LANES), np.float32)
    for h in range(N_HEADS):
        base = _extras_base(h)
        for i in range(N_PIECES):
            pq[i, h, base + i] = 1.0
            oq[0, base + N_PIECES + i] = 1.0
            ok[0, base + i] = 1.0
            pk[i, h, base + N_PIECES + i] = 1.0
    as_bf = lambda a: jnp.asarray(a, BF)
    return as_bf(tri), as_bf(pq), as_bf(pk), as_bf(oq), as_bf(ok)


def _forget_scan(fa, bias_lanes, *, batch, seq, chunk):
    n_c = seq // chunk
    tri, pq, pk, oq, ok = _forget_scan_constants(chunk)
    const2 = lambda b, c: (0, 0)
    const3 = lambda b, c: (0, 0, 0)
    return pl.pallas_call(
        _forget_scan_kernel,
        grid=(batch, n_c),
        in_specs=[
            pl.BlockSpec((chunk, LANES), lambda b, c: (b * n_c + c, 0)),
            pl.BlockSpec((1, LANES), const2),
            pl.BlockSpec((chunk, chunk), const2),
            pl.BlockSpec((N_PIECES, LANES, LANES), const3),
            pl.BlockSpec((N_PIECES, LANES, LANES), const3),
            pl.BlockSpec((1, LANES), const2),
            pl.BlockSpec((1, LANES), const2),
        ],
        out_specs=[
            pl.BlockSpec((1, chunk, LANES), lambda b, c: (b, c, 0)),
            pl.BlockSpec((1, chunk, LANES), lambda b, c: (b, c, 0)),
        ],
        out_shape=[
            jax.ShapeDtypeStruct((batch, seq, LANES), BF),
            jax.ShapeDtypeStruct((batch, seq, LANES), BF),
        ],
        scratch_shapes=[pltpu.VMEM((1, LANES), F32)],
        compiler_params=_cparams(("arbitrary", "arbitrary")),
        name="forget_scan",
    )(fa, bias_lanes, tri, pq, pk, oq, ok)


def _with_bias_lanes(slab, extras, hp, parity):
    lane = lax.broadcasted_iota(jnp.int32, slab.shape, 1)
    own = (lane < HEAD_DIM) if parity == 0 else (lane >= HEAD_DIM)
    base = (HEAD_DIM if parity == 0 else 0) + 2 * N_PIECES * hp
    in_extras = (lane >= base) & (lane < base + 2 * N_PIECES)
    return jnp.where(own, slab, jnp.where(in_extras, extras, jnp.zeros_like(extras)))


def _fox_kernel(q_ref, k_ref, v_ref, gq_ref, gk_ref, o_ref,
                ka_scr, kb_scr, vta_scr, vtb_scr, bias_scr, s_scr, acc_scr, ot_scr, *, tq):
    hp = pl.program_id(1)
    qi = pl.program_id(2)
    k_scrs = (ka_scr, kb_scr)
    vt_scrs = (vta_scr, vtb_scr)
    seq = k_ref.shape[0]

    @pl.when(qi == 0)
    def _prep():
        k2 = k_ref[...]
        gk = gk_ref[0]
        ka_scr[...] = _with_bias_lanes(k2, gk, hp, 0)
        kb_scr[...] = _with_bias_lanes(k2, gk, hp, 1)
        vt = v_ref[...].astype(F32).T.astype(BF)
        ones = jnp.ones((BF16_ROWS, seq), BF)
        for head, vt_scr in enumerate(vt_scrs):
            vt_scr[0:HEAD_DIM, :] = vt[head * HEAD_DIM:(head + 1) * HEAD_DIM, :]
            vt_scr[HEAD_DIM:, :] = ones
        row = lax.broadcasted_iota(jnp.int32, (tq, tq), 0)
        col = lax.broadcasted_iota(jnp.int32, (tq, tq), 1)
        bias_scr[...] = jnp.where(row <= col, 0.0, NEG_INF)

    q2 = q_ref[...]
    gq = gq_ref[0]
    qps = [_with_bias_lanes(q2, gq, hp, head) for head in range(2)]

    def scores(head, start, masked):
        kt = k_scrs[head][pl.ds(start, tq), :]
        st = _dot_nt(kt, qps[head])
        if masked:
            st = st + bias_scr[...]
        s_scr[head] = st
        return jnp.max(st, axis=0, keepdims=True)

    def softmax_pv(head, st, tile, tile_max, m):
        start = pl.multiple_of(tile * tq, tq)
        m_new = jnp.maximum(m, tile_max)
        alpha = jnp.exp2(m - m_new)
        pt = jnp.exp2(st - m_new).astype(BF)
        vt = vt_scrs[head][:, pl.ds(start, tq)]
        acc_scr[head] = alpha * acc_scr[head] + _dot(vt, pt)
        return m_new

    acc_scr[...] = jnp.zeros_like(acc_scr)
    diag = pl.multiple_of(qi * tq, tq)
    first_max = [scores(head, diag, True) for head in range(2)]
    m_init = jnp.full((1, tq), NEG_INF, F32)

    def body(j, carry):
        max_a, max_b, m_a, m_b, prev_tile = carry
        prev = [s_scr[head] for head in range(2)]
        start = pl.multiple_of(j * tq, tq)
        next_max = [scores(head, start, False) for head in range(2)]
        m_a = softmax_pv(0, prev[0], prev_tile, max_a, m_a)
        m_b = softmax_pv(1, prev[1], prev_tile, max_b, m_b)
        return next_max[0], next_max[1], m_a, m_b, j

    max_a, max_b, m_a, m_b, prev_tile = lax.fori_loop(
        0, qi, body, (first_max[0], first_max[1], m_init, m_init, qi))
    softmax_pv(0, s_scr[0], prev_tile, max_a, m_a)
    softmax_pv(1, s_scr[1], prev_tile, max_b, m_b)

    for head in range(2):
        acc = acc_scr[head]
        ot_scr[head * HEAD_DIM:(head + 1) * HEAD_DIM, :] = (
            acc[0:HEAD_DIM, :] / acc[HEAD_DIM:HEAD_DIM + 1, :])
    o_ref[...] = ot_scr[...].T.astype(BF)


def _fox_attention(z, gq, gk, *, batch, seq, tq):
    n_q = seq // tq
    tokens = batch * seq
    return pl.pallas_call(
        functools.partial(_fox_kernel, tq=tq),
        grid=(batch, N_HEAD_PAIRS, n_q),
        in_specs=[
            pl.BlockSpec((tq, LANES), lambda b, hp, qi: (b * n_q + qi, hp)),
            pl.BlockSpec((seq, LANES), lambda b, hp, qi: (b, N_HEAD_PAIRS + hp)),
            pl.BlockSpec((seq, LANES), lambda b, hp, qi: (b, 2 * N_HEAD_PAIRS + hp)),
            pl.BlockSpec((1, tq, LANES), lambda b, hp, qi: (b, qi, 0)),
            pl.BlockSpec((1, seq, LANES), lambda b, hp, qi: (b, 0, 0)),
        ],
        out_specs=pl.BlockSpec((tq, LANES), lambda b, hp, qi: (b * n_q + qi, hp)),
        out_shape=jax.ShapeDtypeStruct((tokens, ATT_WIDTH), BF),
        scratch_shapes=[
            pltpu.VMEM((seq, LANES), BF),
            pltpu.VMEM((seq, LANES), BF),
            pltpu.VMEM((HEAD_DIM + BF16_ROWS, seq), BF),
            pltpu.VMEM((HEAD_DIM + BF16_ROWS, seq), BF),
            pltpu.VMEM((tq, tq), F32),
            pltpu.VMEM((2, tq, tq), F32),
            pltpu.VMEM((2, HEAD_DIM + BF16_ROWS, tq), F32),
            pltpu.VMEM((LANES, tq), F32),
        ],
        compiler_params=_cparams(("arbitrary", "arbitrary", "arbitrary")),
        name="fox_attn",
    )(z, z, z, gq, gk)


def _log2(n):
    assert n > 0 and n & (n - 1) == 0, n
    return n.bit_length() - 1


def _dilated_kernel(q1, k1, v1, q4, k4, v4, q16, k16, v16, o_ref,
                    num1, num4, num16, den1, den4, den16, max1, max4, max16, bias_scr,
                    *, seq, blk, chunk, group, merge_rows):
    sources = ((q1, k1, v1), (q4, k4, v4), (q16, k16, v16))
    num_scrs = (num1, num4, num16)
    den_scrs = (den1, den4, den16)
    max_scrs = (max1, max4, max16)
    n_blocks = seq // blk

    low = lax.broadcasted_iota(jnp.int32, (blk, LANES), 1) < HEAD_DIM
    qrow = lax.broadcasted_iota(jnp.int32, (2 * blk, 2 * blk), 0) % blk
    kcol = lax.broadcasted_iota(jnp.int32, (2 * blk, 2 * blk), 1)
    dist = qrow + blk - kcol
    band = (dist >= 0) & (dist <= blk)
    bias_scr[0] = jnp.where(band & (kcol >= blk), 0.0, NEG_INF)
    bias_scr[1] = jnp.where(band, 0.0, NEG_INF)
    ones = jnp.ones((2 * blk, LANES), BF)

    for idx, (window, dilation) in enumerate(DIL_PATTERNS):
        assert window // dilation == blk
        q_ref, k_ref, v_ref = sources[idx]
        num_scr, den_scr, max_scr = num_scrs[idx], den_scrs[idx], max_scrs[idx]
        sub_shift = _log2(seq // dilation // blk)
        seg_rows = chunk // dilation
        seg = min(blk, seg_rows)

        def load(ref, j, l0):
            parts = []
            for s in range(blk // seg):
                l = l0 + s * seg
                p = lax.shift_right_logical(l, _log2(seg_rows))
                i = l & (seg_rows - 1)
                start = pl.multiple_of(p * chunk + j * seg_rows + i, seg)
                parts.append(ref[pl.ds(start, seg), :])
            return parts[0] if len(parts) == 1 else jnp.concatenate(parts, axis=0)

        def one_block(g):
            j = lax.shift_right_logical(g, sub_shift)
            gs = g & ((1 << sub_shift) - 1)
            l0 = gs * blk
            lp = jnp.maximum(l0 - blk, 0)
            q2 = load(q_ref, j, l0)
            zero = jnp.zeros_like(q2)
            qq = jnp.concatenate([jnp.where(low, q2, zero), jnp.where(low, zero, q2)], axis=0)
            kwin = jnp.concatenate([load(k_ref, j, lp), load(k_ref, j, l0)], axis=0)
            vwin = jnp.concatenate([load(v_ref, j, lp), load(v_ref, j, l0)], axis=0)
            s = _dot_nt(qq, kwin) + bias_scr[jnp.minimum(gs, 1)]
            m = jnp.max(s, axis=1, keepdims=True)
            p = jnp.exp2(s - m).astype(BF)
            pv = _dot(p, jnp.concatenate([vwin, ones], axis=1))
            if dilation == 1:
                dst = pl.ds(pl.multiple_of(l0, blk), blk)
            else:
                dst = pl.ds(l0 * dilation + j, blk, stride=dilation)
            num_scr[dst, :] = jnp.where(low, pv[:blk, :LANES], pv[blk:, :LANES])
            den_scr[dst, :] = jnp.where(low, pv[:blk, LANES:], pv[blk:, LANES:])
            max_scr[dst, :] = jnp.where(low, jnp.broadcast_to(m[:blk], (blk, LANES)),
                                        jnp.broadcast_to(m[blk:], (blk, LANES)))

        def blocks(it, carry):
            for u in range(group):
                one_block(it * group + u)
            return carry

        lax.fori_loop(0, n_blocks // group, blocks, 0)

    def merge(c, carry):
        rows = pl.ds(pl.multiple_of(c * merge_rows, merge_rows), merge_rows)
        maxes = [max_scr[rows, :] for max_scr in max_scrs]
        top = functools.reduce(jnp.maximum, maxes)
        weights = [jnp.exp2(m - top) for m in maxes]
        num = sum(w * num_scr[rows, :] for w, num_scr in zip(weights, num_scrs))
        den = sum(w * den_scr[rows, :] for w, den_scr in zip(weights, den_scrs))
        o_ref[rows, :] = (num / den).astype(BF)
        return carry

    lax.fori_loop(0, seq // merge_rows, merge, 0)


def _dilated_attention(z, zd4, zd16, *, batch, seq):
    blk = DIL_PATTERNS[0][0] // DIL_PATTERNS[0][1]
    tokens = batch * seq
    per_section = ATT_WIDTH // LANES
    col = lambda section: pl.BlockSpec(
        (seq, LANES), lambda b, hp: (b, section * per_section + hp))
    f32_buf = pltpu.VMEM((seq, LANES), F32)
    return pl.pallas_call(
        functools.partial(_dilated_kernel, seq=seq, blk=blk, chunk=DIL_CHUNK, group=4,
                          merge_rows=256),
        grid=(batch, N_HEAD_PAIRS),
        in_specs=[col(DIL_SECTION), col(DIL_SECTION + 1), col(DIL_SECTION + 2),
                  col(0), col(1), col(2), col(0), col(1), col(2)],
        out_specs=pl.BlockSpec((seq, LANES), lambda b, hp: (b, hp)),
        out_shape=jax.ShapeDtypeStruct((tokens, ATT_WIDTH), BF),
        scratch_shapes=[f32_buf] * 9 + [pltpu.VMEM((2, 2 * blk, 2 * blk), F32)],
        compiler_params=_cparams(("arbitrary", "arbitrary")),
        name="dilated_attn",
    )(z, z, z, zd4, zd4, zd4, zd16, zd16, zd16)


def _mix_kernel(ya_ref, yb_ref, ga_ref, gb_ref, x_ref, woa_ref, wob_ref, wout_ref,
                g_ref, o_ref):
    pa = _dot(ya_ref[...], woa_ref[...])
    pb = _dot(yb_ref[...], wob_ref[...])
    mixed = (jax.nn.sigmoid(ga_ref[...].astype(F32)) * pa
             + jax.nn.sigmoid(gb_ref[...].astype(F32)) * pb)
    y = _dot(mixed.astype(BF), wout_ref[...])
    o_ref[...] = x_ref[...] + _rms(y, g_ref[...])


def _mix(ya, yb, z, x2d, woa, wob, wout, g, *, tm):
    tokens = x2d.shape[0]
    gate_blk = lambda off: pl.BlockSpec((tm, D_MODEL), lambda i: (i, off))
    const = lambda i: (0, 0)
    return pl.pallas_call(
        _mix_kernel,
        grid=(tokens // tm,),
        in_specs=[
            pl.BlockSpec((tm, ATT_WIDTH), lambda i: (i, 0)),
            pl.BlockSpec((tm, ATT_WIDTH), lambda i: (i, 0)),
            gate_blk(3),
            gate_blk(4),
            pl.BlockSpec((tm, D_MODEL), lambda i: (i, 0)),
            pl.BlockSpec((ATT_WIDTH, D_MODEL), const),
            pl.BlockSpec((ATT_WIDTH, D_MODEL), const),
            pl.BlockSpec((D_MODEL, D_MODEL), const),
            pl.BlockSpec((1, D_MODEL), const),
        ],
        out_specs=pl.BlockSpec((tm, D_MODEL), lambda i: (i, 0)),
        out_shape=jax.ShapeDtypeStruct((tokens, D_MODEL), F32),
        compiler_params=_cparams(("arbitrary",)),
        name="mix",
    )(ya, yb, z, z, x2d, woa, wob, wout, g)


def _ffn_kernel(x_ref, halo_ref, gpre_ref, wup_ref, cw_ref, cb_ref, wd_ref, gpost_ref, o_ref,
                h_scr, ua0_scr, ub0_scr, ua1_scr, ub1_scr, acc_scr,
                *, tm, tf, tiles_per_seq, n_chunks):
    i = pl.program_id(0)
    halo = BF16_ROWS
    rows = tm // n_chunks
    n_tiles = D_FF // tf
    u_sets = ((ua0_scr, ub0_scr), (ua1_scr, ub1_scr))

    def cols(f, gate):
        return pl.ds(pl.multiple_of(f * tf + (D_FF if gate else 0), LANES), tf)

    def up_proj(f, dst, lo, hi):
        h = h_scr[lo:hi, :]
        dst[0][lo:hi, :] = _dot(h, wup_ref[:, cols(f, False)])
        dst[1][lo:hi, :] = _dot(h, wup_ref[:, cols(f, True)])

    def conv(u_scr, window, post_scale, first, n):
        out = cb_ref[:, window] * post_scale
        for tap in range(CONV_WIDTH):
            lo = halo + first - (CONV_WIDTH - 1) + tap
            out = out + (cw_ref[tap:tap + 1, window] * post_scale) * u_scr[lo:lo + n, :]
        return out

    def down_proj(f, src, first, n):
        a = conv(src[0], cols(f, False), 1.0, first, n)
        half_b = conv(src[1], cols(f, True), 0.5, first, n)
        c0 = math.sqrt(2.0 / math.pi)
        inner = a * (c0 + (c0 * 0.044715) * (a * a))
        hidden = ((a * half_b) * (1.0 + jnp.tanh(inner))).astype(BF)
        return _dot(hidden, wd_ref[pl.ds(pl.multiple_of(f * tf, tf), tf), :])

    def stage(f, parity):
        src, dst = u_sets[1 - parity], u_sets[parity]
        for c in range(n_chunks):
            lo = 0 if c == 0 else halo + c * rows
            up_proj(f, dst, lo, halo + (c + 1) * rows)
            out_rows = slice(c * rows, (c + 1) * rows)
            acc_scr[out_rows, :] += down_proj(f - 1, src, c * rows, rows)

    g = gpre_ref[...]
    hh = _rms(halo_ref[...], g)
    hh = jnp.where(i % tiles_per_seq == 0, jnp.zeros_like(hh), hh)
    h_scr[0:halo, :] = hh.astype(BF)
    h_scr[halo:, :] = _rms(x_ref[...], g).astype(BF)
    acc_scr[...] = jnp.zeros_like(acc_scr)
    up_proj(0, u_sets[0], 0, tm + halo)

    def pair(k, carry):
        stage(2 * k + 1, 1)
        stage(2 * k + 2, 0)
        return carry

    assert n_tiles % 2 == 1
    lax.fori_loop(0, (n_tiles - 1) // 2, pair, 0)
    y = acc_scr[...] + down_proj(n_tiles - 1, u_sets[0], 0, tm)
    o_ref[...] = x_ref[...] + _rms(y, gpost_ref[...])


def _ffn(x1, g_pre, w_up, conv_w, conv_b, w_down, g_post, *, seq, tm, tf):
    tokens = x1.shape[0]
    halo = BF16_ROWS
    per = tm // halo
    resident = lambda shape: pl.BlockSpec(shape, lambda i: (0, 0),
                                          pipeline_mode=pl.Buffered(1))
    u_buf = pltpu.VMEM((tm + halo, tf), F32)
    return pl.pallas_call(
        functools.partial(_ffn_kernel, tm=tm, tf=tf, tiles_per_seq=seq // tm, n_chunks=4),
        grid=(tokens // tm,),
        in_specs=[
            pl.BlockSpec((tm, D_MODEL), lambda i: (i, 0)),
            pl.BlockSpec((halo, D_MODEL), lambda i: (jnp.maximum(i * per - 1, 0), 0)),
            resident((1, D_MODEL)),
            resident((D_MODEL, 2 * D_FF)),
            resident((CONV_WIDTH, 2 * D_FF)),
            resident((1, 2 * D_FF)),
            resident((D_FF, D_MODEL)),
            resident((1, D_MODEL)),
        ],
        out_specs=pl.BlockSpec((tm, D_MODEL), lambda i: (i, 0)),
        out_shape=jax.ShapeDtypeStruct((tokens, D_MODEL), F32),
        scratch_shapes=[
            pltpu.VMEM((tm + halo, D_MODEL), BF),
            u_buf, u_buf, u_buf, u_buf,
            pltpu.VMEM((tm, D_MODEL), F32),
        ],
        compiler_params=_cparams(("arbitrary",)),
        name="ffn",
    )(x1, x1, g_pre, w_up, conv_w, conv_b, w_down, g_post)


def _rope_freq_lanes():
    inv_freq = ROPE_THETA ** (-jnp.arange(ROPE_HALF, dtype=F32) * 2.0 / ROPE_DIM)
    lane = np.arange(LANES) % HEAD_DIM
    return inv_freq[lane % ROPE_HALF].reshape(1, LANES)


def kernel(x, g_pre_mix, w_in, b_forget, w_o_fox, w_o_dil, w_out, g_post_mix,
           g_pre_ffn, w_up, conv_w, conv_b, w_down, g_post_ffn):
    batch, seq, d_model = x.shape
    assert d_model == D_MODEL and seq % 1024 == 0
    depth = w_in.shape[0]
    fox_end = 3 * ATT_WIDTH
    freq_lanes = _rope_freq_lanes()
    x2d = x.reshape(batch * seq, D_MODEL)
    row = lambda v: v.reshape(1, -1)
    for l in range(depth):
        w_main = jnp.concatenate(
            [w_in[l][:, :fox_end], w_in[l][:, fox_end + N_HEADS:]], axis=1).astype(BF)
        w_f = jnp.pad(w_in[l][:, fox_end:fox_end + N_HEADS],
                      ((0, 0), (0, LANES - N_HEADS))).astype(BF)
        bias_lanes = jnp.pad(b_forget[l], (0, LANES - N_HEADS)).reshape(1, LANES)

        z, fa, zd4, zd16 = _in_proj(x2d, row(g_pre_mix[l]), w_main, w_f, freq_lanes,
                                    batch=batch, seq=seq, tm=DIL_CHUNK)
        gq, gk = _forget_scan(fa, bias_lanes, batch=batch, seq=seq, chunk=512)
        ya = _fox_attention(z, gq, gk, batch=batch, seq=seq, tq=512)
        yb = _dilated_attention(z, zd4, zd16, batch=batch, seq=seq)
        x2d = _mix(ya, yb, z, x2d, w_o_fox[l].astype(BF), w_o_dil[l].astype(BF),
                   w_out[l].astype(BF), row(g_post_mix[l]), tm=512)
        x2d = _ffn(x2d, row(g_pre_ffn[l]), w_up[l].astype(BF), conv_w[l],
                   row(conv_b[l]), w_down[l].astype(BF), row(g_post_ffn[l]),
                   seq=seq, tm=1024, tf=256)
    return x2d.reshape(batch, seq, D_MODEL)
```

```python
import functools
import math

import numpy as np
import jax
import jax.numpy as jnp
from jax import lax
from jax.experimental import pallas as pl
from jax.experimental.pallas import tpu as pltpu

D_MODEL = 1024
HEAD_DIM = 64
N_HEADS = 8
ATT_WIDTH = N_HEADS * HEAD_DIM
N_HEAD_PAIRS = N_HEADS // 2
DIL_PATTERNS = ((128, 1), (512, 4), (2048, 16))
ROPE_DIM = HEAD_DIM // 4
ROPE_HALF = ROPE_DIM // 2
ROPE_THETA = 500000.0
D_FF = 2816
CONV_WIDTH = 3
RMS_EPS = 1e-6
NEG_INF = -1e30
Q_SCALE = 1.0 / math.sqrt(HEAD_DIM)
LOG2_E = math.log2(math.e)
UNDERFLOW_BITS = 150.0
SCORE_MARGIN = 1.0

LANES = 128
BF16_ROWS = 16
Z_WIDTH = 3 * ATT_WIDTH * 2 + 2 * D_MODEL
Z_BLK = 512
DIL_SECTION = 3
DIL_CHUNK = 512
N_PIECES = 3
VMEM_LIMIT = 56 * 1024 * 1024

BF = jnp.bfloat16
F32 = jnp.float32


def _cparams(sem, flags=None):
    return pltpu.CompilerParams(dimension_semantics=sem, vmem_limit_bytes=VMEM_LIMIT,
                                flags=flags)


def _rms(xf, g):
    inv = lax.rsqrt(jnp.mean(xf * xf, axis=-1, keepdims=True) + RMS_EPS)
    return xf * inv * g


def _split3(x):
    hi = x.astype(BF)
    r1 = x - hi.astype(F32)
    mid = r1.astype(BF)
    lo = (r1 - mid.astype(F32)).astype(BF)
    return hi, mid, lo


def _dot(a, b):
    return jnp.dot(a, b, preferred_element_type=F32)


def _dot_nt(a, b):
    return lax.dot_general(a, b, (((1,), (1,)), ((), ())), preferred_element_type=F32)


def _in_proj_kernel(x_ref, g_ref, w_ref, wf_ref, freq_ref, z_ref, fa_ref, zd4_ref, zd16_ref,
                    h_scr, cos_scr, sneg_scr, spos_scr, stage_scr, stage4_scr, *, tm):
    p = pl.program_id(0)
    b = pl.program_id(1)
    zd_refs = {4: zd4_ref, 16: zd16_ref}
    n_slabs = Z_BLK // LANES

    @pl.when(b == 0)
    def _tables():
        pos = (p * tm + lax.broadcasted_iota(jnp.int32, (tm, LANES), 0)).astype(F32)
        lane = lax.broadcasted_iota(jnp.int32, (tm, LANES), 1)
        c = lane % HEAD_DIM
        ang = pos * freq_ref[...]
        cs = jnp.cos(ang)
        sn = jnp.sin(ang)
        cos_scr[...] = jnp.where(c < ROPE_DIM, cs, 1.0)
        sneg_scr[...] = jnp.where(c < ROPE_HALF, -sn, 0.0)
        spos_scr[...] = jnp.where((c >= ROPE_HALF) & (c < ROPE_DIM), sn, 0.0)

    def rope(t):
        up = pltpu.roll(t, LANES - ROPE_HALF, 1)
        dn = pltpu.roll(t, ROPE_HALF, 1)
        return t * cos_scr[...] + up * sneg_scr[...] + dn * spos_scr[...]

    def emit_dilated(section, slab, val):
        out_lanes = slice(section * Z_BLK + slab * LANES, section * Z_BLK + (slab + 1) * LANES)
        z_ref[:, (DIL_SECTION + section) * Z_BLK + slab * LANES:
              (DIL_SECTION + section) * Z_BLK + (slab + 1) * LANES] = val.astype(BF)
        stage, stage4 = stage_scr.at[section, slab], stage4_scr.at[section, slab]
        stage[...] = val
        seg4 = tm // 4
        for c in range(4):
            part = stage[pl.ds(c, seg4, stride=4), :]
            stage4[c * seg4:(c + 1) * seg4, :] = part
            zd_refs[4][c * seg4:(c + 1) * seg4, out_lanes] = part.astype(BF)
        seg16 = tm // 16
        for jj in range(16):
            c, a = jj % 4, jj // 4
            part = stage4[pl.ds(c * seg4 + a, seg16, stride=4), :]
            zd_refs[16][jj * seg16:(jj + 1) * seg16, out_lanes] = part.astype(BF)

    h = _rms(x_ref[...], g_ref[...]).astype(BF)
    h_scr[...] = h
    fa_ref[...] = _dot(h, wf_ref[...])

    for j in range(Z_WIDTH // Z_BLK):
        cols = slice(j * Z_BLK, (j + 1) * Z_BLK)
        acc = _dot(h_scr[...], w_ref[:, cols])
        section = j - DIL_SECTION
        if j == 0:
            z_ref[:, cols] = (acc * (Q_SCALE * LOG2_E)).astype(BF)
        elif 0 <= section < 3:
            for slab in range(n_slabs):
                t = acc[:, slab * LANES:(slab + 1) * LANES]
                if section == 0:
                    t = rope(t) * (Q_SCALE * LOG2_E)
                elif section == 1:
                    t = rope(t)
                emit_dilated(section, slab, t)
        else:
            z_ref[:, cols] = acc.astype(BF)


def _in_proj(x2d, g, w_main, w_f, freq_lanes, *, batch, seq, tm):
    n_p = seq // tm
    tokens = batch * seq
    row = lambda p, b: (b * n_p + p, 0)
    resident = lambda shape: pl.BlockSpec(shape, lambda p, b: (0, 0),
                                          pipeline_mode=pl.Buffered(1))
    dil_shape = jax.ShapeDtypeStruct((tokens, 3 * ATT_WIDTH), BF)
    stage_buf = pltpu.VMEM((3, Z_BLK // LANES, tm, LANES), F32)
    return pl.pallas_call(
        functools.partial(_in_proj_kernel, tm=tm),
        grid=(n_p, batch),
        in_specs=[
            pl.BlockSpec((tm, D_MODEL), row),
            resident((1, D_MODEL)),
            resident((D_MODEL, Z_WIDTH)),
            resident((D_MODEL, LANES)),
            resident((1, LANES)),
        ],
        out_specs=[
            pl.BlockSpec((tm, Z_WIDTH), row),
            pl.BlockSpec((tm, LANES), row),
            pl.BlockSpec((tm, 3 * ATT_WIDTH), row),
            pl.BlockSpec((tm, 3 * ATT_WIDTH), row),
        ],
        out_shape=[
            jax.ShapeDtypeStruct((tokens, Z_WIDTH), BF),
            jax.ShapeDtypeStruct((tokens, LANES), F32),
            dil_shape,
            dil_shape,
        ],
        scratch_shapes=[
            pltpu.VMEM((tm, D_MODEL), BF),
            pltpu.VMEM((tm, LANES), F32),
            pltpu.VMEM((tm, LANES), F32),
            pltpu.VMEM((tm, LANES), F32),
            stage_buf,
            stage_buf,
        ],
        compiler_params=_cparams(("arbitrary", "arbitrary")),
        name="in_proj",
    )(x2d, g, w_main, w_f, freq_lanes)


def _extras_base(head):
    return (HEAD_DIM if head % 2 == 0 else 0) + 2 * N_PIECES * (head // 2)


def _forget_scan_kernel(fa_ref, bias_ref, tri_ref, pq_ref, pk_ref, oq_ref, ok_ref,
                        gq_ref, gk_ref, carry_scr):
    c = pl.program_id(1)

    @pl.when(c == 0)
    def _():
        carry_scr[...] = jnp.zeros_like(carry_scr)

    t = fa_ref[...] + bias_ref[...]
    log_f = jnp.minimum(t, 0.0) - jnp.log1p(jnp.exp(-jnp.abs(t)))
    tri = tri_ref[...]
    run = carry_scr[...]
    for piece in _split3(log_f):
        run = run + _dot(tri, piece)
    rows = run.shape[0]
    carry_scr[...] = run[rows - 1:rows, :]
    gq = oq_ref[...].astype(F32)
    gk = ok_ref[...].astype(F32)
    for i, piece in enumerate(_split3(run * LOG2_E)):
        gq = gq + _dot(piece, pq_ref[i])
        gk = gk - _dot(piece, pk_ref[i])
    gq_ref[0] = gq.astype(BF)
    gk_ref[0] = gk.astype(BF)


def _forget_scan_constants(chunk):
    tri = np.tril(np.ones((chunk, chunk), np.float32))
    pq = np.zeros((N_PIECES, LANES, LANES), np.float32)
    pk = np.zeros((N_PIECES, LANES, LANES), np.float32)
    oq = np.zeros((1, LANES), np.float32)
    ok = np.zeros((1, LANES), np.float32)
    for h in range(N_HEADS):
        base = _extras_base(h)
        for i in range(N_PIECES):
            pq[i, h, base + i] = 1.0
            oq[0, base + N_PIECES + i] = 1.0
            ok[0, base + i] = 1.0
            pk[i, h, base + N_PIECES + i] = 1.0
    as_bf = lambda a: jnp.asarray(a, BF)
    return as_bf(tri), as_bf(pq), as_bf(pk), as_bf(oq), as_bf(ok)


def _forget_scan(fa, bias_lanes, *, batch, seq, chunk):
    n_c = seq // chunk
    tri, pq, pk, oq, ok = _forget_scan_constants(chunk)
    const2 = lambda b, c: (0, 0)
    const3 = lambda b, c: (0, 0, 0)
    return pl.pallas_call(
        _forget_scan_kernel,
        grid=(batch, n_c),
        in_specs=[
            pl.BlockSpec((chunk, LANES), lambda b, c: (b * n_c + c, 0)),
            pl.BlockSpec((1, LANES), const2),
            pl.BlockSpec((chunk, chunk), const2),
            pl.BlockSpec((N_PIECES, LANES, LANES), const3),
            pl.BlockSpec((N_PIECES, LANES, LANES), const3),
            pl.BlockSpec((1, LANES), const2),
            pl.BlockSpec((1, LANES), const2),
        ],
        out_specs=[
            pl.BlockSpec((1, chunk, LANES), lambda b, c: (b, c, 0)),
            pl.BlockSpec((1, chunk, LANES), lambda b, c: (b, c, 0)),
        ],
        out_shape=[
            jax.ShapeDtypeStruct((batch, seq, LANES), BF),
            jax.ShapeDtypeStruct((batch, seq, LANES), BF),
        ],
        scratch_shapes=[pltpu.VMEM((1, LANES), F32)],
        compiler_params=_cparams(("arbitrary", "arbitrary")),
        name="forget_scan",
    )(fa, bias_lanes, tri, pq, pk, oq, ok)


def _with_bias_lanes(slab, extras, hp, parity):
    lane = lax.broadcasted_iota(jnp.int32, slab.shape, 1)
    own = (lane < HEAD_DIM) if parity == 0 else (lane >= HEAD_DIM)
    base = (HEAD_DIM if parity == 0 else 0) + 2 * N_PIECES * hp
    in_extras = (lane >= base) & (lane < base + 2 * N_PIECES)
    return jnp.where(own, slab, jnp.where(in_extras, extras, jnp.zeros_like(extras)))


def _fox_kernel(q_ref, k_ref, v_ref, gq_ref, gk_ref, o_ref,
                ka_scr, kb_scr, vta_scr, vtb_scr, bias_scr, s_scr, acc_scr, ot_scr, bound_smem,
                *, tq):
    hp = pl.program_id(1)
    qi = pl.program_id(2)
    k_scrs = (ka_scr, kb_scr)
    vt_scrs = (vta_scr, vtb_scr)
    seq = k_ref.shape[0]
    n_q = seq // tq
    lane_row = lax.broadcasted_iota(jnp.int32, (1, LANES), 1)

    def own_lanes(head, shape):
        lane = lax.broadcasted_iota(jnp.int32, shape, 1)
        return (lane < HEAD_DIM) if head == 0 else (lane >= HEAD_DIM)

    def max_row_norm(slab, head):
        x = jnp.where(own_lanes(head, slab.shape), slab.astype(F32), 0.0)
        sq = jnp.sum(x * x, axis=1, keepdims=True)
        return jnp.sqrt(jnp.max(sq, axis=0, keepdims=True))

    def bias_lane_sum(row, head, first):
        base = (HEAD_DIM if head == 0 else 0) + 2 * N_PIECES * hp + first
        picked = (lane_row >= base) & (lane_row < base + N_PIECES)
        return jnp.sum(jnp.where(picked, row, 0.0), axis=1, keepdims=True)

    @pl.when(qi == 0)
    def _prep():
        k2 = k_ref[...]
        gk = gk_ref[0]
        for head in range(2):
            k_scrs[head][...] = _with_bias_lanes(k2, gk, hp, head)
            bound_smem[head, n_q] = max_row_norm(k2, head)[0, 0]
            for j in range(n_q):
                rows16 = gk_ref[0, (j + 1) * tq - BF16_ROWS:(j + 1) * tq, :].astype(F32)
                last = rows16[BF16_ROWS - 1:BF16_ROWS, :]
                bound_smem[head, j] = -bias_lane_sum(last, head, N_PIECES)[0, 0]
        vt = v_ref[...].astype(F32).T.astype(BF)
        ones = jnp.ones((BF16_ROWS, seq), BF)
        for head, vt_scr in enumerate(vt_scrs):
            vt_scr[0:HEAD_DIM, :] = vt[head * HEAD_DIM:(head + 1) * HEAD_DIM, :]
            vt_scr[HEAD_DIM:, :] = ones
        row = lax.broadcasted_iota(jnp.int32, (tq, tq), 0)
        col = lax.broadcasted_iota(jnp.int32, (tq, tq), 1)
        bias_scr[...] = jnp.where(row <= col, 0.0, NEG_INF)

    q2 = q_ref[...]
    gq = gq_ref[0]
    qps = [_with_bias_lanes(q2, gq, hp, head) for head in range(2)]

    def scores(head, start, masked):
        kt = k_scrs[head][pl.ds(start, tq), :]
        st = _dot_nt(kt, qps[head])
        if masked:
            st = st + bias_scr[...]
        s_scr[head] = st
        return jnp.max(st, axis=0, keepdims=True)

    def softmax_pv(head, st, tile, tile_max, m):
        start = pl.multiple_of(tile * tq, tq)
        m_new = jnp.maximum(m, tile_max)
        alpha = jnp.exp2(m - m_new)
        pt = jnp.exp2(st - m_new).astype(BF)
        vt = vt_scrs[head][:, pl.ds(start, tq)]
        acc_scr[head] = alpha * acc_scr[head] + _dot(vt, pt)
        return m_new

    acc_scr[...] = jnp.zeros_like(acc_scr)
    diag = pl.multiple_of(qi * tq, tq)
    first_max = [scores(head, diag, True) for head in range(2)]
    m_init = jnp.full((1, tq), NEG_INF, F32)

    first_row = gq_ref[0, 0:BF16_ROWS, :].astype(F32)[0:1, :]
    count = jnp.zeros((1, 1), jnp.int32)
    slack = []
    for head in range(2):
        reach = (max_row_norm(q2, head) * bound_smem[head, n_q]
                 + bias_lane_sum(first_row, head, 0) + SCORE_MARGIN)
        lowest_max = jnp.min(first_max[head], axis=1, keepdims=True)
        slack.append(reach - (lowest_max - UNDERFLOW_BITS))
    for j in range(n_q - 1):
        needed = ((slack[0] >= bound_smem[0, j]) | (slack[1] >= bound_smem[1, j])) & (j < qi)
        count = count + jnp.where(needed, 1, 0)
    n_needed = count[0, 0]

    def body(t, carry):
        max_a, max_b, m_a, m_b, prev_tile = carry
        prev = [s_scr[head] for head in range(2)]
        tile = qi - 1 - t
        start = pl.multiple_of(tile * tq, tq)
        next_max = [scores(head, start, False) for head in range(2)]
        m_a = softmax_pv(0, prev[0], prev_tile, max_a, m_a)
        m_b = softmax_pv(1, prev[1], prev_tile, max_b, m_b)
        return next_max[0], next_max[1], m_a, m_b, tile

    max_a, max_b, m_a, m_b, prev_tile = lax.fori_loop(
        0, n_needed, body, (first_max[0], first_max[1], m_init, m_init, qi))
    softmax_pv(0, s_scr[0], prev_tile, max_a, m_a)
    softmax_pv(1, s_scr[1], prev_tile, max_b, m_b)

    for head in range(2):
        acc = acc_scr[head]
        ot_scr[head * HEAD_DIM:(head + 1) * HEAD_DIM, :] = (
            acc[0:HEAD_DIM, :] / acc[HEAD_DIM:HEAD_DIM + 1, :])
    o_ref[...] = ot_scr[...].T.astype(BF)


def _fox_attention(z, gq, gk, *, batch, seq, tq):
    n_q = seq // tq
    tokens = batch * seq
    return pl.pallas_call(
        functools.partial(_fox_kernel, tq=tq),
        grid=(batch, N_HEAD_PAIRS, n_q),
        in_specs=[
            pl.BlockSpec((tq, LANES), lambda b, hp, qi: (b * n_q + qi, hp)),
            pl.BlockSpec((seq, LANES), lambda b, hp, qi: (b, N_HEAD_PAIRS + hp)),
            pl.BlockSpec((seq, LANES), lambda b, hp, qi: (b, 2 * N_HEAD_PAIRS + hp)),
            pl.BlockSpec((1, tq, LANES), lambda b, hp, qi: (b, qi, 0)),
            pl.BlockSpec((1, seq, LANES), lambda b, hp, qi: (b, 0, 0)),
        ],
        out_specs=pl.BlockSpec((tq, LANES), lambda b, hp, qi: (b * n_q + qi, hp)),
        out_shape=jax.ShapeDtypeStruct((tokens, ATT_WIDTH), BF),
        scratch_shapes=[
            pltpu.VMEM((seq, LANES), BF),
            pltpu.VMEM((seq, LANES), BF),
            pltpu.VMEM((HEAD_DIM + BF16_ROWS, seq), BF),
            pltpu.VMEM((HEAD_DIM + BF16_ROWS, seq), BF),
            pltpu.VMEM((tq, tq), F32),
            pltpu.VMEM((2, tq, tq), F32),
            pltpu.VMEM((2, HEAD_DIM + BF16_ROWS, tq), F32),
            pltpu.VMEM((LANES, tq), F32),
            pltpu.SMEM((2, n_q + 1), F32),
        ],
        compiler_params=_cparams(("arbitrary", "arbitrary", "arbitrary")),
        name="fox_attn",
    )(z, z, z, gq, gk)


def _log2(n):
    assert n > 0 and n & (n - 1) == 0, n
    return n.bit_length() - 1


def _dilated_kernel(q1, k1, v1, q4, k4, v4, q16, k16, v16, o_ref,
                    num1, num4, num16, den1, den4, den16, max1, max4, max16, bias_scr,
                    *, seq, blk, chunk, group, merge_rows):
    sources = ((q1, k1, v1), (q4, k4, v4), (q16, k16, v16))
    num_scrs = (num1, num4, num16)
    den_scrs = (den1, den4, den16)
    max_scrs = (max1, max4, max16)
    n_blocks = seq // blk

    low = lax.broadcasted_iota(jnp.int32, (blk, LANES), 1) < HEAD_DIM
    qrow = lax.broadcasted_iota(jnp.int32, (2 * blk, 2 * blk), 0) % blk
    kcol = lax.broadcasted_iota(jnp.int32, (2 * blk, 2 * blk), 1)
    dist = qrow + blk - kcol
    band = (dist >= 0) & (dist <= blk)
    bias_scr[0] = jnp.where(band & (kcol >= blk), 0.0, NEG_INF)
    bias_scr[1] = jnp.where(band, 0.0, NEG_INF)
    ones = jnp.ones((2 * blk, LANES), BF)

    for idx, (window, dilation) in enumerate(DIL_PATTERNS):
        assert window // dilation == blk
        q_ref, k_ref, v_ref = sources[idx]
        num_scr, den_scr, max_scr = num_scrs[idx], den_scrs[idx], max_scrs[idx]
        sub_shift = _log2(seq // dilation // blk)
        seg_rows = chunk // dilation
        seg = min(blk, seg_rows)

        def load(ref, j, l0):
            parts = []
            for s in range(blk // seg):
                l = l0 + s * seg
                p = lax.shift_right_logical(l, _log2(seg_rows))
                i = l & (seg_rows - 1)
                start = pl.multiple_of(p * chunk + j * seg_rows + i, seg)
                parts.append(ref[pl.ds(start, seg), :])
            return parts[0] if len(parts) == 1 else jnp.concatenate(parts, axis=0)

        def one_block(g):
            j = lax.shift_right_logical(g, sub_shift)
            gs = g & ((1 << sub_shift) - 1)
            l0 = gs * blk
            lp = jnp.maximum(l0 - blk, 0)
            q2 = load(q_ref, j, l0)
            zero = jnp.zeros_like(q2)
            qq = jnp.concatenate([jnp.where(low, q2, zero), jnp.where(low, zero, q2)], axis=0)
            kwin = jnp.concatenate([load(k_ref, j, lp), load(k_ref, j, l0)], axis=0)
            vwin = jnp.concatenate([load(v_ref, j, lp), load(v_ref, j, l0)], axis=0)
            s = _dot_nt(qq, kwin) + bias_scr[jnp.minimum(gs, 1)]
            m = jnp.max(s, axis=1, keepdims=True)
            p = jnp.exp2(s - m).astype(BF)
            pv = _dot(p, jnp.concatenate([vwin, ones], axis=1))
            if dilation == 1:
                dst = pl.ds(pl.multiple_of(l0, blk), blk)
            else:
                dst = pl.ds(l0 * dilation + j, blk, stride=dilation)
            num_scr[dst, :] = jnp.where(low, pv[:blk, :LANES], pv[blk:, :LANES])
            den_scr[dst, :] = jnp.where(low, pv[:blk, LANES:], pv[blk:, LANES:])
            max_scr[dst, :] = jnp.where(low, jnp.broadcast_to(m[:blk], (blk, LANES)),
                                        jnp.broadcast_to(m[blk:], (blk, LANES)))

        def blocks(it, carry):
            for u in range(group):
                one_block(it * group + u)
            return carry

        lax.fori_loop(0, n_blocks // group, blocks, 0)

    def merge(c, carry):
        rows = pl.ds(pl.multiple_of(c * merge_rows, merge_rows), merge_rows)
        maxes = [max_scr[rows, :] for max_scr in max_scrs]
        top = functools.reduce(jnp.maximum, maxes)
        weights = [jnp.exp2(m - top) for m in maxes]
        num = sum(w * num_scr[rows, :] for w, num_scr in zip(weights, num_scrs))
        den = sum(w * den_scr[rows, :] for w, den_scr in zip(weights, den_scrs))
        o_ref[rows, :] = (num / den).astype(BF)
        return carry

    lax.fori_loop(0, seq // merge_rows, merge, 0)


def _dilated_attention(z, zd4, zd16, *, batch, seq):
    blk = DIL_PATTERNS[0][0] // DIL_PATTERNS[0][1]
    tokens = batch * seq
    per_section = ATT_WIDTH // LANES
    col = lambda section: pl.BlockSpec(
        (seq, LANES), lambda b, hp: (b, section * per_section + hp))
    f32_buf = pltpu.VMEM((seq, LANES), F32)
    return pl.pallas_call(
        functools.partial(_dilated_kernel, seq=seq, blk=blk, chunk=DIL_CHUNK, group=4,
                          merge_rows=256),
        grid=(batch, N_HEAD_PAIRS),
        in_specs=[col(DIL_SECTION), col(DIL_SECTION + 1), col(DIL_SECTION + 2),
                  col(0), col(1), col(2), col(0), col(1), col(2)],
        out_specs=pl.BlockSpec((seq, LANES), lambda b, hp: (b, hp)),
        out_shape=jax.ShapeDtypeStruct((tokens, ATT_WIDTH), BF),
        scratch_shapes=[f32_buf] * 9 + [pltpu.VMEM((2, 2 * blk, 2 * blk), F32)],
        compiler_params=_cparams(("arbitrary", "arbitrary")),
        name="dilated_attn",
    )(z, z, z, zd4, zd4, zd4, zd16, zd16, zd16)


def _mix_kernel(ya_ref, yb_ref, ga_ref, gb_ref, x_ref, woa_ref, wob_ref, wout_ref,
                g_ref, o_ref):
    pa = _dot(ya_ref[...], woa_ref[...])
    pb = _dot(yb_ref[...], wob_ref[...])
    mixed = (jax.nn.sigmoid(ga_ref[...].astype(F32)) * pa
             + jax.nn.sigmoid(gb_ref[...].astype(F32)) * pb)
    y = _dot(mixed.astype(BF), wout_ref[...])
    o_ref[...] = x_ref[...] + _rms(y, g_ref[...])


def _mix(ya, yb, z, x2d, woa, wob, wout, g, *, tm):
    tokens = x2d.shape[0]
    gate_blk = lambda off: pl.BlockSpec((tm, D_MODEL), lambda i: (i, off))
    const = lambda i: (0, 0)
    return pl.pallas_call(
        _mix_kernel,
        grid=(tokens // tm,),
        in_specs=[
            pl.BlockSpec((tm, ATT_WIDTH), lambda i: (i, 0)),
            pl.BlockSpec((tm, ATT_WIDTH), lambda i: (i, 0)),
            gate_blk(3),
            gate_blk(4),
            pl.BlockSpec((tm, D_MODEL), lambda i: (i, 0)),
            pl.BlockSpec((ATT_WIDTH, D_MODEL), const),
            pl.BlockSpec((ATT_WIDTH, D_MODEL), const),
            pl.BlockSpec((D_MODEL, D_MODEL), const),
            pl.BlockSpec((1, D_MODEL), const),
        ],
        out_specs=pl.BlockSpec((tm, D_MODEL), lambda i: (i, 0)),
        out_shape=jax.ShapeDtypeStruct((tokens, D_MODEL), F32),
        compiler_params=_cparams(("arbitrary",)),
        name="mix",
    )(ya, yb, z, z, x2d, woa, wob, wout, g)


def _ffn_kernel(x_ref, halo_ref, gpre_ref, wup_ref, cw_ref, cb_ref, wd_ref, gpost_ref, o_ref,
                h_scr, ua0_scr, ub0_scr, ua1_scr, ub1_scr, acc_scr,
                *, tm, tf, tiles_per_seq, n_chunks):
    i = pl.program_id(0)
    halo = BF16_ROWS
    rows = tm // n_chunks
    n_tiles = D_FF // tf
    u_sets = ((ua0_scr, ub0_scr), (ua1_scr, ub1_scr))

    def cols(f, gate):
        return pl.ds(pl.multiple_of(f * tf + (D_FF if gate else 0), LANES), tf)

    def up_proj(f, dst, lo, hi):
        h = h_scr[lo:hi, :]
        dst[0][lo:hi, :] = _dot(h, wup_ref[:, cols(f, False)])
        dst[1][lo:hi, :] = _dot(h, wup_ref[:, cols(f, True)])

    def conv(u_scr, window, post_scale, first, n):
        out = cb_ref[:, window] * post_scale
        for tap in range(CONV_WIDTH):
            lo = halo + first - (CONV_WIDTH - 1) + tap
            out = out + (cw_ref[tap:tap + 1, window] * post_scale) * u_scr[lo:lo + n, :]
        return out

    def down_proj(f, src, first, n):
        a = conv(src[0], cols(f, False), 1.0, first, n)
        half_b = conv(src[1], cols(f, True), 0.5, first, n)
        c0 = math.sqrt(2.0 / math.pi)
        inner = a * (c0 + (c0 * 0.044715) * (a * a))
        hidden = ((a * half_b) * (1.0 + jnp.tanh(inner))).astype(BF)
        return _dot(hidden, wd_ref[pl.ds(pl.multiple_of(f * tf, tf), tf), :])

    def stage(f, parity):
        src, dst = u_sets[1 - parity], u_sets[parity]
        for c in range(n_chunks):
            lo = 0 if c == 0 else halo + c * rows
            up_proj(f, dst, lo, halo + (c + 1) * rows)
            out_rows = slice(c * rows, (c + 1) * rows)
            acc_scr[out_rows, :] += down_proj(f - 1, src, c * rows, rows)

    g = gpre_ref[...]
    hh = _rms(halo_ref[...], g)
    hh = jnp.where(i % tiles_per_seq == 0, jnp.zeros_like(hh), hh)
    h_scr[0:halo, :] = hh.astype(BF)
    h_scr[halo:, :] = _rms(x_ref[...], g).astype(BF)
    acc_scr[...] = jnp.zeros_like(acc_scr)
    up_proj(0, u_sets[0], 0, tm + halo)

    def pair(k, carry):
        stage(2 * k + 1, 1)
        stage(2 * k + 2, 0)
        return carry

    assert n_tiles % 2 == 1
    lax.fori_loop(0, (n_tiles - 1) // 2, pair, 0)
    y = acc_scr[...] + down_proj(n_tiles - 1, u_sets[0], 0, tm)
    o_ref[...] = x_ref[...] + _rms(y, gpost_ref[...])


def _ffn(x1, g_pre, w_up, conv_w, conv_b, w_down, g_post, *, seq, tm, tf):
    tokens = x1.shape[0]
    halo = BF16_ROWS
    per = tm // halo
    resident = lambda shape: pl.BlockSpec(shape, lambda i: (0, 0),
                                          pipeline_mode=pl.Buffered(1))
    u_buf = pltpu.VMEM((tm + halo, tf), F32)
    return pl.pallas_call(
        functools.partial(_ffn_kernel, tm=tm, tf=tf, tiles_per_seq=seq // tm, n_chunks=4),
        grid=(tokens // tm,),
        in_specs=[
            pl.BlockSpec((tm, D_MODEL), lambda i: (i, 0)),
            pl.BlockSpec((halo, D_MODEL), lambda i: (jnp.maximum(i * per - 1, 0), 0)),
            resident((1, D_MODEL)),
            resident((D_MODEL, 2 * D_FF)),
            resident((CONV_WIDTH, 2 * D_FF)),
            resident((1, 2 * D_FF)),
            resident((D_FF, D_MODEL)),
            resident((1, D_MODEL)),
        ],
        out_specs=pl.BlockSpec((tm, D_MODEL), lambda i: (i, 0)),
        out_shape=jax.ShapeDtypeStruct((tokens, D_MODEL), F32),
        scratch_shapes=[
            pltpu.VMEM((tm + halo, D_MODEL), BF),
            u_buf, u_buf, u_buf, u_buf,
            pltpu.VMEM((tm, D_MODEL), F32),
        ],
        compiler_params=_cparams(("arbitrary",)),
        name="ffn",
    )(x1, x1, g_pre, w_up, conv_w, conv_b, w_down, g_post)


def _rope_freq_lanes():
    inv_freq = ROPE_THETA ** (-jnp.arange(ROPE_HALF, dtype=F32) * 2.0 / ROPE_DIM)
    lane = np.arange(LANES) % HEAD_DIM
    return inv_freq[lane % ROPE_HALF].reshape(1, LANES)


def kernel(x, g_pre_mix, w_in, b_forget, w_o_fox, w_o_dil, w_out, g_post_mix,
           g_pre_ffn, w_up, conv_w, conv_b, w_down, g_post_ffn):
    batch, seq, d_model = x.shape
    assert d_model == D_MODEL and seq % 1024 == 0
    depth = w_in.shape[0]
    fox_end = 3 * ATT_WIDTH
    freq_lanes = _rope_freq_lanes()
    x2d = x.reshape(batch * seq, D_MODEL)
    row = lambda v: v.reshape(1, -1)
    for l in range(depth):
        w_main = jnp.concatenate(
            [w_in[l][:, :fox_end], w_in[l][:, fox_end + N_HEADS:]], axis=1).astype(BF)
        w_f = jnp.pad(w_in[l][:, fox_end:fox_end + N_HEADS],
                      ((0, 0), (0, LANES - N_HEADS))).astype(BF)
        bias_lanes = jnp.pad(b_forget[l], (0, LANES - N_HEADS)).reshape(1, LANES)

        z, fa, zd4, zd16 = _in_proj(x2d, row(g_pre_mix[l]), w_main, w_f, freq_lanes,
                                    batch=batch, seq=seq, tm=DIL_CHUNK)
        gq, gk = _forget_scan(fa, bias_lanes, batch=batch, seq=seq, chunk=512)
        ya = _fox_attention(z, gq, gk, batch=batch, seq=seq, tq=512)
        yb = _dilated_attention(z, zd4, zd16, batch=batch, seq=seq)
        x2d = _mix(ya, yb, z, x2d, w_o_fox[l].astype(BF), w_o_dil[l].astype(BF),
                   w_out[l].astype(BF), row(g_post_mix[l]), tm=512)
        x2d = _ffn(x2d, row(g_pre_ffn[l]), w_up[l].astype(BF), conv_w[l],
                   row(conv_b[l]), w_down[l].astype(BF), row(g_post_ffn[l]),
                   seq=seq, tm=1024, tf=256)
    return x2d.reshape(batch, seq, D_MODEL)
```

```python
import functools
import math

import numpy as np
import jax
import jax.numpy as jnp
from jax import lax
from jax.experimental import pallas as pl
from jax.experimental.pallas import tpu as pltpu

D_MODEL = 1024
HEAD_DIM = 64
N_HEADS = 8
ATT_WIDTH = N_HEADS * HEAD_DIM
N_HEAD_PAIRS = N_HEADS // 2
DIL_PATTERNS = ((128, 1), (512, 4), (2048, 16))
ROPE_DIM = HEAD_DIM // 4
ROPE_HALF = ROPE_DIM // 2
ROPE_THETA = 500000.0
D_FF = 2816
CONV_WIDTH = 3
RMS_EPS = 1e-6
NEG_INF = -1e30
Q_SCALE = 1.0 / math.sqrt(HEAD_DIM)
LOG2_E = math.log2(math.e)
UNDERFLOW_BITS = 150.0
SCORE_MARGIN = 1.0

LANES = 128
BF16_ROWS = 16
Z_WIDTH = 3 * ATT_WIDTH * 2 + 2 * D_MODEL
Z_BLK = 512
DIL_SECTION = 3
DIL_CHUNK = 512
N_PIECES = 3
VMEM_LIMIT = 56 * 1024 * 1024

BF = jnp.bfloat16
F32 = jnp.float32


def _cparams(sem, flags=None):
    return pltpu.CompilerParams(dimension_semantics=sem, vmem_limit_bytes=VMEM_LIMIT,
                                flags=flags)


def _rms(xf, g):
    inv = lax.rsqrt(jnp.mean(xf * xf, axis=-1, keepdims=True) + RMS_EPS)
    return xf * inv * g


def _split3(x):
    hi = x.astype(BF)
    r1 = x - hi.astype(F32)
    mid = r1.astype(BF)
    lo = (r1 - mid.astype(F32)).astype(BF)
    return hi, mid, lo


def _dot(a, b):
    return jnp.dot(a, b, preferred_element_type=F32)


def _dot_nt(a, b):
    return lax.dot_general(a, b, (((1,), (1,)), ((), ())), preferred_element_type=F32)


def _in_proj_kernel(x_ref, g_ref, w_ref, freq_ref, z_ref, fa_ref, zd4_ref, zd16_ref,
                    h_scr, cos_scr, sneg_scr, spos_scr, stage_scr, stage4_scr, *, tm):
    p = pl.program_id(0)
    b = pl.program_id(1)
    zd_refs = {4: zd4_ref, 16: zd16_ref}
    n_slabs = Z_BLK // LANES

    @pl.when(b == 0)
    def _tables():
        pos = (p * tm + lax.broadcasted_iota(jnp.int32, (tm, LANES), 0)).astype(F32)
        lane = lax.broadcasted_iota(jnp.int32, (tm, LANES), 1)
        c = lane % HEAD_DIM
        ang = pos * freq_ref[...]
        cs = jnp.cos(ang)
        sn = jnp.sin(ang)
        cos_scr[...] = jnp.where(c < ROPE_DIM, cs, 1.0)
        sneg_scr[...] = jnp.where(c < ROPE_HALF, -sn, 0.0)
        spos_scr[...] = jnp.where((c >= ROPE_HALF) & (c < ROPE_DIM), sn, 0.0)

    def rope(t):
        up = pltpu.roll(t, LANES - ROPE_HALF, 1)
        dn = pltpu.roll(t, ROPE_HALF, 1)
        return t * cos_scr[...] + up * sneg_scr[...] + dn * spos_scr[...]

    def emit_dilated(section, slab, val):
        out_lanes = slice(section * Z_BLK + slab * LANES, section * Z_BLK + (slab + 1) * LANES)
        z_ref[:, (DIL_SECTION + section) * Z_BLK + slab * LANES:
              (DIL_SECTION + section) * Z_BLK + (slab + 1) * LANES] = val.astype(BF)
        stage, stage4 = stage_scr.at[section, slab], stage4_scr.at[section, slab]
        stage[...] = val
        seg4 = tm // 4
        for c in range(4):
            part = stage[pl.ds(c, seg4, stride=4), :]
            stage4[c * seg4:(c + 1) * seg4, :] = part
            zd_refs[4][c * seg4:(c + 1) * seg4, out_lanes] = part.astype(BF)
        seg16 = tm // 16
        for jj in range(16):
            c, a = jj % 4, jj // 4
            part = stage4[pl.ds(c * seg4 + a, seg16, stride=4), :]
            zd_refs[16][jj * seg16:(jj + 1) * seg16, out_lanes] = part.astype(BF)

    h = _rms(x_ref[...], g_ref[...]).astype(BF)
    h_scr[...] = h
    fa_ref[...] = _dot(h, w_ref[:, Z_WIDTH:Z_WIDTH + LANES])

    for j in range(Z_WIDTH // Z_BLK):
        cols = slice(j * Z_BLK, (j + 1) * Z_BLK)
        acc = _dot(h_scr[...], w_ref[:, cols])
        section = j - DIL_SECTION
        if j == 0:
            z_ref[:, cols] = (acc * (Q_SCALE * LOG2_E)).astype(BF)
        elif 0 <= section < 3:
            for slab in range(n_slabs):
                t = acc[:, slab * LANES:(slab + 1) * LANES]
                if section == 0:
                    t = rope(t) * (Q_SCALE * LOG2_E)
                elif section == 1:
                    t = rope(t)
                emit_dilated(section, slab, t)
        else:
            z_ref[:, cols] = acc.astype(BF)


def _in_proj(x2d, g, w_all, freq_lanes, *, batch, seq, tm):
    n_p = seq // tm
    tokens = batch * seq
    row = lambda p, b: (b * n_p + p, 0)
    resident = lambda shape: pl.BlockSpec(shape, lambda p, b: (0, 0),
                                          pipeline_mode=pl.Buffered(1))
    dil_shape = jax.ShapeDtypeStruct((tokens, 3 * ATT_WIDTH), BF)
    stage_buf = pltpu.VMEM((3, Z_BLK // LANES, tm, LANES), F32)
    return pl.pallas_call(
        functools.partial(_in_proj_kernel, tm=tm),
        grid=(n_p, batch),
        in_specs=[
            pl.BlockSpec((tm, D_MODEL), row),
            resident((1, D_MODEL)),
            resident((D_MODEL, Z_WIDTH + LANES)),
            resident((1, LANES)),
        ],
        out_specs=[
            pl.BlockSpec((tm, Z_WIDTH), row),
            pl.BlockSpec((tm, LANES), row),
            pl.BlockSpec((tm, 3 * ATT_WIDTH), row),
            pl.BlockSpec((tm, 3 * ATT_WIDTH), row),
        ],
        out_shape=[
            jax.ShapeDtypeStruct((tokens, Z_WIDTH), BF),
            jax.ShapeDtypeStruct((tokens, LANES), F32),
            dil_shape,
            dil_shape,
        ],
        scratch_shapes=[
            pltpu.VMEM((tm, D_MODEL), BF),
            pltpu.VMEM((tm, LANES), F32),
            pltpu.VMEM((tm, LANES), F32),
            pltpu.VMEM((tm, LANES), F32),
            stage_buf,
            stage_buf,
        ],
        compiler_params=_cparams(("arbitrary", "arbitrary")),
        name="in_proj",
    )(x2d, g, w_all, freq_lanes)


def _extras_base(head):
    return (HEAD_DIM if head % 2 == 0 else 0) + 2 * N_PIECES * (head // 2)


def _forget_scan_kernel(fa_ref, bias_ref, tri_ref, pq_ref, pk_ref, oq_ref, ok_ref,
                        gq_ref, gk_ref):
    chunk = tri_ref.shape[0]
    n_chunks = fa_ref.shape[0] // chunk
    t = fa_ref[...] + bias_ref[...]
    log_f = jnp.minimum(t, 0.0) - jnp.log1p(jnp.exp(-jnp.abs(t)))
    wide = jnp.concatenate(
        [log_f[c * chunk:(c + 1) * chunk, :] for c in range(n_chunks)], axis=1)
    tri = tri_ref[...]
    local = sum(_dot(tri, piece) for piece in _split3(wide))
    carry = jnp.zeros((1, LANES), F32)
    parts = []
    for c in range(n_chunks):
        part = local[:, c * LANES:(c + 1) * LANES] + carry
        parts.append(part)
        carry = part[chunk - 1:chunk, :]
    run = jnp.concatenate(parts, axis=0)
    gq = oq_ref[...].astype(F32)
    gk = ok_ref[...].astype(F32)
    for i, piece in enumerate(_split3(run * LOG2_E)):
        gq = gq + _dot(piece, pq_ref[i])
        gk = gk - _dot(piece, pk_ref[i])
    gq_ref[0] = gq.astype(BF)
    gk_ref[0] = gk.astype(BF)


def _forget_scan_constants(chunk):
    tri = np.tril(np.ones((chunk, chunk), np.float32))
    pq = np.zeros((N_PIECES, LANES, LANES), np.float32)
    pk = np.zeros((N_PIECES, LANES, LANES), np.float32)
    oq = np.zeros((1, LANES), np.float32)
    ok = np.zeros((1, LANES), np.float32)
    for h in range(N_HEADS):
        base = _extras_base(h)
        for i in range(N_PIECES):
            pq[i, h, base + i] = 1.0
            oq[0, base + N_PIECES + i] = 1.0
            ok[0, base + i] = 1.0
            pk[i, h, base + N_PIECES + i] = 1.0
    as_bf = lambda a: jnp.asarray(a, BF)
    return as_bf(tri), as_bf(pq), as_bf(pk), as_bf(oq), as_bf(ok)


def _forget_scan(fa, bias_lanes, *, batch, seq, chunk):
    tri, pq, pk, oq, ok = _forget_scan_constants(chunk)
    const2 = lambda b: (0, 0)
    const3 = lambda b: (0, 0, 0)
    return pl.pallas_call(
        _forget_scan_kernel,
        grid=(batch,),
        in_specs=[
            pl.BlockSpec((seq, LANES), lambda b: (b, 0)),
            pl.BlockSpec((1, LANES), const2),
            pl.BlockSpec((chunk, chunk), const2),
            pl.BlockSpec((N_PIECES, LANES, LANES), const3),
            pl.BlockSpec((N_PIECES, LANES, LANES), const3),
            pl.BlockSpec((1, LANES), const2),
            pl.BlockSpec((1, LANES), const2),
        ],
        out_specs=[
            pl.BlockSpec((1, seq, LANES), lambda b: (b, 0, 0)),
            pl.BlockSpec((1, seq, LANES), lambda b: (b, 0, 0)),
        ],
        out_shape=[
            jax.ShapeDtypeStruct((batch, seq, LANES), BF),
            jax.ShapeDtypeStruct((batch, seq, LANES), BF),
        ],
        compiler_params=_cparams(("arbitrary",)),
        name="forget_scan",
    )(fa, bias_lanes, tri, pq, pk, oq, ok)


def _with_bias_lanes(slab, extras, hp, parity):
    lane = lax.broadcasted_iota(jnp.int32, slab.shape, 1)
    own = (lane < HEAD_DIM) if parity == 0 else (lane >= HEAD_DIM)
    base = (HEAD_DIM if parity == 0 else 0) + 2 * N_PIECES * hp
    in_extras = (lane >= base) & (lane < base + 2 * N_PIECES)
    return jnp.where(own, slab, jnp.where(in_extras, extras, jnp.zeros_like(extras)))


def _fox_kernel(q_ref, k_ref, v_ref, gq_ref, gk_ref, o_ref,
                ka_scr, kb_scr, vta_scr, vtb_scr, bias_scr, s_scr, acc_scr, ot_scr, bound_smem,
                *, tq):
    hp = pl.program_id(1)
    qi = pl.program_id(2)
    k_scrs = (ka_scr, kb_scr)
    vt_scrs = (vta_scr, vtb_scr)
    seq = k_ref.shape[0]
    n_q = seq // tq
    lane_row = lax.broadcasted_iota(jnp.int32, (1, LANES), 1)

    def own_lanes(head, shape):
        lane = lax.broadcasted_iota(jnp.int32, shape, 1)
        return (lane < HEAD_DIM) if head == 0 else (lane >= HEAD_DIM)

    def max_row_norm(slab, head):
        x = jnp.where(own_lanes(head, slab.shape), slab.astype(F32), 0.0)
        sq = jnp.sum(x * x, axis=1, keepdims=True)
        return jnp.sqrt(jnp.max(sq, axis=0, keepdims=True))

    def bias_lane_sum(row, head, first):
        base = (HEAD_DIM if head == 0 else 0) + 2 * N_PIECES * hp + first
        picked = (lane_row >= base) & (lane_row < base + N_PIECES)
        return jnp.sum(jnp.where(picked, row, 0.0), axis=1, keepdims=True)

    @pl.when(qi == 0)
    def _prep():
        k2 = k_ref[...]
        gk = gk_ref[0]
        for head in range(2):
            k_scrs[head][...] = _with_bias_lanes(k2, gk, hp, head)
            bound_smem[head, n_q] = max_row_norm(k2, head)[0, 0]
            for j in range(n_q):
                rows16 = gk_ref[0, (j + 1) * tq - BF16_ROWS:(j + 1) * tq, :].astype(F32)
                last = rows16[BF16_ROWS - 1:BF16_ROWS, :]
                bound_smem[head, j] = -bias_lane_sum(last, head, N_PIECES)[0, 0]
        vt = v_ref[...].astype(F32).T.astype(BF)
        ones = jnp.ones((BF16_ROWS, seq), BF)
        for head, vt_scr in enumerate(vt_scrs):
            vt_scr[0:HEAD_DIM, :] = vt[head * HEAD_DIM:(head + 1) * HEAD_DIM, :]
            vt_scr[HEAD_DIM:, :] = ones
        row = lax.broadcasted_iota(jnp.int32, (tq, tq), 0)
        col = lax.broadcasted_iota(jnp.int32, (tq, tq), 1)
        bias_scr[...] = jnp.where(row <= col, 0.0, NEG_INF)

    q2 = q_ref[...]
    gq = gq_ref[0]
    qps = [_with_bias_lanes(q2, gq, hp, head) for head in range(2)]

    def scores(head, start, masked):
        kt = k_scrs[head][pl.ds(start, tq), :]
        st = _dot_nt(kt, qps[head])
        if masked:
            st = st + bias_scr[...]
        s_scr[head] = st
        return jnp.max(st, axis=0, keepdims=True)

    def softmax_pv(head, st, tile, tile_max, m):
        start = pl.multiple_of(tile * tq, tq)
        m_new = jnp.maximum(m, tile_max)
        alpha = jnp.exp2(m - m_new)
        pt = jnp.exp2(st - m_new).astype(BF)
        vt = vt_scrs[head][:, pl.ds(start, tq)]
        acc_scr[head] = alpha * acc_scr[head] + _dot(vt, pt)
        return m_new

    acc_scr[...] = jnp.zeros_like(acc_scr)
    diag = pl.multiple_of(qi * tq, tq)
    first_max = [scores(head, diag, True) for head in range(2)]
    m_init = jnp.full((1, tq), NEG_INF, F32)

    first_row = gq_ref[0, 0:BF16_ROWS, :].astype(F32)[0:1, :]
    count = jnp.zeros((1, 1), jnp.int32)
    slack = []
    for head in range(2):
        reach = (max_row_norm(q2, head) * bound_smem[head, n_q]
                 + bias_lane_sum(first_row, head, 0) + SCORE_MARGIN)
        lowest_max = jnp.min(first_max[head], axis=1, keepdims=True)
        slack.append(reach - (lowest_max - UNDERFLOW_BITS))
    for j in range(n_q - 1):
        needed = ((slack[0] >= bound_smem[0, j]) | (slack[1] >= bound_smem[1, j])) & (j < qi)
        count = count + jnp.where(needed, 1, 0)
    n_needed = count[0, 0]

    def body(t, carry):
        max_a, max_b, m_a, m_b, prev_tile = carry
        prev = [s_scr[head] for head in range(2)]
        tile = qi - 1 - t
        start = pl.multiple_of(tile * tq, tq)
        next_max = [scores(head, start, False) for head in range(2)]
        m_a = softmax_pv(0, prev[0], prev_tile, max_a, m_a)
        m_b = softmax_pv(1, prev[1], prev_tile, max_b, m_b)
        return next_max[0], next_max[1], m_a, m_b, tile

    max_a, max_b, m_a, m_b, prev_tile = lax.fori_loop(
        0, n_needed, body, (first_max[0], first_max[1], m_init, m_init, qi))
    softmax_pv(0, s_scr[0], prev_tile, max_a, m_a)
    softmax_pv(1, s_scr[1], prev_tile, max_b, m_b)

    for head in range(2):
        acc = acc_scr[head]
        ot_scr[head * HEAD_DIM:(head + 1) * HEAD_DIM, :] = (
            acc[0:HEAD_DIM, :] / acc[HEAD_DIM:HEAD_DIM + 1, :])
    o_ref[...] = ot_scr[...].T.astype(BF)


def _fox_attention(z, gq, gk, *, batch, seq, tq):
    n_q = seq // tq
    tokens = batch * seq
    return pl.pallas_call(
        functools.partial(_fox_kernel, tq=tq),
        grid=(batch, N_HEAD_PAIRS, n_q),
        in_specs=[
            pl.BlockSpec((tq, LANES), lambda b, hp, qi: (b * n_q + qi, hp)),
            pl.BlockSpec((seq, LANES), lambda b, hp, qi: (b, N_HEAD_PAIRS + hp)),
            pl.BlockSpec((seq, LANES), lambda b, hp, qi: (b, 2 * N_HEAD_PAIRS + hp)),
            pl.BlockSpec((1, tq, LANES), lambda b, hp, qi: (b, qi, 0)),
            pl.BlockSpec((1, seq, LANES), lambda b, hp, qi: (b, 0, 0)),
        ],
        out_specs=pl.BlockSpec((tq, LANES), lambda b, hp, qi: (b * n_q + qi, hp)),
        out_shape=jax.ShapeDtypeStruct((tokens, ATT_WIDTH), BF),
        scratch_shapes=[
            pltpu.VMEM((seq, LANES), BF),
            pltpu.VMEM((seq, LANES), BF),
            pltpu.VMEM((HEAD_DIM + BF16_ROWS, seq), BF),
            pltpu.VMEM((HEAD_DIM + BF16_ROWS, seq), BF),
            pltpu.VMEM((tq, tq), F32),
            pltpu.VMEM((2, tq, tq), F32),
            pltpu.VMEM((2, HEAD_DIM + BF16_ROWS, tq), F32),
            pltpu.VMEM((LANES, tq), F32),
            pltpu.SMEM((2, n_q + 1), F32),
        ],
        compiler_params=_cparams(("arbitrary", "arbitrary", "arbitrary")),
        name="fox_attn",
    )(z, z, z, gq, gk)


def _log2(n):
    assert n > 0 and n & (n - 1) == 0, n
    return n.bit_length() - 1


def _dilated_kernel(q1, k1, v1, q4, k4, v4, q16, k16, v16, o_ref,
                    num1, num4, num16, den1, den4, den16, max1, max4, max16, bias_scr,
                    *, seq, blk, chunk, group, merge_rows):
    sources = ((q1, k1, v1), (q4, k4, v4), (q16, k16, v16))
    num_scrs = (num1, num4, num16)
    den_scrs = (den1, den4, den16)
    max_scrs = (max1, max4, max16)
    n_blocks = seq // blk

    low = lax.broadcasted_iota(jnp.int32, (blk, LANES), 1) < HEAD_DIM
    qrow = lax.broadcasted_iota(jnp.int32, (2 * blk, 2 * blk), 0) % blk
    kcol = lax.broadcasted_iota(jnp.int32, (2 * blk, 2 * blk), 1)
    dist = qrow + blk - kcol
    band = (dist >= 0) & (dist <= blk)
    bias_scr[0] = jnp.where(band & (kcol >= blk), 0.0, NEG_INF)
    bias_scr[1] = jnp.where(band, 0.0, NEG_INF)
    ones = jnp.ones((2 * blk, LANES), BF)

    for idx, (window, dilation) in enumerate(DIL_PATTERNS):
        assert window // dilation == blk
        q_ref, k_ref, v_ref = sources[idx]
        num_scr, den_scr, max_scr = num_scrs[idx], den_scrs[idx], max_scrs[idx]
        sub_shift = _log2(seq // dilation // blk)
        seg_rows = chunk // dilation
        seg = min(blk, seg_rows)

        def load(ref, j, l0):
            parts = []
            for s in range(blk // seg):
                l = l0 + s * seg
                p = lax.shift_right_logical(l, _log2(seg_rows))
                i = l & (seg_rows - 1)
                start = pl.multiple_of(p * chunk + j * seg_rows + i, seg)
                parts.append(ref[pl.ds(start, seg), :])
            return parts[0] if len(parts) == 1 else jnp.concatenate(parts, axis=0)

        def one_block(g):
            j = lax.shift_right_logical(g, sub_shift)
            gs = g & ((1 << sub_shift) - 1)
            l0 = gs * blk
            lp = jnp.maximum(l0 - blk, 0)
            q2 = load(q_ref, j, l0)
            zero = jnp.zeros_like(q2)
            qq = jnp.concatenate([jnp.where(low, q2, zero), jnp.where(low, zero, q2)], axis=0)
            kwin = jnp.concatenate([load(k_ref, j, lp), load(k_ref, j, l0)], axis=0)
            vwin = jnp.concatenate([load(v_ref, j, lp), load(v_ref, j, l0)], axis=0)
            s = _dot_nt(qq, kwin) + bias_scr[jnp.minimum(gs, 1)]
            m = jnp.max(s, axis=1, keepdims=True)
            p = jnp.exp2(s - m).astype(BF)
            pv = _dot(p, jnp.concatenate([vwin, ones], axis=1))
            if dilation == 1:
                dst = pl.ds(pl.multiple_of(l0, blk), blk)
            else:
                dst = pl.ds(l0 * dilation + j, blk, stride=dilation)
            num_scr[dst, :] = jnp.where(low, pv[:blk, :LANES], pv[blk:, :LANES])
            den_scr[dst, :] = jnp.where(low, pv[:blk, LANES:], pv[blk:, LANES:])
            max_scr[dst, :] = jnp.where(low, jnp.broadcast_to(m[:blk], (blk, LANES)),
                                        jnp.broadcast_to(m[blk:], (blk, LANES)))

        def blocks(it, carry):
            for u in range(group):
                one_block(it * group + u)
            return carry

        lax.fori_loop(0, n_blocks // group, blocks, 0)

    def merge(c, carry):
        rows = pl.ds(pl.multiple_of(c * merge_rows, merge_rows), merge_rows)
        maxes = [max_scr[rows, :] for max_scr in max_scrs]
        top = functools.reduce(jnp.maximum, maxes)
        weights = [jnp.exp2(m - top) for m in maxes]
        num = sum(w * num_scr[rows, :] for w, num_scr in zip(weights, num_scrs))
        den = sum(w * den_scr[rows, :] for w, den_scr in zip(weights, den_scrs))
        o_ref[rows, :] = (num / den).astype(BF)
        return carry

    lax.fori_loop(0, seq // merge_rows, merge, 0)


def _dilated_attention(z, zd4, zd16, *, batch, seq):
    blk = DIL_PATTERNS[0][0] // DIL_PATTERNS[0][1]
    tokens = batch * seq
    per_section = ATT_WIDTH // LANES
    col = lambda section: pl.BlockSpec(
        (seq, LANES), lambda b, hp: (b, section * per_section + hp))
    f32_buf = pltpu.VMEM((seq, LANES), F32)
    return pl.pallas_call(
        functools.partial(_dilated_kernel, seq=seq, blk=blk, chunk=DIL_CHUNK, group=32,
                          merge_rows=256),
        grid=(batch, N_HEAD_PAIRS),
        in_specs=[col(DIL_SECTION), col(DIL_SECTION + 1), col(DIL_SECTION + 2),
                  col(0), col(1), col(2), col(0), col(1), col(2)],
        out_specs=pl.BlockSpec((seq, LANES), lambda b, hp: (b, hp)),
        out_shape=jax.ShapeDtypeStruct((tokens, ATT_WIDTH), BF),
        scratch_shapes=[f32_buf] * 9 + [pltpu.VMEM((2, 2 * blk, 2 * blk), F32)],
        compiler_params=_cparams(("arbitrary", "arbitrary")),
        name="dilated_attn",
    )(z, z, z, zd4, zd4, zd4, zd16, zd16, zd16)


def _mix_kernel(ya_ref, yb_ref, ga_ref, gb_ref, x_ref, woa_ref, wob_ref, wout_ref,
                g_ref, o_ref):
    pa = _dot(ya_ref[...], woa_ref[...])
    pb = _dot(yb_ref[...], wob_ref[...])
    mixed = (jax.nn.sigmoid(ga_ref[...].astype(F32)) * pa
             + jax.nn.sigmoid(gb_ref[...].astype(F32)) * pb)
    y = _dot(mixed.astype(BF), wout_ref[...])
    o_ref[...] = x_ref[...] + _rms(y, g_ref[...])


def _mix(ya, yb, z, x2d, woa, wob, wout, g, *, tm):
    tokens = x2d.shape[0]
    gate_blk = lambda off: pl.BlockSpec((tm, D_MODEL), lambda i: (i, off))
    const = lambda i: (0, 0)
    return pl.pallas_call(
        _mix_kernel,
        grid=(tokens // tm,),
        in_specs=[
            pl.BlockSpec((tm, ATT_WIDTH), lambda i: (i, 0)),
            pl.BlockSpec((tm, ATT_WIDTH), lambda i: (i, 0)),
            gate_blk(3),
            gate_blk(4),
            pl.BlockSpec((tm, D_MODEL), lambda i: (i, 0)),
            pl.BlockSpec((ATT_WIDTH, D_MODEL), const),
            pl.BlockSpec((ATT_WIDTH, D_MODEL), const),
            pl.BlockSpec((D_MODEL, D_MODEL), const),
            pl.BlockSpec((1, D_MODEL), const),
        ],
        out_specs=pl.BlockSpec((tm, D_MODEL), lambda i: (i, 0)),
        out_shape=jax.ShapeDtypeStruct((tokens, D_MODEL), F32),
        compiler_params=_cparams(("arbitrary",)),
        name="mix",
    )(ya, yb, z, z, x2d, woa, wob, wout, g)


def _ffn_kernel(x_ref, halo_ref, gpre_ref, wup_ref, cw_ref, cb_ref, wd_ref, gpost_ref, o_ref,
                h_scr, ua0_scr, ub0_scr, ua1_scr, ub1_scr, acc_scr,
                *, tm, tf, tiles_per_seq, n_chunks):
    i = pl.program_id(0)
    halo = BF16_ROWS
    rows = tm // n_chunks
    n_tiles = D_FF // tf
    u_sets = ((ua0_scr, ub0_scr), (ua1_scr, ub1_scr))

    def cols(f, gate):
        return pl.ds(pl.multiple_of(f * tf + (D_FF if gate else 0), LANES), tf)

    def up_proj(f, dst, lo, hi):
        h = h_scr[lo:hi, :]
        dst[0][lo:hi, :] = _dot(h, wup_ref[:, cols(f, False)])
        dst[1][lo:hi, :] = _dot(h, wup_ref[:, cols(f, True)])

    def conv(u_scr, window, post_scale, first, n):
        out = cb_ref[:, window] * post_scale
        for tap in range(CONV_WIDTH):
            lo = halo + first - (CONV_WIDTH - 1) + tap
            out = out + (cw_ref[tap:tap + 1, window] * post_scale) * u_scr[lo:lo + n, :]
        return out

    def down_proj(f, src, first, n):
        a = conv(src[0], cols(f, False), 1.0, first, n)
        half_b = conv(src[1], cols(f, True), 0.5, first, n)
        c0 = math.sqrt(2.0 / math.pi)
        inner = a * (c0 + (c0 * 0.044715) * (a * a))
        hidden = ((a * half_b) * (1.0 + jnp.tanh(inner))).astype(BF)
        return _dot(hidden, wd_ref[pl.ds(pl.multiple_of(f * tf, tf), tf), :])

    def stage(f, parity):
        src, dst = u_sets[1 - parity], u_sets[parity]
        for c in range(n_chunks):
            lo = 0 if c == 0 else halo + c * rows
            up_proj(f, dst, lo, halo + (c + 1) * rows)
            out_rows = slice(c * rows, (c + 1) * rows)
            acc_scr[out_rows, :] += down_proj(f - 1, src, c * rows, rows)

    g = gpre_ref[...]
    hh = _rms(halo_ref[...], g)
    hh = jnp.where(i % tiles_per_seq == 0, jnp.zeros_like(hh), hh)
    h_scr[0:halo, :] = hh.astype(BF)
    h_scr[halo:, :] = _rms(x_ref[...], g).astype(BF)
    acc_scr[...] = jnp.zeros_like(acc_scr)
    up_proj(0, u_sets[0], 0, tm + halo)

    def pair(k, carry):
        stage(2 * k + 1, 1)
        stage(2 * k + 2, 0)
        return carry

    assert n_tiles % 2 == 1
    lax.fori_loop(0, (n_tiles - 1) // 2, pair, 0)
    y = acc_scr[...] + down_proj(n_tiles - 1, u_sets[0], 0, tm)
    o_ref[...] = x_ref[...] + _rms(y, gpost_ref[...])


def _ffn(x1, g_pre, w_up, conv_w, conv_b, w_down, g_post, *, seq, tm, tf):
    tokens = x1.shape[0]
    halo = BF16_ROWS
    per = tm // halo
    resident = lambda shape: pl.BlockSpec(shape, lambda i: (0, 0),
                                          pipeline_mode=pl.Buffered(1))
    u_buf = pltpu.VMEM((tm + halo, tf), F32)
    return pl.pallas_call(
        functools.partial(_ffn_kernel, tm=tm, tf=tf, tiles_per_seq=seq // tm, n_chunks=4),
        grid=(tokens // tm,),
        in_specs=[
            pl.BlockSpec((tm, D_MODEL), lambda i: (i, 0)),
            pl.BlockSpec((halo, D_MODEL), lambda i: (jnp.maximum(i * per - 1, 0), 0)),
            resident((1, D_MODEL)),
            resident((D_MODEL, 2 * D_FF)),
            resident((CONV_WIDTH, 2 * D_FF)),
            resident((1, 2 * D_FF)),
            resident((D_FF, D_MODEL)),
            resident((1, D_MODEL)),
        ],
        out_specs=pl.BlockSpec((tm, D_MODEL), lambda i: (i, 0)),
        out_shape=jax.ShapeDtypeStruct((tokens, D_MODEL), F32),
        scratch_shapes=[
            pltpu.VMEM((tm + halo, D_MODEL), BF),
            u_buf, u_buf, u_buf, u_buf,
            pltpu.VMEM((tm, D_MODEL), F32),
        ],
        compiler_params=_cparams(("arbitrary",)),
        name="ffn",
    )(x1, x1, g_pre, w_up, conv_w, conv_b, w_down, g_post)


def _rope_freq_lanes():
    inv_freq = ROPE_THETA ** (-jnp.arange(ROPE_HALF, dtype=F32) * 2.0 / ROPE_DIM)
    lane = np.arange(LANES) % HEAD_DIM
    return inv_freq[lane % ROPE_HALF].reshape(1, LANES)


def kernel(x, g_pre_mix, w_in, b_forget, w_o_fox, w_o_dil, w_out, g_post_mix,
           g_pre_ffn, w_up, conv_w, conv_b, w_down, g_post_ffn):
    batch, seq, d_model = x.shape
    assert d_model == D_MODEL and seq % 1024 == 0
    depth = w_in.shape[0]
    fox_end = 3 * ATT_WIDTH
    freq_lanes = _rope_freq_lanes()
    x2d = x.reshape(batch * seq, D_MODEL)
    row = lambda v: v.reshape(1, -1)
    for l in range(depth):
        w_all = jnp.concatenate(
            [w_in[l][:, :fox_end], w_in[l][:, fox_end + N_HEADS:],
             w_in[l][:, fox_end:fox_end + N_HEADS],
             jnp.zeros((D_MODEL, LANES - N_HEADS), w_in.dtype)], axis=1).astype(BF)
        bias_lanes = jnp.pad(b_forget[l], (0, LANES - N_HEADS)).reshape(1, LANES)

        z, fa, zd4, zd16 = _in_proj(x2d, row(g_pre_mix[l]), w_all, freq_lanes,
                                    batch=batch, seq=seq, tm=DIL_CHUNK)
        gq, gk = _forget_scan(fa, bias_lanes, batch=batch, seq=seq, chunk=512)
        ya = _fox_attention(z, gq, gk, batch=batch, seq=seq, tq=512)
        yb = _dilated_attention(z, zd4, zd16, batch=batch, seq=seq)
        x2d = _mix(ya, yb, z, x2d, w_o_fox[l].astype(BF), w_o_dil[l].astype(BF),
                   w_out[l].astype(BF), row(g_post_mix[l]), tm=512)
        x2d = _ffn(x2d, row(g_pre_ffn[l]), w_up[l].astype(BF), conv_w[l],
                   row(conv_b[l]), w_down[l].astype(BF), row(g_post_ffn[l]),
                   seq=seq, tm=1024, tf=256)
    return x2d.reshape(batch, seq, D_MODEL)
```

```python
import functools
import math

import numpy as np
import jax
import jax.numpy as jnp
from jax import lax
from jax.experimental import pallas as pl
from jax.experimental.pallas import tpu as pltpu

D_MODEL = 1024
HEAD_DIM = 64
N_HEADS = 8
ATT_WIDTH = N_HEADS * HEAD_DIM
N_HEAD_PAIRS = N_HEADS // 2
DIL_PATTERNS = ((128, 1), (512, 4), (2048, 16))
ROPE_DIM = HEAD_DIM // 4
ROPE_HALF = ROPE_DIM // 2
ROPE_THETA = 500000.0
D_FF = 2816
CONV_WIDTH = 3
RMS_EPS = 1e-6
NEG_INF = -1e30
Q_SCALE = 1.0 / math.sqrt(HEAD_DIM)
LOG2_E = math.log2(math.e)
UNDERFLOW_BITS = 150.0
SCORE_MARGIN = 1.0

LANES = 128
BF16_ROWS = 16
Z_WIDTH = 3 * ATT_WIDTH * 2 + 2 * D_MODEL
Z_BLK = 512
DIL_SECTION = 3
DIL_CHUNK = 512
N_PIECES = 3
VMEM_LIMIT = 56 * 1024 * 1024

BF = jnp.bfloat16
F32 = jnp.float32


def _cparams(sem, flags=None):
    return pltpu.CompilerParams(dimension_semantics=sem, vmem_limit_bytes=VMEM_LIMIT,
                                flags=flags)


def _rms(xf, g):
    inv = lax.rsqrt(jnp.mean(xf * xf, axis=-1, keepdims=True) + RMS_EPS)
    return xf * inv * g


def _split3(x):
    hi = x.astype(BF)
    r1 = x - hi.astype(F32)
    mid = r1.astype(BF)
    lo = (r1 - mid.astype(F32)).astype(BF)
    return hi, mid, lo


def _dot(a, b):
    return jnp.dot(a, b, preferred_element_type=F32)


def _dot_nt(a, b):
    return lax.dot_general(a, b, (((1,), (1,)), ((), ())), preferred_element_type=F32)


def _in_proj_kernel(x_ref, g_ref, w_ref, freq_ref, z_ref, fa_ref, zd4_ref, zd16_ref,
                    h_scr, cos_scr, sneg_scr, spos_scr, stage_scr, stage4_scr, *, tm):
    p = pl.program_id(0)
    b = pl.program_id(1)
    zd_refs = {4: zd4_ref, 16: zd16_ref}
    n_slabs = Z_BLK // LANES

    @pl.when(b == 0)
    def _tables():
        pos = (p * tm + lax.broadcasted_iota(jnp.int32, (tm, LANES), 0)).astype(F32)
        lane = lax.broadcasted_iota(jnp.int32, (tm, LANES), 1)
        c = lane % HEAD_DIM
        ang = pos * freq_ref[...]
        cs = jnp.cos(ang)
        sn = jnp.sin(ang)
        cos_scr[...] = jnp.where(c < ROPE_DIM, cs, 1.0)
        sneg_scr[...] = jnp.where(c < ROPE_HALF, -sn, 0.0)
        spos_scr[...] = jnp.where((c >= ROPE_HALF) & (c < ROPE_DIM), sn, 0.0)

    def rope(t):
        up = pltpu.roll(t, LANES - ROPE_HALF, 1)
        dn = pltpu.roll(t, ROPE_HALF, 1)
        return t * cos_scr[...] + up * sneg_scr[...] + dn * spos_scr[...]

    def emit_dilated(section, slab, val):
        out_lanes = slice(section * Z_BLK + slab * LANES, section * Z_BLK + (slab + 1) * LANES)
        z_ref[:, (DIL_SECTION + section) * Z_BLK + slab * LANES:
              (DIL_SECTION + section) * Z_BLK + (slab + 1) * LANES] = val.astype(BF)
        stage, stage4 = stage_scr.at[section, slab], stage4_scr.at[section, slab]
        stage[...] = val
        seg4 = tm // 4
        for c in range(4):
            part = stage[pl.ds(c, seg4, stride=4), :]
            stage4[c * seg4:(c + 1) * seg4, :] = part
            zd_refs[4][c * seg4:(c + 1) * seg4, out_lanes] = part.astype(BF)
        seg16 = tm // 16
        for jj in range(16):
            c, a = jj % 4, jj // 4
            part = stage4[pl.ds(c * seg4 + a, seg16, stride=4), :]
            zd_refs[16][jj * seg16:(jj + 1) * seg16, out_lanes] = part.astype(BF)

    h = _rms(x_ref[...], g_ref[...]).astype(BF)
    h_scr[...] = h
    fa_ref[...] = _dot(h, w_ref[:, Z_WIDTH:Z_WIDTH + LANES])

    for j in range(Z_WIDTH // Z_BLK):
        cols = slice(j * Z_BLK, (j + 1) * Z_BLK)
        acc = _dot(h_scr[...], w_ref[:, cols])
        section = j - DIL_SECTION
        if j == 0:
            z_ref[:, cols] = (acc * (Q_SCALE * LOG2_E)).astype(BF)
        elif 0 <= section < 3:
            for slab in range(n_slabs):
                t = acc[:, slab * LANES:(slab + 1) * LANES]
                if section == 0:
                    t = rope(t) * (Q_SCALE * LOG2_E)
                elif section == 1:
                    t = rope(t)
                emit_dilated(section, slab, t)
        else:
            z_ref[:, cols] = acc.astype(BF)


def _in_proj(x2d, g, w_all, freq_lanes, *, batch, seq, tm):
    n_p = seq // tm
    tokens = batch * seq
    row = lambda p, b: (b * n_p + p, 0)
    resident = lambda shape: pl.BlockSpec(shape, lambda p, b: (0, 0),
                                          pipeline_mode=pl.Buffered(1))
    dil_shape = jax.ShapeDtypeStruct((tokens, 3 * ATT_WIDTH), BF)
    stage_buf = pltpu.VMEM((3, Z_BLK // LANES, tm, LANES), F32)
    return pl.pallas_call(
        functools.partial(_in_proj_kernel, tm=tm),
        grid=(n_p, batch),
        in_specs=[
            pl.BlockSpec((tm, D_MODEL), row),
            resident((1, D_MODEL)),
            resident((D_MODEL, Z_WIDTH + LANES)),
            resident((1, LANES)),
        ],
        out_specs=[
            pl.BlockSpec((tm, Z_WIDTH), row),
            pl.BlockSpec((tm, LANES), row),
            pl.BlockSpec((tm, 3 * ATT_WIDTH), row),
            pl.BlockSpec((tm, 3 * ATT_WIDTH), row),
        ],
        out_shape=[
            jax.ShapeDtypeStruct((tokens, Z_WIDTH), BF),
            jax.ShapeDtypeStruct((tokens, LANES), F32),
            dil_shape,
            dil_shape,
        ],
        scratch_shapes=[
            pltpu.VMEM((tm, D_MODEL), BF),
            pltpu.VMEM((tm, LANES), F32),
            pltpu.VMEM((tm, LANES), F32),
            pltpu.VMEM((tm, LANES), F32),
            stage_buf,
            stage_buf,
        ],
        compiler_params=_cparams(("arbitrary", "arbitrary")),
        name="in_proj",
    )(x2d, g, w_all, freq_lanes)


def _extras_base(head):
    return (HEAD_DIM if head % 2 == 0 else 0) + 2 * N_PIECES * (head // 2)


def _forget_scan_kernel(fa_ref, bias_ref, tri_ref, pq_ref, pk_ref, oq_ref, ok_ref,
                        gq_ref, gk_ref):
    chunk = tri_ref.shape[0]
    n_chunks = fa_ref.shape[0] // chunk
    t = fa_ref[...] + bias_ref[...]
    log_f = jnp.minimum(t, 0.0) - jnp.log1p(jnp.exp(-jnp.abs(t)))
    wide = jnp.concatenate(
        [log_f[c * chunk:(c + 1) * chunk, :] for c in range(n_chunks)], axis=1)
    tri = tri_ref[...]
    local = sum(_dot(tri, piece) for piece in _split3(wide))
    carry = jnp.zeros((1, LANES), F32)
    parts = []
    for c in range(n_chunks):
        part = local[:, c * LANES:(c + 1) * LANES] + carry
        parts.append(part)
        carry = part[chunk - 1:chunk, :]
    run = jnp.concatenate(parts, axis=0)
    gq = oq_ref[...].astype(F32)
    gk = ok_ref[...].astype(F32)
    for i, piece in enumerate(_split3(run * LOG2_E)):
        gq = gq + _dot(piece, pq_ref[i])
        gk = gk - _dot(piece, pk_ref[i])
    gq_ref[0] = gq.astype(BF)
    gk_ref[0] = gk.astype(BF)


def _forget_scan_constants(chunk):
    tri = np.tril(np.ones((chunk, chunk), np.float32))
    pq = np.zeros((N_PIECES, LANES, LANES), np.float32)
    pk = np.zeros((N_PIECES, LANES, LANES), np.float32)
    oq = np.zeros((1, LANES), np.float32)
    ok = np.zeros((1, LANES), np.float32)
    for h in range(N_HEADS):
        base = _extras_base(h)
        for i in range(N_PIECES):
            pq[i, h, base + i] = 1.0
            oq[0, base + N_PIECES + i] = 1.0
            ok[0, base + i] = 1.0
            pk[i, h, base + N_PIECES + i] = 1.0
    as_bf = lambda a: jnp.asarray(a, BF)
    return as_bf(tri), as_bf(pq), as_bf(pk), as_bf(oq), as_bf(ok)


def _forget_scan(fa, bias_lanes, *, batch, seq, chunk):
    tri, pq, pk, oq, ok = _forget_scan_constants(chunk)
    const2 = lambda b: (0, 0)
    const3 = lambda b: (0, 0, 0)
    return pl.pallas_call(
        _forget_scan_kernel,
        grid=(batch,),
        in_specs=[
            pl.BlockSpec((seq, LANES), lambda b: (b, 0)),
            pl.BlockSpec((1, LANES), const2),
            pl.BlockSpec((chunk, chunk), const2),
            pl.BlockSpec((N_PIECES, LANES, LANES), const3),
            pl.BlockSpec((N_PIECES, LANES, LANES), const3),
            pl.BlockSpec((1, LANES), const2),
            pl.BlockSpec((1, LANES), const2),
        ],
        out_specs=[
            pl.BlockSpec((1, seq, LANES), lambda b: (b, 0, 0)),
            pl.BlockSpec((1, seq, LANES), lambda b: (b, 0, 0)),
        ],
        out_shape=[
            jax.ShapeDtypeStruct((batch, seq, LANES), BF),
            jax.ShapeDtypeStruct((batch, seq, LANES), BF),
        ],
        compiler_params=_cparams(("arbitrary",)),
        name="forget_scan",
    )(fa, bias_lanes, tri, pq, pk, oq, ok)


def _with_bias_lanes(slab, extras, hp, parity):
    lane = lax.broadcasted_iota(jnp.int32, slab.shape, 1)
    own = (lane < HEAD_DIM) if parity == 0 else (lane >= HEAD_DIM)
    base = (HEAD_DIM if parity == 0 else 0) + 2 * N_PIECES * hp
    in_extras = (lane >= base) & (lane < base + 2 * N_PIECES)
    return jnp.where(own, slab, jnp.where(in_extras, extras, jnp.zeros_like(extras)))


def _fox_kernel(q_ref, k_ref, v_ref, gq_ref, gk_ref, o_ref,
                ka_scr, kb_scr, vta_scr, vtb_scr, qp_scr, bias_scr, s_scr, acc_scr, ot_scr,
                bound_smem, *, tq):
    hp = pl.program_id(1)
    k_scrs = (ka_scr, kb_scr)
    vt_scrs = (vta_scr, vtb_scr)
    seq = k_ref.shape[0]
    n_q = seq // tq
    lane_row = lax.broadcasted_iota(jnp.int32, (1, LANES), 1)

    def own_lanes(head, shape):
        lane = lax.broadcasted_iota(jnp.int32, shape, 1)
        return (lane < HEAD_DIM) if head == 0 else (lane >= HEAD_DIM)

    def max_row_norm(slab, head):
        x = jnp.where(own_lanes(head, slab.shape), slab.astype(F32), 0.0)
        sq = jnp.sum(x * x, axis=1, keepdims=True)
        return jnp.sqrt(jnp.max(sq, axis=0, keepdims=True))

    def bias_lane_sum(row, head, first):
        base = (HEAD_DIM if head == 0 else 0) + 2 * N_PIECES * hp + first
        picked = (lane_row >= base) & (lane_row < base + N_PIECES)
        return jnp.sum(jnp.where(picked, row, 0.0), axis=1, keepdims=True)

    k2 = k_ref[...]
    gk = gk_ref[0]
    for head in range(2):
        k_scrs[head][...] = _with_bias_lanes(k2, gk, hp, head)
        bound_smem[head, n_q] = max_row_norm(k2, head)[0, 0]
        for j in range(n_q):
            rows16 = gk_ref[0, (j + 1) * tq - BF16_ROWS:(j + 1) * tq, :].astype(F32)
            last = rows16[BF16_ROWS - 1:BF16_ROWS, :]
            bound_smem[head, j] = -bias_lane_sum(last, head, N_PIECES)[0, 0]
    vt = v_ref[...].astype(F32).T.astype(BF)
    ones = jnp.ones((BF16_ROWS, seq), BF)
    for head, vt_scr in enumerate(vt_scrs):
        vt_scr[0:HEAD_DIM, :] = vt[head * HEAD_DIM:(head + 1) * HEAD_DIM, :]
        vt_scr[HEAD_DIM:, :] = ones
    row = lax.broadcasted_iota(jnp.int32, (tq, tq), 0)
    col = lax.broadcasted_iota(jnp.int32, (tq, tq), 1)
    bias_scr[...] = jnp.where(row <= col, 0.0, NEG_INF)
    acc_scr[...] = jnp.zeros_like(acc_scr)

    def scores(head, tile, masked):
        kt = k_scrs[head][pl.ds(pl.multiple_of(tile * tq, tq), tq), :]
        st = _dot_nt(kt, qp_scr[head])
        if masked:
            st = st + bias_scr[...]
        s_scr[head] = st
        return jnp.max(st, axis=0, keepdims=True)

    def softmax_pv(head, st, tile, tile_max, m):
        start = pl.multiple_of(tile * tq, tq)
        m_new = jnp.maximum(m, tile_max)
        alpha = jnp.exp2(m - m_new)
        pt = jnp.exp2(st - m_new).astype(BF)
        vt_tile = vt_scrs[head][:, pl.ds(start, tq)]
        acc_scr[head] = alpha * acc_scr[head] + _dot(vt_tile, pt)
        return m_new

    def start_tile(qi):
        rows = pl.ds(pl.multiple_of(qi * tq, tq), tq)
        q2 = q_ref[rows, :]
        gq = gq_ref[0, rows, :]
        first_row = gq[0:BF16_ROWS, :].astype(F32)[0:1, :]
        maxes, slack = [], []
        for head in range(2):
            qp_scr[head] = _with_bias_lanes(q2, gq, hp, head)
            maxes.append(scores(head, qi, True))
            reach = (max_row_norm(q2, head) * bound_smem[head, n_q]
                     + bias_lane_sum(first_row, head, 0) + SCORE_MARGIN)
            lowest_max = jnp.min(maxes[head], axis=1, keepdims=True)
            slack.append(reach - (lowest_max - UNDERFLOW_BITS))
        count = jnp.zeros((1, 1), jnp.int32)
        for j in range(n_q - 1):
            needed = (((slack[0] >= bound_smem[0, j]) | (slack[1] >= bound_smem[1, j]))
                      & (j < qi))
            count = count + jnp.where(needed, 1, 0)
        return maxes[0], maxes[1], count[0, 0]

    def finish_tile(qi):
        for head in range(2):
            acc = acc_scr[head]
            ot_scr[head * HEAD_DIM:(head + 1) * HEAD_DIM, :] = (
                acc[0:HEAD_DIM, :] / acc[HEAD_DIM:HEAD_DIM + 1, :])
        o_ref[pl.ds(pl.multiple_of(qi * tq, tq), tq), :] = ot_scr[...].T.astype(BF)
        acc_scr[...] = jnp.zeros_like(acc_scr)

    m_init = jnp.full((1, tq), NEG_INF, F32)

    def query_tile(qi, carry):
        first_a, first_b, n_needed = carry

        def key_tile(t, inner):
            max_a, max_b, m_a, m_b, prev_tile = inner
            prev = [s_scr[head] for head in range(2)]
            tile = qi - 1 - t
            next_max = [scores(head, tile, False) for head in range(2)]
            m_a = softmax_pv(0, prev[0], prev_tile, max_a, m_a)
            m_b = softmax_pv(1, prev[1], prev_tile, max_b, m_b)
            return next_max[0], next_max[1], m_a, m_b, tile

        max_a, max_b, m_a, m_b, prev_tile = lax.fori_loop(
            0, n_needed, key_tile, (first_a, first_b, m_init, m_init, qi))
        last = [s_scr[head] for head in range(2)]
        softmax_pv(0, last[0], prev_tile, max_a, m_a)
        softmax_pv(1, last[1], prev_tile, max_b, m_b)
        following = start_tile(jnp.minimum(qi + 1, n_q - 1))
        finish_tile(qi)
        return following

    lax.fori_loop(0, n_q, query_tile, start_tile(0))


def _fox_attention(z, gq, gk, *, batch, seq, tq):
    tokens = batch * seq
    col = lambda section: pl.BlockSpec(
        (seq, LANES), lambda b, hp: (b, section * N_HEAD_PAIRS + hp))
    whole = pl.BlockSpec((1, seq, LANES), lambda b, hp: (b, 0, 0))
    ext = HEAD_DIM + BF16_ROWS
    return pl.pallas_call(
        functools.partial(_fox_kernel, tq=tq),
        grid=(batch, N_HEAD_PAIRS),
        in_specs=[col(0), col(1), col(2), whole, whole],
        out_specs=pl.BlockSpec((seq, LANES), lambda b, hp: (b, hp)),
        out_shape=jax.ShapeDtypeStruct((tokens, ATT_WIDTH), BF),
        scratch_shapes=[
            pltpu.VMEM((seq, LANES), BF),
            pltpu.VMEM((seq, LANES), BF),
            pltpu.VMEM((ext, seq), BF),
            pltpu.VMEM((ext, seq), BF),
            pltpu.VMEM((2, tq, LANES), BF),
            pltpu.VMEM((tq, tq), F32),
            pltpu.VMEM((2, tq, tq), F32),
            pltpu.VMEM((2, ext, tq), F32),
            pltpu.VMEM((LANES, tq), F32),
            pltpu.SMEM((2, seq // tq + 1), F32),
        ],
        compiler_params=_cparams(("arbitrary", "arbitrary")),
        name="fox_attn",
    )(z, z, z, gq, gk)


def _log2(n):
    assert n > 0 and n & (n - 1) == 0, n
    return n.bit_length() - 1


def _dilated_kernel(q1, k1, v1, q4, k4, v4, q16, k16, v16, o_ref,
                    num1, num4, num16, den1, den4, den16, max1, max4, max16, bias_scr,
                    *, seq, blk, chunk, group, merge_rows):
    sources = ((q1, k1, v1), (q4, k4, v4), (q16, k16, v16))
    num_scrs = (num1, num4, num16)
    den_scrs = (den1, den4, den16)
    max_scrs = (max1, max4, max16)
    n_blocks = seq // blk

    low = lax.broadcasted_iota(jnp.int32, (blk, LANES), 1) < HEAD_DIM
    qrow = lax.broadcasted_iota(jnp.int32, (2 * blk, 2 * blk), 0) % blk
    kcol = lax.broadcasted_iota(jnp.int32, (2 * blk, 2 * blk), 1)
    dist = qrow + blk - kcol
    band = (dist >= 0) & (dist <= blk)
    bias_scr[0] = jnp.where(band & (kcol >= blk), 0.0, NEG_INF)
    bias_scr[1] = jnp.where(band, 0.0, NEG_INF)
    ones = jnp.ones((2 * blk, LANES), BF)

    for idx, (window, dilation) in enumerate(DIL_PATTERNS):
        assert window // dilation == blk
        q_ref, k_ref, v_ref = sources[idx]
        num_scr, den_scr, max_scr = num_scrs[idx], den_scrs[idx], max_scrs[idx]
        sub_shift = _log2(seq // dilation // blk)
        seg_rows = chunk // dilation
        seg = min(blk, seg_rows)

        def load(ref, j, l0):
            parts = []
            for s in range(blk // seg):
                l = l0 + s * seg
                p = lax.shift_right_logical(l, _log2(seg_rows))
                i = l & (seg_rows - 1)
                start = pl.multiple_of(p * chunk + j * seg_rows + i, seg)
                parts.append(ref[pl.ds(start, seg), :])
            return parts[0] if len(parts) == 1 else jnp.concatenate(parts, axis=0)

        def one_block(g):
            j = lax.shift_right_logical(g, sub_shift)
            gs = g & ((1 << sub_shift) - 1)
            l0 = gs * blk
            lp = jnp.maximum(l0 - blk, 0)
            q2 = load(q_ref, j, l0)
            zero = jnp.zeros_like(q2)
            qq = jnp.concatenate([jnp.where(low, q2, zero), jnp.where(low, zero, q2)], axis=0)
            kwin = jnp.concatenate([load(k_ref, j, lp), load(k_ref, j, l0)], axis=0)
            vwin = jnp.concatenate([load(v_ref, j, lp), load(v_ref, j, l0)], axis=0)
            s = _dot_nt(qq, kwin) + bias_scr[jnp.minimum(gs, 1)]
            m = jnp.max(s, axis=1, keepdims=True)
            p = jnp.exp2(s - m).astype(BF)
            pv = _dot(p, jnp.concatenate([vwin, ones], axis=1))
            if dilation == 1:
                dst = pl.ds(pl.multiple_of(l0, blk), blk)
            else:
                dst = pl.ds(l0 * dilation + j, blk, stride=dilation)
            num_scr[dst, :] = jnp.where(low, pv[:blk, :LANES], pv[blk:, :LANES])
            den_scr[dst, :] = jnp.where(low, pv[:blk, LANES:], pv[blk:, LANES:])
            max_scr[dst, :] = jnp.where(low, jnp.broadcast_to(m[:blk], (blk, LANES)),
                                        jnp.broadcast_to(m[blk:], (blk, LANES)))

        def blocks(it, carry):
            for u in range(group):
                one_block(it * group + u)
            return carry

        lax.fori_loop(0, n_blocks // group, blocks, 0)

    def merge(c, carry):
        rows = pl.ds(pl.multiple_of(c * merge_rows, merge_rows), merge_rows)
        maxes = [max_scr[rows, :] for max_scr in max_scrs]
        top = functools.reduce(jnp.maximum, maxes)
        weights = [jnp.exp2(m - top) for m in maxes]
        num = sum(w * num_scr[rows, :] for w, num_scr in zip(weights, num_scrs))
        den = sum(w * den_scr[rows, :] for w, den_scr in zip(weights, den_scrs))
        o_ref[rows, :] = (num / den).astype(BF)
        return carry

    lax.fori_loop(0, seq // merge_rows, merge, 0)


def _dilated_attention(z, zd4, zd16, *, batch, seq):
    blk = DIL_PATTERNS[0][0] // DIL_PATTERNS[0][1]
    tokens = batch * seq
    per_section = ATT_WIDTH // LANES
    col = lambda section: pl.BlockSpec(
        (seq, LANES), lambda b, hp: (b, section * per_section + hp))
    f32_buf = pltpu.VMEM((seq, LANES), F32)
    return pl.pallas_call(
        functools.partial(_dilated_kernel, seq=seq, blk=blk, chunk=DIL_CHUNK, group=32,
                          merge_rows=256),
        grid=(batch, N_HEAD_PAIRS),
        in_specs=[col(DIL_SECTION), col(DIL_SECTION + 1), col(DIL_SECTION + 2),
                  col(0), col(1), col(2), col(0), col(1), col(2)],
        out_specs=pl.BlockSpec((seq, LANES), lambda b, hp: (b, hp)),
        out_shape=jax.ShapeDtypeStruct((tokens, ATT_WIDTH), BF),
        scratch_shapes=[f32_buf] * 9 + [pltpu.VMEM((2, 2 * blk, 2 * blk), F32)],
        compiler_params=_cparams(("arbitrary", "arbitrary")),
        name="dilated_attn",
    )(z, z, z, zd4, zd4, zd4, zd16, zd16, zd16)


def _mix_kernel(ya_ref, yb_ref, ga_ref, gb_ref, x_ref, woa_ref, wob_ref, wout_ref,
                g_ref, o_ref):
    pa = _dot(ya_ref[...], woa_ref[...])
    pb = _dot(yb_ref[...], wob_ref[...])
    mixed = (jax.nn.sigmoid(ga_ref[...].astype(F32)) * pa
             + jax.nn.sigmoid(gb_ref[...].astype(F32)) * pb)
    y = _dot(mixed.astype(BF), wout_ref[...])
    o_ref[...] = x_ref[...] + _rms(y, g_ref[...])


def _mix(ya, yb, z, x2d, woa, wob, wout, g, *, tm):
    tokens = x2d.shape[0]
    gate_blk = lambda off: pl.BlockSpec((tm, D_MODEL), lambda i: (i, off))
    const = lambda i: (0, 0)
    return pl.pallas_call(
        _mix_kernel,
        grid=(tokens // tm,),
        in_specs=[
            pl.BlockSpec((tm, ATT_WIDTH), lambda i: (i, 0)),
            pl.BlockSpec((tm, ATT_WIDTH), lambda i: (i, 0)),
            gate_blk(3),
            gate_blk(4),
            pl.BlockSpec((tm, D_MODEL), lambda i: (i, 0)),
            pl.BlockSpec((ATT_WIDTH, D_MODEL), const),
            pl.BlockSpec((ATT_WIDTH, D_MODEL), const),
            pl.BlockSpec((D_MODEL, D_MODEL), const),
            pl.BlockSpec((1, D_MODEL), const),
        ],
        out_specs=pl.BlockSpec((tm, D_MODEL), lambda i: (i, 0)),
        out_shape=jax.ShapeDtypeStruct((tokens, D_MODEL), F32),
        compiler_params=_cparams(("arbitrary",)),
        name="mix",
    )(ya, yb, z, z, x2d, woa, wob, wout, g)


def _ffn_kernel(x_ref, halo_ref, gpre_ref, wup_ref, cw_ref, cb_ref, wd_ref, gpost_ref, o_ref,
                h_scr, ua0_scr, ub0_scr, ua1_scr, ub1_scr, acc_scr,
                *, tm, tf, tiles_per_seq, n_chunks):
    i = pl.program_id(0)
    halo = BF16_ROWS
    rows = tm // n_chunks
    n_tiles = D_FF // tf
    u_sets = ((ua0_scr, ub0_scr), (ua1_scr, ub1_scr))

    def cols(f, gate):
        return pl.ds(pl.multiple_of(f * tf + (D_FF if gate else 0), LANES), tf)

    def up_proj(f, dst, lo, hi):
        h = h_scr[lo:hi, :]
        dst[0][lo:hi, :] = _dot(h, wup_ref[:, cols(f, False)])
        dst[1][lo:hi, :] = _dot(h, wup_ref[:, cols(f, True)])

    def conv(u_scr, window, post_scale, first, n):
        out = cb_ref[:, window] * post_scale
        for tap in range(CONV_WIDTH):
            lo = halo + first - (CONV_WIDTH - 1) + tap
            out = out + (cw_ref[tap:tap + 1, window] * post_scale) * u_scr[lo:lo + n, :]
        return out

    def down_proj(f, src, first, n):
        a = conv(src[0], cols(f, False), 1.0, first, n)
        half_b = conv(src[1], cols(f, True), 0.5, first, n)
        c0 = math.sqrt(2.0 / math.pi)
        inner = a * (c0 + (c0 * 0.044715) * (a * a))
        hidden = ((a * half_b) * (1.0 + jnp.tanh(inner))).astype(BF)
        return _dot(hidden, wd_ref[pl.ds(pl.multiple_of(f * tf, tf), tf), :])

    def stage(f, parity):
        src, dst = u_sets[1 - parity], u_sets[parity]
        for c in range(n_chunks):
            lo = 0 if c == 0 else halo + c * rows
            up_proj(f, dst, lo, halo + (c + 1) * rows)
            out_rows = slice(c * rows, (c + 1) * rows)
            acc_scr[out_rows, :] += down_proj(f - 1, src, c * rows, rows)

    g = gpre_ref[...]
    hh = _rms(halo_ref[...], g)
    hh = jnp.where(i % tiles_per_seq == 0, jnp.zeros_like(hh), hh)
    h_scr[0:halo, :] = hh.astype(BF)
    h_scr[halo:, :] = _rms(x_ref[...], g).astype(BF)
    acc_scr[...] = jnp.zeros_like(acc_scr)
    up_proj(0, u_sets[0], 0, tm + halo)

    def pair(k, carry):
        stage(2 * k + 1, 1)
        stage(2 * k + 2, 0)
        return carry

    assert n_tiles % 2 == 1
    lax.fori_loop(0, (n_tiles - 1) // 2, pair, 0)
    y = acc_scr[...] + down_proj(n_tiles - 1, u_sets[0], 0, tm)
    o_ref[...] = x_ref[...] + _rms(y, gpost_ref[...])


def _ffn(x1, g_pre, w_up, conv_w, conv_b, w_down, g_post, *, seq, tm, tf):
    tokens = x1.shape[0]
    halo = BF16_ROWS
    per = tm // halo
    resident = lambda shape: pl.BlockSpec(shape, lambda i: (0, 0),
                                          pipeline_mode=pl.Buffered(1))
    u_buf = pltpu.VMEM((tm + halo, tf), F32)
    return pl.pallas_call(
        functools.partial(_ffn_kernel, tm=tm, tf=tf, tiles_per_seq=seq // tm, n_chunks=4),
        grid=(tokens // tm,),
        in_specs=[
            pl.BlockSpec((tm, D_MODEL), lambda i: (i, 0)),
            pl.BlockSpec((halo, D_MODEL), lambda i: (jnp.maximum(i * per - 1, 0), 0)),
            resident((1, D_MODEL)),
            resident((D_MODEL, 2 * D_FF)),
            resident((CONV_WIDTH, 2 * D_FF)),
            resident((1, 2 * D_FF)),
            resident((D_FF, D_MODEL)),
            resident((1, D_MODEL)),
        ],
        out_specs=pl.BlockSpec((tm, D_MODEL), lambda i: (i, 0)),
        out_shape=jax.ShapeDtypeStruct((tokens, D_MODEL), F32),
        scratch_shapes=[
            pltpu.VMEM((tm + halo, D_MODEL), BF),
            u_buf, u_buf, u_buf, u_buf,
            pltpu.VMEM((tm, D_MODEL), F32),
        ],
        compiler_params=_cparams(("arbitrary",)),
        name="ffn",
    )(x1, x1, g_pre, w_up, conv_w, conv_b, w_down, g_post)


def _rope_freq_lanes():
    inv_freq = ROPE_THETA ** (-jnp.arange(ROPE_HALF, dtype=F32) * 2.0 / ROPE_DIM)
    lane = np.arange(LANES) % HEAD_DIM
    return inv_freq[lane % ROPE_HALF].reshape(1, LANES)


def kernel(x, g_pre_mix, w_in, b_forget, w_o_fox, w_o_dil, w_out, g_post_mix,
           g_pre_ffn, w_up, conv_w, conv_b, w_down, g_post_ffn):
    batch, seq, d_model = x.shape
    assert d_model == D_MODEL and seq % 1024 == 0
    depth = w_in.shape[0]
    fox_end = 3 * ATT_WIDTH
    freq_lanes = _rope_freq_lanes()
    x2d = x.reshape(batch * seq, D_MODEL)
    row = lambda v: v.reshape(1, -1)
    for l in range(depth):
        w_all = jnp.concatenate(
            [w_in[l][:, :fox_end], w_in[l][:, fox_end + N_HEADS:],
             w_in[l][:, fox_end:fox_end + N_HEADS],
             jnp.zeros((D_MODEL, LANES - N_HEADS), w_in.dtype)], axis=1).astype(BF)
        bias_lanes = jnp.pad(b_forget[l], (0, LANES - N_HEADS)).reshape(1, LANES)

        z, fa, zd4, zd16 = _in_proj(x2d, row(g_pre_mix[l]), w_all, freq_lanes,
                                    batch=batch, seq=seq, tm=DIL_CHUNK)
        gq, gk = _forget_scan(fa, bias_lanes, batch=batch, seq=seq, chunk=512)
        ya = _fox_attention(z, gq, gk, batch=batch, seq=seq, tq=512)
        yb = _dilated_attention(z, zd4, zd16, batch=batch, seq=seq)
        x2d = _mix(ya, yb, z, x2d, w_o_fox[l].astype(BF), w_o_dil[l].astype(BF),
                   w_out[l].astype(BF), row(g_post_mix[l]), tm=512)
        x2d = _ffn(x2d, row(g_pre_ffn[l]), w_up[l].astype(BF), conv_w[l],
                   row(conv_b[l]), w_down[l].astype(BF), row(g_post_ffn[l]),
                   seq=seq, tm=1024, tf=256)
    return x2d.reshape(batch, seq, D_MODEL)
```

```python
import functools
import math

import numpy as np
import jax
import jax.numpy as jnp
from jax import lax
from jax.experimental import pallas as pl
from jax.experimental.pallas import tpu as pltpu

D_MODEL = 1024
HEAD_DIM = 64
N_HEADS = 8
ATT_WIDTH = N_HEADS * HEAD_DIM
N_HEAD_PAIRS = N_HEADS // 2
DIL_PATTERNS = ((128, 1), (512, 4), (2048, 16))
ROPE_DIM = HEAD_DIM // 4
ROPE_HALF = ROPE_DIM // 2
ROPE_THETA = 500000.0
D_FF = 2816
CONV_WIDTH = 3
RMS_EPS = 1e-6
NEG_INF = -1e30
Q_SCALE = 1.0 / math.sqrt(HEAD_DIM)
LOG2_E = math.log2(math.e)
UNDERFLOW_BITS = 150.0
SCORE_MARGIN = 1.0

LANES = 128
BF16_ROWS = 16
Z_WIDTH = 3 * ATT_WIDTH * 2 + 2 * D_MODEL
Z_BLK = 512
DIL_SECTION = 3
DIL_CHUNK = 512
N_PIECES = 3
VMEM_LIMIT = 56 * 1024 * 1024

BF = jnp.bfloat16
F32 = jnp.float32


def _cparams(sem, flags=None):
    return pltpu.CompilerParams(dimension_semantics=sem, vmem_limit_bytes=VMEM_LIMIT,
                                flags=flags)


def _rms(xf, g):
    inv = lax.rsqrt(jnp.mean(xf * xf, axis=-1, keepdims=True) + RMS_EPS)
    return xf * inv * g


def _split3(x):
    hi = x.astype(BF)
    r1 = x - hi.astype(F32)
    mid = r1.astype(BF)
    lo = (r1 - mid.astype(F32)).astype(BF)
    return hi, mid, lo


def _dot(a, b):
    return jnp.dot(a, b, preferred_element_type=F32)


def _dot_nt(a, b):
    return lax.dot_general(a, b, (((1,), (1,)), ((), ())), preferred_element_type=F32)


def _in_proj_kernel(x_ref, g_ref, w_raw_ref, freq_ref, z_ref, fa_ref, zd4_ref, zd16_ref,
                    w_ref, h_scr, cos_scr, sneg_scr, spos_scr, stage_scr, stage4_scr, *, tm):
    p = pl.program_id(0)
    b = pl.program_id(1)
    zd_refs = {4: zd4_ref, 16: zd16_ref}
    n_slabs = Z_BLK // LANES

    @pl.when((p == 0) & (b == 0))
    def _regroup_weight():
        fox_end = 3 * ATT_WIDTH
        raw_width = w_raw_ref.shape[1]
        first = fox_end // LANES

        def group(i):
            lo = i * LANES
            return w_raw_ref[:, lo:min(lo + LANES, raw_width)].astype(F32)

        w_ref[:, 0:fox_end] = w_raw_ref[:, 0:fox_end]
        for k in range((Z_WIDTH - fox_end) // LANES):
            shifted = jnp.concatenate(
                [group(first + k)[:, N_HEADS:], group(first + k + 1)[:, :N_HEADS]], axis=1)
            w_ref[:, fox_end + k * LANES:fox_end + (k + 1) * LANES] = shifted.astype(BF)
        w_ref[:, Z_WIDTH:Z_WIDTH + LANES] = jnp.concatenate(
            [group(first)[:, :N_HEADS], jnp.zeros((D_MODEL, LANES - N_HEADS), F32)],
            axis=1).astype(BF)

    @pl.when(b == 0)
    def _tables():
        pos = (p * tm + lax.broadcasted_iota(jnp.int32, (tm, LANES), 0)).astype(F32)
        lane = lax.broadcasted_iota(jnp.int32, (tm, LANES), 1)
        c = lane % HEAD_DIM
        ang = pos * freq_ref[...]
        cs = jnp.cos(ang)
        sn = jnp.sin(ang)
        cos_scr[...] = jnp.where(c < ROPE_DIM, cs, 1.0)
        sneg_scr[...] = jnp.where(c < ROPE_HALF, -sn, 0.0)
        spos_scr[...] = jnp.where((c >= ROPE_HALF) & (c < ROPE_DIM), sn, 0.0)

    def rope(t):
        up = pltpu.roll(t, LANES - ROPE_HALF, 1)
        dn = pltpu.roll(t, ROPE_HALF, 1)
        return t * cos_scr[...] + up * sneg_scr[...] + dn * spos_scr[...]

    def emit_dilated(section, slab, val):
        out_lanes = slice(section * Z_BLK + slab * LANES, section * Z_BLK + (slab + 1) * LANES)
        z_ref[:, (DIL_SECTION + section) * Z_BLK + slab * LANES:
              (DIL_SECTION + section) * Z_BLK + (slab + 1) * LANES] = val.astype(BF)
        stage, stage4 = stage_scr.at[section, slab], stage4_scr.at[section, slab]
        stage[...] = val
        seg4 = tm // 4
        for c in range(4):
            part = stage[pl.ds(c, seg4, stride=4), :]
            stage4[c * seg4:(c + 1) * seg4, :] = part
            zd_refs[4][c * seg4:(c + 1) * seg4, out_lanes] = part.astype(BF)
        seg16 = tm // 16
        for jj in range(16):
            c, a = jj % 4, jj // 4
            part = stage4[pl.ds(c * seg4 + a, seg16, stride=4), :]
            zd_refs[16][jj * seg16:(jj + 1) * seg16, out_lanes] = part.astype(BF)

    h = _rms(x_ref[...], g_ref[...]).astype(BF)
    h_scr[...] = h
    fa_ref[...] = _dot(h, w_ref[:, Z_WIDTH:Z_WIDTH + LANES])

    for j in range(Z_WIDTH // Z_BLK):
        cols = slice(j * Z_BLK, (j + 1) * Z_BLK)
        acc = _dot(h_scr[...], w_ref[:, cols])
        section = j - DIL_SECTION
        if j == 0:
            z_ref[:, cols] = (acc * (Q_SCALE * LOG2_E)).astype(BF)
        elif 0 <= section < 3:
            for slab in range(n_slabs):
                t = acc[:, slab * LANES:(slab + 1) * LANES]
                if section == 0:
                    t = rope(t) * (Q_SCALE * LOG2_E)
                elif section == 1:
                    t = rope(t)
                emit_dilated(section, slab, t)
        else:
            z_ref[:, cols] = acc.astype(BF)


def _in_proj(x2d, g, w_raw, freq_lanes, *, batch, seq, tm):
    n_p = seq // tm
    tokens = batch * seq
    row = lambda p, b: (b * n_p + p, 0)
    resident = lambda shape: pl.BlockSpec(shape, lambda p, b: (0, 0),
                                          pipeline_mode=pl.Buffered(1))
    dil_shape = jax.ShapeDtypeStruct((tokens, 3 * ATT_WIDTH), BF)
    stage_buf = pltpu.VMEM((3, Z_BLK // LANES, tm, LANES), F32)
    return pl.pallas_call(
        functools.partial(_in_proj_kernel, tm=tm),
        grid=(n_p, batch),
        in_specs=[
            pl.BlockSpec((tm, D_MODEL), row),
            resident((1, D_MODEL)),
            resident((D_MODEL, Z_WIDTH + N_HEADS)),
            resident((1, LANES)),
        ],
        out_specs=[
            pl.BlockSpec((tm, Z_WIDTH), row),
            pl.BlockSpec((tm, LANES), row),
            pl.BlockSpec((tm, 3 * ATT_WIDTH), row),
            pl.BlockSpec((tm, 3 * ATT_WIDTH), row),
        ],
        out_shape=[
            jax.ShapeDtypeStruct((tokens, Z_WIDTH), BF),
            jax.ShapeDtypeStruct((tokens, LANES), F32),
            dil_shape,
            dil_shape,
        ],
        scratch_shapes=[
            pltpu.VMEM((D_MODEL, Z_WIDTH + LANES), BF),
            pltpu.VMEM((tm, D_MODEL), BF),
            pltpu.VMEM((tm, LANES), F32),
            pltpu.VMEM((tm, LANES), F32),
            pltpu.VMEM((tm, LANES), F32),
            stage_buf,
            stage_buf,
        ],
        compiler_params=_cparams(("arbitrary", "arbitrary")),
        name="in_proj",
    )(x2d, g, w_raw, freq_lanes)


def _extras_base(head):
    return (HEAD_DIM if head % 2 == 0 else 0) + 2 * N_PIECES * (head // 2)


def _forget_scan_kernel(fa_ref, bias_ref, tri_ref, pq_ref, pk_ref, oq_ref, ok_ref,
                        gq_ref, gk_ref):
    chunk = tri_ref.shape[0]
    n_chunks = fa_ref.shape[0] // chunk
    t = fa_ref[...] + bias_ref[...]
    log_f = jnp.minimum(t, 0.0) - jnp.log1p(jnp.exp(-jnp.abs(t)))
    wide = jnp.concatenate(
        [log_f[c * chunk:(c + 1) * chunk, :] for c in range(n_chunks)], axis=1)
    tri = tri_ref[...]
    local = sum(_dot(tri, piece) for piece in _split3(wide))
    carry = jnp.zeros((1, LANES), F32)
    parts = []
    for c in range(n_chunks):
        part = local[:, c * LANES:(c + 1) * LANES] + carry
        parts.append(part)
        carry = part[chunk - 1:chunk, :]
    run = jnp.concatenate(parts, axis=0)
    gq = oq_ref[...].astype(F32)
    gk = ok_ref[...].astype(F32)
    for i, piece in enumerate(_split3(run * LOG2_E)):
        gq = gq + _dot(piece, pq_ref[i])
        gk = gk - _dot(piece, pk_ref[i])
    gq_ref[0] = gq.astype(BF)
    gk_ref[0] = gk.astype(BF)


def _forget_scan_constants(chunk):
    tri = np.tril(np.ones((chunk, chunk), np.float32))
    pq = np.zeros((N_PIECES, LANES, LANES), np.float32)
    pk = np.zeros((N_PIECES, LANES, LANES), np.float32)
    oq = np.zeros((1, LANES), np.float32)
    ok = np.zeros((1, LANES), np.float32)
    for h in range(N_HEADS):
        base = _extras_base(h)
        for i in range(N_PIECES):
            pq[i, h, base + i] = 1.0
            oq[0, base + N_PIECES + i] = 1.0
            ok[0, base + i] = 1.0
            pk[i, h, base + N_PIECES + i] = 1.0
    as_bf = lambda a: jnp.asarray(a, BF)
    return as_bf(tri), as_bf(pq), as_bf(pk), as_bf(oq), as_bf(ok)


def _forget_scan(fa, bias_lanes, *, batch, seq, chunk):
    tri, pq, pk, oq, ok = _forget_scan_constants(chunk)
    const2 = lambda b: (0, 0)
    const3 = lambda b: (0, 0, 0)
    return pl.pallas_call(
        _forget_scan_kernel,
        grid=(batch,),
        in_specs=[
            pl.BlockSpec((seq, LANES), lambda b: (b, 0)),
            pl.BlockSpec((1, LANES), const2),
            pl.BlockSpec((chunk, chunk), const2),
            pl.BlockSpec((N_PIECES, LANES, LANES), const3),
            pl.BlockSpec((N_PIECES, LANES, LANES), const3),
            pl.BlockSpec((1, LANES), const2),
            pl.BlockSpec((1, LANES), const2),
        ],
        out_specs=[
            pl.BlockSpec((1, seq, LANES), lambda b: (b, 0, 0)),
            pl.BlockSpec((1, seq, LANES), lambda b: (b, 0, 0)),
        ],
        out_shape=[
            jax.ShapeDtypeStruct((batch, seq, LANES), BF),
            jax.ShapeDtypeStruct((batch, seq, LANES), BF),
        ],
        compiler_params=_cparams(("arbitrary",)),
        name="forget_scan",
    )(fa, bias_lanes, tri, pq, pk, oq, ok)


def _with_bias_lanes(slab, extras, hp, parity):
    lane = lax.broadcasted_iota(jnp.int32, slab.shape, 1)
    own = (lane < HEAD_DIM) if parity == 0 else (lane >= HEAD_DIM)
    base = (HEAD_DIM if parity == 0 else 0) + 2 * N_PIECES * hp
    in_extras = (lane >= base) & (lane < base + 2 * N_PIECES)
    return jnp.where(own, slab, jnp.where(in_extras, extras, jnp.zeros_like(extras)))


def _fox_kernel(q_ref, k_ref, v_ref, gq_ref, gk_ref, o_ref,
                ka_scr, kb_scr, vta_scr, vtb_scr, qp_scr, bias_scr, s_scr, acc_scr, ot_scr,
                bound_smem, *, tq):
    hp = pl.program_id(1)
    k_scrs = (ka_scr, kb_scr)
    vt_scrs = (vta_scr, vtb_scr)
    seq = k_ref.shape[0]
    n_q = seq // tq
    lane_row = lax.broadcasted_iota(jnp.int32, (1, LANES), 1)

    def own_lanes(head, shape):
        lane = lax.broadcasted_iota(jnp.int32, shape, 1)
        return (lane < HEAD_DIM) if head == 0 else (lane >= HEAD_DIM)

    def max_row_norm(slab, head):
        x = jnp.where(own_lanes(head, slab.shape), slab.astype(F32), 0.0)
        sq = jnp.sum(x * x, axis=1, keepdims=True)
        return jnp.sqrt(jnp.max(sq, axis=0, keepdims=True))

    def bias_lane_sum(row, head, first):
        base = (HEAD_DIM if head == 0 else 0) + 2 * N_PIECES * hp + first
        picked = (lane_row >= base) & (lane_row < base + N_PIECES)
        return jnp.sum(jnp.where(picked, row, 0.0), axis=1, keepdims=True)

    k2 = k_ref[...]
    gk = gk_ref[0]
    for head in range(2):
        k_scrs[head][...] = _with_bias_lanes(k2, gk, hp, head)
        bound_smem[head, n_q] = max_row_norm(k2, head)[0, 0]
        for j in range(n_q):
            rows16 = gk_ref[0, (j + 1) * tq - BF16_ROWS:(j + 1) * tq, :].astype(F32)
            last = rows16[BF16_ROWS - 1:BF16_ROWS, :]
            bound_smem[head, j] = -bias_lane_sum(last, head, N_PIECES)[0, 0]
    vt = v_ref[...].astype(F32).T.astype(BF)
    ones = jnp.ones((BF16_ROWS, seq), BF)
    for head, vt_scr in enumerate(vt_scrs):
        vt_scr[0:HEAD_DIM, :] = vt[head * HEAD_DIM:(head + 1) * HEAD_DIM, :]
        vt_scr[HEAD_DIM:, :] = ones
    row = lax.broadcasted_iota(jnp.int32, (tq, tq), 0)
    col = lax.broadcasted_iota(jnp.int32, (tq, tq), 1)
    bias_scr[...] = jnp.where(row <= col, 0.0, NEG_INF)
    acc_scr[...] = jnp.zeros_like(acc_scr)

    def scores(head, tile, masked):
        kt = k_scrs[head][pl.ds(pl.multiple_of(tile * tq, tq), tq), :]
        st = _dot_nt(kt, qp_scr[head])
        if masked:
            st = st + bias_scr[...]
        s_scr[head] = st
        return jnp.max(st, axis=0, keepdims=True)

    def softmax_pv(head, st, tile, tile_max, m):
        start = pl.multiple_of(tile * tq, tq)
        m_new = jnp.maximum(m, tile_max)
        alpha = jnp.exp2(m - m_new)
        pt = jnp.exp2(st - m_new).astype(BF)
        vt_tile = vt_scrs[head][:, pl.ds(start, tq)]
        acc_scr[head] = alpha * acc_scr[head] + _dot(vt_tile, pt)
        return m_new

    def start_tile(qi):
        rows = pl.ds(pl.multiple_of(qi * tq, tq), tq)
        q2 = q_ref[rows, :]
        gq = gq_ref[0, rows, :]
        first_row = gq[0:BF16_ROWS, :].astype(F32)[0:1, :]
        maxes, slack = [], []
        for head in range(2):
            qp_scr[head] = _with_bias_lanes(q2, gq, hp, head)
            maxes.append(scores(head, qi, True))
            reach = (max_row_norm(q2, head) * bound_smem[head, n_q]
                     + bias_lane_sum(first_row, head, 0) + SCORE_MARGIN)
            lowest_max = jnp.min(maxes[head], axis=1, keepdims=True)
            slack.append(reach - (lowest_max - UNDERFLOW_BITS))
        count = jnp.zeros((1, 1), jnp.int32)
        for j in range(n_q - 1):
            needed = (((slack[0] >= bound_smem[0, j]) | (slack[1] >= bound_smem[1, j]))
                      & (j < qi))
            count = count + jnp.where(needed, 1, 0)
        return maxes[0], maxes[1], count[0, 0]

    def finish_tile(qi):
        for head in range(2):
            acc = acc_scr[head]
            ot_scr[head * HEAD_DIM:(head + 1) * HEAD_DIM, :] = (
                acc[0:HEAD_DIM, :] / acc[HEAD_DIM:HEAD_DIM + 1, :])
        o_ref[pl.ds(pl.multiple_of(qi * tq, tq), tq), :] = ot_scr[...].T.astype(BF)
        acc_scr[...] = jnp.zeros_like(acc_scr)

    m_init = jnp.full((1, tq), NEG_INF, F32)

    def query_tile(qi, carry):
        first_a, first_b, n_needed = carry

        def key_tile(t, inner):
            max_a, max_b, m_a, m_b, prev_tile = inner
            prev = [s_scr[head] for head in range(2)]
            tile = qi - 1 - t
            next_max = [scores(head, tile, False) for head in range(2)]
            m_a = softmax_pv(0, prev[0], prev_tile, max_a, m_a)
            m_b = softmax_pv(1, prev[1], prev_tile, max_b, m_b)
            return next_max[0], next_max[1], m_a, m_b, tile

        max_a, max_b, m_a, m_b, prev_tile = lax.fori_loop(
            0, n_needed, key_tile, (first_a, first_b, m_init, m_init, qi))
        last = [s_scr[head] for head in range(2)]
        softmax_pv(0, last[0], prev_tile, max_a, m_a)
        softmax_pv(1, last[1], prev_tile, max_b, m_b)
        following = start_tile(jnp.minimum(qi + 1, n_q - 1))
        finish_tile(qi)
        return following

    lax.fori_loop(0, n_q, query_tile, start_tile(0))


def _fox_attention(z, gq, gk, *, batch, seq, tq):
    tokens = batch * seq
    col = lambda section: pl.BlockSpec(
        (seq, LANES), lambda b, hp: (b, section * N_HEAD_PAIRS + hp))
    whole = pl.BlockSpec((1, seq, LANES), lambda b, hp: (b, 0, 0))
    ext = HEAD_DIM + BF16_ROWS
    return pl.pallas_call(
        functools.partial(_fox_kernel, tq=tq),
        grid=(batch, N_HEAD_PAIRS),
        in_specs=[col(0), col(1), col(2), whole, whole],
        out_specs=pl.BlockSpec((seq, LANES), lambda b, hp: (b, hp)),
        out_shape=jax.ShapeDtypeStruct((tokens, ATT_WIDTH), BF),
        scratch_shapes=[
            pltpu.VMEM((seq, LANES), BF),
            pltpu.VMEM((seq, LANES), BF),
            pltpu.VMEM((ext, seq), BF),
            pltpu.VMEM((ext, seq), BF),
            pltpu.VMEM((2, tq, LANES), BF),
            pltpu.VMEM((tq, tq), F32),
            pltpu.VMEM((2, tq, tq), F32),
            pltpu.VMEM((2, ext, tq), F32),
            pltpu.VMEM((LANES, tq), F32),
            pltpu.SMEM((2, seq // tq + 1), F32),
        ],
        compiler_params=_cparams(("arbitrary", "arbitrary")),
        name="fox_attn",
    )(z, z, z, gq, gk)


def _log2(n):
    assert n > 0 and n & (n - 1) == 0, n
    return n.bit_length() - 1


def _dilated_kernel(q1, k1, v1, q4, k4, v4, q16, k16, v16, o_ref,
                    num1, num4, num16, den1, den4, den16, max1, max4, max16, bias_scr,
                    *, seq, blk, chunk, group, merge_rows):
    sources = ((q1, k1, v1), (q4, k4, v4), (q16, k16, v16))
    num_scrs = (num1, num4, num16)
    den_scrs = (den1, den4, den16)
    max_scrs = (max1, max4, max16)
    n_blocks = seq // blk

    low = lax.broadcasted_iota(jnp.int32, (blk, LANES), 1) < HEAD_DIM
    qrow = lax.broadcasted_iota(jnp.int32, (2 * blk, 2 * blk), 0) % blk
    kcol = lax.broadcasted_iota(jnp.int32, (2 * blk, 2 * blk), 1)
    dist = qrow + blk - kcol
    band = (dist >= 0) & (dist <= blk)
    bias_scr[0] = jnp.where(band & (kcol >= blk), 0.0, NEG_INF)
    bias_scr[1] = jnp.where(band, 0.0, NEG_INF)
    ones = jnp.ones((2 * blk, LANES), BF)

    for idx, (window, dilation) in enumerate(DIL_PATTERNS):
        assert window // dilation == blk
        q_ref, k_ref, v_ref = sources[idx]
        num_scr, den_scr, max_scr = num_scrs[idx], den_scrs[idx], max_scrs[idx]
        sub_shift = _log2(seq // dilation // blk)
        seg_rows = chunk // dilation
        seg = min(blk, seg_rows)

        def load(ref, j, l0):
            parts = []
            for s in range(blk // seg):
                l = l0 + s * seg
                p = lax.shift_right_logical(l, _log2(seg_rows))
                i = l & (seg_rows - 1)
                start = pl.multiple_of(p * chunk + j * seg_rows + i, seg)
                parts.append(ref[pl.ds(start, seg), :])
            return parts[0] if len(parts) == 1 else jnp.concatenate(parts, axis=0)

        def one_block(g):
            j = lax.shift_right_logical(g, sub_shift)
            gs = g & ((1 << sub_shift) - 1)
            l0 = gs * blk
            lp = jnp.maximum(l0 - blk, 0)
            q2 = load(q_ref, j, l0)
            zero = jnp.zeros_like(q2)
            qq = jnp.concatenate([jnp.where(low, q2, zero), jnp.where(low, zero, q2)], axis=0)
            kwin = jnp.concatenate([load(k_ref, j, lp), load(k_ref, j, l0)], axis=0)
            vwin = jnp.concatenate([load(v_ref, j, lp), load(v_ref, j, l0)], axis=0)
            s = _dot_nt(qq, kwin) + bias_scr[jnp.minimum(gs, 1)]
            m = jnp.max(s, axis=1, keepdims=True)
            p = jnp.exp2(s - m).astype(BF)
            pv = _dot(p, jnp.concatenate([vwin, ones], axis=1))
            if dilation == 1:
                dst = pl.ds(pl.multiple_of(l0, blk), blk)
            else:
                dst = pl.ds(l0 * dilation + j, blk, stride=dilation)
            num_scr[dst, :] = jnp.where(low, pv[:blk, :LANES], pv[blk:, :LANES])
            den_scr[dst, :] = jnp.where(low, pv[:blk, LANES:], pv[blk:, LANES:])
            max_scr[dst, :] = jnp.where(low, jnp.broadcast_to(m[:blk], (blk, LANES)),
                                        jnp.broadcast_to(m[blk:], (blk, LANES)))

        def blocks(it, carry):
            for u in range(group):
                one_block(it * group + u)
            return carry

        lax.fori_loop(0, n_blocks // group, blocks, 0)

    def merge(c, carry):
        rows = pl.ds(pl.multiple_of(c * merge_rows, merge_rows), merge_rows)
        maxes = [max_scr[rows, :] for max_scr in max_scrs]
        top = functools.reduce(jnp.maximum, maxes)
        weights = [jnp.exp2(m - top) for m in maxes]
        num = sum(w * num_scr[rows, :] for w, num_scr in zip(weights, num_scrs))
        den = sum(w * den_scr[rows, :] for w, den_scr in zip(weights, den_scrs))
        o_ref[rows, :] = (num / den).astype(BF)
        return carry

    lax.fori_loop(0, seq // merge_rows, merge, 0)


def _dilated_attention(z, zd4, zd16, *, batch, seq):
    blk = DIL_PATTERNS[0][0] // DIL_PATTERNS[0][1]
    tokens = batch * seq
    per_section = ATT_WIDTH // LANES
    col = lambda section: pl.BlockSpec(
        (seq, LANES), lambda b, hp: (b, section * per_section + hp))
    f32_buf = pltpu.VMEM((seq, LANES), F32)
    return pl.pallas_call(
        functools.partial(_dilated_kernel, seq=seq, blk=blk, chunk=DIL_CHUNK, group=32,
                          merge_rows=256),
        grid=(batch, N_HEAD_PAIRS),
        in_specs=[col(DIL_SECTION), col(DIL_SECTION + 1), col(DIL_SECTION + 2),
                  col(0), col(1), col(2), col(0), col(1), col(2)],
        out_specs=pl.BlockSpec((seq, LANES), lambda b, hp: (b, hp)),
        out_shape=jax.ShapeDtypeStruct((tokens, ATT_WIDTH), BF),
        scratch_shapes=[f32_buf] * 9 + [pltpu.VMEM((2, 2 * blk, 2 * blk), F32)],
        compiler_params=_cparams(("arbitrary", "arbitrary")),
        name="dilated_attn",
    )(z, z, z, zd4, zd4, zd4, zd16, zd16, zd16)


def _mix_kernel(ya_ref, yb_ref, ga_ref, gb_ref, x_ref, woa_ref, wob_ref, wout_ref,
                g_ref, o_ref):
    pa = _dot(ya_ref[...], woa_ref[...])
    pb = _dot(yb_ref[...], wob_ref[...])
    mixed = (jax.nn.sigmoid(ga_ref[...].astype(F32)) * pa
             + jax.nn.sigmoid(gb_ref[...].astype(F32)) * pb)
    y = _dot(mixed.astype(BF), wout_ref[...])
    o_ref[...] = x_ref[...] + _rms(y, g_ref[...])


def _mix(ya, yb, z, x2d, woa, wob, wout, g, *, tm):
    tokens = x2d.shape[0]
    gate_blk = lambda off: pl.BlockSpec((tm, D_MODEL), lambda i: (i, off))
    const = lambda i: (0, 0)
    return pl.pallas_call(
        _mix_kernel,
        grid=(tokens // tm,),
        in_specs=[
            pl.BlockSpec((tm, ATT_WIDTH), lambda i: (i, 0)),
            pl.BlockSpec((tm, ATT_WIDTH), lambda i: (i, 0)),
            gate_blk(3),
            gate_blk(4),
            pl.BlockSpec((tm, D_MODEL), lambda i: (i, 0)),
            pl.BlockSpec((ATT_WIDTH, D_MODEL), const),
            pl.BlockSpec((ATT_WIDTH, D_MODEL), const),
            pl.BlockSpec((D_MODEL, D_MODEL), const),
            pl.BlockSpec((1, D_MODEL), const),
        ],
        out_specs=pl.BlockSpec((tm, D_MODEL), lambda i: (i, 0)),
        out_shape=jax.ShapeDtypeStruct((tokens, D_MODEL), F32),
        compiler_params=_cparams(("arbitrary",)),
        name="mix",
    )(ya, yb, z, z, x2d, woa, wob, wout, g)


def _ffn_kernel(x_ref, halo_ref, gpre_ref, wup_ref, cw_ref, cb_ref, wd_ref, gpost_ref, o_ref,
                h_scr, ua0_scr, ub0_scr, ua1_scr, ub1_scr, acc_scr,
                *, tm, tf, tiles_per_seq, n_chunks):
    i = pl.program_id(0)
    halo = BF16_ROWS
    rows = tm // n_chunks
    n_tiles = D_FF // tf
    u_sets = ((ua0_scr, ub0_scr), (ua1_scr, ub1_scr))

    def cols(f, gate):
        return pl.ds(pl.multiple_of(f * tf + (D_FF if gate else 0), LANES), tf)

    def up_proj(f, dst, lo, hi):
        h = h_scr[lo:hi, :]
        dst[0][lo:hi, :] = _dot(h, wup_ref[:, cols(f, False)])
        dst[1][lo:hi, :] = _dot(h, wup_ref[:, cols(f, True)])

    def conv(u_scr, window, post_scale, first, n):
        out = cb_ref[:, window] * post_scale
        for tap in range(CONV_WIDTH):
            lo = halo + first - (CONV_WIDTH - 1) + tap
            out = out + (cw_ref[tap:tap + 1, window] * post_scale) * u_scr[lo:lo + n, :]
        return out

    def down_proj(f, src, first, n):
        a = conv(src[0], cols(f, False), 1.0, first, n)
        half_b = conv(src[1], cols(f, True), 0.5, first, n)
        c0 = math.sqrt(2.0 / math.pi)
        inner = a * (c0 + (c0 * 0.044715) * (a * a))
        hidden = ((a * half_b) * (1.0 + jnp.tanh(inner))).astype(BF)
        return _dot(hidden, wd_ref[pl.ds(pl.multiple_of(f * tf, tf), tf), :])

    def stage(f, parity):
        src, dst = u_sets[1 - parity], u_sets[parity]
        for c in range(n_chunks):
            lo = 0 if c == 0 else halo + c * rows
            up_proj(f, dst, lo, halo + (c + 1) * rows)
            out_rows = slice(c * rows, (c + 1) * rows)
            acc_scr[out_rows, :] += down_proj(f - 1, src, c * rows, rows)

    g = gpre_ref[...]
    hh = _rms(halo_ref[...], g)
    hh = jnp.where(i % tiles_per_seq == 0, jnp.zeros_like(hh), hh)
    h_scr[0:halo, :] = hh.astype(BF)
    h_scr[halo:, :] = _rms(x_ref[...], g).astype(BF)
    acc_scr[...] = jnp.zeros_like(acc_scr)
    up_proj(0, u_sets[0], 0, tm + halo)

    def pair(k, carry):
        stage(2 * k + 1, 1)
        stage(2 * k + 2, 0)
        return carry

    assert n_tiles % 2 == 1
    lax.fori_loop(0, (n_tiles - 1) // 2, pair, 0)
    y = acc_scr[...] + down_proj(n_tiles - 1, u_sets[0], 0, tm)
    o_ref[...] = x_ref[...] + _rms(y, gpost_ref[...])


def _ffn(x1, g_pre, w_up, conv_w, conv_b, w_down, g_post, *, seq, tm, tf):
    tokens = x1.shape[0]
    halo = BF16_ROWS
    per = tm // halo
    resident = lambda shape: pl.BlockSpec(shape, lambda i: (0, 0),
                                          pipeline_mode=pl.Buffered(1))
    u_buf = pltpu.VMEM((tm + halo, tf), F32)
    return pl.pallas_call(
        functools.partial(_ffn_kernel, tm=tm, tf=tf, tiles_per_seq=seq // tm, n_chunks=4),
        grid=(tokens // tm,),
        in_specs=[
            pl.BlockSpec((tm, D_MODEL), lambda i: (i, 0)),
            pl.BlockSpec((halo, D_MODEL), lambda i: (jnp.maximum(i * per - 1, 0), 0)),
            resident((1, D_MODEL)),
            resident((D_MODEL, 2 * D_FF)),
            resident((CONV_WIDTH, 2 * D_FF)),
            resident((1, 2 * D_FF)),
            resident((D_FF, D_MODEL)),
            resident((1, D_MODEL)),
        ],
        out_specs=pl.BlockSpec((tm, D_MODEL), lambda i: (i, 0)),
        out_shape=jax.ShapeDtypeStruct((tokens, D_MODEL), F32),
        scratch_shapes=[
            pltpu.VMEM((tm + halo, D_MODEL), BF),
            u_buf, u_buf, u_buf, u_buf,
            pltpu.VMEM((tm, D_MODEL), F32),
        ],
        compiler_params=_cparams(("arbitrary",)),
        name="ffn",
    )(x1, x1, g_pre, w_up, conv_w, conv_b, w_down, g_post)


def _rope_freq_lanes():
    inv_freq = ROPE_THETA ** (-jnp.arange(ROPE_HALF, dtype=F32) * 2.0 / ROPE_DIM)
    return jnp.tile(inv_freq, LANES // ROPE_HALF).reshape(1, LANES)


def kernel(x, g_pre_mix, w_in, b_forget, w_o_fox, w_o_dil, w_out, g_post_mix,
           g_pre_ffn, w_up, conv_w, conv_b, w_down, g_post_ffn):
    batch, seq, d_model = x.shape
    assert d_model == D_MODEL and seq % 1024 == 0
    depth = w_in.shape[0]
    freq_lanes = _rope_freq_lanes()
    x2d = x.reshape(batch * seq, D_MODEL)
    row = lambda v: v.reshape(1, -1)
    for l in range(depth):
        bias_lanes = jnp.pad(b_forget[l], (0, LANES - N_HEADS)).reshape(1, LANES)

        z, fa, zd4, zd16 = _in_proj(x2d, row(g_pre_mix[l]), w_in[l].astype(BF), freq_lanes,
                                    batch=batch, seq=seq, tm=DIL_CHUNK)
        gq, gk = _forget_scan(fa, bias_lanes, batch=batch, seq=seq, chunk=512)
        ya = _fox_attention(z, gq, gk, batch=batch, seq=seq, tq=512)
        yb = _dilated_attention(z, zd4, zd16, batch=batch, seq=seq)
        x2d = _mix(ya, yb, z, x2d, w_o_fox[l].astype(BF), w_o_dil[l].astype(BF),
                   w_out[l].astype(BF), row(g_post_mix[l]), tm=1024)
        x2d = _ffn(x2d, row(g_pre_ffn[l]), w_up[l].astype(BF), conv_w[l],
                   row(conv_b[l]), w_down[l].astype(BF), row(g_post_ffn[l]),
                   seq=seq, tm=1024, tf=256)
    return x2d.reshape(batch, seq, D_MODEL)
```

```python
import functools
import math

import numpy as np
import jax
import jax.numpy as jnp
from jax import lax
from jax.experimental import pallas as pl
from jax.experimental.pallas import tpu as pltpu

D_MODEL = 1024
HEAD_DIM = 64
N_HEADS = 8
ATT_WIDTH = N_HEADS * HEAD_DIM
N_HEAD_PAIRS = N_HEADS // 2
DIL_PATTERNS = ((128, 1), (512, 4), (2048, 16))
ROPE_DIM = HEAD_DIM // 4
ROPE_HALF = ROPE_DIM // 2
ROPE_THETA = 500000.0
D_FF = 2816
CONV_WIDTH = 3
RMS_EPS = 1e-6
NEG_INF = -1e30
Q_SCALE = 1.0 / math.sqrt(HEAD_DIM)
LOG2_E = math.log2(math.e)
UNDERFLOW_BITS = 150.0
SCORE_MARGIN = 1.0

LANES = 128
SUBLANES = 8
BF16_ROWS = 16
Z_WIDTH = 3 * ATT_WIDTH * 2 + 2 * D_MODEL
Z_BLK = 512
DIL_SECTION = 3
DIL_CHUNK = 512
N_PIECES = 3
VMEM_LIMIT = 56 * 1024 * 1024

BF = jnp.bfloat16
F32 = jnp.float32


def _cparams(sem, flags=None):
    return pltpu.CompilerParams(dimension_semantics=sem, vmem_limit_bytes=VMEM_LIMIT,
                                flags=flags)


def _rms(xf, g):
    inv = lax.rsqrt(jnp.mean(xf * xf, axis=-1, keepdims=True) + RMS_EPS)
    return xf * inv * g


def _split3(x):
    hi = x.astype(BF)
    r1 = x - hi.astype(F32)
    mid = r1.astype(BF)
    lo = (r1 - mid.astype(F32)).astype(BF)
    return hi, mid, lo


def _dot(a, b):
    return jnp.dot(a, b, preferred_element_type=F32)


def _dot_nt(a, b):
    return lax.dot_general(a, b, (((1,), (1,)), ((), ())), preferred_element_type=F32)


def _in_proj_kernel(x_ref, g_ref, w_raw_ref, freq_ref, z_ref, fa_ref, zd4_ref, zd16_ref,
                    w_ref, h_scr, cos_scr, sneg_scr, spos_scr, stage_scr, stage4_scr, *, tm):
    p = pl.program_id(0)
    b = pl.program_id(1)
    zd_refs = {4: zd4_ref, 16: zd16_ref}
    n_slabs = Z_BLK // LANES

    @pl.when((p == 0) & (b == 0))
    def _regroup_weight():
        fox_end = 3 * ATT_WIDTH
        raw_width = w_raw_ref.shape[1]
        first = fox_end // LANES

        def group(i):
            lo = i * LANES
            return w_raw_ref[:, lo:min(lo + LANES, raw_width)].astype(F32)

        w_ref[:, 0:fox_end] = w_raw_ref[:, 0:fox_end]
        for k in range((Z_WIDTH - fox_end) // LANES):
            shifted = jnp.concatenate(
                [group(first + k)[:, N_HEADS:], group(first + k + 1)[:, :N_HEADS]], axis=1)
            w_ref[:, fox_end + k * LANES:fox_end + (k + 1) * LANES] = shifted.astype(BF)
        w_ref[:, Z_WIDTH:Z_WIDTH + LANES] = jnp.concatenate(
            [group(first)[:, :N_HEADS], jnp.zeros((D_MODEL, LANES - N_HEADS), F32)],
            axis=1).astype(BF)

    @pl.when(b == 0)
    def _tables():
        pos = (p * tm + lax.broadcasted_iota(jnp.int32, (tm, LANES), 0)).astype(F32)
        lane = lax.broadcasted_iota(jnp.int32, (tm, LANES), 1)
        c = lane % HEAD_DIM
        ang = pos * freq_ref[...]
        cs = jnp.cos(ang)
        sn = jnp.sin(ang)
        cos_scr[...] = jnp.where(c < ROPE_DIM, cs, 1.0)
        sneg_scr[...] = jnp.where(c < ROPE_HALF, -sn, 0.0)
        spos_scr[...] = jnp.where((c >= ROPE_HALF) & (c < ROPE_DIM), sn, 0.0)

    def rope(t):
        up = pltpu.roll(t, LANES - ROPE_HALF, 1)
        dn = pltpu.roll(t, ROPE_HALF, 1)
        return t * cos_scr[...] + up * sneg_scr[...] + dn * spos_scr[...]

    def emit_dilated(section, slab, val):
        out_lanes = slice(section * Z_BLK + slab * LANES, section * Z_BLK + (slab + 1) * LANES)
        z_ref[:, (DIL_SECTION + section) * Z_BLK + slab * LANES:
              (DIL_SECTION + section) * Z_BLK + (slab + 1) * LANES] = val.astype(BF)
        stage, stage4 = stage_scr.at[section, slab], stage4_scr.at[section, slab]
        stage[...] = val
        seg4 = tm // 4
        for c in range(4):
            part = stage[pl.ds(c, seg4, stride=4), :]
            stage4[c * seg4:(c + 1) * seg4, :] = part
            zd_refs[4][c * seg4:(c + 1) * seg4, out_lanes] = part.astype(BF)
        seg16 = tm // 16
        for jj in range(16):
            c, a = jj % 4, jj // 4
            part = stage4[pl.ds(c * seg4 + a, seg16, stride=4), :]
            zd_refs[16][jj * seg16:(jj + 1) * seg16, out_lanes] = part.astype(BF)

    h = _rms(x_ref[...], g_ref[...]).astype(BF)
    h_scr[...] = h
    fa_ref[...] = _dot(h, w_ref[:, Z_WIDTH:Z_WIDTH + LANES])

    for j in range(Z_WIDTH // Z_BLK):
        cols = slice(j * Z_BLK, (j + 1) * Z_BLK)
        acc = _dot(h_scr[...], w_ref[:, cols])
        section = j - DIL_SECTION
        if j == 0:
            z_ref[:, cols] = (acc * (Q_SCALE * LOG2_E)).astype(BF)
        elif 0 <= section < 3:
            for slab in range(n_slabs):
                t = acc[:, slab * LANES:(slab + 1) * LANES]
                if section == 0:
                    t = rope(t) * (Q_SCALE * LOG2_E)
                elif section == 1:
                    t = rope(t)
                emit_dilated(section, slab, t)
        else:
            z_ref[:, cols] = acc.astype(BF)


def _in_proj(x2d, g, w_raw, freq_lanes, *, batch, seq, tm):
    n_p = seq // tm
    tokens = batch * seq
    row = lambda p, b: (b * n_p + p, 0)
    resident = lambda shape: pl.BlockSpec(shape, lambda p, b: (0, 0),
                                          pipeline_mode=pl.Buffered(1))
    dil_shape = jax.ShapeDtypeStruct((tokens, 3 * ATT_WIDTH), BF)
    stage_buf = pltpu.VMEM((3, Z_BLK // LANES, tm, LANES), F32)
    return pl.pallas_call(
        functools.partial(_in_proj_kernel, tm=tm),
        grid=(n_p, batch),
        in_specs=[
            pl.BlockSpec((tm, D_MODEL), row),
            resident((1, D_MODEL)),
            resident((D_MODEL, Z_WIDTH + N_HEADS)),
            resident((1, LANES)),
        ],
        out_specs=[
            pl.BlockSpec((tm, Z_WIDTH), row),
            pl.BlockSpec((tm, LANES), row),
            pl.BlockSpec((tm, 3 * ATT_WIDTH), row),
            pl.BlockSpec((tm, 3 * ATT_WIDTH), row),
        ],
        out_shape=[
            jax.ShapeDtypeStruct((tokens, Z_WIDTH), BF),
            jax.ShapeDtypeStruct((tokens, LANES), F32),
            dil_shape,
            dil_shape,
        ],
        scratch_shapes=[
            pltpu.VMEM((D_MODEL, Z_WIDTH + LANES), BF),
            pltpu.VMEM((tm, D_MODEL), BF),
            pltpu.VMEM((tm, LANES), F32),
            pltpu.VMEM((tm, LANES), F32),
            pltpu.VMEM((tm, LANES), F32),
            stage_buf,
            stage_buf,
        ],
        compiler_params=_cparams(("arbitrary", "arbitrary")),
        name="in_proj",
    )(x2d, g, w_raw, freq_lanes)


def _extras_base(head):
    return (HEAD_DIM if head % 2 == 0 else 0) + 2 * N_PIECES * (head // 2)


def _forget_scan_kernel(fa_ref, bias_ref, tri_ref, pq_ref, pk_ref, oq_ref, ok_ref,
                        gq_ref, gk_ref):
    chunk = tri_ref.shape[0]
    n_chunks = fa_ref.shape[0] // chunk
    t = fa_ref[...] + bias_ref[...]
    log_f = jnp.minimum(t, 0.0) - jnp.log1p(jnp.exp(-jnp.abs(t)))
    wide = jnp.concatenate(
        [log_f[c * chunk:(c + 1) * chunk, :] for c in range(n_chunks)], axis=1)
    tri = tri_ref[...]
    local = sum(_dot(tri, piece) for piece in _split3(wide))
    carry = jnp.zeros((1, LANES), F32)
    parts = []
    for c in range(n_chunks):
        part = local[:, c * LANES:(c + 1) * LANES] + carry
        parts.append(part)
        carry = part[chunk - 1:chunk, :]
    run = jnp.concatenate(parts, axis=0)
    gq = oq_ref[...].astype(F32)
    gk = ok_ref[...].astype(F32)
    for i, piece in enumerate(_split3(run * LOG2_E)):
        gq = gq + _dot(piece, pq_ref[i])
        gk = gk - _dot(piece, pk_ref[i])
    gq_ref[0] = gq.astype(BF)
    gk_ref[0] = gk.astype(BF)


def _forget_scan_constants(chunk):
    tri = np.tril(np.ones((chunk, chunk), np.float32))
    pq = np.zeros((N_PIECES, LANES, LANES), np.float32)
    pk = np.zeros((N_PIECES, LANES, LANES), np.float32)
    oq = np.zeros((1, LANES), np.float32)
    ok = np.zeros((1, LANES), np.float32)
    for h in range(N_HEADS):
        base = _extras_base(h)
        for i in range(N_PIECES):
            pq[i, h, base + i] = 1.0
            oq[0, base + N_PIECES + i] = 1.0
            ok[0, base + i] = 1.0
            pk[i, h, base + N_PIECES + i] = 1.0
    as_bf = lambda a: jnp.asarray(a, BF)
    return as_bf(tri), as_bf(pq), as_bf(pk), as_bf(oq), as_bf(ok)


def _forget_scan(fa, bias_lanes, *, batch, seq, chunk):
    tri, pq, pk, oq, ok = _forget_scan_constants(chunk)
    const2 = lambda b: (0, 0)
    const3 = lambda b: (0, 0, 0)
    return pl.pallas_call(
        _forget_scan_kernel,
        grid=(batch,),
        in_specs=[
            pl.BlockSpec((seq, LANES), lambda b: (b, 0)),
            pl.BlockSpec((1, LANES), const2),
            pl.BlockSpec((chunk, chunk), const2),
            pl.BlockSpec((N_PIECES, LANES, LANES), const3),
            pl.BlockSpec((N_PIECES, LANES, LANES), const3),
            pl.BlockSpec((1, LANES), const2),
            pl.BlockSpec((1, LANES), const2),
        ],
        out_specs=[
            pl.BlockSpec((1, seq, LANES), lambda b: (b, 0, 0)),
            pl.BlockSpec((1, seq, LANES), lambda b: (b, 0, 0)),
        ],
        out_shape=[
            jax.ShapeDtypeStruct((batch, seq, LANES), BF),
            jax.ShapeDtypeStruct((batch, seq, LANES), BF),
        ],
        compiler_params=_cparams(("arbitrary",)),
        name="forget_scan",
    )(fa, bias_lanes, tri, pq, pk, oq, ok)


def _with_bias_lanes(slab, extras, hp, parity):
    lane = lax.broadcasted_iota(jnp.int32, slab.shape, 1)
    own = (lane < HEAD_DIM) if parity == 0 else (lane >= HEAD_DIM)
    base = (HEAD_DIM if parity == 0 else 0) + 2 * N_PIECES * hp
    in_extras = (lane >= base) & (lane < base + 2 * N_PIECES)
    return jnp.where(own, slab, jnp.where(in_extras, extras, jnp.zeros_like(extras)))


def _fox_kernel(q_ref, k_ref, v_ref, gq_ref, gk_ref, o_ref,
                ka_scr, kb_scr, vta_scr, vtb_scr, qp_scr, bias_scr, s_scr, acc_scr, ot_scr,
                bound_smem, *, tq):
    hp = pl.program_id(1)
    k_scrs = (ka_scr, kb_scr)
    vt_scrs = (vta_scr, vtb_scr)
    seq = k_ref.shape[0]
    n_q = seq // tq
    lane_row = lax.broadcasted_iota(jnp.int32, (1, LANES), 1)

    def own_lanes(head, shape):
        lane = lax.broadcasted_iota(jnp.int32, shape, 1)
        return (lane < HEAD_DIM) if head == 0 else (lane >= HEAD_DIM)

    def max_row_norm(slab, head):
        x = jnp.where(own_lanes(head, slab.shape), slab.astype(F32), 0.0)
        sq = jnp.sum(x * x, axis=1, keepdims=True)
        return jnp.sqrt(jnp.max(sq, axis=0, keepdims=True))

    def bias_lane_sum(row, head, first):
        base = (HEAD_DIM if head == 0 else 0) + 2 * N_PIECES * hp + first
        picked = (lane_row >= base) & (lane_row < base + N_PIECES)
        return jnp.sum(jnp.where(picked, row, 0.0), axis=1, keepdims=True)

    k2 = k_ref[...]
    gk = gk_ref[0]
    for head in range(2):
        k_scrs[head][...] = _with_bias_lanes(k2, gk, hp, head)
        bound_smem[head, n_q] = max_row_norm(k2, head)[0, 0]
        for j in range(n_q):
            rows16 = gk_ref[0, (j + 1) * tq - BF16_ROWS:(j + 1) * tq, :].astype(F32)
            last = rows16[BF16_ROWS - 1:BF16_ROWS, :]
            bound_smem[head, j] = -bias_lane_sum(last, head, N_PIECES)[0, 0]
    vt = v_ref[...].astype(F32).T.astype(BF)
    ones = jnp.ones((BF16_ROWS, seq), BF)
    for head, vt_scr in enumerate(vt_scrs):
        vt_scr[0:HEAD_DIM, :] = vt[head * HEAD_DIM:(head + 1) * HEAD_DIM, :]
        vt_scr[HEAD_DIM:, :] = ones
    row = lax.broadcasted_iota(jnp.int32, (tq, tq), 0)
    col = lax.broadcasted_iota(jnp.int32, (tq, tq), 1)
    bias_scr[...] = jnp.where(row <= col, 0.0, NEG_INF)
    acc_scr[...] = jnp.zeros_like(acc_scr)

    def scores(head, tile, masked):
        kt = k_scrs[head][pl.ds(pl.multiple_of(tile * tq, tq), tq), :]
        st = _dot_nt(kt, qp_scr[head])
        if masked:
            st = st + bias_scr[...]
        s_scr[head] = st
        return jnp.max(st, axis=0, keepdims=True)

    def softmax_pv(head, st, tile, tile_max, m):
        start = pl.multiple_of(tile * tq, tq)
        m_new = jnp.maximum(m, tile_max)
        alpha = jnp.exp2(m - m_new)
        pt = jnp.exp2(st - m_new).astype(BF)
        vt_tile = vt_scrs[head][:, pl.ds(start, tq)]
        acc_scr[head] = alpha * acc_scr[head] + _dot(vt_tile, pt)
        return m_new

    def start_tile(qi):
        rows = pl.ds(pl.multiple_of(qi * tq, tq), tq)
        q2 = q_ref[rows, :]
        gq = gq_ref[0, rows, :]
        first_row = gq[0:BF16_ROWS, :].astype(F32)[0:1, :]
        maxes, slack = [], []
        for head in range(2):
            qp_scr[head] = _with_bias_lanes(q2, gq, hp, head)
            maxes.append(scores(head, qi, True))
            reach = (max_row_norm(q2, head) * bound_smem[head, n_q]
                     + bias_lane_sum(first_row, head, 0) + SCORE_MARGIN)
            lowest_max = jnp.min(maxes[head], axis=1, keepdims=True)
            slack.append(reach - (lowest_max - UNDERFLOW_BITS))
        count = jnp.zeros((1, 1), jnp.int32)
        for j in range(n_q - 1):
            needed = (((slack[0] >= bound_smem[0, j]) | (slack[1] >= bound_smem[1, j]))
                      & (j < qi))
            count = count + jnp.where(needed, 1, 0)
        return maxes[0], maxes[1], count[0, 0]

    def finish_tile(qi):
        for head in range(2):
            acc = acc_scr[head]
            ot_scr[head * HEAD_DIM:(head + 1) * HEAD_DIM, :] = (
                acc[0:HEAD_DIM, :] / acc[HEAD_DIM:HEAD_DIM + 1, :])
        o_ref[pl.ds(pl.multiple_of(qi * tq, tq), tq), :] = ot_scr[...].T.astype(BF)
        acc_scr[...] = jnp.zeros_like(acc_scr)

    m_init = jnp.full((1, tq), NEG_INF, F32)

    def query_tile(qi, carry):
        first_a, first_b, n_needed = carry

        def key_tile(t, inner):
            max_a, max_b, m_a, m_b, prev_tile = inner
            prev = [s_scr[head] for head in range(2)]
            tile = qi - 1 - t
            next_max = [scores(head, tile, False) for head in range(2)]
            m_a = softmax_pv(0, prev[0], prev_tile, max_a, m_a)
            m_b = softmax_pv(1, prev[1], prev_tile, max_b, m_b)
            return next_max[0], next_max[1], m_a, m_b, tile

        max_a, max_b, m_a, m_b, prev_tile = lax.fori_loop(
            0, n_needed, key_tile, (first_a, first_b, m_init, m_init, qi))
        last = [s_scr[head] for head in range(2)]
        softmax_pv(0, last[0], prev_tile, max_a, m_a)
        softmax_pv(1, last[1], prev_tile, max_b, m_b)
        following = start_tile(jnp.minimum(qi + 1, n_q - 1))
        finish_tile(qi)
        return following

    lax.fori_loop(0, n_q, query_tile, start_tile(0))


def _fox_attention(z, gq, gk, *, batch, seq, tq):
    tokens = batch * seq
    col = lambda section: pl.BlockSpec(
        (seq, LANES), lambda b, hp: (b, section * N_HEAD_PAIRS + hp))
    whole = pl.BlockSpec((1, seq, LANES), lambda b, hp: (b, 0, 0))
    ext = HEAD_DIM + BF16_ROWS
    return pl.pallas_call(
        functools.partial(_fox_kernel, tq=tq),
        grid=(batch, N_HEAD_PAIRS),
        in_specs=[col(0), col(1), col(2), whole, whole],
        out_specs=pl.BlockSpec((seq, LANES), lambda b, hp: (b, hp)),
        out_shape=jax.ShapeDtypeStruct((tokens, ATT_WIDTH), BF),
        scratch_shapes=[
            pltpu.VMEM((seq, LANES), BF),
            pltpu.VMEM((seq, LANES), BF),
            pltpu.VMEM((ext, seq), BF),
            pltpu.VMEM((ext, seq), BF),
            pltpu.VMEM((2, tq, LANES), BF),
            pltpu.VMEM((tq, tq), F32),
            pltpu.VMEM((2, tq, tq), F32),
            pltpu.VMEM((2, ext, tq), F32),
            pltpu.VMEM((LANES, tq), F32),
            pltpu.SMEM((2, seq // tq + 1), F32),
        ],
        compiler_params=_cparams(("arbitrary", "arbitrary")),
        name="fox_attn",
    )(z, z, z, gq, gk)


def _log2(n):
    assert n > 0 and n & (n - 1) == 0, n
    return n.bit_length() - 1


def _dilated_kernel(q1, k1, v1, q4, k4, v4, q16, k16, v16, o_ref,
                    num1, num4, num16, den1, den4, den16, max1, max4, max16, bias_scr,
                    *, seq, blk, chunk, group, merge_rows):
    sources = ((q1, k1, v1), (q4, k4, v4), (q16, k16, v16))
    num_scrs = (num1, num4, num16)
    den_scrs = (den1, den4, den16)
    max_scrs = (max1, max4, max16)
    n_blocks = seq // blk

    low = lax.broadcasted_iota(jnp.int32, (blk, LANES), 1) < HEAD_DIM
    qrow = lax.broadcasted_iota(jnp.int32, (2 * blk, 2 * blk), 0) % blk
    kcol = lax.broadcasted_iota(jnp.int32, (2 * blk, 2 * blk), 1)
    dist = qrow + blk - kcol
    band = (dist >= 0) & (dist <= blk)
    bias_scr[0] = jnp.where(band & (kcol >= blk), 0.0, NEG_INF)
    bias_scr[1] = jnp.where(band, 0.0, NEG_INF)
    ones = jnp.ones((2 * blk, LANES), BF)

    for idx, (window, dilation) in enumerate(DIL_PATTERNS):
        assert window // dilation == blk
        q_ref, k_ref, v_ref = sources[idx]
        num_scr, den_scr, max_scr = num_scrs[idx], den_scrs[idx], max_scrs[idx]
        sub_shift = _log2(seq // dilation // blk)
        seg_rows = chunk // dilation
        seg = min(blk, seg_rows)

        def load(ref, j, l0):
            parts = []
            for s in range(blk // seg):
                l = l0 + s * seg
                p = lax.shift_right_logical(l, _log2(seg_rows))
                i = l & (seg_rows - 1)
                start = pl.multiple_of(p * chunk + j * seg_rows + i, seg)
                parts.append(ref[pl.ds(start, seg), :])
            return parts[0] if len(parts) == 1 else jnp.concatenate(parts, axis=0)

        def one_block(g):
            j = lax.shift_right_logical(g, sub_shift)
            gs = g & ((1 << sub_shift) - 1)
            l0 = gs * blk
            lp = jnp.maximum(l0 - blk, 0)
            q2 = load(q_ref, j, l0)
            zero = jnp.zeros_like(q2)
            qq = jnp.concatenate([jnp.where(low, q2, zero), jnp.where(low, zero, q2)], axis=0)
            kwin = jnp.concatenate([load(k_ref, j, lp), load(k_ref, j, l0)], axis=0)
            vwin = jnp.concatenate([load(v_ref, j, lp), load(v_ref, j, l0)], axis=0)
            s = _dot_nt(qq, kwin) + bias_scr[jnp.minimum(gs, 1)]
            m = jnp.max(s, axis=1, keepdims=True)
            p = jnp.exp2(s - m).astype(BF)
            pv = _dot(p, jnp.concatenate([vwin, ones], axis=1))
            if dilation == 1:
                dst = pl.ds(pl.multiple_of(l0, blk), blk)
            else:
                dst = pl.ds(l0 * dilation + j, blk, stride=dilation)
            num_scr[dst, :] = jnp.where(low, pv[:blk, :LANES], pv[blk:, :LANES])
            den_scr[dst, :] = jnp.where(low, pv[:blk, LANES:], pv[blk:, LANES:])
            max_scr[dst, :] = jnp.where(low, jnp.broadcast_to(m[:blk], (blk, LANES)),
                                        jnp.broadcast_to(m[blk:], (blk, LANES)))

        def blocks(it, carry):
            for u in range(group):
                one_block(it * group + u)
            return carry

        lax.fori_loop(0, n_blocks // group, blocks, 0)

    def merge(c, carry):
        rows = pl.ds(pl.multiple_of(c * merge_rows, merge_rows), merge_rows)
        maxes = [max_scr[rows, :] for max_scr in max_scrs]
        top = functools.reduce(jnp.maximum, maxes)
        weights = [jnp.exp2(m - top) for m in maxes]
        num = sum(w * num_scr[rows, :] for w, num_scr in zip(weights, num_scrs))
        den = sum(w * den_scr[rows, :] for w, den_scr in zip(weights, den_scrs))
        o_ref[rows, :] = (num / den).astype(BF)
        return carry

    lax.fori_loop(0, seq // merge_rows, merge, 0)


def _dilated_attention(z, zd4, zd16, *, batch, seq):
    blk = DIL_PATTERNS[0][0] // DIL_PATTERNS[0][1]
    tokens = batch * seq
    per_section = ATT_WIDTH // LANES
    col = lambda section: pl.BlockSpec(
        (seq, LANES), lambda b, hp: (b, section * per_section + hp))
    f32_buf = pltpu.VMEM((seq, LANES), F32)
    return pl.pallas_call(
        functools.partial(_dilated_kernel, seq=seq, blk=blk, chunk=DIL_CHUNK, group=32,
                          merge_rows=256),
        grid=(batch, N_HEAD_PAIRS),
        in_specs=[col(DIL_SECTION), col(DIL_SECTION + 1), col(DIL_SECTION + 2),
                  col(0), col(1), col(2), col(0), col(1), col(2)],
        out_specs=pl.BlockSpec((seq, LANES), lambda b, hp: (b, hp)),
        out_shape=jax.ShapeDtypeStruct((tokens, ATT_WIDTH), BF),
        scratch_shapes=[f32_buf] * 9 + [pltpu.VMEM((2, 2 * blk, 2 * blk), F32)],
        compiler_params=_cparams(("arbitrary", "arbitrary")),
        name="dilated_attn",
    )(z, z, z, zd4, zd4, zd4, zd16, zd16, zd16)


def _mix_kernel(ya_ref, yb_ref, ga_ref, gb_ref, x_ref, woa_ref, wob_ref, wout_ref,
                g_ref, o_ref):
    pa = _dot(ya_ref[...], woa_ref[...])
    pb = _dot(yb_ref[...], wob_ref[...])
    mixed = (jax.nn.sigmoid(ga_ref[...].astype(F32)) * pa
             + jax.nn.sigmoid(gb_ref[...].astype(F32)) * pb)
    y = _dot(mixed.astype(BF), wout_ref[...])
    o_ref[...] = x_ref[...] + _rms(y, g_ref[...])


def _mix(ya, yb, z, x2d, woa, wob, wout, g, *, tm):
    tokens = x2d.shape[0]
    gate_blk = lambda off: pl.BlockSpec((tm, D_MODEL), lambda i: (i, off))
    const = lambda i: (0, 0)
    return pl.pallas_call(
        _mix_kernel,
        grid=(tokens // tm,),
        in_specs=[
            pl.BlockSpec((tm, ATT_WIDTH), lambda i: (i, 0)),
            pl.BlockSpec((tm, ATT_WIDTH), lambda i: (i, 0)),
            gate_blk(3),
            gate_blk(4),
            pl.BlockSpec((tm, D_MODEL), lambda i: (i, 0)),
            pl.BlockSpec((ATT_WIDTH, D_MODEL), const),
            pl.BlockSpec((ATT_WIDTH, D_MODEL), const),
            pl.BlockSpec((D_MODEL, D_MODEL), const),
            pl.BlockSpec((1, D_MODEL), const),
        ],
        out_specs=pl.BlockSpec((tm, D_MODEL), lambda i: (i, 0)),
        out_shape=jax.ShapeDtypeStruct((tokens, D_MODEL), F32),
        compiler_params=_cparams(("arbitrary",)),
        name="mix",
    )(ya, yb, z, z, x2d, woa, wob, wout, g)


def _ffn_kernel(x_ref, halo_ref, gpre_ref, wup_ref, cw_ref, cb_ref, wd_ref, gpost_ref, o_ref,
                h_scr, ua0_scr, ub0_scr, ua1_scr, ub1_scr, acc_scr, rows_scr,
                *, tm, tf, tiles_per_seq, n_chunks):
    i = pl.program_id(0)
    halo = BF16_ROWS
    n_rows = tm + halo
    n_groups = n_rows // SUBLANES
    chunk = n_rows // n_chunks
    assert n_groups * SUBLANES == n_rows and chunk * n_chunks == n_rows
    assert chunk % BF16_ROWS == 0 and n_groups % 2 == 0
    n_tiles = D_FF // tf
    u_sets = ((ua0_scr, ub0_scr), (ua1_scr, ub1_scr))

    def group_rows(g):
        return pl.ds(g, SUBLANES, stride=n_groups)

    def cols(f, gate):
        return pl.ds(pl.multiple_of(f * tf + (D_FF if gate else 0), LANES), tf)

    def up_proj(f, dst, lo, hi):
        h = h_scr[lo:hi, :]
        dst[0][lo:hi, :] = _dot(h, wup_ref[:, cols(f, False)])
        dst[1][lo:hi, :] = _dot(h, wup_ref[:, cols(f, True)])

    def tap_rows(u_scr, back, first, n):
        lo = first - back * SUBLANES
        if lo >= 0:
            return u_scr[lo:lo + n, :]
        wrapped = [pltpu.roll(u_scr[n_rows + k * SUBLANES:n_rows + (k + 1) * SUBLANES, :], 1, 0)
                   for k in range(lo // SUBLANES, 0)]
        return jnp.concatenate(wrapped + [u_scr[0:n + lo, :]], axis=0)

    def conv(u_scr, window, post_scale, first, n):
        out = cb_ref[:, window] * post_scale
        for tap in range(CONV_WIDTH):
            back = CONV_WIDTH - 1 - tap
            out = out + (cw_ref[tap:tap + 1, window] * post_scale) * tap_rows(u_scr, back, first, n)
        return out

    def down_proj(f, src, first, n):
        a = conv(src[0], cols(f, False), 1.0, first, n)
        half_b = conv(src[1], cols(f, True), 0.5, first, n)
        c0 = math.sqrt(2.0 / math.pi)
        inner = a * (c0 + (c0 * 0.044715) * (a * a))
        hidden = ((a * half_b) * (1.0 + jnp.tanh(inner))).astype(BF)
        return _dot(hidden, wd_ref[pl.ds(pl.multiple_of(f * tf, tf), tf), :])

    def stage(f, parity):
        src, dst = u_sets[1 - parity], u_sets[parity]
        for c in range(n_chunks):
            up_proj(f, dst, c * chunk, (c + 1) * chunk)
            out_rows = slice(c * chunk, (c + 1) * chunk)
            acc_scr[out_rows, :] += down_proj(f - 1, src, c * chunk, chunk)

    n_slabs = D_MODEL // LANES
    g = gpre_ref[...]
    hh = _rms(halo_ref[...], g)
    hh = jnp.where(i % tiles_per_seq == 0, jnp.zeros_like(hh), hh)
    hx = _rms(x_ref[...], g)
    for s in range(n_slabs):
        rows_scr[s, 0:halo, :] = hh[:, s * LANES:(s + 1) * LANES]
        rows_scr[s, halo:, :] = hx[:, s * LANES:(s + 1) * LANES]
    for gg in range(0, n_groups, 2):
        pair = jnp.concatenate(
            [jnp.concatenate([rows_scr[s, group_rows(gg + d), :] for s in range(n_slabs)], axis=1)
             for d in range(2)], axis=0)
        h_scr[gg * SUBLANES:(gg + 2) * SUBLANES, :] = pair.astype(BF)
    acc_scr[...] = jnp.zeros_like(acc_scr)
    up_proj(0, u_sets[0], 0, n_rows)

    def pair_of_tiles(k, carry):
        stage(2 * k + 1, 1)
        stage(2 * k + 2, 0)
        return carry

    assert n_tiles % 2 == 1
    lax.fori_loop(0, (n_tiles - 1) // 2, pair_of_tiles, 0)
    gpost = gpost_ref[...]
    for c in range(n_chunks):
        y = acc_scr[c * chunk:(c + 1) * chunk, :] + down_proj(n_tiles - 1, u_sets[0], c * chunk, chunk)
        normed = _rms(y, gpost)
        for gg in range(chunk // SUBLANES):
            g_abs = c * (chunk // SUBLANES) + gg
            for s in range(n_slabs):
                rows_scr[s, group_rows(g_abs), :] = normed[gg * SUBLANES:(gg + 1) * SUBLANES,
                                                           s * LANES:(s + 1) * LANES]
    for s in range(n_slabs):
        lanes = slice(s * LANES, (s + 1) * LANES)
        o_ref[:, lanes] = x_ref[:, lanes] + rows_scr[s, halo:, :]


def _ffn(x1, g_pre, w_up, conv_w, conv_b, w_down, g_post, *, seq, tm, tf):
    tokens = x1.shape[0]
    halo = BF16_ROWS
    per = tm // halo
    resident = lambda shape: pl.BlockSpec(shape, lambda i: (0, 0),
                                          pipeline_mode=pl.Buffered(1))
    u_buf = pltpu.VMEM((tm + halo, tf), F32)
    return pl.pallas_call(
        functools.partial(_ffn_kernel, tm=tm, tf=tf, tiles_per_seq=seq // tm, n_chunks=5),
        grid=(tokens // tm,),
        in_specs=[
            pl.BlockSpec((tm, D_MODEL), lambda i: (i, 0)),
            pl.BlockSpec((halo, D_MODEL), lambda i: (jnp.maximum(i * per - 1, 0), 0)),
            resident((1, D_MODEL)),
            resident((D_MODEL, 2 * D_FF)),
            resident((CONV_WIDTH, 2 * D_FF)),
            resident((1, 2 * D_FF)),
            resident((D_FF, D_MODEL)),
            resident((1, D_MODEL)),
        ],
        out_specs=pl.BlockSpec((tm, D_MODEL), lambda i: (i, 0)),
        out_shape=jax.ShapeDtypeStruct((tokens, D_MODEL), F32),
        scratch_shapes=[
            pltpu.VMEM((tm + halo, D_MODEL), BF),
            u_buf, u_buf, u_buf, u_buf,
            pltpu.VMEM((tm + halo, D_MODEL), F32),
            pltpu.VMEM((D_MODEL // LANES, tm + halo, LANES), F32),
        ],
        compiler_params=_cparams(("arbitrary",)),
        name="ffn",
    )(x1, x1, g_pre, w_up, conv_w, conv_b, w_down, g_post)


def _rope_freq_lanes():
    inv_freq = ROPE_THETA ** (-jnp.arange(ROPE_HALF, dtype=F32) * 2.0 / ROPE_DIM)
    return jnp.tile(inv_freq, LANES // ROPE_HALF).reshape(1, LANES)


def kernel(x, g_pre_mix, w_in, b_forget, w_o_fox, w_o_dil, w_out, g_post_mix,
           g_pre_ffn, w_up, conv_w, conv_b, w_down, g_post_ffn):
    batch, seq, d_model = x.shape
    assert d_model == D_MODEL and seq % 1024 == 0
    depth = w_in.shape[0]
    freq_lanes = _rope_freq_lanes()
    x2d = x.reshape(batch * seq, D_MODEL)
    row = lambda v: v.reshape(1, -1)
    for l in range(depth):
        bias_lanes = jnp.pad(b_forget[l], (0, LANES - N_HEADS)).reshape(1, LANES)

        z, fa, zd4, zd16 = _in_proj(x2d, row(g_pre_mix[l]), w_in[l].astype(BF), freq_lanes,
                                    batch=batch, seq=seq, tm=DIL_CHUNK)
        gq, gk = _forget_scan(fa, bias_lanes, batch=batch, seq=seq, chunk=512)
        ya = _fox_attention(z, gq, gk, batch=batch, seq=seq, tq=512)
        yb = _dilated_attention(z, zd4, zd16, batch=batch, seq=seq)
        x2d = _mix(ya, yb, z, x2d, w_o_fox[l].astype(BF), w_o_dil[l].astype(BF),
                   w_out[l].astype(BF), row(g_post_mix[l]), tm=1024)
        x2d = _ffn(x2d, row(g_pre_ffn[l]), w_up[l].astype(BF), conv_w[l],
                   row(conv_b[l]), w_down[l].astype(BF), row(g_post_ffn[l]),
                   seq=seq, tm=1024, tf=256)
    return x2d.reshape(batch, seq, D_MODEL)
```

```python
import functools
import math

import numpy as np
import jax
import jax.numpy as jnp
from jax import lax
from jax.experimental import pallas as pl
from jax.experimental.pallas import tpu as pltpu

D_MODEL = 1024
HEAD_DIM = 64
N_HEADS = 8
ATT_WIDTH = N_HEADS * HEAD_DIM
N_HEAD_PAIRS = N_HEADS // 2
DIL_PATTERNS = ((128, 1), (512, 4), (2048, 16))
ROPE_DIM = HEAD_DIM // 4
ROPE_HALF = ROPE_DIM // 2
ROPE_THETA = 500000.0
D_FF = 2816
CONV_WIDTH = 3
RMS_EPS = 1e-6
NEG_INF = -1e30
Q_SCALE = 1.0 / math.sqrt(HEAD_DIM)
LOG2_E = math.log2(math.e)
UNDERFLOW_BITS = 150.0
SCORE_MARGIN = 1.0
NORM_SLACK = 1.0 + 2.0 ** -7

LANES = 128
SUBLANES = 8
BF16_ROWS = 16
Z_WIDTH = 3 * ATT_WIDTH * 2 + 2 * D_MODEL
Z_BLK = 512
DIL_SECTION = 3
DIL_CHUNK = 512
N_PIECES = 3
VMEM_LIMIT = 56 * 1024 * 1024

BF = jnp.bfloat16
F32 = jnp.float32


def _cparams(sem, flags=None):
    return pltpu.CompilerParams(dimension_semantics=sem, vmem_limit_bytes=VMEM_LIMIT,
                                flags=flags)


def _rms(xf, g):
    inv = lax.rsqrt(jnp.mean(xf * xf, axis=-1, keepdims=True) + RMS_EPS)
    return xf * inv * g


def _split3(x):
    hi = x.astype(BF)
    r1 = x - hi.astype(F32)
    mid = r1.astype(BF)
    lo = (r1 - mid.astype(F32)).astype(BF)
    return hi, mid, lo


def _dot(a, b):
    return jnp.dot(a, b, preferred_element_type=F32)


def _dot_nt(a, b):
    return lax.dot_general(a, b, (((1,), (1,)), ((), ())), preferred_element_type=F32)


def _in_proj_kernel(x_ref, g_ref, w_raw_ref, freq_ref, z_ref, fa_ref, zd4_ref, zd16_ref,
                    w_ref, h_scr, cos_scr, sneg_scr, spos_scr, stage_scr, stage4_scr, *, tm):
    p = pl.program_id(0)
    b = pl.program_id(1)
    zd_refs = {4: zd4_ref, 16: zd16_ref}
    n_slabs = Z_BLK // LANES

    @pl.when((p == 0) & (b == 0))
    def _regroup_weight():
        fox_end = 3 * ATT_WIDTH
        raw_width = w_raw_ref.shape[1]
        first = fox_end // LANES

        def group(i):
            lo = i * LANES
            return w_raw_ref[:, lo:min(lo + LANES, raw_width)].astype(F32)

        w_ref[:, 0:fox_end] = w_raw_ref[:, 0:fox_end]
        for k in range((Z_WIDTH - fox_end) // LANES):
            shifted = jnp.concatenate(
                [group(first + k)[:, N_HEADS:], group(first + k + 1)[:, :N_HEADS]], axis=1)
            w_ref[:, fox_end + k * LANES:fox_end + (k + 1) * LANES] = shifted.astype(BF)
        w_ref[:, Z_WIDTH:Z_WIDTH + LANES] = jnp.concatenate(
            [group(first)[:, :N_HEADS], jnp.zeros((D_MODEL, LANES - N_HEADS), F32)],
            axis=1).astype(BF)

    @pl.when(b == 0)
    def _tables():
        pos = (p * tm + lax.broadcasted_iota(jnp.int32, (tm, LANES), 0)).astype(F32)
        lane = lax.broadcasted_iota(jnp.int32, (tm, LANES), 1)
        c = lane % HEAD_DIM
        ang = pos * freq_ref[...]
        cs = jnp.cos(ang)
        sn = jnp.sin(ang)
        cos_scr[...] = jnp.where(c < ROPE_DIM, cs, 1.0)
        sneg_scr[...] = jnp.where(c < ROPE_HALF, -sn, 0.0)
        spos_scr[...] = jnp.where((c >= ROPE_HALF) & (c < ROPE_DIM), sn, 0.0)

    def rope(t):
        up = pltpu.roll(t, LANES - ROPE_HALF, 1)
        dn = pltpu.roll(t, ROPE_HALF, 1)
        return t * cos_scr[...] + up * sneg_scr[...] + dn * spos_scr[...]

    def emit_dilated(section, slab, val):
        out_lanes = slice(section * Z_BLK + slab * LANES, section * Z_BLK + (slab + 1) * LANES)
        z_ref[:, (DIL_SECTION + section) * Z_BLK + slab * LANES:
              (DIL_SECTION + section) * Z_BLK + (slab + 1) * LANES] = val.astype(BF)
        stage, stage4 = stage_scr.at[section, slab], stage4_scr.at[section, slab]
        stage[...] = val
        seg4 = tm // 4
        for c in range(4):
            part = stage[pl.ds(c, seg4, stride=4), :]
            stage4[c * seg4:(c + 1) * seg4, :] = part
            zd_refs[4][c * seg4:(c + 1) * seg4, out_lanes] = part.astype(BF)
        seg16 = tm // 16
        for jj in range(16):
            c, a = jj % 4, jj // 4
            part = stage4[pl.ds(c * seg4 + a, seg16, stride=4), :]
            zd_refs[16][jj * seg16:(jj + 1) * seg16, out_lanes] = part.astype(BF)

    h = _rms(x_ref[...], g_ref[...]).astype(BF)
    h_scr[...] = h
    fa_ref[...] = _dot(h, w_ref[:, Z_WIDTH:Z_WIDTH + LANES])

    for j in range(Z_WIDTH // Z_BLK):
        cols = slice(j * Z_BLK, (j + 1) * Z_BLK)
        acc = _dot(h_scr[...], w_ref[:, cols])
        section = j - DIL_SECTION
        if j == 0:
            z_ref[:, cols] = (acc * (Q_SCALE * LOG2_E)).astype(BF)
        elif 0 <= section < 3:
            for slab in range(n_slabs):
                t = acc[:, slab * LANES:(slab + 1) * LANES]
                if section == 0:
                    t = rope(t) * (Q_SCALE * LOG2_E)
                elif section == 1:
                    t = rope(t)
                emit_dilated(section, slab, t)
        else:
            z_ref[:, cols] = acc.astype(BF)


def _in_proj(x2d, g, w_raw, freq_lanes, *, batch, seq, tm):
    n_p = seq // tm
    tokens = batch * seq
    row = lambda p, b: (b * n_p + p, 0)
    resident = lambda shape: pl.BlockSpec(shape, lambda p, b: (0, 0),
                                          pipeline_mode=pl.Buffered(1))
    dil_shape = jax.ShapeDtypeStruct((tokens, 3 * ATT_WIDTH), BF)
    stage_buf = pltpu.VMEM((3, Z_BLK // LANES, tm, LANES), F32)
    return pl.pallas_call(
        functools.partial(_in_proj_kernel, tm=tm),
        grid=(n_p, batch),
        in_specs=[
            pl.BlockSpec((tm, D_MODEL), row),
            resident((1, D_MODEL)),
            resident((D_MODEL, Z_WIDTH + N_HEADS)),
            resident((1, LANES)),
        ],
        out_specs=[
            pl.BlockSpec((tm, Z_WIDTH), row),
            pl.BlockSpec((tm, LANES), row),
            pl.BlockSpec((tm, 3 * ATT_WIDTH), row),
            pl.BlockSpec((tm, 3 * ATT_WIDTH), row),
        ],
        out_shape=[
            jax.ShapeDtypeStruct((tokens, Z_WIDTH), BF),
            jax.ShapeDtypeStruct((tokens, LANES), F32),
            dil_shape,
            dil_shape,
        ],
        scratch_shapes=[
            pltpu.VMEM((D_MODEL, Z_WIDTH + LANES), BF),
            pltpu.VMEM((tm, D_MODEL), BF),
            pltpu.VMEM((tm, LANES), F32),
            pltpu.VMEM((tm, LANES), F32),
            pltpu.VMEM((tm, LANES), F32),
            stage_buf,
            stage_buf,
        ],
        compiler_params=_cparams(("arbitrary", "arbitrary")),
        name="in_proj",
    )(x2d, g, w_raw, freq_lanes)


def _extras_base(head):
    return (HEAD_DIM if head % 2 == 0 else 0) + 2 * N_PIECES * (head // 2)


def _forget_scan_kernel(fa_ref, bias_ref, tri_ref, pq_ref, pk_ref, oq_ref, ok_ref,
                        gq_ref, gk_ref):
    chunk = tri_ref.shape[0]
    n_chunks = fa_ref.shape[0] // chunk
    t = fa_ref[...] + bias_ref[...]
    log_f = jnp.minimum(t, 0.0) - jnp.log1p(jnp.exp(-jnp.abs(t)))
    wide = jnp.concatenate(
        [log_f[c * chunk:(c + 1) * chunk, :] for c in range(n_chunks)], axis=1)
    tri = tri_ref[...]
    local = sum(_dot(tri, piece) for piece in _split3(wide))
    carry = jnp.zeros((1, LANES), F32)
    parts = []
    for c in range(n_chunks):
        part = local[:, c * LANES:(c + 1) * LANES] + carry
        parts.append(part)
        carry = part[chunk - 1:chunk, :]
    run = jnp.concatenate(parts, axis=0)
    gq = oq_ref[...].astype(F32)
    gk = ok_ref[...].astype(F32)
    for i, piece in enumerate(_split3(run * LOG2_E)):
        gq = gq + _dot(piece, pq_ref[i])
        gk = gk - _dot(piece, pk_ref[i])
    gq_ref[0] = gq.astype(BF)
    gk_ref[0] = gk.astype(BF)


def _forget_scan_constants(chunk):
    tri = np.tril(np.ones((chunk, chunk), np.float32))
    pq = np.zeros((N_PIECES, LANES, LANES), np.float32)
    pk = np.zeros((N_PIECES, LANES, LANES), np.float32)
    oq = np.zeros((1, LANES), np.float32)
    ok = np.zeros((1, LANES), np.float32)
    for h in range(N_HEADS):
        base = _extras_base(h)
        for i in range(N_PIECES):
            pq[i, h, base + i] = 1.0
            oq[0, base + N_PIECES + i] = 1.0
            ok[0, base + i] = 1.0
            pk[i, h, base + N_PIECES + i] = 1.0
    as_bf = lambda a: jnp.asarray(a, BF)
    return as_bf(tri), as_bf(pq), as_bf(pk), as_bf(oq), as_bf(ok)


def _forget_scan(fa, bias_lanes, *, batch, seq, chunk):
    tri, pq, pk, oq, ok = _forget_scan_constants(chunk)
    const2 = lambda b: (0, 0)
    const3 = lambda b: (0, 0, 0)
    return pl.pallas_call(
        _forget_scan_kernel,
        grid=(batch,),
        in_specs=[
            pl.BlockSpec((seq, LANES), lambda b: (b, 0)),
            pl.BlockSpec((1, LANES), const2),
            pl.BlockSpec((chunk, chunk), const2),
            pl.BlockSpec((N_PIECES, LANES, LANES), const3),
            pl.BlockSpec((N_PIECES, LANES, LANES), const3),
            pl.BlockSpec((1, LANES), const2),
            pl.BlockSpec((1, LANES), const2),
        ],
        out_specs=[
            pl.BlockSpec((1, seq, LANES), lambda b: (b, 0, 0)),
            pl.BlockSpec((1, seq, LANES), lambda b: (b, 0, 0)),
        ],
        out_shape=[
            jax.ShapeDtypeStruct((batch, seq, LANES), BF),
            jax.ShapeDtypeStruct((batch, seq, LANES), BF),
        ],
        compiler_params=_cparams(("arbitrary",)),
        name="forget_scan",
    )(fa, bias_lanes, tri, pq, pk, oq, ok)


def _with_bias_lanes(slab, extras, hp, parity):
    lane = lax.broadcasted_iota(jnp.int32, slab.shape, 1)
    own = (lane < HEAD_DIM) if parity == 0 else (lane >= HEAD_DIM)
    base = (HEAD_DIM if parity == 0 else 0) + 2 * N_PIECES * hp
    in_extras = (lane >= base) & (lane < base + 2 * N_PIECES)
    return jnp.where(own, slab, jnp.where(in_extras, extras, jnp.zeros_like(extras)))


def _fox_kernel(q_ref, k_ref, v_ref, gq_ref, gk_ref, o_ref,
                ka_scr, kb_scr, vta_scr, vtb_scr, qp_scr, bias_scr, s_scr, acc_scr, ot_scr,
                bound_smem, *, tq):
    hp = pl.program_id(1)
    k_scrs = (ka_scr, kb_scr)
    vt_scrs = (vta_scr, vtb_scr)
    seq = k_ref.shape[0]
    n_q = seq // tq
    lane_row = lax.broadcasted_iota(jnp.int32, (1, LANES), 1)

    sel_row = lax.broadcasted_iota(jnp.int32, (LANES, LANES), 0)
    sel_col = lax.broadcasted_iota(jnp.int32, (LANES, LANES), 1)
    head_sum = jnp.where(sel_row // HEAD_DIM == sel_col, 1.0, 0.0).astype(BF)

    def max_row_norm(slab, head):
        lane = lax.broadcasted_iota(jnp.int32, slab.shape, 1)
        own = (lane < HEAD_DIM) if head == 0 else (lane >= HEAD_DIM)
        x = jnp.where(own, slab.astype(F32), 0.0)
        sq = jnp.sum(x * x, axis=1, keepdims=True)
        return jnp.sqrt(jnp.max(sq, axis=0, keepdims=True))

    def max_row_norms(slab):
        x = slab.astype(F32)
        per_row = _dot((x * x).astype(BF), head_sum)
        return jnp.sqrt(jnp.max(per_row, axis=0, keepdims=True) * NORM_SLACK)

    def bias_lane_sum(row, head, first):
        base = (HEAD_DIM if head == 0 else 0) + 2 * N_PIECES * hp + first
        picked = (lane_row >= base) & (lane_row < base + N_PIECES)
        return jnp.sum(jnp.where(picked, row, 0.0), axis=1, keepdims=True)

    k2 = k_ref[...]
    gk = gk_ref[0]
    k_norms = max_row_norms(k2)
    for head in range(2):
        k_scrs[head][...] = _with_bias_lanes(k2, gk, hp, head)
        bound_smem[head, n_q] = k_norms[0, head]
        for j in range(n_q):
            rows16 = gk_ref[0, (j + 1) * tq - BF16_ROWS:(j + 1) * tq, :].astype(F32)
            last = rows16[BF16_ROWS - 1:BF16_ROWS, :]
            bound_smem[head, j] = -bias_lane_sum(last, head, N_PIECES)[0, 0]
    vt = v_ref[...].T
    ones = jnp.ones((BF16_ROWS, seq), BF)
    for head, vt_scr in enumerate(vt_scrs):
        vt_scr[0:HEAD_DIM, :] = vt[head * HEAD_DIM:(head + 1) * HEAD_DIM, :]
        vt_scr[HEAD_DIM:, :] = ones
    row = lax.broadcasted_iota(jnp.int32, (tq, tq), 0)
    col = lax.broadcasted_iota(jnp.int32, (tq, tq), 1)
    bias_scr[...] = jnp.where(row <= col, 0.0, NEG_INF)
    acc_scr[...] = jnp.zeros_like(acc_scr)

    def scores(head, tile, masked):
        kt = k_scrs[head][pl.ds(pl.multiple_of(tile * tq, tq), tq), :]
        st = _dot(kt, qp_scr[head])
        if masked:
            st = st + bias_scr[...]
        s_scr[head] = st
        return jnp.max(st, axis=0, keepdims=True)

    def softmax_pv(head, st, tile, tile_max, m):
        start = pl.multiple_of(tile * tq, tq)
        m_new = jnp.maximum(m, tile_max)
        alpha = jnp.exp2(m - m_new)
        pt = jnp.exp2(st - m_new).astype(BF)
        vt_tile = vt_scrs[head][:, pl.ds(start, tq)]
        acc_scr[head] = alpha * acc_scr[head] + _dot(vt_tile, pt)
        return m_new

    def start_tile(qi):
        rows = pl.ds(pl.multiple_of(qi * tq, tq), tq)
        q2 = q_ref[rows, :]
        gq = gq_ref[0, rows, :]
        first_row = gq[0:BF16_ROWS, :].astype(F32)[0:1, :]
        maxes, slack = [], []
        for head in range(2):
            qp_scr[head] = _with_bias_lanes(q2, gq, hp, head).T
            maxes.append(scores(head, qi, True))
            reach = (max_row_norm(q2, head) * bound_smem[head, n_q]
                     + bias_lane_sum(first_row, head, 0) + SCORE_MARGIN)
            lowest_max = jnp.min(maxes[head], axis=1, keepdims=True)
            slack.append(reach - (lowest_max - UNDERFLOW_BITS))
        count = jnp.zeros((1, 1), jnp.int32)
        for j in range(n_q - 1):
            needed = (((slack[0] >= bound_smem[0, j]) | (slack[1] >= bound_smem[1, j]))
                      & (j < qi))
            count = count + jnp.where(needed, 1, 0)
        return maxes[0], maxes[1], count[0, 0]

    def finish_tile(qi):
        for head in range(2):
            acc = acc_scr[head]
            ot_scr[head * HEAD_DIM:(head + 1) * HEAD_DIM, :] = (
                acc[0:HEAD_DIM, :] / acc[HEAD_DIM:HEAD_DIM + 1, :])
        o_ref[pl.ds(pl.multiple_of(qi * tq, tq), tq), :] = ot_scr[...].T.astype(BF)
        acc_scr[...] = jnp.zeros_like(acc_scr)

    m_init = jnp.full((1, tq), NEG_INF, F32)

    def query_tile(qi, carry):
        first_a, first_b, n_needed = carry

        def key_tile(t, inner):
            max_a, max_b, m_a, m_b, prev_tile = inner
            prev = [s_scr[head] for head in range(2)]
            tile = qi - 1 - t
            next_max = [scores(head, tile, False) for head in range(2)]
            m_a = softmax_pv(0, prev[0], prev_tile, max_a, m_a)
            m_b = softmax_pv(1, prev[1], prev_tile, max_b, m_b)
            return next_max[0], next_max[1], m_a, m_b, tile

        max_a, max_b, m_a, m_b, prev_tile = lax.fori_loop(
            0, n_needed, key_tile, (first_a, first_b, m_init, m_init, qi))
        last = [s_scr[head] for head in range(2)]
        softmax_pv(0, last[0], prev_tile, max_a, m_a)
        softmax_pv(1, last[1], prev_tile, max_b, m_b)
        following = start_tile(jnp.minimum(qi + 1, n_q - 1))
        finish_tile(qi)
        return following

    lax.fori_loop(0, n_q, query_tile, start_tile(0))


def _fox_attention(z, gq, gk, *, batch, seq, tq):
    tokens = batch * seq
    col = lambda section: pl.BlockSpec(
        (seq, LANES), lambda b, hp: (b, section * N_HEAD_PAIRS + hp))
    whole = pl.BlockSpec((1, seq, LANES), lambda b, hp: (b, 0, 0))
    ext = HEAD_DIM + BF16_ROWS
    return pl.pallas_call(
        functools.partial(_fox_kernel, tq=tq),
        grid=(batch, N_HEAD_PAIRS),
        in_specs=[col(0), col(1), col(2), whole, whole],
        out_specs=pl.BlockSpec((seq, LANES), lambda b, hp: (b, hp)),
        out_shape=jax.ShapeDtypeStruct((tokens, ATT_WIDTH), BF),
        scratch_shapes=[
            pltpu.VMEM((seq, LANES), BF),
            pltpu.VMEM((seq, LANES), BF),
            pltpu.VMEM((ext, seq), BF),
            pltpu.VMEM((ext, seq), BF),
            pltpu.VMEM((2, LANES, tq), BF),
            pltpu.VMEM((tq, tq), F32),
            pltpu.VMEM((2, tq, tq), F32),
            pltpu.VMEM((2, ext, tq), F32),
            pltpu.VMEM((LANES, tq), F32),
            pltpu.SMEM((2, seq // tq + 1), F32),
        ],
        compiler_params=_cparams(("arbitrary", "arbitrary")),
        name="fox_attn",
    )(z, z, z, gq, gk)


def _log2(n):
    assert n > 0 and n & (n - 1) == 0, n
    return n.bit_length() - 1


def _dilated_kernel(q1, k1, v1, q4, k4, v4, q16, k16, v16, o_ref,
                    num1, num4, num16, den1, den4, den16, max1, max4, max16, bias_scr,
                    *, seq, blk, chunk, group, merge_rows):
    sources = ((q1, k1, v1), (q4, k4, v4), (q16, k16, v16))
    num_scrs = (num1, num4, num16)
    den_scrs = (den1, den4, den16)
    max_scrs = (max1, max4, max16)
    n_blocks = seq // blk

    low = lax.broadcasted_iota(jnp.int32, (blk, LANES), 1) < HEAD_DIM
    qrow = lax.broadcasted_iota(jnp.int32, (2 * blk, 2 * blk), 0) % blk
    kcol = lax.broadcasted_iota(jnp.int32, (2 * blk, 2 * blk), 1)
    dist = qrow + blk - kcol
    band = (dist >= 0) & (dist <= blk)
    bias_scr[0] = jnp.where(band & (kcol >= blk), 0.0, NEG_INF)
    bias_scr[1] = jnp.where(band, 0.0, NEG_INF)
    ones = jnp.ones((2 * blk, LANES), BF)

    for idx, (window, dilation) in enumerate(DIL_PATTERNS):
        assert window // dilation == blk
        q_ref, k_ref, v_ref = sources[idx]
        num_scr, den_scr, max_scr = num_scrs[idx], den_scrs[idx], max_scrs[idx]
        sub_shift = _log2(seq // dilation // blk)
        seg_rows = chunk // dilation
        seg = min(blk, seg_rows)

        def load(ref, j, l0):
            parts = []
            for s in range(blk // seg):
                l = l0 + s * seg
                p = lax.shift_right_logical(l, _log2(seg_rows))
                i = l & (seg_rows - 1)
                start = pl.multiple_of(p * chunk + j * seg_rows + i, seg)
                parts.append(ref[pl.ds(start, seg), :])
            return parts[0] if len(parts) == 1 else jnp.concatenate(parts, axis=0)

        def one_block(g):
            j = lax.shift_right_logical(g, sub_shift)
            gs = g & ((1 << sub_shift) - 1)
            l0 = gs * blk
            lp = jnp.maximum(l0 - blk, 0)
            q2 = load(q_ref, j, l0)
            zero = jnp.zeros_like(q2)
            qq = jnp.concatenate([jnp.where(low, q2, zero), jnp.where(low, zero, q2)], axis=0)
            kwin = jnp.concatenate([load(k_ref, j, lp), load(k_ref, j, l0)], axis=0)
            vwin = jnp.concatenate([load(v_ref, j, lp), load(v_ref, j, l0)], axis=0)
            s = _dot_nt(qq, kwin) + bias_scr[jnp.minimum(gs, 1)]
            m = jnp.max(s, axis=1, keepdims=True)
            p = jnp.exp2(s - m).astype(BF)
            pv = _dot(p, jnp.concatenate([vwin, ones], axis=1))
            if dilation == 1:
                dst = pl.ds(pl.multiple_of(l0, blk), blk)
            else:
                dst = pl.ds(l0 * dilation + j, blk, stride=dilation)
            num_scr[dst, :] = jnp.where(low, pv[:blk, :LANES], pv[blk:, :LANES])
            den_scr[dst, :] = jnp.where(low, pv[:blk, LANES:], pv[blk:, LANES:])
            max_scr[dst, :] = jnp.where(low, jnp.broadcast_to(m[:blk], (blk, LANES)),
                                        jnp.broadcast_to(m[blk:], (blk, LANES)))

        def blocks(it, carry):
            for u in range(group):
                one_block(it * group + u)
            return carry

        lax.fori_loop(0, n_blocks // group, blocks, 0)

    def merge(c, carry):
        rows = pl.ds(pl.multiple_of(c * merge_rows, merge_rows), merge_rows)
        maxes = [max_scr[rows, :] for max_scr in max_scrs]
        top = functools.reduce(jnp.maximum, maxes)
        weights = [jnp.exp2(m - top) for m in maxes]
        num = sum(w * num_scr[rows, :] for w, num_scr in zip(weights, num_scrs))
        den = sum(w * den_scr[rows, :] for w, den_scr in zip(weights, den_scrs))
        o_ref[rows, :] = (num / den).astype(BF)
        return carry

    lax.fori_loop(0, seq // merge_rows, merge, 0)


def _dilated_attention(z, zd4, zd16, *, batch, seq):
    blk = DIL_PATTERNS[0][0] // DIL_PATTERNS[0][1]
    tokens = batch * seq
    per_section = ATT_WIDTH // LANES
    col = lambda section: pl.BlockSpec(
        (seq, LANES), lambda b, hp: (b, section * per_section + hp))
    f32_buf = pltpu.VMEM((seq, LANES), F32)
    return pl.pallas_call(
        functools.partial(_dilated_kernel, seq=seq, blk=blk, chunk=DIL_CHUNK, group=32,
                          merge_rows=256),
        grid=(batch, N_HEAD_PAIRS),
        in_specs=[col(DIL_SECTION), col(DIL_SECTION + 1), col(DIL_SECTION + 2),
                  col(0), col(1), col(2), col(0), col(1), col(2)],
        out_specs=pl.BlockSpec((seq, LANES), lambda b, hp: (b, hp)),
        out_shape=jax.ShapeDtypeStruct((tokens, ATT_WIDTH), BF),
        scratch_shapes=[f32_buf] * 9 + [pltpu.VMEM((2, 2 * blk, 2 * blk), F32)],
        compiler_params=_cparams(("arbitrary", "arbitrary")),
        name="dilated_attn",
    )(z, z, z, zd4, zd4, zd4, zd16, zd16, zd16)


def _mix_kernel(ya_ref, yb_ref, ga_ref, gb_ref, x_ref, woa_ref, wob_ref, wout_ref,
                g_ref, o_ref):
    pa = _dot(ya_ref[...], woa_ref[...])
    pb = _dot(yb_ref[...], wob_ref[...])
    mixed = (jax.nn.sigmoid(ga_ref[...].astype(F32)) * pa
             + jax.nn.sigmoid(gb_ref[...].astype(F32)) * pb)
    y = _dot(mixed.astype(BF), wout_ref[...])
    o_ref[...] = x_ref[...] + _rms(y, g_ref[...])


def _mix(ya, yb, z, x2d, woa, wob, wout, g, *, tm):
    tokens = x2d.shape[0]
    gate_blk = lambda off: pl.BlockSpec((tm, D_MODEL), lambda i: (i, off))
    const = lambda i: (0, 0)
    return pl.pallas_call(
        _mix_kernel,
        grid=(tokens // tm,),
        in_specs=[
            pl.BlockSpec((tm, ATT_WIDTH), lambda i: (i, 0)),
            pl.BlockSpec((tm, ATT_WIDTH), lambda i: (i, 0)),
            gate_blk(3),
            gate_blk(4),
            pl.BlockSpec((tm, D_MODEL), lambda i: (i, 0)),
            pl.BlockSpec((ATT_WIDTH, D_MODEL), const),
            pl.BlockSpec((ATT_WIDTH, D_MODEL), const),
            pl.BlockSpec((D_MODEL, D_MODEL), const),
            pl.BlockSpec((1, D_MODEL), const),
        ],
        out_specs=pl.BlockSpec((tm, D_MODEL), lambda i: (i, 0)),
        out_shape=jax.ShapeDtypeStruct((tokens, D_MODEL), F32),
        compiler_params=_cparams(("arbitrary",)),
        name="mix",
    )(ya, yb, z, z, x2d, woa, wob, wout, g)


def _ffn_kernel(x_ref, halo_ref, gpre_ref, wup_ref, cw_ref, cb_ref, wd_ref, gpost_ref, o_ref,
                h_scr, ua0_scr, ub0_scr, ua1_scr, ub1_scr, acc_scr, rows_scr,
                *, tm, tf, tiles_per_seq, n_chunks):
    i = pl.program_id(0)
    halo = BF16_ROWS
    n_rows = tm + halo
    n_groups = n_rows // SUBLANES
    chunk = n_rows // n_chunks
    assert n_groups * SUBLANES == n_rows and chunk * n_chunks == n_rows
    assert chunk % BF16_ROWS == 0 and n_groups % 2 == 0
    n_tiles = D_FF // tf
    u_sets = ((ua0_scr, ub0_scr), (ua1_scr, ub1_scr))

    def group_rows(g):
        return pl.ds(g, SUBLANES, stride=n_groups)

    def cols(f, gate):
        return pl.ds(pl.multiple_of(f * tf + (D_FF if gate else 0), LANES), tf)

    def up_proj(f, dst, lo, hi):
        h = h_scr[lo:hi, :]
        dst[0][lo:hi, :] = _dot(h, wup_ref[:, cols(f, False)])
        dst[1][lo:hi, :] = _dot(h, wup_ref[:, cols(f, True)])

    def tap_rows(u_scr, back, first, n):
        lo = first - back * SUBLANES
        if lo >= 0:
            return u_scr[lo:lo + n, :]
        wrapped = [pltpu.roll(u_scr[n_rows + k * SUBLANES:n_rows + (k + 1) * SUBLANES, :], 1, 0)
                   for k in range(lo // SUBLANES, 0)]
        return jnp.concatenate(wrapped + [u_scr[0:n + lo, :]], axis=0)

    def conv(u_scr, window, post_scale, first, n):
        out = cb_ref[:, window] * post_scale
        for tap in range(CONV_WIDTH):
            back = CONV_WIDTH - 1 - tap
            out = out + (cw_ref[tap:tap + 1, window] * post_scale) * tap_rows(u_scr, back, first, n)
        return out

    def down_proj(f, src, first, n):
        a = conv(src[0], cols(f, False), 1.0, first, n)
        half_b = conv(src[1], cols(f, True), 0.5, first, n)
        c0 = math.sqrt(2.0 / math.pi)
        inner = a * (c0 + (c0 * 0.044715) * (a * a))
        hidden = ((a * half_b) * (1.0 + jnp.tanh(inner))).astype(BF)
        return _dot(hidden, wd_ref[pl.ds(pl.multiple_of(f * tf, tf), tf), :])

    def stage(f, parity):
        src, dst = u_sets[1 - parity], u_sets[parity]
        for c in range(n_chunks):
            up_proj(f, dst, c * chunk, (c + 1) * chunk)
            out_rows = slice(c * chunk, (c + 1) * chunk)
            acc_scr[out_rows, :] += down_proj(f - 1, src, c * chunk, chunk)

    n_slabs = D_MODEL // LANES
    g = gpre_ref[...]
    hh = _rms(halo_ref[...], g)
    hh = jnp.where(i % tiles_per_seq == 0, jnp.zeros_like(hh), hh)
    hx = _rms(x_ref[...], g)
    for s in range(n_slabs):
        rows_scr[s, 0:halo, :] = hh[:, s * LANES:(s + 1) * LANES]
        rows_scr[s, halo:, :] = hx[:, s * LANES:(s + 1) * LANES]
    for gg in range(0, n_groups, 2):
        pair = jnp.concatenate(
            [jnp.concatenate([rows_scr[s, group_rows(gg + d), :] for s in range(n_slabs)], axis=1)
             for d in range(2)], axis=0)
        h_scr[gg * SUBLANES:(gg + 2) * SUBLANES, :] = pair.astype(BF)
    acc_scr[...] = jnp.zeros_like(acc_scr)
    up_proj(0, u_sets[0], 0, n_rows)

    def pair_of_tiles(k, carry):
        stage(2 * k + 1, 1)
        stage(2 * k + 2, 0)
        return carry

    assert n_tiles % 2 == 1
    lax.fori_loop(0, (n_tiles - 1) // 2, pair_of_tiles, 0)
    gpost = gpost_ref[...]
    for c in range(n_chunks):
        y = acc_scr[c * chunk:(c + 1) * chunk, :] + down_proj(n_tiles - 1, u_sets[0], c * chunk, chunk)
        normed = _rms(y, gpost)
        for gg in range(chunk // SUBLANES):
            g_abs = c * (chunk // SUBLANES) + gg
            for s in range(n_slabs):
                rows_scr[s, group_rows(g_abs), :] = normed[gg * SUBLANES:(gg + 1) * SUBLANES,
                                                           s * LANES:(s + 1) * LANES]
    for s in range(n_slabs):
        lanes = slice(s * LANES, (s + 1) * LANES)
        o_ref[:, lanes] = x_ref[:, lanes] + rows_scr[s, halo:, :]


def _ffn(x1, g_pre, w_up, conv_w, conv_b, w_down, g_post, *, seq, tm, tf):
    tokens = x1.shape[0]
    halo = BF16_ROWS
    per = tm // halo
    resident = lambda shape: pl.BlockSpec(shape, lambda i: (0, 0),
                                          pipeline_mode=pl.Buffered(1))
    u_buf = pltpu.VMEM((tm + halo, tf), F32)
    return pl.pallas_call(
        functools.partial(_ffn_kernel, tm=tm, tf=tf, tiles_per_seq=seq // tm, n_chunks=5),
        grid=(tokens // tm,),
        in_specs=[
            pl.BlockSpec((tm, D_MODEL), lambda i: (i, 0)),
            pl.BlockSpec((halo, D_MODEL), lambda i: (jnp.maximum(i * per - 1, 0), 0)),
            resident((1, D_MODEL)),
            resident((D_MODEL, 2 * D_FF)),
            resident((CONV_WIDTH, 2 * D_FF)),
            resident((1, 2 * D_FF)),
            resident((D_FF, D_MODEL)),
            resident((1, D_MODEL)),
        ],
        out_specs=pl.BlockSpec((tm, D_MODEL), lambda i: (i, 0)),
        out_shape=jax.ShapeDtypeStruct((tokens, D_MODEL), F32),
        scratch_shapes=[
            pltpu.VMEM((tm + halo, D_MODEL), BF),
            u_buf, u_buf, u_buf, u_buf,
            pltpu.VMEM((tm + halo, D_MODEL), F32),
            pltpu.VMEM((D_MODEL // LANES, tm + halo, LANES), F32),
        ],
        compiler_params=_cparams(("arbitrary",)),
        name="ffn",
    )(x1, x1, g_pre, w_up, conv_w, conv_b, w_down, g_post)


def _rope_freq_lanes():
    inv_freq = ROPE_THETA ** (-jnp.arange(ROPE_HALF, dtype=F32) * 2.0 / ROPE_DIM)
    return jnp.tile(inv_freq, LANES // ROPE_HALF).reshape(1, LANES)


def kernel(x, g_pre_mix, w_in, b_forget, w_o_fox, w_o_dil, w_out, g_post_mix,
           g_pre_ffn, w_up, conv_w, conv_b, w_down, g_post_ffn):
    batch, seq, d_model = x.shape
    assert d_model == D_MODEL and seq % 1024 == 0
    depth = w_in.shape[0]
    freq_lanes = _rope_freq_lanes()
    x2d = x.reshape(batch * seq, D_MODEL)
    row = lambda v: v.reshape(1, -1)
    for l in range(depth):
        bias_lanes = jnp.pad(b_forget[l], (0, LANES - N_HEADS)).reshape(1, LANES)

        z, fa, zd4, zd16 = _in_proj(x2d, row(g_pre_mix[l]), w_in[l].astype(BF), freq_lanes,
                                    batch=batch, seq=seq, tm=DIL_CHUNK)
        gq, gk = _forget_scan(fa, bias_lanes, batch=batch, seq=seq, chunk=512)
        ya = _fox_attention(z, gq, gk, batch=batch, seq=seq, tq=512)
        yb = _dilated_attention(z, zd4, zd16, batch=batch, seq=seq)
        x2d = _mix(ya, yb, z, x2d, w_o_fox[l].astype(BF), w_o_dil[l].astype(BF),
                   w_out[l].astype(BF), row(g_post_mix[l]), tm=1024)
        x2d = _ffn(x2d, row(g_pre_ffn[l]), w_up[l].astype(BF), conv_w[l],
                   row(conv_b[l]), w_down[l].astype(BF), row(g_post_ffn[l]),
                   seq=seq, tm=1024, tf=256)
    return x2d.reshape(batch, seq, D_MODEL)
```

```python
import functools
import math

import numpy as np
import jax
import jax.numpy as jnp
from jax import lax
from jax.experimental import pallas as pl
from jax.experimental.pallas import tpu as pltpu

D_MODEL = 1024
HEAD_DIM = 64
N_HEADS = 8
ATT_WIDTH = N_HEADS * HEAD_DIM
N_HEAD_PAIRS = N_HEADS // 2
DIL_PATTERNS = ((128, 1), (512, 4), (2048, 16))
ROPE_DIM = HEAD_DIM // 4
ROPE_HALF = ROPE_DIM // 2
ROPE_THETA = 500000.0
D_FF = 2816
CONV_WIDTH = 3
RMS_EPS = 1e-6
NEG_INF = -1e30
Q_SCALE = 1.0 / math.sqrt(HEAD_DIM)
LOG2_E = math.log2(math.e)
UNDERFLOW_BITS = 150.0
SCORE_MARGIN = 1.0
NORM_SLACK = 1.0 + 2.0 ** -7

LANES = 128
SUBLANES = 8
BF16_ROWS = 16
Z_WIDTH = 3 * ATT_WIDTH * 2 + 2 * D_MODEL
Z_BLK = 512
DIL_SECTION = 3
DIL_CHUNK = 512
N_PIECES = 3
VMEM_LIMIT = 56 * 1024 * 1024

BF = jnp.bfloat16
F32 = jnp.float32


def _cparams(sem, flags=None):
    return pltpu.CompilerParams(dimension_semantics=sem, vmem_limit_bytes=VMEM_LIMIT,
                                flags=flags)


def _rms(xf, g):
    inv = lax.rsqrt(jnp.mean(xf * xf, axis=-1, keepdims=True) + RMS_EPS)
    return xf * inv * g


def _split3(x):
    hi = x.astype(BF)
    r1 = x - hi.astype(F32)
    mid = r1.astype(BF)
    lo = (r1 - mid.astype(F32)).astype(BF)
    return hi, mid, lo


def _dot(a, b):
    return jnp.dot(a, b, preferred_element_type=F32)


def _dot_nt(a, b):
    return lax.dot_general(a, b, (((1,), (1,)), ((), ())), preferred_element_type=F32)


def _in_proj_kernel(x_ref, g_ref, w_raw_ref, freq_ref, z_ref, fa_ref, zd4_ref, zd16_ref,
                    w_ref, h_scr, cos_scr, sneg_scr, spos_scr, stage_scr, stage4_scr, *, tm):
    p = pl.program_id(0)
    b = pl.program_id(1)
    zd_refs = {4: zd4_ref, 16: zd16_ref}
    n_slabs = Z_BLK // LANES

    @pl.when((p == 0) & (b == 0))
    def _regroup_weight():
        fox_end = 3 * ATT_WIDTH
        raw_width = w_raw_ref.shape[1]
        first = fox_end // LANES

        def group(i):
            lo = i * LANES
            return w_raw_ref[:, lo:min(lo + LANES, raw_width)].astype(F32)

        w_ref[:, 0:fox_end] = w_raw_ref[:, 0:fox_end]
        for k in range((Z_WIDTH - fox_end) // LANES):
            shifted = jnp.concatenate(
                [group(first + k)[:, N_HEADS:], group(first + k + 1)[:, :N_HEADS]], axis=1)
            w_ref[:, fox_end + k * LANES:fox_end + (k + 1) * LANES] = shifted.astype(BF)
        w_ref[:, Z_WIDTH:Z_WIDTH + LANES] = jnp.concatenate(
            [group(first)[:, :N_HEADS], jnp.zeros((D_MODEL, LANES - N_HEADS), F32)],
            axis=1).astype(BF)

    @pl.when(b == 0)
    def _tables():
        pos = (p * tm + lax.broadcasted_iota(jnp.int32, (tm, LANES), 0)).astype(F32)
        lane = lax.broadcasted_iota(jnp.int32, (tm, LANES), 1)
        c = lane % HEAD_DIM
        ang = pos * freq_ref[...]
        cs = jnp.cos(ang)
        sn = jnp.sin(ang)
        cos_scr[...] = jnp.where(c < ROPE_DIM, cs, 1.0)
        sneg_scr[...] = jnp.where(c < ROPE_HALF, -sn, 0.0)
        spos_scr[...] = jnp.where((c >= ROPE_HALF) & (c < ROPE_DIM), sn, 0.0)

    def rope(t):
        up = pltpu.roll(t, LANES - ROPE_HALF, 1)
        dn = pltpu.roll(t, ROPE_HALF, 1)
        return t * cos_scr[...] + up * sneg_scr[...] + dn * spos_scr[...]

    def emit_dilated(section, slab, val):
        out_lanes = slice(section * Z_BLK + slab * LANES, section * Z_BLK + (slab + 1) * LANES)
        z_ref[:, (DIL_SECTION + section) * Z_BLK + slab * LANES:
              (DIL_SECTION + section) * Z_BLK + (slab + 1) * LANES] = val.astype(BF)
        stage, stage4 = stage_scr.at[section, slab], stage4_scr.at[section, slab]
        stage[...] = val
        seg4 = tm // 4
        for c in range(4):
            part = stage[pl.ds(c, seg4, stride=4), :]
            stage4[c * seg4:(c + 1) * seg4, :] = part
            zd_refs[4][c * seg4:(c + 1) * seg4, out_lanes] = part.astype(BF)
        seg16 = tm // 16
        for jj in range(16):
            c, a = jj % 4, jj // 4
            part = stage4[pl.ds(c * seg4 + a, seg16, stride=4), :]
            zd_refs[16][jj * seg16:(jj + 1) * seg16, out_lanes] = part.astype(BF)

    h = _rms(x_ref[...], g_ref[...]).astype(BF)
    h_scr[...] = h
    fa_ref[...] = _dot(h, w_ref[:, Z_WIDTH:Z_WIDTH + LANES])

    for j in range(Z_WIDTH // Z_BLK):
        cols = slice(j * Z_BLK, (j + 1) * Z_BLK)
        acc = _dot(h_scr[...], w_ref[:, cols])
        section = j - DIL_SECTION
        if j == 0:
            z_ref[:, cols] = (acc * (Q_SCALE * LOG2_E)).astype(BF)
        elif 0 <= section < 3:
            for slab in range(n_slabs):
                t = acc[:, slab * LANES:(slab + 1) * LANES]
                if section == 0:
                    t = rope(t) * (Q_SCALE * LOG2_E)
                elif section == 1:
                    t = rope(t)
                emit_dilated(section, slab, t)
        else:
            z_ref[:, cols] = acc.astype(BF)


def _in_proj(x2d, g, w_raw, freq_lanes, *, batch, seq, tm):
    n_p = seq // tm
    tokens = batch * seq
    row = lambda p, b: (b * n_p + p, 0)
    resident = lambda shape: pl.BlockSpec(shape, lambda p, b: (0, 0),
                                          pipeline_mode=pl.Buffered(1))
    dil_shape = jax.ShapeDtypeStruct((tokens, 3 * ATT_WIDTH), BF)
    stage_buf = pltpu.VMEM((3, Z_BLK // LANES, tm, LANES), F32)
    return pl.pallas_call(
        functools.partial(_in_proj_kernel, tm=tm),
        grid=(n_p, batch),
        in_specs=[
            pl.BlockSpec((tm, D_MODEL), row),
            resident((1, D_MODEL)),
            resident((D_MODEL, Z_WIDTH + N_HEADS)),
            resident((1, LANES)),
        ],
        out_specs=[
            pl.BlockSpec((tm, Z_WIDTH), row),
            pl.BlockSpec((tm, LANES), row),
            pl.BlockSpec((tm, 3 * ATT_WIDTH), row),
            pl.BlockSpec((tm, 3 * ATT_WIDTH), row),
        ],
        out_shape=[
            jax.ShapeDtypeStruct((tokens, Z_WIDTH), BF),
            jax.ShapeDtypeStruct((tokens, LANES), F32),
            dil_shape,
            dil_shape,
        ],
        scratch_shapes=[
            pltpu.VMEM((D_MODEL, Z_WIDTH + LANES), BF),
            pltpu.VMEM((tm, D_MODEL), BF),
            pltpu.VMEM((tm, LANES), F32),
            pltpu.VMEM((tm, LANES), F32),
            pltpu.VMEM((tm, LANES), F32),
            stage_buf,
            stage_buf,
        ],
        compiler_params=_cparams(("arbitrary", "arbitrary")),
        name="in_proj",
    )(x2d, g, w_raw, freq_lanes)


def _extras_base(head):
    return (HEAD_DIM if head % 2 == 0 else 0) + 2 * N_PIECES * (head // 2)


def _forget_scan_kernel(fa_ref, bias_ref, tri_ref, pq_ref, pk_ref, oq_ref, ok_ref,
                        gq_ref, gk_ref):
    chunk = tri_ref.shape[0]
    n_chunks = fa_ref.shape[0] // chunk
    t = fa_ref[...] + bias_ref[...]
    log_f = jnp.minimum(t, 0.0) - jnp.log1p(jnp.exp(-jnp.abs(t)))
    wide = jnp.concatenate(
        [log_f[c * chunk:(c + 1) * chunk, :] for c in range(n_chunks)], axis=1)
    tri = tri_ref[...]
    local = sum(_dot(tri, piece) for piece in _split3(wide))
    carry = jnp.zeros((1, LANES), F32)
    parts = []
    for c in range(n_chunks):
        part = local[:, c * LANES:(c + 1) * LANES] + carry
        parts.append(part)
        carry = part[chunk - 1:chunk, :]
    run = jnp.concatenate(parts, axis=0)
    gq = oq_ref[...].astype(F32)
    gk = ok_ref[...].astype(F32)
    for i, piece in enumerate(_split3(run * LOG2_E)):
        gq = gq + _dot(piece, pq_ref[i])
        gk = gk - _dot(piece, pk_ref[i])
    gq_ref[0] = gq.astype(BF)
    gk_ref[0] = gk.astype(BF)


def _forget_scan_constants(chunk):
    tri = np.tril(np.ones((chunk, chunk), np.float32))
    pq = np.zeros((N_PIECES, LANES, LANES), np.float32)
    pk = np.zeros((N_PIECES, LANES, LANES), np.float32)
    oq = np.zeros((1, LANES), np.float32)
    ok = np.zeros((1, LANES), np.float32)
    for h in range(N_HEADS):
        base = _extras_base(h)
        for i in range(N_PIECES):
            pq[i, h, base + i] = 1.0
            oq[0, base + N_PIECES + i] = 1.0
            ok[0, base + i] = 1.0
            pk[i, h, base + N_PIECES + i] = 1.0
    as_bf = lambda a: jnp.asarray(a, BF)
    return as_bf(tri), as_bf(pq), as_bf(pk), as_bf(oq), as_bf(ok)


def _forget_scan(fa, bias_lanes, *, batch, seq, chunk):
    tri, pq, pk, oq, ok = _forget_scan_constants(chunk)
    const2 = lambda b: (0, 0)
    const3 = lambda b: (0, 0, 0)
    return pl.pallas_call(
        _forget_scan_kernel,
        grid=(batch,),
        in_specs=[
            pl.BlockSpec((seq, LANES), lambda b: (b, 0)),
            pl.BlockSpec((1, LANES), const2),
            pl.BlockSpec((chunk, chunk), const2),
            pl.BlockSpec((N_PIECES, LANES, LANES), const3),
            pl.BlockSpec((N_PIECES, LANES, LANES), const3),
            pl.BlockSpec((1, LANES), const2),
            pl.BlockSpec((1, LANES), const2),
        ],
        out_specs=[
            pl.BlockSpec((1, seq, LANES), lambda b: (b, 0, 0)),
            pl.BlockSpec((1, seq, LANES), lambda b: (b, 0, 0)),
        ],
        out_shape=[
            jax.ShapeDtypeStruct((batch, seq, LANES), BF),
            jax.ShapeDtypeStruct((batch, seq, LANES), BF),
        ],
        compiler_params=_cparams(("arbitrary",)),
        name="forget_scan",
    )(fa, bias_lanes, tri, pq, pk, oq, ok)


def _with_bias_lanes(slab, extras, hp, parity):
    lane = lax.broadcasted_iota(jnp.int32, slab.shape, 1)
    own = (lane < HEAD_DIM) if parity == 0 else (lane >= HEAD_DIM)
    base = (HEAD_DIM if parity == 0 else 0) + 2 * N_PIECES * hp
    in_extras = (lane >= base) & (lane < base + 2 * N_PIECES)
    return jnp.where(own, slab, jnp.where(in_extras, extras, jnp.zeros_like(extras)))


def _fox_kernel(q_ref, k_ref, v_ref, gq_ref, gk_ref, o_ref,
                ka_scr, kb_scr, vta_scr, vtb_scr, qp_scr, bias_scr, s_scr, acc_scr, ot_scr,
                bound_smem, *, tq):
    hp = pl.program_id(1)
    k_scrs = (ka_scr, kb_scr)
    vt_scrs = (vta_scr, vtb_scr)
    seq = k_ref.shape[0]
    n_q = seq // tq
    lane_row = lax.broadcasted_iota(jnp.int32, (1, LANES), 1)

    sel_row = lax.broadcasted_iota(jnp.int32, (LANES, LANES), 0)
    sel_col = lax.broadcasted_iota(jnp.int32, (LANES, LANES), 1)
    head_sum = jnp.where(sel_row // HEAD_DIM == sel_col, 1.0, 0.0).astype(BF)

    def max_row_norm(slab, head):
        lane = lax.broadcasted_iota(jnp.int32, slab.shape, 1)
        own = (lane < HEAD_DIM) if head == 0 else (lane >= HEAD_DIM)
        x = jnp.where(own, slab.astype(F32), 0.0)
        sq = jnp.sum(x * x, axis=1, keepdims=True)
        return jnp.sqrt(jnp.max(sq, axis=0, keepdims=True))

    def max_row_norms(slab):
        x = slab.astype(F32)
        per_row = _dot((x * x).astype(BF), head_sum)
        return jnp.sqrt(jnp.max(per_row, axis=0, keepdims=True) * NORM_SLACK)

    def bias_lane_sum(row, head, first):
        base = (HEAD_DIM if head == 0 else 0) + 2 * N_PIECES * hp + first
        picked = (lane_row >= base) & (lane_row < base + N_PIECES)
        return jnp.sum(jnp.where(picked, row, 0.0), axis=1, keepdims=True)

    k2 = k_ref[...]
    gk = gk_ref[0]
    k_norms = max_row_norms(k2)
    for head in range(2):
        k_scrs[head][...] = _with_bias_lanes(k2, gk, hp, head)
        bound_smem[head, n_q] = k_norms[0, head]
        for j in range(n_q):
            rows16 = gk_ref[0, (j + 1) * tq - BF16_ROWS:(j + 1) * tq, :].astype(F32)
            last = rows16[BF16_ROWS - 1:BF16_ROWS, :]
            bound_smem[head, j] = -bias_lane_sum(last, head, N_PIECES)[0, 0]
    vt = v_ref[...].T
    ones = jnp.ones((BF16_ROWS, seq), BF)
    for head, vt_scr in enumerate(vt_scrs):
        vt_scr[0:HEAD_DIM, :] = vt[head * HEAD_DIM:(head + 1) * HEAD_DIM, :]
        vt_scr[HEAD_DIM:, :] = ones
    row = lax.broadcasted_iota(jnp.int32, (tq, tq), 0)
    col = lax.broadcasted_iota(jnp.int32, (tq, tq), 1)
    bias_scr[...] = jnp.where(row <= col, 0.0, NEG_INF)
    acc_scr[...] = jnp.zeros_like(acc_scr)

    def scores(head, tile, masked):
        kt = k_scrs[head][pl.ds(pl.multiple_of(tile * tq, tq), tq), :]
        st = _dot(kt, qp_scr[head])
        if masked:
            st = st + bias_scr[...]
        s_scr[head] = st
        return jnp.max(st, axis=0, keepdims=True)

    def softmax_pv(head, st, tile, tile_max, m):
        start = pl.multiple_of(tile * tq, tq)
        m_new = jnp.maximum(m, tile_max)
        alpha = jnp.exp2(m - m_new)
        pt = jnp.exp2(st - m_new).astype(BF)
        vt_tile = vt_scrs[head][:, pl.ds(start, tq)]
        acc_scr[head] = alpha * acc_scr[head] + _dot(vt_tile, pt)
        return m_new

    def start_tile(qi):
        rows = pl.ds(pl.multiple_of(qi * tq, tq), tq)
        q2 = q_ref[rows, :]
        gq = gq_ref[0, rows, :]
        first_row = gq[0:BF16_ROWS, :].astype(F32)[0:1, :]
        maxes, slack = [], []
        for head in range(2):
            qp_scr[head] = _with_bias_lanes(q2, gq, hp, head).T
            maxes.append(scores(head, qi, True))
            reach = (max_row_norm(q2, head) * bound_smem[head, n_q]
                     + bias_lane_sum(first_row, head, 0) + SCORE_MARGIN)
            lowest_max = jnp.min(maxes[head], axis=1, keepdims=True)
            slack.append(reach - (lowest_max - UNDERFLOW_BITS))
        count = jnp.zeros((1, 1), jnp.int32)
        for j in range(n_q - 1):
            needed = (((slack[0] >= bound_smem[0, j]) | (slack[1] >= bound_smem[1, j]))
                      & (j < qi))
            count = count + jnp.where(needed, 1, 0)
        return maxes[0], maxes[1], count[0, 0]

    def finish_tile(qi):
        for head in range(2):
            acc = acc_scr[head]
            ot_scr[head * HEAD_DIM:(head + 1) * HEAD_DIM, :] = (
                acc[0:HEAD_DIM, :] / acc[HEAD_DIM:HEAD_DIM + 1, :])
        o_ref[pl.ds(pl.multiple_of(qi * tq, tq), tq), :] = ot_scr[...].T.astype(BF)
        acc_scr[...] = jnp.zeros_like(acc_scr)

    m_init = jnp.full((1, tq), NEG_INF, F32)

    def query_tile(qi, carry):
        first_a, first_b, n_needed = carry

        def key_tile(t, inner):
            max_a, max_b, m_a, m_b, prev_tile = inner
            prev = [s_scr[head] for head in range(2)]
            tile = qi - 1 - t
            next_max = [scores(head, tile, False) for head in range(2)]
            m_a = softmax_pv(0, prev[0], prev_tile, max_a, m_a)
            m_b = softmax_pv(1, prev[1], prev_tile, max_b, m_b)
            return next_max[0], next_max[1], m_a, m_b, tile

        max_a, max_b, m_a, m_b, prev_tile = lax.fori_loop(
            0, n_needed, key_tile, (first_a, first_b, m_init, m_init, qi))
        last = [s_scr[head] for head in range(2)]
        softmax_pv(0, last[0], prev_tile, max_a, m_a)
        softmax_pv(1, last[1], prev_tile, max_b, m_b)
        following = start_tile(jnp.minimum(qi + 1, n_q - 1))
        finish_tile(qi)
        return following

    lax.fori_loop(0, n_q, query_tile, start_tile(0))


def _fox_attention(z, gq, gk, *, batch, seq, tq):
    tokens = batch * seq
    col = lambda section: pl.BlockSpec(
        (seq, LANES), lambda b, hp: (b, section * N_HEAD_PAIRS + hp))
    whole = pl.BlockSpec((1, seq, LANES), lambda b, hp: (b, 0, 0))
    ext = HEAD_DIM + BF16_ROWS
    return pl.pallas_call(
        functools.partial(_fox_kernel, tq=tq),
        grid=(batch, N_HEAD_PAIRS),
        in_specs=[col(0), col(1), col(2), whole, whole],
        out_specs=pl.BlockSpec((seq, LANES), lambda b, hp: (b, hp)),
        out_shape=jax.ShapeDtypeStruct((tokens, ATT_WIDTH), BF),
        scratch_shapes=[
            pltpu.VMEM((seq, LANES), BF),
            pltpu.VMEM((seq, LANES), BF),
            pltpu.VMEM((ext, seq), BF),
            pltpu.VMEM((ext, seq), BF),
            pltpu.VMEM((2, LANES, tq), BF),
            pltpu.VMEM((tq, tq), F32),
            pltpu.VMEM((2, tq, tq), F32),
            pltpu.VMEM((2, ext, tq), F32),
            pltpu.VMEM((LANES, tq), F32),
            pltpu.SMEM((2, seq // tq + 1), F32),
        ],
        compiler_params=_cparams(("arbitrary", "arbitrary")),
        name="fox_attn",
    )(z, z, z, gq, gk)


def _log2(n):
    assert n > 0 and n & (n - 1) == 0, n
    return n.bit_length() - 1


def _dilated_kernel(q1, k1, v1, q4, k4, v4, q16, k16, v16, o_ref,
                    num1, num4, num16, den1, den4, den16, max1, max4, max16, bias_scr,
                    *, seq, blk, chunk, group, merge_rows):
    sources = ((q1, k1, v1), (q4, k4, v4), (q16, k16, v16))
    num_scrs = (num1, num4, num16)
    den_scrs = (den1, den4, den16)
    max_scrs = (max1, max4, max16)
    n_blocks = seq // blk

    low = lax.broadcasted_iota(jnp.int32, (blk, LANES), 1) < HEAD_DIM
    qrow = lax.broadcasted_iota(jnp.int32, (2 * blk, 2 * blk), 0) % blk
    kcol = lax.broadcasted_iota(jnp.int32, (2 * blk, 2 * blk), 1)
    dist = qrow + blk - kcol
    band = (dist >= 0) & (dist <= blk)
    bias_scr[0] = jnp.where(band & (kcol >= blk), 0.0, NEG_INF)
    bias_scr[1] = jnp.where(band, 0.0, NEG_INF)
    ones = jnp.ones((2 * blk, LANES), BF)

    for idx, (window, dilation) in enumerate(DIL_PATTERNS):
        assert window // dilation == blk
        q_ref, k_ref, v_ref = sources[idx]
        num_scr, den_scr, max_scr = num_scrs[idx], den_scrs[idx], max_scrs[idx]
        sub_shift = _log2(seq // dilation // blk)
        seg_rows = chunk // dilation
        seg = min(blk, seg_rows)

        def load(ref, j, l0):
            parts = []
            for s in range(blk // seg):
                l = l0 + s * seg
                p = lax.shift_right_logical(l, _log2(seg_rows))
                i = l & (seg_rows - 1)
                start = pl.multiple_of(p * chunk + j * seg_rows + i, seg)
                parts.append(ref[pl.ds(start, seg), :])
            return parts[0] if len(parts) == 1 else jnp.concatenate(parts, axis=0)

        def one_block(g):
            j = lax.shift_right_logical(g, sub_shift)
            gs = g & ((1 << sub_shift) - 1)
            l0 = gs * blk
            lp = jnp.maximum(l0 - blk, 0)
            q2 = load(q_ref, j, l0)
            zero = jnp.zeros_like(q2)
            qq = jnp.concatenate([jnp.where(low, q2, zero), jnp.where(low, zero, q2)], axis=0)
            kwin = jnp.concatenate([load(k_ref, j, lp), load(k_ref, j, l0)], axis=0)
            vwin = jnp.concatenate([load(v_ref, j, lp), load(v_ref, j, l0)], axis=0)
            s = _dot_nt(qq, kwin) + bias_scr[jnp.minimum(gs, 1)]
            m = jnp.max(s, axis=1, keepdims=True)
            p = jnp.exp2(s - m).astype(BF)
            pv = _dot(p, jnp.concatenate([vwin, ones], axis=1))
            if dilation == 1:
                dst = pl.ds(pl.multiple_of(l0, blk), blk)
            else:
                dst = pl.ds(l0 * dilation + j, blk, stride=dilation)
            num_scr[dst, :] = jnp.where(low, pv[:blk, :LANES], pv[blk:, :LANES])
            den_scr[dst, :] = jnp.where(low, pv[:blk, LANES:], pv[blk:, LANES:])
            max_scr[dst, :] = jnp.where(low, jnp.broadcast_to(m[:blk], (blk, LANES)),
                                        jnp.broadcast_to(m[blk:], (blk, LANES)))

        def blocks(it, carry):
            for u in range(group):
                one_block(it * group + u)
            return carry

        lax.fori_loop(0, n_blocks // group, blocks, 0)

    def merge(c, carry):
        rows = pl.ds(pl.multiple_of(c * merge_rows, merge_rows), merge_rows)
        maxes = [max_scr[rows, :] for max_scr in max_scrs]
        top = functools.reduce(jnp.maximum, maxes)
        weights = [jnp.exp2(m - top) for m in maxes]
        num = sum(w * num_scr[rows, :] for w, num_scr in zip(weights, num_scrs))
        den = sum(w * den_scr[rows, :] for w, den_scr in zip(weights, den_scrs))
        o_ref[rows, :] = (num / den).astype(BF)
        return carry

    lax.fori_loop(0, seq // merge_rows, merge, 0)


def _dilated_attention(z, zd4, zd16, *, batch, seq):
    blk = DIL_PATTERNS[0][0] // DIL_PATTERNS[0][1]
    tokens = batch * seq
    per_section = ATT_WIDTH // LANES
    col = lambda section: pl.BlockSpec(
        (seq, LANES), lambda b, hp: (b, section * per_section + hp))
    f32_buf = pltpu.VMEM((seq, LANES), F32)
    return pl.pallas_call(
        functools.partial(_dilated_kernel, seq=seq, blk=blk, chunk=DIL_CHUNK, group=32,
                          merge_rows=256),
        grid=(batch, N_HEAD_PAIRS),
        in_specs=[col(DIL_SECTION), col(DIL_SECTION + 1), col(DIL_SECTION + 2),
                  col(0), col(1), col(2), col(0), col(1), col(2)],
        out_specs=pl.BlockSpec((seq, LANES), lambda b, hp: (b, hp)),
        out_shape=jax.ShapeDtypeStruct((tokens, ATT_WIDTH), BF),
        scratch_shapes=[f32_buf] * 9 + [pltpu.VMEM((2, 2 * blk, 2 * blk), F32)],
        compiler_params=_cparams(("arbitrary", "arbitrary")),
        name="dilated_attn",
    )(z, z, z, zd4, zd4, zd4, zd16, zd16, zd16)


def _mix_kernel(ya_ref, yb_ref, ga_ref, gb_ref, x_ref, woa_ref, wob_ref, wout_ref,
                g_ref, o_ref):
    pa = _dot(ya_ref[...], woa_ref[...])
    pb = _dot(yb_ref[...], wob_ref[...])
    mixed = (jax.nn.sigmoid(ga_ref[...].astype(F32)) * pa
             + jax.nn.sigmoid(gb_ref[...].astype(F32)) * pb)
    y = _dot(mixed.astype(BF), wout_ref[...])
    o_ref[...] = x_ref[...] + _rms(y, g_ref[...])


def _mix(ya, yb, z, x2d, woa, wob, wout, g, *, tm):
    tokens = x2d.shape[0]
    gate_blk = lambda off: pl.BlockSpec((tm, D_MODEL), lambda i: (i, off))
    const = lambda i: (0, 0)
    return pl.pallas_call(
        _mix_kernel,
        grid=(tokens // tm,),
        in_specs=[
            pl.BlockSpec((tm, ATT_WIDTH), lambda i: (i, 0)),
            pl.BlockSpec((tm, ATT_WIDTH), lambda i: (i, 0)),
            gate_blk(3),
            gate_blk(4),
            pl.BlockSpec((tm, D_MODEL), lambda i: (i, 0)),
            pl.BlockSpec((ATT_WIDTH, D_MODEL), const),
            pl.BlockSpec((ATT_WIDTH, D_MODEL), const),
            pl.BlockSpec((D_MODEL, D_MODEL), const),
            pl.BlockSpec((1, D_MODEL), const),
        ],
        out_specs=pl.BlockSpec((tm, D_MODEL), lambda i: (i, 0)),
        out_shape=jax.ShapeDtypeStruct((tokens, D_MODEL), F32),
        compiler_params=_cparams(("arbitrary",)),
        name="mix",
    )(ya, yb, z, z, x2d, woa, wob, wout, g)


def _ffn_kernel(x_ref, halo_ref, gpre_ref, wup_ref, cw_ref, cb_ref, wd_ref, gpost_ref, o_ref,
                h_scr, ua0_scr, ub0_scr, ua1_scr, ub1_scr, acc_scr, rows_scr,
                *, tm, tf, tiles_per_seq, n_chunks):
    i = pl.program_id(0)
    halo = BF16_ROWS
    n_rows = tm + halo
    n_groups = n_rows // SUBLANES
    chunk = n_rows // n_chunks
    assert n_groups * SUBLANES == n_rows and chunk * n_chunks == n_rows
    assert chunk % BF16_ROWS == 0 and n_groups % 2 == 0
    n_tiles = D_FF // tf
    u_sets = ((ua0_scr, ub0_scr), (ua1_scr, ub1_scr))

    def group_rows(g):
        return pl.ds(g, SUBLANES, stride=n_groups)

    def cols(f, gate):
        return pl.ds(pl.multiple_of(f * tf + (D_FF if gate else 0), LANES), tf)

    def up_proj(f, dst, lo, hi):
        h = h_scr[lo:hi, :]
        dst[0][lo:hi, :] = _dot(h, wup_ref[:, cols(f, False)])
        dst[1][lo:hi, :] = _dot(h, wup_ref[:, cols(f, True)])

    def tap_rows(u_scr, back, first, n):
        lo = first - back * SUBLANES
        if lo >= 0:
            return u_scr[lo:lo + n, :]
        wrapped = [pltpu.roll(u_scr[n_rows + k * SUBLANES:n_rows + (k + 1) * SUBLANES, :], 1, 0)
                   for k in range(lo // SUBLANES, 0)]
        return jnp.concatenate(wrapped + [u_scr[0:n + lo, :]], axis=0)

    def conv(u_scr, window, post_scale, first, n):
        out = cb_ref[:, window] * post_scale
        for tap in range(CONV_WIDTH):
            back = CONV_WIDTH - 1 - tap
            out = out + (cw_ref[tap:tap + 1, window] * post_scale) * tap_rows(u_scr, back, first, n)
        return out

    def down_proj(f, src, first, n):
        a = conv(src[0], cols(f, False), 1.0, first, n)
        half_b = conv(src[1], cols(f, True), 0.5, first, n)
        c0 = math.sqrt(2.0 / math.pi)
        inner = a * (c0 + (c0 * 0.044715) * (a * a))
        hidden = ((a * half_b) * (1.0 + jnp.tanh(inner))).astype(BF)
        return _dot(hidden, wd_ref[pl.ds(pl.multiple_of(f * tf, tf), tf), :])

    def stage(f, parity):
        src, dst = u_sets[1 - parity], u_sets[parity]
        for c in range(n_chunks):
            up_proj(f, dst, c * chunk, (c + 1) * chunk)
            out_rows = slice(c * chunk, (c + 1) * chunk)
            acc_scr[out_rows, :] += down_proj(f - 1, src, c * chunk, chunk)

    n_slabs = D_MODEL // LANES
    g = gpre_ref[...]
    hh = _rms(halo_ref[...], g)
    hh = jnp.where(i % tiles_per_seq == 0, jnp.zeros_like(hh), hh)
    hx = _rms(x_ref[...], g)
    for s in range(n_slabs):
        rows_scr[s, 0:halo, :] = hh[:, s * LANES:(s + 1) * LANES]
        rows_scr[s, halo:, :] = hx[:, s * LANES:(s + 1) * LANES]
    for gg in range(0, n_groups, 2):
        pair = jnp.concatenate(
            [jnp.concatenate([rows_scr[s, group_rows(gg + d), :] for s in range(n_slabs)], axis=1)
             for d in range(2)], axis=0)
        h_scr[gg * SUBLANES:(gg + 2) * SUBLANES, :] = pair.astype(BF)
    acc_scr[...] = jnp.zeros_like(acc_scr)
    up_proj(0, u_sets[0], 0, n_rows)

    def pair_of_tiles(k, carry):
        stage(2 * k + 1, 1)
        stage(2 * k + 2, 0)
        return carry

    assert n_tiles % 2 == 1
    for k in range((n_tiles - 1) // 2):
        pair_of_tiles(k, 0)
    gpost = gpost_ref[...]
    for c in range(n_chunks):
        y = acc_scr[c * chunk:(c + 1) * chunk, :] + down_proj(n_tiles - 1, u_sets[0], c * chunk, chunk)
        normed = _rms(y, gpost)
        for gg in range(chunk // SUBLANES):
            g_abs = c * (chunk // SUBLANES) + gg
            for s in range(n_slabs):
                rows_scr[s, group_rows(g_abs), :] = normed[gg * SUBLANES:(gg + 1) * SUBLANES,
                                                           s * LANES:(s + 1) * LANES]
    for s in range(n_slabs):
        lanes = slice(s * LANES, (s + 1) * LANES)
        o_ref[:, lanes] = x_ref[:, lanes] + rows_scr[s, halo:, :]


def _ffn(x1, g_pre, w_up, conv_w, conv_b, w_down, g_post, *, seq, tm, tf):
    tokens = x1.shape[0]
    halo = BF16_ROWS
    per = tm // halo
    resident = lambda shape: pl.BlockSpec(shape, lambda i: (0, 0),
                                          pipeline_mode=pl.Buffered(1))
    u_buf = pltpu.VMEM((tm + halo, tf), F32)
    return pl.pallas_call(
        functools.partial(_ffn_kernel, tm=tm, tf=tf, tiles_per_seq=seq // tm, n_chunks=5),
        grid=(tokens // tm,),
        in_specs=[
            pl.BlockSpec((tm, D_MODEL), lambda i: (i, 0)),
            pl.BlockSpec((halo, D_MODEL), lambda i: (jnp.maximum(i * per - 1, 0), 0)),
            resident((1, D_MODEL)),
            resident((D_MODEL, 2 * D_FF)),
            resident((CONV_WIDTH, 2 * D_FF)),
            resident((1, 2 * D_FF)),
            resident((D_FF, D_MODEL)),
            resident((1, D_MODEL)),
        ],
        out_specs=pl.BlockSpec((tm, D_MODEL), lambda i: (i, 0)),
        out_shape=jax.ShapeDtypeStruct((tokens, D_MODEL), F32),
        scratch_shapes=[
            pltpu.VMEM((tm + halo, D_MODEL), BF),
            u_buf, u_buf, u_buf, u_buf,
            pltpu.VMEM((tm + halo, D_MODEL), F32),
            pltpu.VMEM((D_MODEL // LANES, tm + halo, LANES), F32),
        ],
        compiler_params=_cparams(("arbitrary",)),
        name="ffn",
    )(x1, x1, g_pre, w_up, conv_w, conv_b, w_down, g_post)


def _rope_freq_lanes():
    inv_freq = ROPE_THETA ** (-jnp.arange(ROPE_HALF, dtype=F32) * 2.0 / ROPE_DIM)
    return jnp.tile(inv_freq, LANES // ROPE_HALF).reshape(1, LANES)


def kernel(x, g_pre_mix, w_in, b_forget, w_o_fox, w_o_dil, w_out, g_post_mix,
           g_pre_ffn, w_up, conv_w, conv_b, w_down, g_post_ffn):
    batch, seq, d_model = x.shape
    assert d_model == D_MODEL and seq % 1024 == 0
    depth = w_in.shape[0]
    freq_lanes = _rope_freq_lanes()
    x2d = x.reshape(batch * seq, D_MODEL)
    row = lambda v: v.reshape(1, -1)
    for l in range(depth):
        bias_lanes = jnp.pad(b_forget[l], (0, LANES - N_HEADS)).reshape(1, LANES)

        z, fa, zd4, zd16 = _in_proj(x2d, row(g_pre_mix[l]), w_in[l].astype(BF), freq_lanes,
                                    batch=batch, seq=seq, tm=DIL_CHUNK)
        gq, gk = _forget_scan(fa, bias_lanes, batch=batch, seq=seq, chunk=512)
        ya = _fox_attention(z, gq, gk, batch=batch, seq=seq, tq=512)
        yb = _dilated_attention(z, zd4, zd16, batch=batch, seq=seq)
        x2d = _mix(ya, yb, z, x2d, w_o_fox[l].astype(BF), w_o_dil[l].astype(BF),
                   w_out[l].astype(BF), row(g_post_mix[l]), tm=1024)
        x2d = _ffn(x2d, row(g_pre_ffn[l]), w_up[l].astype(BF), conv_w[l],
                   row(conv_b[l]), w_down[l].astype(BF), row(g_post_ffn[l]),
                   seq=seq, tm=1024, tf=256)
    return x2d.reshape(batch, seq, D_MODEL)
```

```python
import functools
import math

import numpy as np
import jax
import jax.numpy as jnp
from jax import lax
from jax.experimental import pallas as pl
from jax.experimental.pallas import tpu as pltpu

D_MODEL = 1024
HEAD_DIM = 64
N_HEADS = 8
ATT_WIDTH = N_HEADS * HEAD_DIM
N_HEAD_PAIRS = N_HEADS // 2
DIL_PATTERNS = ((128, 1), (512, 4), (2048, 16))
ROPE_DIM = HEAD_DIM // 4
ROPE_HALF = ROPE_DIM // 2
ROPE_THETA = 500000.0
D_FF = 2816
CONV_WIDTH = 3
RMS_EPS = 1e-6
NEG_INF = -1e30
Q_SCALE = 1.0 / math.sqrt(HEAD_DIM)
LOG2_E = math.log2(math.e)
UNDERFLOW_BITS = 150.0
SCORE_MARGIN = 1.0
NORM_SLACK = 1.0 + 2.0 ** -7

LANES = 128
SUBLANES = 8
BF16_ROWS = 16
Z_WIDTH = 3 * ATT_WIDTH * 2 + 2 * D_MODEL
Z_BLK = 512
DIL_SECTION = 3
DIL_CHUNK = 512
N_PIECES = 3
VMEM_LIMIT = 56 * 1024 * 1024

BF = jnp.bfloat16
F32 = jnp.float32


def _cparams(sem, flags=None):
    return pltpu.CompilerParams(dimension_semantics=sem, vmem_limit_bytes=VMEM_LIMIT,
                                flags=flags)


def _rms(xf, g):
    inv = lax.rsqrt(jnp.mean(xf * xf, axis=-1, keepdims=True) + RMS_EPS)
    return xf * inv * g


def _split3(x):
    hi = x.astype(BF)
    r1 = x - hi.astype(F32)
    mid = r1.astype(BF)
    lo = (r1 - mid.astype(F32)).astype(BF)
    return hi, mid, lo


def _dot(a, b):
    return jnp.dot(a, b, preferred_element_type=F32)


def _dot_nt(a, b):
    return lax.dot_general(a, b, (((1,), (1,)), ((), ())), preferred_element_type=F32)


def _in_proj_kernel(x_ref, g_ref, w_raw_ref, freq_ref, z_ref, fa_ref, zd4_ref, zd16_ref,
                    w_ref, h_scr, cos_scr, sneg_scr, spos_scr, stage_scr, stage4_scr, *, tm):
    p = pl.program_id(0)
    b = pl.program_id(1)
    zd_refs = {4: zd4_ref, 16: zd16_ref}
    n_slabs = Z_BLK // LANES

    @pl.when((p == 0) & (b == 0))
    def _regroup_weight():
        fox_end = 3 * ATT_WIDTH
        raw_width = w_raw_ref.shape[1]
        first = fox_end // LANES

        def group(i):
            lo = i * LANES
            return w_raw_ref[:, lo:min(lo + LANES, raw_width)].astype(F32)

        w_ref[:, 0:fox_end] = w_raw_ref[:, 0:fox_end]
        for k in range((Z_WIDTH - fox_end) // LANES):
            shifted = jnp.concatenate(
                [group(first + k)[:, N_HEADS:], group(first + k + 1)[:, :N_HEADS]], axis=1)
            w_ref[:, fox_end + k * LANES:fox_end + (k + 1) * LANES] = shifted.astype(BF)
        w_ref[:, Z_WIDTH:Z_WIDTH + LANES] = jnp.concatenate(
            [group(first)[:, :N_HEADS], jnp.zeros((D_MODEL, LANES - N_HEADS), F32)],
            axis=1).astype(BF)

    @pl.when(b == 0)
    def _tables():
        pos = (p * tm + lax.broadcasted_iota(jnp.int32, (tm, LANES), 0)).astype(F32)
        lane = lax.broadcasted_iota(jnp.int32, (tm, LANES), 1)
        c = lane % HEAD_DIM
        ang = pos * freq_ref[...]
        cs = jnp.cos(ang)
        sn = jnp.sin(ang)
        cos_scr[...] = jnp.where(c < ROPE_DIM, cs, 1.0)
        sneg_scr[...] = jnp.where(c < ROPE_HALF, -sn, 0.0)
        spos_scr[...] = jnp.where((c >= ROPE_HALF) & (c < ROPE_DIM), sn, 0.0)

    def rope(t):
        up = pltpu.roll(t, LANES - ROPE_HALF, 1)
        dn = pltpu.roll(t, ROPE_HALF, 1)
        return t * cos_scr[...] + up * sneg_scr[...] + dn * spos_scr[...]

    def emit_dilated(section, slab, val):
        out_lanes = slice(section * Z_BLK + slab * LANES, section * Z_BLK + (slab + 1) * LANES)
        z_ref[:, (DIL_SECTION + section) * Z_BLK + slab * LANES:
              (DIL_SECTION + section) * Z_BLK + (slab + 1) * LANES] = val.astype(BF)
        stage, stage4 = stage_scr.at[section, slab], stage4_scr.at[section, slab]
        stage[...] = val
        seg4 = tm // 4
        for c in range(4):
            part = stage[pl.ds(c, seg4, stride=4), :]
            stage4[c * seg4:(c + 1) * seg4, :] = part
            zd_refs[4][c * seg4:(c + 1) * seg4, out_lanes] = part.astype(BF)
        seg16 = tm // 16
        for jj in range(16):
            c, a = jj % 4, jj // 4
            part = stage4[pl.ds(c * seg4 + a, seg16, stride=4), :]
            zd_refs[16][jj * seg16:(jj + 1) * seg16, out_lanes] = part.astype(BF)

    h = _rms(x_ref[...], g_ref[...]).astype(BF)
    h_scr[...] = h
    fa_ref[...] = _dot(h, w_ref[:, Z_WIDTH:Z_WIDTH + LANES])

    for j in range(Z_WIDTH // Z_BLK):
        cols = slice(j * Z_BLK, (j + 1) * Z_BLK)
        acc = _dot(h_scr[...], w_ref[:, cols])
        section = j - DIL_SECTION
        if j == 0:
            z_ref[:, cols] = (acc * (Q_SCALE * LOG2_E)).astype(BF)
        elif 0 <= section < 3:
            for slab in range(n_slabs):
                t = acc[:, slab * LANES:(slab + 1) * LANES]
                if section == 0:
                    t = rope(t) * (Q_SCALE * LOG2_E)
                elif section == 1:
                    t = rope(t)
                emit_dilated(section, slab, t)
        else:
            z_ref[:, cols] = acc.astype(BF)


def _in_proj(x2d, g, w_raw, freq_lanes, *, batch, seq, tm):
    n_p = seq // tm
    tokens = batch * seq
    row = lambda p, b: (b * n_p + p, 0)
    resident = lambda shape: pl.BlockSpec(shape, lambda p, b: (0, 0),
                                          pipeline_mode=pl.Buffered(1))
    dil_shape = jax.ShapeDtypeStruct((tokens, 3 * ATT_WIDTH), BF)
    stage_buf = pltpu.VMEM((3, Z_BLK // LANES, tm, LANES), F32)
    return pl.pallas_call(
        functools.partial(_in_proj_kernel, tm=tm),
        grid=(n_p, batch),
        in_specs=[
            pl.BlockSpec((tm, D_MODEL), row),
            resident((1, D_MODEL)),
            resident((D_MODEL, Z_WIDTH + N_HEADS)),
            resident((1, LANES)),
        ],
        out_specs=[
            pl.BlockSpec((tm, Z_WIDTH), row),
            pl.BlockSpec((tm, LANES), row),
            pl.BlockSpec((tm, 3 * ATT_WIDTH), row),
            pl.BlockSpec((tm, 3 * ATT_WIDTH), row),
        ],
        out_shape=[
            jax.ShapeDtypeStruct((tokens, Z_WIDTH), BF),
            jax.ShapeDtypeStruct((tokens, LANES), F32),
            dil_shape,
            dil_shape,
        ],
        scratch_shapes=[
            pltpu.VMEM((D_MODEL, Z_WIDTH + LANES), BF),
            pltpu.VMEM((tm, D_MODEL), BF),
            pltpu.VMEM((tm, LANES), F32),
            pltpu.VMEM((tm, LANES), F32),
            pltpu.VMEM((tm, LANES), F32),
            stage_buf,
            stage_buf,
        ],
        compiler_params=_cparams(("arbitrary", "arbitrary")),
        name="in_proj",
    )(x2d, g, w_raw, freq_lanes)


def _extras_base(head):
    return (HEAD_DIM if head % 2 == 0 else 0) + 2 * N_PIECES * (head // 2)


def _forget_scan_kernel(fa_ref, bias_ref, tri_ref, pq_ref, pk_ref, oq_ref, ok_ref,
                        gq_ref, gk_ref):
    chunk = tri_ref.shape[0]
    n_chunks = fa_ref.shape[0] // chunk
    t = fa_ref[...] + bias_ref[...]
    log_f = jnp.minimum(t, 0.0) - jnp.log1p(jnp.exp(-jnp.abs(t)))
    wide = jnp.concatenate(
        [log_f[c * chunk:(c + 1) * chunk, :] for c in range(n_chunks)], axis=1)
    tri = tri_ref[...]
    local = sum(_dot(tri, piece) for piece in _split3(wide))
    carry = jnp.zeros((1, LANES), F32)
    parts = []
    for c in range(n_chunks):
        part = local[:, c * LANES:(c + 1) * LANES] + carry
        parts.append(part)
        carry = part[chunk - 1:chunk, :]
    run = jnp.concatenate(parts, axis=0)
    gq = oq_ref[...].astype(F32)
    gk = ok_ref[...].astype(F32)
    for i, piece in enumerate(_split3(run * LOG2_E)):
        gq = gq + _dot(piece, pq_ref[i])
        gk = gk - _dot(piece, pk_ref[i])
    gq_ref[0] = gq.astype(BF)
    gk_ref[0] = gk.astype(BF)


def _forget_scan_constants(chunk):
    tri = np.tril(np.ones((chunk, chunk), np.float32))
    pq = np.zeros((N_PIECES, LANES, LANES), np.float32)
    pk = np.zeros((N_PIECES, LANES, LANES), np.float32)
    oq = np.zeros((1, LANES), np.float32)
    ok = np.zeros((1, LANES), np.float32)
    for h in range(N_HEADS):
        base = _extras_base(h)
        for i in range(N_PIECES):
            pq[i, h, base + i] = 1.0
            oq[0, base + N_PIECES + i] = 1.0
            ok[0, base + i] = 1.0
            pk[i, h, base + N_PIECES + i] = 1.0
    as_bf = lambda a: jnp.asarray(a, BF)
    return as_bf(tri), as_bf(pq), as_bf(pk), as_bf(oq), as_bf(ok)


def _forget_scan(fa, bias_lanes, *, batch, seq, chunk):
    tri, pq, pk, oq, ok = _forget_scan_constants(chunk)
    const2 = lambda b: (0, 0)
    const3 = lambda b: (0, 0, 0)
    return pl.pallas_call(
        _forget_scan_kernel,
        grid=(batch,),
        in_specs=[
            pl.BlockSpec((seq, LANES), lambda b: (b, 0)),
            pl.BlockSpec((1, LANES), const2),
            pl.BlockSpec((chunk, chunk), const2),
            pl.BlockSpec((N_PIECES, LANES, LANES), const3),
            pl.BlockSpec((N_PIECES, LANES, LANES), const3),
            pl.BlockSpec((1, LANES), const2),
            pl.BlockSpec((1, LANES), const2),
        ],
        out_specs=[
            pl.BlockSpec((1, seq, LANES), lambda b: (b, 0, 0)),
            pl.BlockSpec((1, seq, LANES), lambda b: (b, 0, 0)),
        ],
        out_shape=[
            jax.ShapeDtypeStruct((batch, seq, LANES), BF),
            jax.ShapeDtypeStruct((batch, seq, LANES), BF),
        ],
        compiler_params=_cparams(("arbitrary",)),
        name="forget_scan",
    )(fa, bias_lanes, tri, pq, pk, oq, ok)


def _with_bias_lanes(slab, extras, hp, parity):
    lane = lax.broadcasted_iota(jnp.int32, slab.shape, 1)
    own = (lane < HEAD_DIM) if parity == 0 else (lane >= HEAD_DIM)
    base = (HEAD_DIM if parity == 0 else 0) + 2 * N_PIECES * hp
    in_extras = (lane >= base) & (lane < base + 2 * N_PIECES)
    return jnp.where(own, slab, jnp.where(in_extras, extras, jnp.zeros_like(extras)))


def _fox_kernel(q_ref, k_ref, v_ref, gq_ref, gk_ref, o_ref,
                ka_scr, kb_scr, vta_scr, vtb_scr, qp_scr, bias_scr, s_scr, acc_scr, ot_scr,
                bound_smem, *, tq):
    hp = pl.program_id(1)
    k_scrs = (ka_scr, kb_scr)
    vt_scrs = (vta_scr, vtb_scr)
    seq = k_ref.shape[0]
    n_q = seq // tq
    lane_row = lax.broadcasted_iota(jnp.int32, (1, LANES), 1)

    sel_row = lax.broadcasted_iota(jnp.int32, (LANES, LANES), 0)
    sel_col = lax.broadcasted_iota(jnp.int32, (LANES, LANES), 1)
    head_sum = jnp.where(sel_row // HEAD_DIM == sel_col, 1.0, 0.0).astype(BF)

    def max_row_norm(slab, head):
        lane = lax.broadcasted_iota(jnp.int32, slab.shape, 1)
        own = (lane < HEAD_DIM) if head == 0 else (lane >= HEAD_DIM)
        x = jnp.where(own, slab.astype(F32), 0.0)
        sq = jnp.sum(x * x, axis=1, keepdims=True)
        return jnp.sqrt(jnp.max(sq, axis=0, keepdims=True))

    def max_row_norms(slab):
        x = slab.astype(F32)
        per_row = _dot((x * x).astype(BF), head_sum)
        return jnp.sqrt(jnp.max(per_row, axis=0, keepdims=True) * NORM_SLACK)

    def bias_lane_sum(row, head, first):
        base = (HEAD_DIM if head == 0 else 0) + 2 * N_PIECES * hp + first
        picked = (lane_row >= base) & (lane_row < base + N_PIECES)
        return jnp.sum(jnp.where(picked, row, 0.0), axis=1, keepdims=True)

    k2 = k_ref[...]
    gk = gk_ref[0]
    k_norms = max_row_norms(k2)
    for head in range(2):
        k_scrs[head][...] = _with_bias_lanes(k2, gk, hp, head)
        bound_smem[head, n_q] = k_norms[0, head]
        for j in range(n_q):
            rows16 = gk_ref[0, (j + 1) * tq - BF16_ROWS:(j + 1) * tq, :].astype(F32)
            last = rows16[BF16_ROWS - 1:BF16_ROWS, :]
            bound_smem[head, j] = -bias_lane_sum(last, head, N_PIECES)[0, 0]
    vt = v_ref[...].T
    ones = jnp.ones((BF16_ROWS, seq), BF)
    for head, vt_scr in enumerate(vt_scrs):
        vt_scr[0:HEAD_DIM, :] = vt[head * HEAD_DIM:(head + 1) * HEAD_DIM, :]
        vt_scr[HEAD_DIM:, :] = ones
    row = lax.broadcasted_iota(jnp.int32, (tq, tq), 0)
    col = lax.broadcasted_iota(jnp.int32, (tq, tq), 1)
    bias_scr[...] = jnp.where(row <= col, 0.0, NEG_INF)
    acc_scr[...] = jnp.zeros_like(acc_scr)

    def scores(head, tile, masked):
        kt = k_scrs[head][pl.ds(pl.multiple_of(tile * tq, tq), tq), :]
        st = _dot(kt, qp_scr[head])
        if masked:
            st = st + bias_scr[...]
        s_scr[head] = st
        return jnp.max(st, axis=0, keepdims=True)

    def softmax_pv(head, st, tile, tile_max, m):
        start = pl.multiple_of(tile * tq, tq)
        m_new = jnp.maximum(m, tile_max)
        alpha = jnp.exp2(m - m_new)
        pt = jnp.exp2(st - m_new).astype(BF)
        vt_tile = vt_scrs[head][:, pl.ds(start, tq)]
        acc_scr[head] = alpha * acc_scr[head] + _dot(vt_tile, pt)
        return m_new

    def start_tile(qi):
        rows = pl.ds(pl.multiple_of(qi * tq, tq), tq)
        q2 = q_ref[rows, :]
        gq = gq_ref[0, rows, :]
        first_row = gq[0:BF16_ROWS, :].astype(F32)[0:1, :]
        maxes, slack = [], []
        for head in range(2):
            qp_scr[head] = _with_bias_lanes(q2, gq, hp, head).T
            maxes.append(scores(head, qi, True))
            reach = (max_row_norm(q2, head) * bound_smem[head, n_q]
                     + bias_lane_sum(first_row, head, 0) + SCORE_MARGIN)
            lowest_max = jnp.min(maxes[head], axis=1, keepdims=True)
            slack.append(reach - (lowest_max - UNDERFLOW_BITS))
        count = jnp.zeros((1, 1), jnp.int32)
        for j in range(n_q - 1):
            needed = (((slack[0] >= bound_smem[0, j]) | (slack[1] >= bound_smem[1, j]))
                      & (j < qi))
            count = count + jnp.where(needed, 1, 0)
        return maxes[0], maxes[1], count[0, 0]

    def finish_tile(qi):
        for head in range(2):
            acc = acc_scr[head]
            ot_scr[head * HEAD_DIM:(head + 1) * HEAD_DIM, :] = (
                acc[0:HEAD_DIM, :] / acc[HEAD_DIM:HEAD_DIM + 1, :])
        o_ref[pl.ds(pl.multiple_of(qi * tq, tq), tq), :] = ot_scr[...].T.astype(BF)
        acc_scr[...] = jnp.zeros_like(acc_scr)

    m_init = jnp.full((1, tq), NEG_INF, F32)

    def query_tile(qi, carry):
        first_a, first_b, n_needed = carry

        def key_tile(t, inner):
            max_a, max_b, m_a, m_b, prev_tile = inner
            prev = [s_scr[head] for head in range(2)]
            tile = qi - 1 - t
            next_max = [scores(head, tile, False) for head in range(2)]
            m_a = softmax_pv(0, prev[0], prev_tile, max_a, m_a)
            m_b = softmax_pv(1, prev[1], prev_tile, max_b, m_b)
            return next_max[0], next_max[1], m_a, m_b, tile

        max_a, max_b, m_a, m_b, prev_tile = lax.fori_loop(
            0, n_needed, key_tile, (first_a, first_b, m_init, m_init, jnp.int32(qi)))
        last = [s_scr[head] for head in range(2)]
        softmax_pv(0, last[0], prev_tile, max_a, m_a)
        softmax_pv(1, last[1], prev_tile, max_b, m_b)
        following = start_tile(min(qi + 1, n_q - 1))
        finish_tile(qi)
        return following

    carry = start_tile(0)
    for qi in range(n_q):
        carry = query_tile(qi, carry)


def _fox_attention(z, gq, gk, *, batch, seq, tq):
    tokens = batch * seq
    col = lambda section: pl.BlockSpec(
        (seq, LANES), lambda b, hp: (b, section * N_HEAD_PAIRS + hp))
    whole = pl.BlockSpec((1, seq, LANES), lambda b, hp: (b, 0, 0))
    ext = HEAD_DIM + BF16_ROWS
    return pl.pallas_call(
        functools.partial(_fox_kernel, tq=tq),
        grid=(batch, N_HEAD_PAIRS),
        in_specs=[col(0), col(1), col(2), whole, whole],
        out_specs=pl.BlockSpec((seq, LANES), lambda b, hp: (b, hp)),
        out_shape=jax.ShapeDtypeStruct((tokens, ATT_WIDTH), BF),
        scratch_shapes=[
            pltpu.VMEM((seq, LANES), BF),
            pltpu.VMEM((seq, LANES), BF),
            pltpu.VMEM((ext, seq), BF),
            pltpu.VMEM((ext, seq), BF),
            pltpu.VMEM((2, LANES, tq), BF),
            pltpu.VMEM((tq, tq), F32),
            pltpu.VMEM((2, tq, tq), F32),
            pltpu.VMEM((2, ext, tq), F32),
            pltpu.VMEM((LANES, tq), F32),
            pltpu.SMEM((2, seq // tq + 1), F32),
        ],
        compiler_params=_cparams(("arbitrary", "arbitrary")),
        name="fox_attn",
    )(z, z, z, gq, gk)


def _log2(n):
    assert n > 0 and n & (n - 1) == 0, n
    return n.bit_length() - 1


def _dilated_kernel(q1, k1, v1, q4, k4, v4, q16, k16, v16, o_ref,
                    num1, num4, num16, den1, den4, den16, max1, max4, max16, bias_scr,
                    *, seq, blk, chunk, group, merge_rows):
    sources = ((q1, k1, v1), (q4, k4, v4), (q16, k16, v16))
    num_scrs = (num1, num4, num16)
    den_scrs = (den1, den4, den16)
    max_scrs = (max1, max4, max16)
    n_blocks = seq // blk

    low = lax.broadcasted_iota(jnp.int32, (blk, LANES), 1) < HEAD_DIM
    qrow = lax.broadcasted_iota(jnp.int32, (2 * blk, 2 * blk), 0) % blk
    kcol = lax.broadcasted_iota(jnp.int32, (2 * blk, 2 * blk), 1)
    dist = qrow + blk - kcol
    band = (dist >= 0) & (dist <= blk)
    bias_scr[0] = jnp.where(band & (kcol >= blk), 0.0, NEG_INF)
    bias_scr[1] = jnp.where(band, 0.0, NEG_INF)
    ones = jnp.ones((2 * blk, LANES), BF)

    for idx, (window, dilation) in enumerate(DIL_PATTERNS):
        assert window // dilation == blk
        q_ref, k_ref, v_ref = sources[idx]
        num_scr, den_scr, max_scr = num_scrs[idx], den_scrs[idx], max_scrs[idx]
        sub_shift = _log2(seq // dilation // blk)
        seg_rows = chunk // dilation
        seg = min(blk, seg_rows)

        def load(ref, j, l0):
            parts = []
            for s in range(blk // seg):
                l = l0 + s * seg
                p = lax.shift_right_logical(l, _log2(seg_rows))
                i = l & (seg_rows - 1)
                start = pl.multiple_of(p * chunk + j * seg_rows + i, seg)
                parts.append(ref[pl.ds(start, seg), :])
            return parts[0] if len(parts) == 1 else jnp.concatenate(parts, axis=0)

        def one_block(g):
            j = lax.shift_right_logical(g, sub_shift)
            gs = g & ((1 << sub_shift) - 1)
            l0 = gs * blk
            lp = jnp.maximum(l0 - blk, 0)
            q2 = load(q_ref, j, l0)
            zero = jnp.zeros_like(q2)
            qq = jnp.concatenate([jnp.where(low, q2, zero), jnp.where(low, zero, q2)], axis=0)
            kwin = jnp.concatenate([load(k_ref, j, lp), load(k_ref, j, l0)], axis=0)
            vwin = jnp.concatenate([load(v_ref, j, lp), load(v_ref, j, l0)], axis=0)
            s = _dot_nt(qq, kwin) + bias_scr[jnp.minimum(gs, 1)]
            m = jnp.max(s, axis=1, keepdims=True)
            p = jnp.exp2(s - m).astype(BF)
            pv = _dot(p, jnp.concatenate([vwin, ones], axis=1))
            if dilation == 1:
                dst = pl.ds(pl.multiple_of(l0, blk), blk)
            else:
                dst = pl.ds(l0 * dilation + j, blk, stride=dilation)
            num_scr[dst, :] = jnp.where(low, pv[:blk, :LANES], pv[blk:, :LANES])
            den_scr[dst, :] = jnp.where(low, pv[:blk, LANES:], pv[blk:, LANES:])
            max_scr[dst, :] = jnp.where(low, jnp.broadcast_to(m[:blk], (blk, LANES)),
                                        jnp.broadcast_to(m[blk:], (blk, LANES)))

        def blocks(it, carry):
            for u in range(group):
                one_block(it * group + u)
            return carry

        lax.fori_loop(0, n_blocks // group, blocks, 0)

    def merge(c, carry):
        rows = pl.ds(pl.multiple_of(c * merge_rows, merge_rows), merge_rows)
        maxes = [max_scr[rows, :] for max_scr in max_scrs]
        top = functools.reduce(jnp.maximum, maxes)
        weights = [jnp.exp2(m - top) for m in maxes]
        num = sum(w * num_scr[rows, :] for w, num_scr in zip(weights, num_scrs))
        den = sum(w * den_scr[rows, :] for w, den_scr in zip(weights, den_scrs))
        o_ref[rows, :] = (num / den).astype(BF)
        return carry

    lax.fori_loop(0, seq // merge_rows, merge, 0)


def _dilated_attention(z, zd4, zd16, *, batch, seq):
    blk = DIL_PATTERNS[0][0] // DIL_PATTERNS[0][1]
    tokens = batch * seq
    per_section = ATT_WIDTH // LANES
    col = lambda section: pl.BlockSpec(
        (seq, LANES), lambda b, hp: (b, section * per_section + hp))
    f32_buf = pltpu.VMEM((seq, LANES), F32)
    return pl.pallas_call(
        functools.partial(_dilated_kernel, seq=seq, blk=blk, chunk=DIL_CHUNK, group=32,
                          merge_rows=256),
        grid=(batch, N_HEAD_PAIRS),
        in_specs=[col(DIL_SECTION), col(DIL_SECTION + 1), col(DIL_SECTION + 2),
                  col(0), col(1), col(2), col(0), col(1), col(2)],
        out_specs=pl.BlockSpec((seq, LANES), lambda b, hp: (b, hp)),
        out_shape=jax.ShapeDtypeStruct((tokens, ATT_WIDTH), BF),
        scratch_shapes=[f32_buf] * 9 + [pltpu.VMEM((2, 2 * blk, 2 * blk), F32)],
        compiler_params=_cparams(("arbitrary", "arbitrary")),
        name="dilated_attn",
    )(z, z, z, zd4, zd4, zd4, zd16, zd16, zd16)


def _mix_kernel(ya_ref, yb_ref, ga_ref, gb_ref, x_ref, woa_ref, wob_ref, wout_ref,
                g_ref, o_ref):
    pa = _dot(ya_ref[...], woa_ref[...])
    pb = _dot(yb_ref[...], wob_ref[...])
    mixed = (jax.nn.sigmoid(ga_ref[...].astype(F32)) * pa
             + jax.nn.sigmoid(gb_ref[...].astype(F32)) * pb)
    y = _dot(mixed.astype(BF), wout_ref[...])
    o_ref[...] = x_ref[...] + _rms(y, g_ref[...])


def _mix(ya, yb, z, x2d, woa, wob, wout, g, *, tm):
    tokens = x2d.shape[0]
    gate_blk = lambda off: pl.BlockSpec((tm, D_MODEL), lambda i: (i, off))
    const = lambda i: (0, 0)
    return pl.pallas_call(
        _mix_kernel,
        grid=(tokens // tm,),
        in_specs=[
            pl.BlockSpec((tm, ATT_WIDTH), lambda i: (i, 0)),
            pl.BlockSpec((tm, ATT_WIDTH), lambda i: (i, 0)),
            gate_blk(3),
            gate_blk(4),
            pl.BlockSpec((tm, D_MODEL), lambda i: (i, 0)),
            pl.BlockSpec((ATT_WIDTH, D_MODEL), const),
            pl.BlockSpec((ATT_WIDTH, D_MODEL), const),
            pl.BlockSpec((D_MODEL, D_MODEL), const),
            pl.BlockSpec((1, D_MODEL), const),
        ],
        out_specs=pl.BlockSpec((tm, D_MODEL), lambda i: (i, 0)),
        out_shape=jax.ShapeDtypeStruct((tokens, D_MODEL), F32),
        compiler_params=_cparams(("arbitrary",)),
        name="mix",
    )(ya, yb, z, z, x2d, woa, wob, wout, g)


def _ffn_kernel(x_ref, halo_ref, gpre_ref, wup_ref, cw_ref, cb_ref, wd_ref, gpost_ref, o_ref,
                h_scr, ua0_scr, ub0_scr, ua1_scr, ub1_scr, acc_scr, rows_scr,
                *, tm, tf, tiles_per_seq, n_chunks):
    i = pl.program_id(0)
    halo = BF16_ROWS
    n_rows = tm + halo
    n_groups = n_rows // SUBLANES
    chunk = n_rows // n_chunks
    assert n_groups * SUBLANES == n_rows and chunk * n_chunks == n_rows
    assert chunk % BF16_ROWS == 0 and n_groups % 2 == 0
    n_tiles = D_FF // tf
    u_sets = ((ua0_scr, ub0_scr), (ua1_scr, ub1_scr))

    def group_rows(g):
        return pl.ds(g, SUBLANES, stride=n_groups)

    def cols(f, gate):
        return pl.ds(pl.multiple_of(f * tf + (D_FF if gate else 0), LANES), tf)

    def up_proj(f, dst, lo, hi):
        h = h_scr[lo:hi, :]
        dst[0][lo:hi, :] = _dot(h, wup_ref[:, cols(f, False)])
        dst[1][lo:hi, :] = _dot(h, wup_ref[:, cols(f, True)])

    def tap_rows(u_scr, back, first, n):
        lo = first - back * SUBLANES
        if lo >= 0:
            return u_scr[lo:lo + n, :]
        wrapped = [pltpu.roll(u_scr[n_rows + k * SUBLANES:n_rows + (k + 1) * SUBLANES, :], 1, 0)
                   for k in range(lo // SUBLANES, 0)]
        return jnp.concatenate(wrapped + [u_scr[0:n + lo, :]], axis=0)

    def conv(u_scr, window, post_scale, first, n):
        out = cb_ref[:, window] * post_scale
        for tap in range(CONV_WIDTH):
            back = CONV_WIDTH - 1 - tap
            out = out + (cw_ref[tap:tap + 1, window] * post_scale) * tap_rows(u_scr, back, first, n)
        return out

    def down_proj(f, src, first, n):
        a = conv(src[0], cols(f, False), 1.0, first, n)
        half_b = conv(src[1], cols(f, True), 0.5, first, n)
        c0 = math.sqrt(2.0 / math.pi)
        inner = a * (c0 + (c0 * 0.044715) * (a * a))
        hidden = ((a * half_b) * (1.0 + jnp.tanh(inner))).astype(BF)
        return _dot(hidden, wd_ref[pl.ds(pl.multiple_of(f * tf, tf), tf), :])

    def stage(f, parity):
        src, dst = u_sets[1 - parity], u_sets[parity]
        for c in range(n_chunks):
            up_proj(f, dst, c * chunk, (c + 1) * chunk)
            out_rows = slice(c * chunk, (c + 1) * chunk)
            acc_scr[out_rows, :] += down_proj(f - 1, src, c * chunk, chunk)

    n_slabs = D_MODEL // LANES
    g = gpre_ref[...]
    hh = _rms(halo_ref[...], g)
    hh = jnp.where(i % tiles_per_seq == 0, jnp.zeros_like(hh), hh)
    hx = _rms(x_ref[...], g)
    for s in range(n_slabs):
        rows_scr[s, 0:halo, :] = hh[:, s * LANES:(s + 1) * LANES]
        rows_scr[s, halo:, :] = hx[:, s * LANES:(s + 1) * LANES]
    for gg in range(0, n_groups, 2):
        pair = jnp.concatenate(
            [jnp.concatenate([rows_scr[s, group_rows(gg + d), :] for s in range(n_slabs)], axis=1)
             for d in range(2)], axis=0)
        h_scr[gg * SUBLANES:(gg + 2) * SUBLANES, :] = pair.astype(BF)
    acc_scr[...] = jnp.zeros_like(acc_scr)
    up_proj(0, u_sets[0], 0, n_rows)

    def pair_of_tiles(k, carry):
        stage(2 * k + 1, 1)
        stage(2 * k + 2, 0)
        return carry

    assert n_tiles % 2 == 1
    for k in range((n_tiles - 1) // 2):
        pair_of_tiles(k, 0)
    gpost = gpost_ref[...]
    for c in range(n_chunks):
        y = acc_scr[c * chunk:(c + 1) * chunk, :] + down_proj(n_tiles - 1, u_sets[0], c * chunk, chunk)
        normed = _rms(y, gpost)
        for gg in range(chunk // SUBLANES):
            g_abs = c * (chunk // SUBLANES) + gg
            for s in range(n_slabs):
                rows_scr[s, group_rows(g_abs), :] = normed[gg * SUBLANES:(gg + 1) * SUBLANES,
                                                           s * LANES:(s + 1) * LANES]
    for s in range(n_slabs):
        lanes = slice(s * LANES, (s + 1) * LANES)
        o_ref[:, lanes] = x_ref[:, lanes] + rows_scr[s, halo:, :]


def _ffn(x1, g_pre, w_up, conv_w, conv_b, w_down, g_post, *, seq, tm, tf):
    tokens = x1.shape[0]
    halo = BF16_ROWS
    per = tm // halo
    resident = lambda shape: pl.BlockSpec(shape, lambda i: (0, 0),
                                          pipeline_mode=pl.Buffered(1))
    u_buf = pltpu.VMEM((tm + halo, tf), F32)
    return pl.pallas_call(
        functools.partial(_ffn_kernel, tm=tm, tf=tf, tiles_per_seq=seq // tm, n_chunks=5),
        grid=(tokens // tm,),
        in_specs=[
            pl.BlockSpec((tm, D_MODEL), lambda i: (i, 0)),
            pl.BlockSpec((halo, D_MODEL), lambda i: (jnp.maximum(i * per - 1, 0), 0)),
            resident((1, D_MODEL)),
            resident((D_MODEL, 2 * D_FF)),
            resident((CONV_WIDTH, 2 * D_FF)),
            resident((1, 2 * D_FF)),
            resident((D_FF, D_MODEL)),
            resident((1, D_MODEL)),
        ],
        out_specs=pl.BlockSpec((tm, D_MODEL), lambda i: (i, 0)),
        out_shape=jax.ShapeDtypeStruct((tokens, D_MODEL), F32),
        scratch_shapes=[
            pltpu.VMEM((tm + halo, D_MODEL), BF),
            u_buf, u_buf, u_buf, u_buf,
            pltpu.VMEM((tm + halo, D_MODEL), F32),
            pltpu.VMEM((D_MODEL // LANES, tm + halo, LANES), F32),
        ],
        compiler_params=_cparams(("arbitrary",)),
        name="ffn",
    )(x1, x1, g_pre, w_up, conv_w, conv_b, w_down, g_post)


def _rope_freq_lanes():
    inv_freq = ROPE_THETA ** (-jnp.arange(ROPE_HALF, dtype=F32) * 2.0 / ROPE_DIM)
    return jnp.tile(inv_freq, LANES // ROPE_HALF).reshape(1, LANES)


def kernel(x, g_pre_mix, w_in, b_forget, w_o_fox, w_o_dil, w_out, g_post_mix,
           g_pre_ffn, w_up, conv_w, conv_b, w_down, g_post_ffn):
    batch, seq, d_model = x.shape
    assert d_model == D_MODEL and seq % 1024 == 0
    depth = w_in.shape[0]
    freq_lanes = _rope_freq_lanes()
    x2d = x.reshape(batch * seq, D_MODEL)
    row = lambda v: v.reshape(1, -1)
    for l in range(depth):
        bias_lanes = jnp.pad(b_forget[l], (0, LANES - N_HEADS)).reshape(1, LANES)

        z, fa, zd4, zd16 = _in_proj(x2d, row(g_pre_mix[l]), w_in[l].astype(BF), freq_lanes,
                                    batch=batch, seq=seq, tm=DIL_CHUNK)
        gq, gk = _forget_scan(fa, bias_lanes, batch=batch, seq=seq, chunk=512)
        ya = _fox_attention(z, gq, gk, batch=batch, seq=seq, tq=512)
        yb = _dilated_attention(z, zd4, zd16, batch=batch, seq=seq)
        x2d = _mix(ya, yb, z, x2d, w_o_fox[l].astype(BF), w_o_dil[l].astype(BF),
                   w_out[l].astype(BF), row(g_post_mix[l]), tm=1024)
        x2d = _ffn(x2d, row(g_pre_ffn[l]), w_up[l].astype(BF), conv_w[l],
                   row(conv_b[l]), w_down[l].astype(BF), row(g_post_ffn[l]),
                   seq=seq, tm=1024, tf=256)
    return x2d.reshape(batch, seq, D_MODEL)
```

```python
import functools
import math

import numpy as np
import jax
import jax.numpy as jnp
from jax import lax
from jax.experimental import pallas as pl
from jax.experimental.pallas import tpu as pltpu

D_MODEL = 1024
HEAD_DIM = 64
N_HEADS = 8
ATT_WIDTH = N_HEADS * HEAD_DIM
N_HEAD_PAIRS = N_HEADS // 2
DIL_PATTERNS = ((128, 1), (512, 4), (2048, 16))
ROPE_DIM = HEAD_DIM // 4
ROPE_HALF = ROPE_DIM // 2
ROPE_THETA = 500000.0
D_FF = 2816
CONV_WIDTH = 3
RMS_EPS = 1e-6
NEG_INF = -1e30
Q_SCALE = 1.0 / math.sqrt(HEAD_DIM)
LOG2_E = math.log2(math.e)
UNDERFLOW_BITS = 150.0
SCORE_MARGIN = 1.0
NORM_SLACK = 1.0 + 2.0 ** -7

LANES = 128
SUBLANES = 8
BF16_ROWS = 16
Z_WIDTH = 3 * ATT_WIDTH * 2 + 2 * D_MODEL
Z_BLK = 512
DIL_SECTION = 3
DIL_CHUNK = 512
N_PIECES = 3
VMEM_LIMIT = 56 * 1024 * 1024

SCAN_CHUNK = 512
FOX_TILE = 512
DIL_MERGE_ROWS = 256
MIX_TOKENS = 1024
FFN_TOKENS = 1024
FFN_HIDDEN_TILE = 256
FFN_ROW_CHUNKS = 5

BF = jnp.bfloat16
F32 = jnp.float32


def _cparams(sem, flags=None):
    return pltpu.CompilerParams(dimension_semantics=sem, vmem_limit_bytes=VMEM_LIMIT,
                                flags=flags)


def _rms(xf, g):
    inv = lax.rsqrt(jnp.mean(xf * xf, axis=-1, keepdims=True) + RMS_EPS)
    return xf * inv * g


def _split3(x):
    hi = x.astype(BF)
    r1 = x - hi.astype(F32)
    mid = r1.astype(BF)
    lo = (r1 - mid.astype(F32)).astype(BF)
    return hi, mid, lo


def _dot(a, b):
    return jnp.dot(a, b, preferred_element_type=F32)


def _dot_nt(a, b):
    return lax.dot_general(a, b, (((1,), (1,)), ((), ())), preferred_element_type=F32)


def _in_proj_kernel(x_ref, g_ref, w_raw_ref, freq_ref, z_ref, fa_ref, zd4_ref, zd16_ref,
                    w_ref, h_scr, cos_scr, sneg_scr, spos_scr, stage_scr, stage4_scr, *, tm):
    p = pl.program_id(0)
    b = pl.program_id(1)
    zd_refs = {4: zd4_ref, 16: zd16_ref}
    n_slabs = Z_BLK // LANES

    @pl.when((p == 0) & (b == 0))
    def _regroup_weight():
        fox_end = 3 * ATT_WIDTH
        raw_width = w_raw_ref.shape[1]
        first = fox_end // LANES

        def group(i):
            lo = i * LANES
            return w_raw_ref[:, lo:min(lo + LANES, raw_width)].astype(F32)

        w_ref[:, 0:fox_end] = w_raw_ref[:, 0:fox_end]
        for k in range((Z_WIDTH - fox_end) // LANES):
            shifted = jnp.concatenate(
                [group(first + k)[:, N_HEADS:], group(first + k + 1)[:, :N_HEADS]], axis=1)
            w_ref[:, fox_end + k * LANES:fox_end + (k + 1) * LANES] = shifted.astype(BF)
        w_ref[:, Z_WIDTH:Z_WIDTH + LANES] = jnp.concatenate(
            [group(first)[:, :N_HEADS], jnp.zeros((D_MODEL, LANES - N_HEADS), F32)],
            axis=1).astype(BF)

    @pl.when(b == 0)
    def _tables():
        pos = (p * tm + lax.broadcasted_iota(jnp.int32, (tm, LANES), 0)).astype(F32)
        lane = lax.broadcasted_iota(jnp.int32, (tm, LANES), 1)
        c = lane % HEAD_DIM
        ang = pos * freq_ref[...]
        cs = jnp.cos(ang)
        sn = jnp.sin(ang)
        cos_scr[...] = jnp.where(c < ROPE_DIM, cs, 1.0)
        sneg_scr[...] = jnp.where(c < ROPE_HALF, -sn, 0.0)
        spos_scr[...] = jnp.where((c >= ROPE_HALF) & (c < ROPE_DIM), sn, 0.0)

    def rope(t):
        up = pltpu.roll(t, LANES - ROPE_HALF, 1)
        dn = pltpu.roll(t, ROPE_HALF, 1)
        return t * cos_scr[...] + up * sneg_scr[...] + dn * spos_scr[...]

    def emit_dilated(section, slab, val):
        out_lanes = slice(section * Z_BLK + slab * LANES, section * Z_BLK + (slab + 1) * LANES)
        z_ref[:, (DIL_SECTION + section) * Z_BLK + slab * LANES:
              (DIL_SECTION + section) * Z_BLK + (slab + 1) * LANES] = val.astype(BF)
        stage, stage4 = stage_scr.at[section, slab], stage4_scr.at[section, slab]
        stage[...] = val
        seg4 = tm // 4
        for c in range(4):
            part = stage[pl.ds(c, seg4, stride=4), :]
            stage4[c * seg4:(c + 1) * seg4, :] = part
            zd_refs[4][c * seg4:(c + 1) * seg4, out_lanes] = part.astype(BF)
        seg16 = tm // 16
        for jj in range(16):
            c, a = jj % 4, jj // 4
            part = stage4[pl.ds(c * seg4 + a, seg16, stride=4), :]
            zd_refs[16][jj * seg16:(jj + 1) * seg16, out_lanes] = part.astype(BF)

    h = _rms(x_ref[...], g_ref[...]).astype(BF)
    h_scr[...] = h
    fa_ref[...] = _dot(h, w_ref[:, Z_WIDTH:Z_WIDTH + LANES])

    for j in range(Z_WIDTH // Z_BLK):
        cols = slice(j * Z_BLK, (j + 1) * Z_BLK)
        acc = _dot(h_scr[...], w_ref[:, cols])
        section = j - DIL_SECTION
        if j == 0:
            z_ref[:, cols] = (acc * (Q_SCALE * LOG2_E)).astype(BF)
        elif 0 <= section < 3:
            for slab in range(n_slabs):
                t = acc[:, slab * LANES:(slab + 1) * LANES]
                if section == 0:
                    t = rope(t) * (Q_SCALE * LOG2_E)
                elif section == 1:
                    t = rope(t)
                emit_dilated(section, slab, t)
        else:
            z_ref[:, cols] = acc.astype(BF)


def _in_proj(x2d, g, w_raw, freq_lanes, *, batch, seq, tm):
    n_p = seq // tm
    tokens = batch * seq
    row = lambda p, b: (b * n_p + p, 0)
    resident = lambda shape: pl.BlockSpec(shape, lambda p, b: (0, 0),
                                          pipeline_mode=pl.Buffered(1))
    dil_shape = jax.ShapeDtypeStruct((tokens, 3 * ATT_WIDTH), BF)
    stage_buf = pltpu.VMEM((3, Z_BLK // LANES, tm, LANES), F32)
    return pl.pallas_call(
        functools.partial(_in_proj_kernel, tm=tm),
        grid=(n_p, batch),
        in_specs=[
            pl.BlockSpec((tm, D_MODEL), row),
            resident((1, D_MODEL)),
            resident((D_MODEL, Z_WIDTH + N_HEADS)),
            resident((1, LANES)),
        ],
        out_specs=[
            pl.BlockSpec((tm, Z_WIDTH), row),
            pl.BlockSpec((tm, LANES), row),
            pl.BlockSpec((tm, 3 * ATT_WIDTH), row),
            pl.BlockSpec((tm, 3 * ATT_WIDTH), row),
        ],
        out_shape=[
            jax.ShapeDtypeStruct((tokens, Z_WIDTH), BF),
            jax.ShapeDtypeStruct((tokens, LANES), F32),
            dil_shape,
            dil_shape,
        ],
        scratch_shapes=[
            pltpu.VMEM((D_MODEL, Z_WIDTH + LANES), BF),
            pltpu.VMEM((tm, D_MODEL), BF),
            pltpu.VMEM((tm, LANES), F32),
            pltpu.VMEM((tm, LANES), F32),
            pltpu.VMEM((tm, LANES), F32),
            stage_buf,
            stage_buf,
        ],
        compiler_params=_cparams(("arbitrary", "arbitrary")),
        name="in_proj",
    )(x2d, g, w_raw, freq_lanes)


def _extras_base(head):
    return (HEAD_DIM if head % 2 == 0 else 0) + 2 * N_PIECES * (head // 2)


def _forget_scan_kernel(fa_ref, bias_ref, tri_ref, pq_ref, pk_ref, oq_ref, ok_ref,
                        gq_ref, gk_ref):
    chunk = tri_ref.shape[0]
    n_chunks = fa_ref.shape[0] // chunk
    t = fa_ref[...] + bias_ref[...]
    log_f = jnp.minimum(t, 0.0) - jnp.log1p(jnp.exp(-jnp.abs(t)))
    wide = jnp.concatenate(
        [log_f[c * chunk:(c + 1) * chunk, :] for c in range(n_chunks)], axis=1)
    tri = tri_ref[...]
    local = sum(_dot(tri, piece) for piece in _split3(wide))
    carry = jnp.zeros((1, LANES), F32)
    parts = []
    for c in range(n_chunks):
        part = local[:, c * LANES:(c + 1) * LANES] + carry
        parts.append(part)
        carry = part[chunk - 1:chunk, :]
    run = jnp.concatenate(parts, axis=0)
    gq = oq_ref[...].astype(F32)
    gk = ok_ref[...].astype(F32)
    for i, piece in enumerate(_split3(run * LOG2_E)):
        gq = gq + _dot(piece, pq_ref[i])
        gk = gk - _dot(piece, pk_ref[i])
    gq_ref[0] = gq.astype(BF)
    gk_ref[0] = gk.astype(BF)


def _forget_scan_constants(chunk):
    tri = np.tril(np.ones((chunk, chunk), np.float32))
    pq = np.zeros((N_PIECES, LANES, LANES), np.float32)
    pk = np.zeros((N_PIECES, LANES, LANES), np.float32)
    oq = np.zeros((1, LANES), np.float32)
    ok = np.zeros((1, LANES), np.float32)
    for h in range(N_HEADS):
        base = _extras_base(h)
        for i in range(N_PIECES):
            pq[i, h, base + i] = 1.0
            oq[0, base + N_PIECES + i] = 1.0
            ok[0, base + i] = 1.0
            pk[i, h, base + N_PIECES + i] = 1.0
    as_bf = lambda a: jnp.asarray(a, BF)
    return as_bf(tri), as_bf(pq), as_bf(pk), as_bf(oq), as_bf(ok)


def _forget_scan(fa, bias_lanes, *, batch, seq, chunk):
    tri, pq, pk, oq, ok = _forget_scan_constants(chunk)
    const2 = lambda b: (0, 0)
    const3 = lambda b: (0, 0, 0)
    return pl.pallas_call(
        _forget_scan_kernel,
        grid=(batch,),
        in_specs=[
            pl.BlockSpec((seq, LANES), lambda b: (b, 0)),
            pl.BlockSpec((1, LANES), const2),
            pl.BlockSpec((chunk, chunk), const2),
            pl.BlockSpec((N_PIECES, LANES, LANES), const3),
            pl.BlockSpec((N_PIECES, LANES, LANES), const3),
            pl.BlockSpec((1, LANES), const2),
            pl.BlockSpec((1, LANES), const2),
        ],
        out_specs=[
            pl.BlockSpec((1, seq, LANES), lambda b: (b, 0, 0)),
            pl.BlockSpec((1, seq, LANES), lambda b: (b, 0, 0)),
        ],
        out_shape=[
            jax.ShapeDtypeStruct((batch, seq, LANES), BF),
            jax.ShapeDtypeStruct((batch, seq, LANES), BF),
        ],
        compiler_params=_cparams(("arbitrary",)),
        name="forget_scan",
    )(fa, bias_lanes, tri, pq, pk, oq, ok)


def _with_bias_lanes(slab, extras, hp, parity):
    lane = lax.broadcasted_iota(jnp.int32, slab.shape, 1)
    own = (lane < HEAD_DIM) if parity == 0 else (lane >= HEAD_DIM)
    base = (HEAD_DIM if parity == 0 else 0) + 2 * N_PIECES * hp
    in_extras = (lane >= base) & (lane < base + 2 * N_PIECES)
    return jnp.where(own, slab, jnp.where(in_extras, extras, jnp.zeros_like(extras)))


def _fox_kernel(q_ref, k_ref, v_ref, gq_ref, gk_ref, o_ref,
                ka_scr, kb_scr, vta_scr, vtb_scr, qp_scr, bias_scr, s_scr, acc_scr, ot_scr,
                bound_smem, *, tq):
    hp = pl.program_id(1)
    k_scrs = (ka_scr, kb_scr)
    vt_scrs = (vta_scr, vtb_scr)
    seq = k_ref.shape[0]
    n_q = seq // tq
    lane_row = lax.broadcasted_iota(jnp.int32, (1, LANES), 1)

    sel_row = lax.broadcasted_iota(jnp.int32, (LANES, LANES), 0)
    sel_col = lax.broadcasted_iota(jnp.int32, (LANES, LANES), 1)
    head_sum = jnp.where(sel_row // HEAD_DIM == sel_col, 1.0, 0.0).astype(BF)

    def max_row_norm(slab, head):
        lane = lax.broadcasted_iota(jnp.int32, slab.shape, 1)
        own = (lane < HEAD_DIM) if head == 0 else (lane >= HEAD_DIM)
        x = jnp.where(own, slab.astype(F32), 0.0)
        sq = jnp.sum(x * x, axis=1, keepdims=True)
        return jnp.sqrt(jnp.max(sq, axis=0, keepdims=True))

    def max_row_norms(slab):
        x = slab.astype(F32)
        per_row = _dot((x * x).astype(BF), head_sum)
        return jnp.sqrt(jnp.max(per_row, axis=0, keepdims=True) * NORM_SLACK)

    def bias_lane_sum(row, head, first):
        base = (HEAD_DIM if head == 0 else 0) + 2 * N_PIECES * hp + first
        picked = (lane_row >= base) & (lane_row < base + N_PIECES)
        return jnp.sum(jnp.where(picked, row, 0.0), axis=1, keepdims=True)

    k2 = k_ref[...]
    gk = gk_ref[0]
    k_norms = max_row_norms(k2)
    for head in range(2):
        k_scrs[head][...] = _with_bias_lanes(k2, gk, hp, head)
        bound_smem[head, n_q] = k_norms[0, head]
        for j in range(n_q):
            rows16 = gk_ref[0, (j + 1) * tq - BF16_ROWS:(j + 1) * tq, :].astype(F32)
            last = rows16[BF16_ROWS - 1:BF16_ROWS, :]
            bound_smem[head, j] = -bias_lane_sum(last, head, N_PIECES)[0, 0]
    vt = v_ref[...].T
    ones = jnp.ones((BF16_ROWS, seq), BF)
    for head, vt_scr in enumerate(vt_scrs):
        vt_scr[0:HEAD_DIM, :] = vt[head * HEAD_DIM:(head + 1) * HEAD_DIM, :]
        vt_scr[HEAD_DIM:, :] = ones
    row = lax.broadcasted_iota(jnp.int32, (tq, tq), 0)
    col = lax.broadcasted_iota(jnp.int32, (tq, tq), 1)
    bias_scr[...] = jnp.where(row <= col, 0.0, NEG_INF)
    acc_scr[...] = jnp.zeros_like(acc_scr)

    def scores(head, tile, masked):
        kt = k_scrs[head][pl.ds(pl.multiple_of(tile * tq, tq), tq), :]
        st = _dot(kt, qp_scr[head])
        if masked:
            st = st + bias_scr[...]
        s_scr[head] = st
        return jnp.max(st, axis=0, keepdims=True)

    def softmax_pv(head, st, tile, tile_max, m):
        start = pl.multiple_of(tile * tq, tq)
        m_new = jnp.maximum(m, tile_max)
        alpha = jnp.exp2(m - m_new)
        pt = jnp.exp2(st - m_new).astype(BF)
        vt_tile = vt_scrs[head][:, pl.ds(start, tq)]
        acc_scr[head] = alpha * acc_scr[head] + _dot(vt_tile, pt)
        return m_new

    def start_tile(qi):
        rows = pl.ds(pl.multiple_of(qi * tq, tq), tq)
        q2 = q_ref[rows, :]
        gq = gq_ref[0, rows, :]
        first_row = gq[0:BF16_ROWS, :].astype(F32)[0:1, :]
        maxes, slack = [], []
        for head in range(2):
            qp_scr[head] = _with_bias_lanes(q2, gq, hp, head).T
            maxes.append(scores(head, qi, True))
            reach = (max_row_norm(q2, head) * bound_smem[head, n_q]
                     + bias_lane_sum(first_row, head, 0) + SCORE_MARGIN)
            lowest_max = jnp.min(maxes[head], axis=1, keepdims=True)
            slack.append(reach - (lowest_max - UNDERFLOW_BITS))
        count = jnp.zeros((1, 1), jnp.int32)
        for j in range(n_q - 1):
            needed = (((slack[0] >= bound_smem[0, j]) | (slack[1] >= bound_smem[1, j]))
                      & (j < qi))
            count = count + jnp.where(needed, 1, 0)
        return maxes[0], maxes[1], count[0, 0]

    def finish_tile(qi):
        for head in range(2):
            acc = acc_scr[head]
            ot_scr[head * HEAD_DIM:(head + 1) * HEAD_DIM, :] = (
                acc[0:HEAD_DIM, :] / acc[HEAD_DIM:HEAD_DIM + 1, :])
        o_ref[pl.ds(pl.multiple_of(qi * tq, tq), tq), :] = ot_scr[...].T.astype(BF)
        acc_scr[...] = jnp.zeros_like(acc_scr)

    m_init = jnp.full((1, tq), NEG_INF, F32)

    def query_tile(qi, carry):
        first_a, first_b, n_needed = carry

        def key_tile(t, inner):
            max_a, max_b, m_a, m_b, prev_tile = inner
            prev = [s_scr[head] for head in range(2)]
            tile = qi - 1 - t
            next_max = [scores(head, tile, False) for head in range(2)]
            m_a = softmax_pv(0, prev[0], prev_tile, max_a, m_a)
            m_b = softmax_pv(1, prev[1], prev_tile, max_b, m_b)
            return next_max[0], next_max[1], m_a, m_b, tile

        max_a, max_b, m_a, m_b, prev_tile = lax.fori_loop(
            0, n_needed, key_tile, (first_a, first_b, m_init, m_init, qi))
        last = [s_scr[head] for head in range(2)]
        softmax_pv(0, last[0], prev_tile, max_a, m_a)
        softmax_pv(1, last[1], prev_tile, max_b, m_b)
        following = start_tile(jnp.minimum(qi + 1, n_q - 1))
        finish_tile(qi)
        return following

    lax.fori_loop(0, n_q, query_tile, start_tile(0))


def _fox_attention(z, gq, gk, *, batch, seq, tq):
    tokens = batch * seq
    col = lambda section: pl.BlockSpec(
        (seq, LANES), lambda b, hp: (b, section * N_HEAD_PAIRS + hp))
    whole = pl.BlockSpec((1, seq, LANES), lambda b, hp: (b, 0, 0))
    ext = HEAD_DIM + BF16_ROWS
    return pl.pallas_call(
        functools.partial(_fox_kernel, tq=tq),
        grid=(batch, N_HEAD_PAIRS),
        in_specs=[col(0), col(1), col(2), whole, whole],
        out_specs=pl.BlockSpec((seq, LANES), lambda b, hp: (b, hp)),
        out_shape=jax.ShapeDtypeStruct((tokens, ATT_WIDTH), BF),
        scratch_shapes=[
            pltpu.VMEM((seq, LANES), BF),
            pltpu.VMEM((seq, LANES), BF),
            pltpu.VMEM((ext, seq), BF),
            pltpu.VMEM((ext, seq), BF),
            pltpu.VMEM((2, LANES, tq), BF),
            pltpu.VMEM((tq, tq), F32),
            pltpu.VMEM((2, tq, tq), F32),
            pltpu.VMEM((2, ext, tq), F32),
            pltpu.VMEM((LANES, tq), F32),
            pltpu.SMEM((2, seq // tq + 1), F32),
        ],
        compiler_params=_cparams(("arbitrary", "arbitrary")),
        name="fox_attn",
    )(z, z, z, gq, gk)


def _log2(n):
    assert n > 0 and n & (n - 1) == 0, n
    return n.bit_length() - 1


def _dilated_kernel(q1, k1, v1, q4, k4, v4, q16, k16, v16, o_ref,
                    num1, num4, num16, den1, den4, den16, max1, max4, max16, bias_scr,
                    *, seq, blk, chunk, group, merge_rows):
    sources = ((q1, k1, v1), (q4, k4, v4), (q16, k16, v16))
    num_scrs = (num1, num4, num16)
    den_scrs = (den1, den4, den16)
    max_scrs = (max1, max4, max16)
    n_blocks = seq // blk

    low = lax.broadcasted_iota(jnp.int32, (blk, LANES), 1) < HEAD_DIM
    qrow = lax.broadcasted_iota(jnp.int32, (2 * blk, 2 * blk), 0) % blk
    kcol = lax.broadcasted_iota(jnp.int32, (2 * blk, 2 * blk), 1)
    dist = qrow + blk - kcol
    band = (dist >= 0) & (dist <= blk)
    bias_scr[0] = jnp.where(band & (kcol >= blk), 0.0, NEG_INF)
    bias_scr[1] = jnp.where(band, 0.0, NEG_INF)
    ones = jnp.ones((2 * blk, LANES), BF)

    for idx, (window, dilation) in enumerate(DIL_PATTERNS):
        assert window // dilation == blk
        q_ref, k_ref, v_ref = sources[idx]
        num_scr, den_scr, max_scr = num_scrs[idx], den_scrs[idx], max_scrs[idx]
        sub_shift = _log2(seq // dilation // blk)
        seg_rows = chunk // dilation
        seg = min(blk, seg_rows)

        def load(ref, j, l0):
            parts = []
            for s in range(blk // seg):
                l = l0 + s * seg
                p = lax.shift_right_logical(l, _log2(seg_rows))
                i = l & (seg_rows - 1)
                start = pl.multiple_of(p * chunk + j * seg_rows + i, seg)
                parts.append(ref[pl.ds(start, seg), :])
            return parts[0] if len(parts) == 1 else jnp.concatenate(parts, axis=0)

        def one_block(g):
            j = lax.shift_right_logical(g, sub_shift)
            gs = g & ((1 << sub_shift) - 1)
            l0 = gs * blk
            lp = jnp.maximum(l0 - blk, 0)
            q2 = load(q_ref, j, l0)
            zero = jnp.zeros_like(q2)
            qq = jnp.concatenate([jnp.where(low, q2, zero), jnp.where(low, zero, q2)], axis=0)
            kwin = jnp.concatenate([load(k_ref, j, lp), load(k_ref, j, l0)], axis=0)
            vwin = jnp.concatenate([load(v_ref, j, lp), load(v_ref, j, l0)], axis=0)
            s = _dot_nt(qq, kwin) + bias_scr[jnp.minimum(gs, 1)]
            m = jnp.max(s, axis=1, keepdims=True)
            p = jnp.exp2(s - m).astype(BF)
            pv = _dot(p, jnp.concatenate([vwin, ones], axis=1))
            if dilation == 1:
                dst = pl.ds(pl.multiple_of(l0, blk), blk)
            else:
                dst = pl.ds(l0 * dilation + j, blk, stride=dilation)
            num_scr[dst, :] = jnp.where(low, pv[:blk, :LANES], pv[blk:, :LANES])
            den_scr[dst, :] = jnp.where(low, pv[:blk, LANES:], pv[blk:, LANES:])
            max_scr[dst, :] = jnp.where(low, jnp.broadcast_to(m[:blk], (blk, LANES)),
                                        jnp.broadcast_to(m[blk:], (blk, LANES)))

        def blocks(it, carry):
            for u in range(group):
                one_block(it * group + u)
            return carry

        lax.fori_loop(0, n_blocks // group, blocks, 0)

    def merge(c, carry):
        rows = pl.ds(pl.multiple_of(c * merge_rows, merge_rows), merge_rows)
        maxes = [max_scr[rows, :] for max_scr in max_scrs]
        top = functools.reduce(jnp.maximum, maxes)
        weights = [jnp.exp2(m - top) for m in maxes]
        num = sum(w * num_scr[rows, :] for w, num_scr in zip(weights, num_scrs))
        den = sum(w * den_scr[rows, :] for w, den_scr in zip(weights, den_scrs))
        o_ref[rows, :] = (num / den).astype(BF)
        return carry

    lax.fori_loop(0, seq // merge_rows, merge, 0)


def _dilated_attention(z, zd4, zd16, *, batch, seq):
    blk = DIL_PATTERNS[0][0] // DIL_PATTERNS[0][1]
    tokens = batch * seq
    per_section = ATT_WIDTH // LANES
    col = lambda section: pl.BlockSpec(
        (seq, LANES), lambda b, hp: (b, section * per_section + hp))
    f32_buf = pltpu.VMEM((seq, LANES), F32)
    return pl.pallas_call(
        functools.partial(_dilated_kernel, seq=seq, blk=blk, chunk=DIL_CHUNK,
                          group=seq // blk, merge_rows=DIL_MERGE_ROWS),
        grid=(batch, N_HEAD_PAIRS),
        in_specs=[col(DIL_SECTION), col(DIL_SECTION + 1), col(DIL_SECTION + 2),
                  col(0), col(1), col(2), col(0), col(1), col(2)],
        out_specs=pl.BlockSpec((seq, LANES), lambda b, hp: (b, hp)),
        out_shape=jax.ShapeDtypeStruct((tokens, ATT_WIDTH), BF),
        scratch_shapes=[f32_buf] * 9 + [pltpu.VMEM((2, 2 * blk, 2 * blk), F32)],
        compiler_params=_cparams(("arbitrary", "arbitrary")),
        name="dilated_attn",
    )(z, z, z, zd4, zd4, zd4, zd16, zd16, zd16)


def _mix_kernel(ya_ref, yb_ref, ga_ref, gb_ref, x_ref, woa_ref, wob_ref, wout_ref,
                g_ref, o_ref):
    pa = _dot(ya_ref[...], woa_ref[...])
    pb = _dot(yb_ref[...], wob_ref[...])
    mixed = (jax.nn.sigmoid(ga_ref[...].astype(F32)) * pa
             + jax.nn.sigmoid(gb_ref[...].astype(F32)) * pb)
    y = _dot(mixed.astype(BF), wout_ref[...])
    o_ref[...] = x_ref[...] + _rms(y, g_ref[...])


def _mix(ya, yb, z, x2d, woa, wob, wout, g, *, tm):
    tokens = x2d.shape[0]
    gate_blk = lambda off: pl.BlockSpec((tm, D_MODEL), lambda i: (i, off))
    const = lambda i: (0, 0)
    return pl.pallas_call(
        _mix_kernel,
        grid=(tokens // tm,),
        in_specs=[
            pl.BlockSpec((tm, ATT_WIDTH), lambda i: (i, 0)),
            pl.BlockSpec((tm, ATT_WIDTH), lambda i: (i, 0)),
            gate_blk(3),
            gate_blk(4),
            pl.BlockSpec((tm, D_MODEL), lambda i: (i, 0)),
            pl.BlockSpec((ATT_WIDTH, D_MODEL), const),
            pl.BlockSpec((ATT_WIDTH, D_MODEL), const),
            pl.BlockSpec((D_MODEL, D_MODEL), const),
            pl.BlockSpec((1, D_MODEL), const),
        ],
        out_specs=pl.BlockSpec((tm, D_MODEL), lambda i: (i, 0)),
        out_shape=jax.ShapeDtypeStruct((tokens, D_MODEL), F32),
        compiler_params=_cparams(("arbitrary",)),
        name="mix",
    )(ya, yb, z, z, x2d, woa, wob, wout, g)


def _ffn_kernel(x_ref, halo_ref, gpre_ref, wup_ref, cw_ref, cb_ref, wd_ref, gpost_ref, o_ref,
                h_scr, ua0_scr, ub0_scr, ua1_scr, ub1_scr, acc_scr, rows_scr,
                *, tm, tf, tiles_per_seq, n_chunks):
    i = pl.program_id(0)
    halo = BF16_ROWS
    n_rows = tm + halo
    n_groups = n_rows // SUBLANES
    chunk = n_rows // n_chunks
    assert n_groups * SUBLANES == n_rows and chunk * n_chunks == n_rows
    assert chunk % BF16_ROWS == 0 and n_groups % 2 == 0
    n_tiles = D_FF // tf
    u_sets = ((ua0_scr, ub0_scr), (ua1_scr, ub1_scr))

    def group_rows(g):
        return pl.ds(g, SUBLANES, stride=n_groups)

    def cols(f, gate):
        return pl.ds(pl.multiple_of(f * tf + (D_FF if gate else 0), LANES), tf)

    def up_proj(f, dst, lo, hi):
        h = h_scr[lo:hi, :]
        dst[0][lo:hi, :] = _dot(h, wup_ref[:, cols(f, False)])
        dst[1][lo:hi, :] = _dot(h, wup_ref[:, cols(f, True)])

    def tap_rows(u_scr, back, first, n):
        lo = first - back * SUBLANES
        if lo >= 0:
            return u_scr[lo:lo + n, :]
        wrapped = [pltpu.roll(u_scr[n_rows + k * SUBLANES:n_rows + (k + 1) * SUBLANES, :], 1, 0)
                   for k in range(lo // SUBLANES, 0)]
        return jnp.concatenate(wrapped + [u_scr[0:n + lo, :]], axis=0)

    def conv(u_scr, window, post_scale, first, n):
        out = cb_ref[:, window] * post_scale
        for tap in range(CONV_WIDTH):
            back = CONV_WIDTH - 1 - tap
            out = out + (cw_ref[tap:tap + 1, window] * post_scale) * tap_rows(u_scr, back, first, n)
        return out

    def down_proj(f, src, first, n):
        a = conv(src[0], cols(f, False), 1.0, first, n)
        half_b = conv(src[1], cols(f, True), 0.5, first, n)
        c0 = math.sqrt(2.0 / math.pi)
        inner = a * (c0 + (c0 * 0.044715) * (a * a))
        hidden = ((a * half_b) * (1.0 + jnp.tanh(inner))).astype(BF)
        return _dot(hidden, wd_ref[pl.ds(pl.multiple_of(f * tf, tf), tf), :])

    def stage(f, parity):
        src, dst = u_sets[1 - parity], u_sets[parity]
        for c in range(n_chunks):
            up_proj(f, dst, c * chunk, (c + 1) * chunk)
            out_rows = slice(c * chunk, (c + 1) * chunk)
            acc_scr[out_rows, :] += down_proj(f - 1, src, c * chunk, chunk)

    n_slabs = D_MODEL // LANES
    g = gpre_ref[...]
    hh = _rms(halo_ref[...], g)
    hh = jnp.where(i % tiles_per_seq == 0, jnp.zeros_like(hh), hh)
    hx = _rms(x_ref[...], g)
    for s in range(n_slabs):
        rows_scr[s, 0:halo, :] = hh[:, s * LANES:(s + 1) * LANES]
        rows_scr[s, halo:, :] = hx[:, s * LANES:(s + 1) * LANES]
    for gg in range(0, n_groups, 2):
        pair = jnp.concatenate(
            [jnp.concatenate([rows_scr[s, group_rows(gg + d), :] for s in range(n_slabs)], axis=1)
             for d in range(2)], axis=0)
        h_scr[gg * SUBLANES:(gg + 2) * SUBLANES, :] = pair.astype(BF)
    acc_scr[...] = jnp.zeros_like(acc_scr)
    up_proj(0, u_sets[0], 0, n_rows)

    def pair_of_tiles(k, carry):
        stage(2 * k + 1, 1)
        stage(2 * k + 2, 0)
        return carry

    assert n_tiles % 2 == 1
    for k in range((n_tiles - 1) // 2):
        pair_of_tiles(k, 0)
    gpost = gpost_ref[...]
    for c in range(n_chunks):
        y = acc_scr[c * chunk:(c + 1) * chunk, :] + down_proj(n_tiles - 1, u_sets[0], c * chunk, chunk)
        normed = _rms(y, gpost)
        for gg in range(chunk // SUBLANES):
            g_abs = c * (chunk // SUBLANES) + gg
            for s in range(n_slabs):
                rows_scr[s, group_rows(g_abs), :] = normed[gg * SUBLANES:(gg + 1) * SUBLANES,
                                                           s * LANES:(s + 1) * LANES]
    for s in range(n_slabs):
        lanes = slice(s * LANES, (s + 1) * LANES)
        o_ref[:, lanes] = x_ref[:, lanes] + rows_scr[s, halo:, :]


def _ffn(x1, g_pre, w_up, conv_w, conv_b, w_down, g_post, *, seq, tm, tf):
    tokens = x1.shape[0]
    halo = BF16_ROWS
    per = tm // halo
    resident = lambda shape: pl.BlockSpec(shape, lambda i: (0, 0),
                                          pipeline_mode=pl.Buffered(1))
    u_buf = pltpu.VMEM((tm + halo, tf), F32)
    return pl.pallas_call(
        functools.partial(_ffn_kernel, tm=tm, tf=tf, tiles_per_seq=seq // tm,
                          n_chunks=FFN_ROW_CHUNKS),
        grid=(tokens // tm,),
        in_specs=[
            pl.BlockSpec((tm, D_MODEL), lambda i: (i, 0)),
            pl.BlockSpec((halo, D_MODEL), lambda i: (jnp.maximum(i * per - 1, 0), 0)),
            resident((1, D_MODEL)),
            resident((D_MODEL, 2 * D_FF)),
            resident((CONV_WIDTH, 2 * D_FF)),
            resident((1, 2 * D_FF)),
            resident((D_FF, D_MODEL)),
            resident((1, D_MODEL)),
        ],
        out_specs=pl.BlockSpec((tm, D_MODEL), lambda i: (i, 0)),
        out_shape=jax.ShapeDtypeStruct((tokens, D_MODEL), F32),
        scratch_shapes=[
            pltpu.VMEM((tm + halo, D_MODEL), BF),
            u_buf, u_buf, u_buf, u_buf,
            pltpu.VMEM((tm + halo, D_MODEL), F32),
            pltpu.VMEM((D_MODEL // LANES, tm + halo, LANES), F32),
        ],
        compiler_params=_cparams(("arbitrary",)),
        name="ffn",
    )(x1, x1, g_pre, w_up, conv_w, conv_b, w_down, g_post)


def _rope_freq_lanes():
    inv_freq = ROPE_THETA ** (-jnp.arange(ROPE_HALF, dtype=F32) * 2.0 / ROPE_DIM)
    return jnp.tile(inv_freq, LANES // ROPE_HALF).reshape(1, LANES)


def kernel(x, g_pre_mix, w_in, b_forget, w_o_fox, w_o_dil, w_out, g_post_mix,
           g_pre_ffn, w_up, conv_w, conv_b, w_down, g_post_ffn):
    batch, seq, d_model = x.shape
    assert d_model == D_MODEL and seq % max(MIX_TOKENS, FFN_TOKENS, FOX_TILE, DIL_CHUNK) == 0
    depth = w_in.shape[0]
    freq_lanes = _rope_freq_lanes()
    x2d = x.reshape(batch * seq, D_MODEL)
    row = lambda v: v.reshape(1, -1)
    for l in range(depth):
        bias_lanes = jnp.pad(b_forget[l], (0, LANES - N_HEADS)).reshape(1, LANES)

        z, fa, zd4, zd16 = _in_proj(x2d, row(g_pre_mix[l]), w_in[l].astype(BF), freq_lanes,
                                    batch=batch, seq=seq, tm=DIL_CHUNK)
        gq, gk = _forget_scan(fa, bias_lanes, batch=batch, seq=seq, chunk=SCAN_CHUNK)
        ya = _fox_attention(z, gq, gk, batch=batch, seq=seq, tq=FOX_TILE)
        yb = _dilated_attention(z, zd4, zd16, batch=batch, seq=seq)
        x2d = _mix(ya, yb, z, x2d, w_o_fox[l].astype(BF), w_o_dil[l].astype(BF),
                   w_out[l].astype(BF), row(g_post_mix[l]), tm=MIX_TOKENS)
        x2d = _ffn(x2d, row(g_pre_ffn[l]), w_up[l].astype(BF), conv_w[l],
                   row(conv_b[l]), w_down[l].astype(BF), row(g_post_ffn[l]),
                   seq=seq, tm=FFN_TOKENS, tf=FFN_HIDDEN_TILE)
    return x2d.reshape(batch, seq, D_MODEL)
```

```python
import functools
import math

import numpy as np
import jax
import jax.numpy as jnp
from jax import lax
from jax.experimental import pallas as pl
from jax.experimental.pallas import tpu as pltpu

D_MODEL = 1024
HEAD_DIM = 64
N_HEADS = 8
ATT_WIDTH = N_HEADS * HEAD_DIM
N_HEAD_PAIRS = N_HEADS // 2
DIL_PATTERNS = ((128, 1), (512, 4), (2048, 16))
ROPE_DIM = HEAD_DIM // 4
ROPE_HALF = ROPE_DIM // 2
ROPE_THETA = 500000.0
D_FF = 2816
CONV_WIDTH = 3
RMS_EPS = 1e-6
NEG_INF = -1e30
Q_SCALE = 1.0 / math.sqrt(HEAD_DIM)
LOG2_E = math.log2(math.e)
UNDERFLOW_BITS = 150.0
SCORE_MARGIN = 1.0
NORM_SLACK = 1.0 + 2.0 ** -7

LANES = 128
SUBLANES = 8
BF16_ROWS = 16
Z_WIDTH = 3 * ATT_WIDTH * 2 + 2 * D_MODEL
Z_BLK = 512
DIL_SECTION = 3
DIL_CHUNK = 512
N_PIECES = 3
VMEM_LIMIT = 56 * 1024 * 1024

SCAN_CHUNK = 512
FOX_TILE = 512
DIL_MERGE_ROWS = 256
MIX_TOKENS = 1024
FFN_TOKENS = 1024
FFN_HIDDEN_TILE = 256
FFN_ROW_CHUNKS = 5

BF = jnp.bfloat16
F32 = jnp.float32


def _cparams(sem, flags=None):
    return pltpu.CompilerParams(dimension_semantics=sem, vmem_limit_bytes=VMEM_LIMIT,
                                flags=flags)


def _rms(xf, g):
    inv = lax.rsqrt(jnp.mean(xf * xf, axis=-1, keepdims=True) + RMS_EPS)
    return xf * inv * g


def _split3(x):
    hi = x.astype(BF)
    r1 = x - hi.astype(F32)
    mid = r1.astype(BF)
    lo = (r1 - mid.astype(F32)).astype(BF)
    return hi, mid, lo


def _dot(a, b):
    return jnp.dot(a, b, preferred_element_type=F32)


def _dot_nt(a, b):
    return lax.dot_general(a, b, (((1,), (1,)), ((), ())), preferred_element_type=F32)


def _in_proj_kernel(x_ref, g_ref, w_raw_ref, freq_ref, z_ref, fa_ref, zd4_ref, zd16_ref,
                    w_ref, h_scr, cos_scr, sneg_scr, spos_scr, stage_scr, stage4_scr, *, tm):
    p = pl.program_id(0)
    b = pl.program_id(1)
    zd_refs = {4: zd4_ref, 16: zd16_ref}
    n_slabs = Z_BLK // LANES

    @pl.when((p == 0) & (b == 0))
    def _regroup_weight():
        fox_end = 3 * ATT_WIDTH
        raw_width = w_raw_ref.shape[1]
        first = fox_end // LANES

        def group(i):
            lo = i * LANES
            return w_raw_ref[:, lo:min(lo + LANES, raw_width)].astype(F32)

        w_ref[:, 0:fox_end] = w_raw_ref[:, 0:fox_end]
        for k in range((Z_WIDTH - fox_end) // LANES):
            shifted = jnp.concatenate(
                [group(first + k)[:, N_HEADS:], group(first + k + 1)[:, :N_HEADS]], axis=1)
            w_ref[:, fox_end + k * LANES:fox_end + (k + 1) * LANES] = shifted.astype(BF)
        w_ref[:, Z_WIDTH:Z_WIDTH + LANES] = jnp.concatenate(
            [group(first)[:, :N_HEADS], jnp.zeros((D_MODEL, LANES - N_HEADS), F32)],
            axis=1).astype(BF)

    @pl.when(b == 0)
    def _tables():
        pos = (p * tm + lax.broadcasted_iota(jnp.int32, (tm, LANES), 0)).astype(F32)
        lane = lax.broadcasted_iota(jnp.int32, (tm, LANES), 1)
        c = lane % HEAD_DIM
        ang = pos * freq_ref[...]
        cs = jnp.cos(ang)
        sn = jnp.sin(ang)
        cos_scr[...] = jnp.where(c < ROPE_DIM, cs, 1.0)
        sneg_scr[...] = jnp.where(c < ROPE_HALF, -sn, 0.0)
        spos_scr[...] = jnp.where((c >= ROPE_HALF) & (c < ROPE_DIM), sn, 0.0)

    def rope(t):
        up = pltpu.roll(t, LANES - ROPE_HALF, 1)
        dn = pltpu.roll(t, ROPE_HALF, 1)
        return t * cos_scr[...] + up * sneg_scr[...] + dn * spos_scr[...]

    def emit_dilated(section, slab, val):
        out_lanes = slice(section * Z_BLK + slab * LANES, section * Z_BLK + (slab + 1) * LANES)
        z_ref[:, (DIL_SECTION + section) * Z_BLK + slab * LANES:
              (DIL_SECTION + section) * Z_BLK + (slab + 1) * LANES] = val.astype(BF)
        stage, stage4 = stage_scr.at[section, slab], stage4_scr.at[section, slab]
        stage[...] = val
        seg4 = tm // 4
        for c in range(4):
            part = stage[pl.ds(c, seg4, stride=4), :]
            stage4[c * seg4:(c + 1) * seg4, :] = part
            zd_refs[4][c * seg4:(c + 1) * seg4, out_lanes] = part.astype(BF)
        seg16 = tm // 16
        for jj in range(16):
            c, a = jj % 4, jj // 4
            part = stage4[pl.ds(c * seg4 + a, seg16, stride=4), :]
            zd_refs[16][jj * seg16:(jj + 1) * seg16, out_lanes] = part.astype(BF)

    h = _rms(x_ref[...], g_ref[...]).astype(BF)
    h_scr[...] = h
    fa_ref[...] = _dot(h, w_ref[:, Z_WIDTH:Z_WIDTH + LANES])

    for j in range(Z_WIDTH // Z_BLK):
        cols = slice(j * Z_BLK, (j + 1) * Z_BLK)
        acc = _dot(h_scr[...], w_ref[:, cols])
        section = j - DIL_SECTION
        if j == 0:
            z_ref[:, cols] = (acc * (Q_SCALE * LOG2_E)).astype(BF)
        elif 0 <= section < 3:
            for slab in range(n_slabs):
                t = acc[:, slab * LANES:(slab + 1) * LANES]
                if section == 0:
                    t = rope(t) * (Q_SCALE * LOG2_E)
                elif section == 1:
                    t = rope(t)
                emit_dilated(section, slab, t)
        else:
            z_ref[:, cols] = acc.astype(BF)


def _in_proj(x2d, g, w_raw, freq_lanes, *, batch, seq, tm):
    n_p = seq // tm
    tokens = batch * seq
    row = lambda p, b: (b * n_p + p, 0)
    resident = lambda shape: pl.BlockSpec(shape, lambda p, b: (0, 0),
                                          pipeline_mode=pl.Buffered(1))
    dil_shape = jax.ShapeDtypeStruct((tokens, 3 * ATT_WIDTH), BF)
    stage_buf = pltpu.VMEM((3, Z_BLK // LANES, tm, LANES), F32)
    return pl.pallas_call(
        functools.partial(_in_proj_kernel, tm=tm),
        grid=(n_p, batch),
        in_specs=[
            pl.BlockSpec((tm, D_MODEL), row),
            resident((1, D_MODEL)),
            resident((D_MODEL, Z_WIDTH + N_HEADS)),
            resident((1, LANES)),
        ],
        out_specs=[
            pl.BlockSpec((tm, Z_WIDTH), row),
            pl.BlockSpec((tm, LANES), row),
            pl.BlockSpec((tm, 3 * ATT_WIDTH), row),
            pl.BlockSpec((tm, 3 * ATT_WIDTH), row),
        ],
        out_shape=[
            jax.ShapeDtypeStruct((tokens, Z_WIDTH), BF),
            jax.ShapeDtypeStruct((tokens, LANES), F32),
            dil_shape,
            dil_shape,
        ],
        scratch_shapes=[
            pltpu.VMEM((D_MODEL, Z_WIDTH + LANES), BF),
            pltpu.VMEM((tm, D_MODEL), BF),
            pltpu.VMEM((tm, LANES), F32),
            pltpu.VMEM((tm, LANES), F32),
            pltpu.VMEM((tm, LANES), F32),
            stage_buf,
            stage_buf,
        ],
        compiler_params=_cparams(("arbitrary", "arbitrary")),
        name="in_proj",
    )(x2d, g, w_raw, freq_lanes)


def _extras_base(head):
    return (HEAD_DIM if head % 2 == 0 else 0) + 2 * N_PIECES * (head // 2)


def _forget_scan_kernel(fa_ref, bias_ref, tri_ref, pq_ref, pk_ref, oq_ref, ok_ref,
                        gq_ref, gk_ref):
    chunk = tri_ref.shape[0]
    n_chunks = fa_ref.shape[0] // chunk
    t = fa_ref[...] + bias_ref[...]
    log_f = jnp.minimum(t, 0.0) - jnp.log1p(jnp.exp(-jnp.abs(t)))
    wide = jnp.concatenate(
        [log_f[c * chunk:(c + 1) * chunk, :] for c in range(n_chunks)], axis=1)
    tri = tri_ref[...]
    local = sum(_dot(tri, piece) for piece in _split3(wide))
    carry = jnp.zeros((1, LANES), F32)
    parts = []
    for c in range(n_chunks):
        part = local[:, c * LANES:(c + 1) * LANES] + carry
        parts.append(part)
        carry = part[chunk - 1:chunk, :]
    run = jnp.concatenate(parts, axis=0)
    gq = oq_ref[...].astype(F32)
    gk = ok_ref[...].astype(F32)
    for i, piece in enumerate(_split3(run * LOG2_E)):
        gq = gq + _dot(piece, pq_ref[i])
        gk = gk - _dot(piece, pk_ref[i])
    gq_ref[0] = gq.astype(BF)
    gk_ref[0] = gk.astype(BF)


def _forget_scan_constants(chunk):
    tri = np.tril(np.ones((chunk, chunk), np.float32))
    pq = np.zeros((N_PIECES, LANES, LANES), np.float32)
    pk = np.zeros((N_PIECES, LANES, LANES), np.float32)
    oq = np.zeros((1, LANES), np.float32)
    ok = np.zeros((1, LANES), np.float32)
    for h in range(N_HEADS):
        base = _extras_base(h)
        for i in range(N_PIECES):
            pq[i, h, base + i] = 1.0
            oq[0, base + N_PIECES + i] = 1.0
            ok[0, base + i] = 1.0
            pk[i, h, base + N_PIECES + i] = 1.0
    as_bf = lambda a: jnp.asarray(a, BF)
    return as_bf(tri), as_bf(pq), as_bf(pk), as_bf(oq), as_bf(ok)


def _forget_scan(fa, bias_lanes, *, batch, seq, chunk):
    tri, pq, pk, oq, ok = _forget_scan_constants(chunk)
    const2 = lambda b: (0, 0)
    const3 = lambda b: (0, 0, 0)
    return pl.pallas_call(
        _forget_scan_kernel,
        grid=(batch,),
        in_specs=[
            pl.BlockSpec((seq, LANES), lambda b: (b, 0)),
            pl.BlockSpec((1, LANES), const2),
            pl.BlockSpec((chunk, chunk), const2),
            pl.BlockSpec((N_PIECES, LANES, LANES), const3),
            pl.BlockSpec((N_PIECES, LANES, LANES), const3),
            pl.BlockSpec((1, LANES), const2),
            pl.BlockSpec((1, LANES), const2),
        ],
        out_specs=[
            pl.BlockSpec((1, seq, LANES), lambda b: (b, 0, 0)),
            pl.BlockSpec((1, seq, LANES), lambda b: (b, 0, 0)),
        ],
        out_shape=[
            jax.ShapeDtypeStruct((batch, seq, LANES), BF),
            jax.ShapeDtypeStruct((batch, seq, LANES), BF),
        ],
        compiler_params=_cparams(("arbitrary",)),
        name="forget_scan",
    )(fa, bias_lanes, tri, pq, pk, oq, ok)


def _with_bias_lanes(slab, extras, hp, parity):
    lane = lax.broadcasted_iota(jnp.int32, slab.shape, 1)
    own = (lane < HEAD_DIM) if parity == 0 else (lane >= HEAD_DIM)
    base = (HEAD_DIM if parity == 0 else 0) + 2 * N_PIECES * hp
    in_extras = (lane >= base) & (lane < base + 2 * N_PIECES)
    return jnp.where(own, slab, jnp.where(in_extras, extras, jnp.zeros_like(extras)))


def _fox_kernel(q_ref, k_ref, v_ref, gq_ref, gk_ref, o_ref,
                ka_scr, kb_scr, vta_scr, vtb_scr, qp_scr, bias_scr, s_scr, acc_scr, ot_scr,
                bound_smem, *, tq):
    hp = pl.program_id(1)
    k_scrs = (ka_scr, kb_scr)
    vt_scrs = (vta_scr, vtb_scr)
    seq = k_ref.shape[0]
    n_q = seq // tq
    lane_row = lax.broadcasted_iota(jnp.int32, (1, LANES), 1)

    sel_row = lax.broadcasted_iota(jnp.int32, (LANES, LANES), 0)
    sel_col = lax.broadcasted_iota(jnp.int32, (LANES, LANES), 1)
    head_sum = jnp.where(sel_row // HEAD_DIM == sel_col, 1.0, 0.0).astype(BF)

    def max_row_norm(slab, head):
        lane = lax.broadcasted_iota(jnp.int32, slab.shape, 1)
        own = (lane < HEAD_DIM) if head == 0 else (lane >= HEAD_DIM)
        x = jnp.where(own, slab.astype(F32), 0.0)
        sq = jnp.sum(x * x, axis=1, keepdims=True)
        return jnp.sqrt(jnp.max(sq, axis=0, keepdims=True))

    def max_row_norms(slab):
        x = slab.astype(F32)
        per_row = _dot((x * x).astype(BF), head_sum)
        return jnp.sqrt(jnp.max(per_row, axis=0, keepdims=True) * NORM_SLACK)

    def bias_lane_sum(row, head, first):
        base = (HEAD_DIM if head == 0 else 0) + 2 * N_PIECES * hp + first
        picked = (lane_row >= base) & (lane_row < base + N_PIECES)
        return jnp.sum(jnp.where(picked, row, 0.0), axis=1, keepdims=True)

    k2 = k_ref[...]
    gk = gk_ref[0]
    k_norms = max_row_norms(k2)
    for head in range(2):
        k_scrs[head][...] = _with_bias_lanes(k2, gk, hp, head)
        bound_smem[head, n_q] = k_norms[0, head]
        for j in range(n_q):
            rows16 = gk_ref[0, (j + 1) * tq - BF16_ROWS:(j + 1) * tq, :].astype(F32)
            last = rows16[BF16_ROWS - 1:BF16_ROWS, :]
            bound_smem[head, j] = -bias_lane_sum(last, head, N_PIECES)[0, 0]
    vt = v_ref[...].T
    ones = jnp.ones((BF16_ROWS, seq), BF)
    for head, vt_scr in enumerate(vt_scrs):
        vt_scr[0:HEAD_DIM, :] = vt[head * HEAD_DIM:(head + 1) * HEAD_DIM, :]
        vt_scr[HEAD_DIM:, :] = ones
    row = lax.broadcasted_iota(jnp.int32, (tq, tq), 0)
    col = lax.broadcasted_iota(jnp.int32, (tq, tq), 1)
    bias_scr[...] = jnp.where(row <= col, 0.0, NEG_INF)
    acc_scr[...] = jnp.zeros_like(acc_scr)

    def scores(head, tile, masked):
        kt = k_scrs[head][pl.ds(pl.multiple_of(tile * tq, tq), tq), :]
        st = _dot(kt, qp_scr[head])
        if masked:
            st = st + bias_scr[...]
        s_scr[head] = st
        return jnp.max(st, axis=0, keepdims=True)

    def softmax_pv(head, st, tile, tile_max, m):
        start = pl.multiple_of(tile * tq, tq)
        m_new = jnp.maximum(m, tile_max)
        alpha = jnp.exp2(m - m_new)
        pt = jnp.exp2(st - m_new).astype(BF)
        vt_tile = vt_scrs[head][:, pl.ds(start, tq)]
        acc_scr[head] = alpha * acc_scr[head] + _dot(vt_tile, pt)
        return m_new

    def start_tile(qi):
        rows = pl.ds(pl.multiple_of(qi * tq, tq), tq)
        q2 = q_ref[rows, :]
        gq = gq_ref[0, rows, :]
        first_row = gq[0:BF16_ROWS, :].astype(F32)[0:1, :]
        maxes, slack = [], []
        for head in range(2):
            qp_scr[head] = _with_bias_lanes(q2, gq, hp, head).T
            maxes.append(scores(head, qi, True))
            reach = (max_row_norm(q2, head) * bound_smem[head, n_q]
                     + bias_lane_sum(first_row, head, 0) + SCORE_MARGIN)
            lowest_max = jnp.min(maxes[head], axis=1, keepdims=True)
            slack.append(reach - (lowest_max - UNDERFLOW_BITS))
        count = jnp.zeros((1, 1), jnp.int32)
        for j in range(n_q - 1):
            needed = (((slack[0] >= bound_smem[0, j]) | (slack[1] >= bound_smem[1, j]))
                      & (j < qi))
            count = count + jnp.where(needed, 1, 0)
        return maxes[0], maxes[1], count[0, 0]

    def finish_tile(qi):
        for head in range(2):
            acc = acc_scr[head]
            ot_scr[head * HEAD_DIM:(head + 1) * HEAD_DIM, :] = (
                acc[0:HEAD_DIM, :] / acc[HEAD_DIM:HEAD_DIM + 1, :])
        o_ref[pl.ds(pl.multiple_of(qi * tq, tq), tq), :] = ot_scr[...].T.astype(BF)
        acc_scr[...] = jnp.zeros_like(acc_scr)

    m_init = jnp.full((1, tq), NEG_INF, F32)

    def query_tile(qi, carry):
        first_a, first_b, n_needed = carry

        def key_tile(t, inner):
            max_a, max_b, m_a, m_b, prev_tile = inner
            prev = [s_scr[head] for head in range(2)]
            tile = qi - 1 - t
            next_max = [scores(head, tile, False) for head in range(2)]
            m_a = softmax_pv(0, prev[0], prev_tile, max_a, m_a)
            m_b = softmax_pv(1, prev[1], prev_tile, max_b, m_b)
            return next_max[0], next_max[1], m_a, m_b, tile

        def two_key_tiles(u, inner):
            return key_tile(2 * u + 1, key_tile(2 * u, inner))

        inner = lax.fori_loop(0, n_needed // 2, two_key_tiles,
                              (first_a, first_b, m_init, m_init, qi))
        max_a, max_b, m_a, m_b, prev_tile = lax.cond(
            n_needed % 2 == 1, lambda c: key_tile(n_needed - 1, c), lambda c: c, inner)
        last = [s_scr[head] for head in range(2)]
        softmax_pv(0, last[0], prev_tile, max_a, m_a)
        softmax_pv(1, last[1], prev_tile, max_b, m_b)
        following = start_tile(jnp.minimum(qi + 1, n_q - 1))
        finish_tile(qi)
        return following

    lax.fori_loop(0, n_q, query_tile, start_tile(0))


def _fox_attention(z, gq, gk, *, batch, seq, tq):
    tokens = batch * seq
    col = lambda section: pl.BlockSpec(
        (seq, LANES), lambda b, hp: (b, section * N_HEAD_PAIRS + hp))
    whole = pl.BlockSpec((1, seq, LANES), lambda b, hp: (b, 0, 0))
    ext = HEAD_DIM + BF16_ROWS
    return pl.pallas_call(
        functools.partial(_fox_kernel, tq=tq),
        grid=(batch, N_HEAD_PAIRS),
        in_specs=[col(0), col(1), col(2), whole, whole],
        out_specs=pl.BlockSpec((seq, LANES), lambda b, hp: (b, hp)),
        out_shape=jax.ShapeDtypeStruct((tokens, ATT_WIDTH), BF),
        scratch_shapes=[
            pltpu.VMEM((seq, LANES), BF),
            pltpu.VMEM((seq, LANES), BF),
            pltpu.VMEM((ext, seq), BF),
            pltpu.VMEM((ext, seq), BF),
            pltpu.VMEM((2, LANES, tq), BF),
            pltpu.VMEM((tq, tq), F32),
            pltpu.VMEM((2, tq, tq), F32),
            pltpu.VMEM((2, ext, tq), F32),
            pltpu.VMEM((LANES, tq), F32),
            pltpu.SMEM((2, seq // tq + 1), F32),
        ],
        compiler_params=_cparams(("arbitrary", "arbitrary")),
        name="fox_attn",
    )(z, z, z, gq, gk)


def _log2(n):
    assert n > 0 and n & (n - 1) == 0, n
    return n.bit_length() - 1


def _dilated_kernel(q1, k1, v1, q4, k4, v4, q16, k16, v16, o_ref,
                    num1, num4, num16, den1, den4, den16, max1, max4, max16, bias_scr,
                    *, seq, blk, chunk, group, merge_rows):
    sources = ((q1, k1, v1), (q4, k4, v4), (q16, k16, v16))
    num_scrs = (num1, num4, num16)
    den_scrs = (den1, den4, den16)
    max_scrs = (max1, max4, max16)
    n_blocks = seq // blk

    low = lax.broadcasted_iota(jnp.int32, (blk, LANES), 1) < HEAD_DIM
    qrow = lax.broadcasted_iota(jnp.int32, (2 * blk, 2 * blk), 0) % blk
    kcol = lax.broadcasted_iota(jnp.int32, (2 * blk, 2 * blk), 1)
    dist = qrow + blk - kcol
    band = (dist >= 0) & (dist <= blk)
    bias_scr[0] = jnp.where(band & (kcol >= blk), 0.0, NEG_INF)
    bias_scr[1] = jnp.where(band, 0.0, NEG_INF)
    ones = jnp.ones((2 * blk, LANES), BF)

    for idx, (window, dilation) in enumerate(DIL_PATTERNS):
        assert window // dilation == blk
        q_ref, k_ref, v_ref = sources[idx]
        num_scr, den_scr, max_scr = num_scrs[idx], den_scrs[idx], max_scrs[idx]
        sub_shift = _log2(seq // dilation // blk)
        seg_rows = chunk // dilation
        seg = min(blk, seg_rows)

        def load(ref, j, l0):
            parts = []
            for s in range(blk // seg):
                l = l0 + s * seg
                p = lax.shift_right_logical(l, _log2(seg_rows))
                i = l & (seg_rows - 1)
                start = pl.multiple_of(p * chunk + j * seg_rows + i, seg)
                parts.append(ref[pl.ds(start, seg), :])
            return parts[0] if len(parts) == 1 else jnp.concatenate(parts, axis=0)

        def one_block(g):
            j = lax.shift_right_logical(g, sub_shift)
            gs = g & ((1 << sub_shift) - 1)
            l0 = gs * blk
            lp = jnp.maximum(l0 - blk, 0)
            q2 = load(q_ref, j, l0)
            zero = jnp.zeros_like(q2)
            qq = jnp.concatenate([jnp.where(low, q2, zero), jnp.where(low, zero, q2)], axis=0)
            kwin = jnp.concatenate([load(k_ref, j, lp), load(k_ref, j, l0)], axis=0)
            vwin = jnp.concatenate([load(v_ref, j, lp), load(v_ref, j, l0)], axis=0)
            s = _dot_nt(qq, kwin) + bias_scr[jnp.minimum(gs, 1)]
            m = jnp.max(s, axis=1, keepdims=True)
            p = jnp.exp2(s - m).astype(BF)
            pv = _dot(p, jnp.concatenate([vwin, ones], axis=1))
            if dilation == 1:
                dst = pl.ds(pl.multiple_of(l0, blk), blk)
            else:
                dst = pl.ds(l0 * dilation + j, blk, stride=dilation)
            num_scr[dst, :] = jnp.where(low, pv[:blk, :LANES], pv[blk:, :LANES])
            den_scr[dst, :] = jnp.where(low, pv[:blk, LANES:], pv[blk:, LANES:])
            max_scr[dst, :] = jnp.where(low, jnp.broadcast_to(m[:blk], (blk, LANES)),
                                        jnp.broadcast_to(m[blk:], (blk, LANES)))

        def blocks(it, carry):
            for u in range(group):
                one_block(it * group + u)
            return carry

        lax.fori_loop(0, n_blocks // group, blocks, 0)

    def merge(c, carry):
        rows = pl.ds(pl.multiple_of(c * merge_rows, merge_rows), merge_rows)
        maxes = [max_scr[rows, :] for max_scr in max_scrs]
        top = functools.reduce(jnp.maximum, maxes)
        weights = [jnp.exp2(m - top) for m in maxes]
        num = sum(w * num_scr[rows, :] for w, num_scr in zip(weights, num_scrs))
        den = sum(w * den_scr[rows, :] for w, den_scr in zip(weights, den_scrs))
        o_ref[rows, :] = (num / den).astype(BF)
        return carry

    lax.fori_loop(0, seq // merge_rows, merge, 0)


def _dilated_attention(z, zd4, zd16, *, batch, seq):
    blk = DIL_PATTERNS[0][0] // DIL_PATTERNS[0][1]
    tokens = batch * seq
    per_section = ATT_WIDTH // LANES
    col = lambda section: pl.BlockSpec(
        (seq, LANES), lambda b, hp: (b, section * per_section + hp))
    f32_buf = pltpu.VMEM((seq, LANES), F32)
    return pl.pallas_call(
        functools.partial(_dilated_kernel, seq=seq, blk=blk, chunk=DIL_CHUNK,
                          group=seq // blk, merge_rows=DIL_MERGE_ROWS),
        grid=(batch, N_HEAD_PAIRS),
        in_specs=[col(DIL_SECTION), col(DIL_SECTION + 1), col(DIL_SECTION + 2),
                  col(0), col(1), col(2), col(0), col(1), col(2)],
        out_specs=pl.BlockSpec((seq, LANES), lambda b, hp: (b, hp)),
        out_shape=jax.ShapeDtypeStruct((tokens, ATT_WIDTH), BF),
        scratch_shapes=[f32_buf] * 9 + [pltpu.VMEM((2, 2 * blk, 2 * blk), F32)],
        compiler_params=_cparams(("arbitrary", "arbitrary")),
        name="dilated_attn",
    )(z, z, z, zd4, zd4, zd4, zd16, zd16, zd16)


def _mix_kernel(ya_ref, yb_ref, ga_ref, gb_ref, x_ref, woa_ref, wob_ref, wout_ref,
                g_ref, o_ref):
    pa = _dot(ya_ref[...], woa_ref[...])
    pb = _dot(yb_ref[...], wob_ref[...])
    mixed = (jax.nn.sigmoid(ga_ref[...].astype(F32)) * pa
             + jax.nn.sigmoid(gb_ref[...].astype(F32)) * pb)
    y = _dot(mixed.astype(BF), wout_ref[...])
    o_ref[...] = x_ref[...] + _rms(y, g_ref[...])


def _mix(ya, yb, z, x2d, woa, wob, wout, g, *, tm):
    tokens = x2d.shape[0]
    gate_blk = lambda off: pl.BlockSpec((tm, D_MODEL), lambda i: (i, off))
    const = lambda i: (0, 0)
    return pl.pallas_call(
        _mix_kernel,
        grid=(tokens // tm,),
        in_specs=[
            pl.BlockSpec((tm, ATT_WIDTH), lambda i: (i, 0)),
            pl.BlockSpec((tm, ATT_WIDTH), lambda i: (i, 0)),
            gate_blk(3),
            gate_blk(4),
            pl.BlockSpec((tm, D_MODEL), lambda i: (i, 0)),
            pl.BlockSpec((ATT_WIDTH, D_MODEL), const),
            pl.BlockSpec((ATT_WIDTH, D_MODEL), const),
            pl.BlockSpec((D_MODEL, D_MODEL), const),
            pl.BlockSpec((1, D_MODEL), const),
        ],
        out_specs=pl.BlockSpec((tm, D_MODEL), lambda i: (i, 0)),
        out_shape=jax.ShapeDtypeStruct((tokens, D_MODEL), F32),
        compiler_params=_cparams(("arbitrary",)),
        name="mix",
    )(ya, yb, z, z, x2d, woa, wob, wout, g)


def _ffn_kernel(x_ref, halo_ref, gpre_ref, wup_ref, cw_ref, cb_ref, wd_ref, gpost_ref, o_ref,
                h_scr, ua0_scr, ub0_scr, ua1_scr, ub1_scr, acc_scr, rows_scr,
                *, tm, tf, tiles_per_seq, n_chunks):
    i = pl.program_id(0)
    halo = BF16_ROWS
    n_rows = tm + halo
    n_groups = n_rows // SUBLANES
    chunk = n_rows // n_chunks
    assert n_groups * SUBLANES == n_rows and chunk * n_chunks == n_rows
    assert chunk % BF16_ROWS == 0 and n_groups % 2 == 0
    n_tiles = D_FF // tf
    u_sets = ((ua0_scr, ub0_scr), (ua1_scr, ub1_scr))

    def group_rows(g):
        return pl.ds(g, SUBLANES, stride=n_groups)

    def cols(f, gate):
        return pl.ds(pl.multiple_of(f * tf + (D_FF if gate else 0), LANES), tf)

    def up_proj(f, dst, lo, hi):
        h = h_scr[lo:hi, :]
        dst[0][lo:hi, :] = _dot(h, wup_ref[:, cols(f, False)])
        dst[1][lo:hi, :] = _dot(h, wup_ref[:, cols(f, True)])

    def tap_rows(u_scr, back, first, n):
        lo = first - back * SUBLANES
        if lo >= 0:
            return u_scr[lo:lo + n, :]
        wrapped = [pltpu.roll(u_scr[n_rows + k * SUBLANES:n_rows + (k + 1) * SUBLANES, :], 1, 0)
                   for k in range(lo // SUBLANES, 0)]
        return jnp.concatenate(wrapped + [u_scr[0:n + lo, :]], axis=0)

    def conv(u_scr, window, post_scale, first, n):
        out = cb_ref[:, window] * post_scale
        for tap in range(CONV_WIDTH):
            back = CONV_WIDTH - 1 - tap
            out = out + (cw_ref[tap:tap + 1, window] * post_scale) * tap_rows(u_scr, back, first, n)
        return out

    def down_proj(f, src, first, n):
        a = conv(src[0], cols(f, False), 1.0, first, n)
        half_b = conv(src[1], cols(f, True), 0.5, first, n)
        c0 = math.sqrt(2.0 / math.pi)
        inner = a * (c0 + (c0 * 0.044715) * (a * a))
        hidden = ((a * half_b) * (1.0 + jnp.tanh(inner))).astype(BF)
        return _dot(hidden, wd_ref[pl.ds(pl.multiple_of(f * tf, tf), tf), :])

    def stage(f, parity):
        src, dst = u_sets[1 - parity], u_sets[parity]
        for c in range(n_chunks):
            up_proj(f, dst, c * chunk, (c + 1) * chunk)
            out_rows = slice(c * chunk, (c + 1) * chunk)
            acc_scr[out_rows, :] += down_proj(f - 1, src, c * chunk, chunk)

    n_slabs = D_MODEL // LANES
    g = gpre_ref[...]
    hh = _rms(halo_ref[...], g)
    hh = jnp.where(i % tiles_per_seq == 0, jnp.zeros_like(hh), hh)
    hx = _rms(x_ref[...], g)
    for s in range(n_slabs):
        rows_scr[s, 0:halo, :] = hh[:, s * LANES:(s + 1) * LANES]
        rows_scr[s, halo:, :] = hx[:, s * LANES:(s + 1) * LANES]
    for gg in range(0, n_groups, 2):
        pair = jnp.concatenate(
            [jnp.concatenate([rows_scr[s, group_rows(gg + d), :] for s in range(n_slabs)], axis=1)
             for d in range(2)], axis=0)
        h_scr[gg * SUBLANES:(gg + 2) * SUBLANES, :] = pair.astype(BF)
    acc_scr[...] = jnp.zeros_like(acc_scr)
    up_proj(0, u_sets[0], 0, n_rows)

    def pair_of_tiles(k, carry):
        stage(2 * k + 1, 1)
        stage(2 * k + 2, 0)
        return carry

    assert n_tiles % 2 == 1
    for k in range((n_tiles - 1) // 2):
        pair_of_tiles(k, 0)
    gpost = gpost_ref[...]
    for c in range(n_chunks):
        y = acc_scr[c * chunk:(c + 1) * chunk, :] + down_proj(n_tiles - 1, u_sets[0], c * chunk, chunk)
        normed = _rms(y, gpost)
        for gg in range(chunk // SUBLANES):
            g_abs = c * (chunk // SUBLANES) + gg
            for s in range(n_slabs):
                rows_scr[s, group_rows(g_abs), :] = normed[gg * SUBLANES:(gg + 1) * SUBLANES,
                                                           s * LANES:(s + 1) * LANES]
    for s in range(n_slabs):
        lanes = slice(s * LANES, (s + 1) * LANES)
        o_ref[:, lanes] = x_ref[:, lanes] + rows_scr[s, halo:, :]


def _ffn(x1, g_pre, w_up, conv_w, conv_b, w_down, g_post, *, seq, tm, tf):
    tokens = x1.shape[0]
    halo = BF16_ROWS
    per = tm // halo
    resident = lambda shape: pl.BlockSpec(shape, lambda i: (0, 0),
                                          pipeline_mode=pl.Buffered(1))
    u_buf = pltpu.VMEM((tm + halo, tf), F32)
    return pl.pallas_call(
        functools.partial(_ffn_kernel, tm=tm, tf=tf, tiles_per_seq=seq // tm,
                          n_chunks=FFN_ROW_CHUNKS),
        grid=(tokens // tm,),
        in_specs=[
            pl.BlockSpec((tm, D_MODEL), lambda i: (i, 0)),
            pl.BlockSpec((halo, D_MODEL), lambda i: (jnp.maximum(i * per - 1, 0), 0)),
            resident((1, D_MODEL)),
            resident((D_MODEL, 2 * D_FF)),
            resident((CONV_WIDTH, 2 * D_FF)),
            resident((1, 2 * D_FF)),
            resident((D_FF, D_MODEL)),
            resident((1, D_MODEL)),
        ],
        out_specs=pl.BlockSpec((tm, D_MODEL), lambda i: (i, 0)),
        out_shape=jax.ShapeDtypeStruct((tokens, D_MODEL), F32),
        scratch_shapes=[
            pltpu.VMEM((tm + halo, D_MODEL), BF),
            u_buf, u_buf, u_buf, u_buf,
            pltpu.VMEM((tm + halo, D_MODEL), F32),
            pltpu.VMEM((D_MODEL // LANES, tm + halo, LANES), F32),
        ],
        compiler_params=_cparams(("arbitrary",)),
        name="ffn",
    )(x1, x1, g_pre, w_up, conv_w, conv_b, w_down, g_post)


def _rope_freq_lanes():
    inv_freq = ROPE_THETA ** (-jnp.arange(ROPE_HALF, dtype=F32) * 2.0 / ROPE_DIM)
    return jnp.tile(inv_freq, LANES // ROPE_HALF).reshape(1, LANES)


def kernel(x, g_pre_mix, w_in, b_forget, w_o_fox, w_o_dil, w_out, g_post_mix,
           g_pre_ffn, w_up, conv_w, conv_b, w_down, g_post_ffn):
    batch, seq, d_model = x.shape
    assert d_model == D_MODEL and seq % max(MIX_TOKENS, FFN_TOKENS, FOX_TILE, DIL_CHUNK) == 0
    depth = w_in.shape[0]
    freq_lanes = _rope_freq_lanes()
    x2d = x.reshape(batch * seq, D_MODEL)
    row = lambda v: v.reshape(1, -1)
    for l in range(depth):
        bias_lanes = jnp.pad(b_forget[l], (0, LANES - N_HEADS)).reshape(1, LANES)

        z, fa, zd4, zd16 = _in_proj(x2d, row(g_pre_mix[l]), w_in[l].astype(BF), freq_lanes,
                                    batch=batch, seq=seq, tm=DIL_CHUNK)
        gq, gk = _forget_scan(fa, bias_lanes, batch=batch, seq=seq, chunk=SCAN_CHUNK)
        ya = _fox_attention(z, gq, gk, batch=batch, seq=seq, tq=FOX_TILE)
        yb = _dilated_attention(z, zd4, zd16, batch=batch, seq=seq)
        x2d = _mix(ya, yb, z, x2d, w_o_fox[l].astype(BF), w_o_dil[l].astype(BF),
                   w_out[l].astype(BF), row(g_post_mix[l]), tm=MIX_TOKENS)
        x2d = _ffn(x2d, row(g_pre_ffn[l]), w_up[l].astype(BF), conv_w[l],
                   row(conv_b[l]), w_down[l].astype(BF), row(g_post_ffn[l]),
                   seq=seq, tm=FFN_TOKENS, tf=FFN_HIDDEN_TILE)
    return x2d.reshape(batch, seq, D_MODEL)
```

```python
import functools
import math

import numpy as np
import jax
import jax.numpy as jnp
from jax import lax
from jax.experimental import pallas as pl
from jax.experimental.pallas import tpu as pltpu

D_MODEL = 1024
HEAD_DIM = 64
N_HEADS = 8
ATT_WIDTH = N_HEADS * HEAD_DIM
N_HEAD_PAIRS = N_HEADS // 2
DIL_PATTERNS = ((128, 1), (512, 4), (2048, 16))
ROPE_DIM = HEAD_DIM // 4
ROPE_HALF = ROPE_DIM // 2
ROPE_THETA = 500000.0
D_FF = 2816
CONV_WIDTH = 3
RMS_EPS = 1e-6
NEG_INF = -1e30
Q_SCALE = 1.0 / math.sqrt(HEAD_DIM)
LOG2_E = math.log2(math.e)
UNDERFLOW_BITS = 150.0
SCORE_MARGIN = 1.0
NORM_SLACK = 1.0 + 2.0 ** -7

LANES = 128
SUBLANES = 8
BF16_ROWS = 16
Z_WIDTH = 3 * ATT_WIDTH * 2 + 2 * D_MODEL
Z_BLK = 512
DIL_SECTION = 3
DIL_CHUNK = 512
N_PIECES = 3
VMEM_LIMIT = 56 * 1024 * 1024

SCAN_CHUNK = 512
FOX_TILE = 512
DIL_MERGE_ROWS = 256
MIX_TOKENS = 1024
FFN_TOKENS = 1024
FFN_HIDDEN_TILE = 256
FFN_ROW_CHUNKS = 5

BF = jnp.bfloat16
F32 = jnp.float32


def _cparams(sem, flags=None):
    return pltpu.CompilerParams(dimension_semantics=sem, vmem_limit_bytes=VMEM_LIMIT,
                                flags=flags)


def _rms(xf, g):
    inv = lax.rsqrt(jnp.mean(xf * xf, axis=-1, keepdims=True) + RMS_EPS)
    return xf * inv * g


def _split3(x):
    hi = x.astype(BF)
    r1 = x - hi.astype(F32)
    mid = r1.astype(BF)
    lo = (r1 - mid.astype(F32)).astype(BF)
    return hi, mid, lo


def _dot(a, b):
    return jnp.dot(a, b, preferred_element_type=F32)


def _dot_nt(a, b):
    return lax.dot_general(a, b, (((1,), (1,)), ((), ())), preferred_element_type=F32)


def _in_proj_kernel(x_ref, g_ref, w_raw_ref, freq_ref, z_ref, fa_ref, zd4_ref, zd16_ref,
                    w_ref, h_scr, cos_scr, sneg_scr, spos_scr, stage_scr, stage4_scr, *, tm):
    p = pl.program_id(0)
    b = pl.program_id(1)
    zd_refs = {4: zd4_ref, 16: zd16_ref}
    n_slabs = Z_BLK // LANES

    @pl.when((p == 0) & (b == 0))
    def _regroup_weight():
        fox_end = 3 * ATT_WIDTH
        raw_width = w_raw_ref.shape[1]
        first = fox_end // LANES

        def group(i):
            lo = i * LANES
            return w_raw_ref[:, lo:min(lo + LANES, raw_width)].astype(F32)

        w_ref[:, 0:fox_end] = w_raw_ref[:, 0:fox_end]
        for k in range((Z_WIDTH - fox_end) // LANES):
            shifted = jnp.concatenate(
                [group(first + k)[:, N_HEADS:], group(first + k + 1)[:, :N_HEADS]], axis=1)
            w_ref[:, fox_end + k * LANES:fox_end + (k + 1) * LANES] = shifted.astype(BF)
        w_ref[:, Z_WIDTH:Z_WIDTH + LANES] = jnp.concatenate(
            [group(first)[:, :N_HEADS], jnp.zeros((D_MODEL, LANES - N_HEADS), F32)],
            axis=1).astype(BF)

    @pl.when(b == 0)
    def _tables():
        pos = (p * tm + lax.broadcasted_iota(jnp.int32, (tm, LANES), 0)).astype(F32)
        lane = lax.broadcasted_iota(jnp.int32, (tm, LANES), 1)
        c = lane % HEAD_DIM
        ang = pos * freq_ref[...]
        cs = jnp.cos(ang)
        sn = jnp.sin(ang)
        cos_scr[...] = jnp.where(c < ROPE_DIM, cs, 1.0)
        sneg_scr[...] = jnp.where(c < ROPE_HALF, -sn, 0.0)
        spos_scr[...] = jnp.where((c >= ROPE_HALF) & (c < ROPE_DIM), sn, 0.0)

    def rope(t):
        up = pltpu.roll(t, LANES - ROPE_HALF, 1)
        dn = pltpu.roll(t, ROPE_HALF, 1)
        return t * cos_scr[...] + up * sneg_scr[...] + dn * spos_scr[...]

    def emit_dilated(section, slab, val):
        out_lanes = slice(section * Z_BLK + slab * LANES, section * Z_BLK + (slab + 1) * LANES)
        z_ref[:, (DIL_SECTION + section) * Z_BLK + slab * LANES:
              (DIL_SECTION + section) * Z_BLK + (slab + 1) * LANES] = val.astype(BF)
        stage, stage4 = stage_scr.at[section, slab], stage4_scr.at[section, slab]
        stage[...] = val
        seg4 = tm // 4
        for c in range(4):
            part = stage[pl.ds(c, seg4, stride=4), :]
            stage4[c * seg4:(c + 1) * seg4, :] = part
            zd_refs[4][c * seg4:(c + 1) * seg4, out_lanes] = part.astype(BF)
        seg16 = tm // 16
        for jj in range(16):
            c, a = jj % 4, jj // 4
            part = stage4[pl.ds(c * seg4 + a, seg16, stride=4), :]
            zd_refs[16][jj * seg16:(jj + 1) * seg16, out_lanes] = part.astype(BF)

    h = _rms(x_ref[...], g_ref[...]).astype(BF)
    h_scr[...] = h
    fa_ref[...] = _dot(h, w_ref[:, Z_WIDTH:Z_WIDTH + LANES])

    for j in range(Z_WIDTH // Z_BLK):
        cols = slice(j * Z_BLK, (j + 1) * Z_BLK)
        acc = _dot(h_scr[...], w_ref[:, cols])
        section = j - DIL_SECTION
        if j == 0:
            z_ref[:, cols] = (acc * (Q_SCALE * LOG2_E)).astype(BF)
        elif 0 <= section < 3:
            for slab in range(n_slabs):
                t = acc[:, slab * LANES:(slab + 1) * LANES]
                if section == 0:
                    t = rope(t) * (Q_SCALE * LOG2_E)
                elif section == 1:
                    t = rope(t)
                emit_dilated(section, slab, t)
        else:
            z_ref[:, cols] = acc.astype(BF)


def _in_proj(x2d, g, w_raw, freq_lanes, *, batch, seq, tm):
    n_p = seq // tm
    tokens = batch * seq
    row = lambda p, b: (b * n_p + p, 0)
    resident = lambda shape: pl.BlockSpec(shape, lambda p, b: (0, 0),
                                          pipeline_mode=pl.Buffered(1))
    dil_shape = jax.ShapeDtypeStruct((tokens, 3 * ATT_WIDTH), BF)
    stage_buf = pltpu.VMEM((3, Z_BLK // LANES, tm, LANES), F32)
    return pl.pallas_call(
        functools.partial(_in_proj_kernel, tm=tm),
        grid=(n_p, batch),
        in_specs=[
            pl.BlockSpec((tm, D_MODEL), row),
            resident((1, D_MODEL)),
            resident((D_MODEL, Z_WIDTH + N_HEADS)),
            resident((1, LANES)),
        ],
        out_specs=[
            pl.BlockSpec((tm, Z_WIDTH), row),
            pl.BlockSpec((tm, LANES), row),
            pl.BlockSpec((tm, 3 * ATT_WIDTH), row),
            pl.BlockSpec((tm, 3 * ATT_WIDTH), row),
        ],
        out_shape=[
            jax.ShapeDtypeStruct((tokens, Z_WIDTH), BF),
            jax.ShapeDtypeStruct((tokens, LANES), F32),
            dil_shape,
            dil_shape,
        ],
        scratch_shapes=[
            pltpu.VMEM((D_MODEL, Z_WIDTH + LANES), BF),
            pltpu.VMEM((tm, D_MODEL), BF),
            pltpu.VMEM((tm, LANES), F32),
            pltpu.VMEM((tm, LANES), F32),
            pltpu.VMEM((tm, LANES), F32),
            stage_buf,
            stage_buf,
        ],
        compiler_params=_cparams(("arbitrary", "arbitrary")),
        name="in_proj",
    )(x2d, g, w_raw, freq_lanes)


def _extras_base(head):
    return (HEAD_DIM if head % 2 == 0 else 0) + 2 * N_PIECES * (head // 2)


def _forget_scan_kernel(fa_ref, bias_ref, tri_ref, pq_ref, pk_ref, oq_ref, ok_ref,
                        gq_ref, gk_ref):
    chunk = tri_ref.shape[0]
    n_chunks = fa_ref.shape[0] // chunk
    t = fa_ref[...] + bias_ref[...]
    log_f = jnp.minimum(t, 0.0) - jnp.log1p(jnp.exp(-jnp.abs(t)))
    wide = jnp.concatenate(
        [log_f[c * chunk:(c + 1) * chunk, :] for c in range(n_chunks)], axis=1)
    tri = tri_ref[...]
    local = sum(_dot(tri, piece) for piece in _split3(wide))
    carry = jnp.zeros((1, LANES), F32)
    parts = []
    for c in range(n_chunks):
        part = local[:, c * LANES:(c + 1) * LANES] + carry
        parts.append(part)
        carry = part[chunk - 1:chunk, :]
    run = jnp.concatenate(parts, axis=0)
    gq = oq_ref[...].astype(F32)
    gk = ok_ref[...].astype(F32)
    for i, piece in enumerate(_split3(run * LOG2_E)):
        gq = gq + _dot(piece, pq_ref[i])
        gk = gk - _dot(piece, pk_ref[i])
    gq_ref[0] = gq.astype(BF)
    gk_ref[0] = gk.astype(BF)


def _forget_scan_constants(chunk):
    tri = np.tril(np.ones((chunk, chunk), np.float32))
    pq = np.zeros((N_PIECES, LANES, LANES), np.float32)
    pk = np.zeros((N_PIECES, LANES, LANES), np.float32)
    oq = np.zeros((1, LANES), np.float32)
    ok = np.zeros((1, LANES), np.float32)
    for h in range(N_HEADS):
        base = _extras_base(h)
        for i in range(N_PIECES):
            pq[i, h, base + i] = 1.0
            oq[0, base + N_PIECES + i] = 1.0
            ok[0, base + i] = 1.0
            pk[i, h, base + N_PIECES + i] = 1.0
    as_bf = lambda a: jnp.asarray(a, BF)
    return as_bf(tri), as_bf(pq), as_bf(pk), as_bf(oq), as_bf(ok)


def _forget_scan(fa, bias_lanes, *, batch, seq, chunk):
    tri, pq, pk, oq, ok = _forget_scan_constants(chunk)
    const2 = lambda b: (0, 0)
    const3 = lambda b: (0, 0, 0)
    return pl.pallas_call(
        _forget_scan_kernel,
        grid=(batch,),
        in_specs=[
            pl.BlockSpec((seq, LANES), lambda b: (b, 0)),
            pl.BlockSpec((1, LANES), const2),
            pl.BlockSpec((chunk, chunk), const2),
            pl.BlockSpec((N_PIECES, LANES, LANES), const3),
            pl.BlockSpec((N_PIECES, LANES, LANES), const3),
            pl.BlockSpec((1, LANES), const2),
            pl.BlockSpec((1, LANES), const2),
        ],
        out_specs=[
            pl.BlockSpec((1, seq, LANES), lambda b: (b, 0, 0)),
            pl.BlockSpec((1, seq, LANES), lambda b: (b, 0, 0)),
        ],
        out_shape=[
            jax.ShapeDtypeStruct((batch, seq, LANES), BF),
            jax.ShapeDtypeStruct((batch, seq, LANES), BF),
        ],
        compiler_params=_cparams(("arbitrary",)),
        name="forget_scan",
    )(fa, bias_lanes, tri, pq, pk, oq, ok)


def _with_bias_lanes(slab, extras, hp, parity):
    lane = lax.broadcasted_iota(jnp.int32, slab.shape, 1)
    own = (lane < HEAD_DIM) if parity == 0 else (lane >= HEAD_DIM)
    base = (HEAD_DIM if parity == 0 else 0) + 2 * N_PIECES * hp
    in_extras = (lane >= base) & (lane < base + 2 * N_PIECES)
    return jnp.where(own, slab, jnp.where(in_extras, extras, jnp.zeros_like(extras)))


def _fox_kernel(q_ref, k_ref, v_ref, gq_ref, gk_ref, o_ref,
                ka_scr, kb_scr, vta_scr, vtb_scr, qp_scr, bias_scr, s_scr, acc_scr, ot_scr,
                bound_smem, *, tq):
    hp = pl.program_id(1)
    k_scrs = (ka_scr, kb_scr)
    vt_scrs = (vta_scr, vtb_scr)
    seq = k_ref.shape[0]
    n_q = seq // tq
    lane_row = lax.broadcasted_iota(jnp.int32, (1, LANES), 1)

    sel_row = lax.broadcasted_iota(jnp.int32, (LANES, LANES), 0)
    sel_col = lax.broadcasted_iota(jnp.int32, (LANES, LANES), 1)
    head_sum = jnp.where(sel_row // HEAD_DIM == sel_col, 1.0, 0.0).astype(BF)

    def max_row_norm(slab, head):
        lane = lax.broadcasted_iota(jnp.int32, slab.shape, 1)
        own = (lane < HEAD_DIM) if head == 0 else (lane >= HEAD_DIM)
        x = jnp.where(own, slab.astype(F32), 0.0)
        sq = jnp.sum(x * x, axis=1, keepdims=True)
        return jnp.sqrt(jnp.max(sq, axis=0, keepdims=True))

    def max_row_norms(slab):
        x = slab.astype(F32)
        per_row = _dot((x * x).astype(BF), head_sum)
        return jnp.sqrt(jnp.max(per_row, axis=0, keepdims=True) * NORM_SLACK)

    def bias_lane_sum(row, head, first):
        base = (HEAD_DIM if head == 0 else 0) + 2 * N_PIECES * hp + first
        picked = (lane_row >= base) & (lane_row < base + N_PIECES)
        return jnp.sum(jnp.where(picked, row, 0.0), axis=1, keepdims=True)

    k2 = k_ref[...]
    gk = gk_ref[0]
    k_norms = max_row_norms(k2)
    for head in range(2):
        k_scrs[head][...] = _with_bias_lanes(k2, gk, hp, head)
        bound_smem[head, n_q] = k_norms[0, head]
        for j in range(n_q):
            rows16 = gk_ref[0, (j + 1) * tq - BF16_ROWS:(j + 1) * tq, :].astype(F32)
            last = rows16[BF16_ROWS - 1:BF16_ROWS, :]
            bound_smem[head, j] = -bias_lane_sum(last, head, N_PIECES)[0, 0]
    vt = v_ref[...].T
    ones = jnp.ones((BF16_ROWS, seq), BF)
    for head, vt_scr in enumerate(vt_scrs):
        vt_scr[0:HEAD_DIM, :] = vt[head * HEAD_DIM:(head + 1) * HEAD_DIM, :]
        vt_scr[HEAD_DIM:, :] = ones
    row = lax.broadcasted_iota(jnp.int32, (tq, tq), 0)
    col = lax.broadcasted_iota(jnp.int32, (tq, tq), 1)
    bias_scr[...] = jnp.where(row <= col, 0.0, NEG_INF)
    acc_scr[...] = jnp.zeros_like(acc_scr)

    def scores(head, tile, masked):
        kt = k_scrs[head][pl.ds(pl.multiple_of(tile * tq, tq), tq), :]
        st = _dot(kt, qp_scr[head])
        if masked:
            st = st + bias_scr[...]
        s_scr[head] = st
        return jnp.max(st, axis=0, keepdims=True)

    def softmax_pv(head, st, tile, tile_max, m):
        start = pl.multiple_of(tile * tq, tq)
        m_new = jnp.maximum(m, tile_max)
        alpha = jnp.exp2(m - m_new)
        pt = jnp.exp2(st - m_new).astype(BF)
        vt_tile = vt_scrs[head][:, pl.ds(start, tq)]
        acc_scr[head] = alpha * acc_scr[head] + _dot(vt_tile, pt)
        return m_new

    def start_tile(qi):
        rows = pl.ds(pl.multiple_of(qi * tq, tq), tq)
        q2 = q_ref[rows, :]
        gq = gq_ref[0, rows, :]
        first_row = gq[0:BF16_ROWS, :].astype(F32)[0:1, :]
        maxes, slack = [], []
        for head in range(2):
            qp_scr[head] = _with_bias_lanes(q2, gq, hp, head).T
            maxes.append(scores(head, qi, True))
            reach = (max_row_norm(q2, head) * bound_smem[head, n_q]
                     + bias_lane_sum(first_row, head, 0) + SCORE_MARGIN)
            lowest_max = jnp.min(maxes[head], axis=1, keepdims=True)
            slack.append(reach - (lowest_max - UNDERFLOW_BITS))
        count = jnp.zeros((1, 1), jnp.int32)
        for j in range(n_q - 1):
            needed = (((slack[0] >= bound_smem[0, j]) | (slack[1] >= bound_smem[1, j]))
                      & (j < qi))
            count = count + jnp.where(needed, 1, 0)
        return maxes[0], maxes[1], count[0, 0]

    def finish_tile(qi):
        for head in range(2):
            acc = acc_scr[head]
            ot_scr[head * HEAD_DIM:(head + 1) * HEAD_DIM, :] = (
                acc[0:HEAD_DIM, :] / acc[HEAD_DIM:HEAD_DIM + 1, :])
        o_ref[pl.ds(pl.multiple_of(qi * tq, tq), tq), :] = ot_scr[...].T.astype(BF)
        acc_scr[...] = jnp.zeros_like(acc_scr)

    m_init = jnp.full((1, tq), NEG_INF, F32)

    def query_tile(qi, carry):
        first_a, first_b, n_needed = carry

        def key_tile(t, inner):
            max_a, max_b, m_a, m_b, prev_tile = inner
            prev = [s_scr[head] for head in range(2)]
            tile = qi - 1 - t
            next_max = [scores(head, tile, False) for head in range(2)]
            m_a = softmax_pv(0, prev[0], prev_tile, max_a, m_a)
            m_b = softmax_pv(1, prev[1], prev_tile, max_b, m_b)
            return next_max[0], next_max[1], m_a, m_b, tile

        def two_key_tiles(u, inner):
            return key_tile(2 * u + 1, key_tile(2 * u, inner))

        inner = lax.fori_loop(0, n_needed // 2, two_key_tiles,
                              (first_a, first_b, m_init, m_init, qi))
        max_a, max_b, m_a, m_b, prev_tile = lax.cond(
            n_needed % 2 == 1, lambda c: key_tile(n_needed - 1, c), lambda c: c, inner)
        last = [s_scr[head] for head in range(2)]
        softmax_pv(0, last[0], prev_tile, max_a, m_a)
        softmax_pv(1, last[1], prev_tile, max_b, m_b)
        following = start_tile(jnp.minimum(qi + 1, n_q - 1))
        finish_tile(qi)
        return following

    lax.fori_loop(0, n_q, query_tile, start_tile(0))


def _fox_attention(z, gq, gk, *, batch, seq, tq):
    tokens = batch * seq
    col = lambda section: pl.BlockSpec(
        (seq, LANES), lambda b, hp: (b, section * N_HEAD_PAIRS + hp))
    whole = pl.BlockSpec((1, seq, LANES), lambda b, hp: (b, 0, 0))
    ext = HEAD_DIM + BF16_ROWS
    return pl.pallas_call(
        functools.partial(_fox_kernel, tq=tq),
        grid=(batch, N_HEAD_PAIRS),
        in_specs=[col(0), col(1), col(2), whole, whole],
        out_specs=pl.BlockSpec((seq, LANES), lambda b, hp: (b, hp)),
        out_shape=jax.ShapeDtypeStruct((tokens, ATT_WIDTH), BF),
        scratch_shapes=[
            pltpu.VMEM((seq, LANES), BF),
            pltpu.VMEM((seq, LANES), BF),
            pltpu.VMEM((ext, seq), BF),
            pltpu.VMEM((ext, seq), BF),
            pltpu.VMEM((2, LANES, tq), BF),
            pltpu.VMEM((tq, tq), F32),
            pltpu.VMEM((2, tq, tq), F32),
            pltpu.VMEM((2, ext, tq), F32),
            pltpu.VMEM((LANES, tq), F32),
            pltpu.SMEM((2, seq // tq + 1), F32),
        ],
        compiler_params=_cparams(("arbitrary", "arbitrary")),
        name="fox_attn",
    )(z, z, z, gq, gk)


def _log2(n):
    assert n > 0 and n & (n - 1) == 0, n
    return n.bit_length() - 1


def _dilated_kernel(q1, k1, v1, q4, k4, v4, q16, k16, v16, o_ref,
                    num1, num4, num16, den1, den4, den16, max1, max4, max16, bias_scr,
                    *, seq, blk, chunk, group, merge_rows):
    sources = ((q1, k1, v1), (q4, k4, v4), (q16, k16, v16))
    num_scrs = (num1, num4, num16)
    den_scrs = (den1, den4, den16)
    max_scrs = (max1, max4, max16)
    n_blocks = seq // blk

    low = lax.broadcasted_iota(jnp.int32, (blk, LANES), 1) < HEAD_DIM
    qrow = lax.broadcasted_iota(jnp.int32, (2 * blk, 2 * blk), 0) % blk
    kcol = lax.broadcasted_iota(jnp.int32, (2 * blk, 2 * blk), 1)
    dist = qrow + blk - kcol
    band = (dist >= 0) & (dist <= blk)
    bias_scr[0] = jnp.where(band & (kcol >= blk), 0.0, NEG_INF)
    bias_scr[1] = jnp.where(band, 0.0, NEG_INF)
    ones = jnp.ones((2 * blk, LANES), BF)

    for idx, (window, dilation) in enumerate(DIL_PATTERNS):
        assert window // dilation == blk
        q_ref, k_ref, v_ref = sources[idx]
        num_scr, den_scr, max_scr = num_scrs[idx], den_scrs[idx], max_scrs[idx]
        sub_shift = _log2(seq // dilation // blk)
        seg_rows = chunk // dilation
        seg = min(blk, seg_rows)

        def load(ref, j, l0):
            parts = []
            for s in range(blk // seg):
                l = l0 + s * seg
                p = lax.shift_right_logical(l, _log2(seg_rows))
                i = l & (seg_rows - 1)
                start = pl.multiple_of(p * chunk + j * seg_rows + i, seg)
                parts.append(ref[pl.ds(start, seg), :])
            return parts[0] if len(parts) == 1 else jnp.concatenate(parts, axis=0)

        def one_block(g):
            j = lax.shift_right_logical(g, sub_shift)
            gs = g & ((1 << sub_shift) - 1)
            l0 = gs * blk
            lp = jnp.maximum(l0 - blk, 0)
            q2 = load(q_ref, j, l0)
            zero = jnp.zeros_like(q2)
            qq = jnp.concatenate([jnp.where(low, q2, zero), jnp.where(low, zero, q2)], axis=0)
            kwin = jnp.concatenate([load(k_ref, j, lp), load(k_ref, j, l0)], axis=0)
            vwin = jnp.concatenate([load(v_ref, j, lp), load(v_ref, j, l0)], axis=0)
            s = _dot_nt(qq, kwin) + bias_scr[jnp.minimum(gs, 1)]
            m = jnp.max(s, axis=1, keepdims=True)
            p = jnp.exp2(s - m).astype(BF)
            pv = _dot(p, jnp.concatenate([vwin, ones], axis=1))
            if dilation == 1:
                dst = pl.ds(pl.multiple_of(l0, blk), blk)
            else:
                dst = pl.ds(l0 * dilation + j, blk, stride=dilation)
            num_scr[dst, :] = jnp.where(low, pv[:blk, :LANES], pv[blk:, :LANES])
            den_scr[dst, :] = jnp.where(low, pv[:blk, LANES:], pv[blk:, LANES:])
            max_scr[dst, :] = jnp.where(low, jnp.broadcast_to(m[:blk], (blk, LANES)),
                                        jnp.broadcast_to(m[blk:], (blk, LANES)))

        def blocks(it, carry):
            for u in range(group):
                one_block(it * group + u)
            return carry

        lax.fori_loop(0, n_blocks // group, blocks, 0)

    def merge(c, carry):
        rows = pl.ds(c * merge_rows, merge_rows)
        maxes = [max_scr[rows, :] for max_scr in max_scrs]
        top = functools.reduce(jnp.maximum, maxes)
        weights = [jnp.exp2(m - top) for m in maxes]
        num = sum(w * num_scr[rows, :] for w, num_scr in zip(weights, num_scrs))
        den = sum(w * den_scr[rows, :] for w, den_scr in zip(weights, den_scrs))
        o_ref[rows, :] = (num / den).astype(BF)
        return carry

    for c in range(seq // merge_rows):
        merge(c, 0)


def _dilated_attention(z, zd4, zd16, *, batch, seq):
    blk = DIL_PATTERNS[0][0] // DIL_PATTERNS[0][1]
    tokens = batch * seq
    per_section = ATT_WIDTH // LANES
    col = lambda section: pl.BlockSpec(
        (seq, LANES), lambda b, hp: (b, section * per_section + hp))
    f32_buf = pltpu.VMEM((seq, LANES), F32)
    return pl.pallas_call(
        functools.partial(_dilated_kernel, seq=seq, blk=blk, chunk=DIL_CHUNK,
                          group=seq // blk, merge_rows=DIL_MERGE_ROWS),
        grid=(batch, N_HEAD_PAIRS),
        in_specs=[col(DIL_SECTION), col(DIL_SECTION + 1), col(DIL_SECTION + 2),
                  col(0), col(1), col(2), col(0), col(1), col(2)],
        out_specs=pl.BlockSpec((seq, LANES), lambda b, hp: (b, hp)),
        out_shape=jax.ShapeDtypeStruct((tokens, ATT_WIDTH), BF),
        scratch_shapes=[f32_buf] * 9 + [pltpu.VMEM((2, 2 * blk, 2 * blk), F32)],
        compiler_params=_cparams(("arbitrary", "arbitrary")),
        name="dilated_attn",
    )(z, z, z, zd4, zd4, zd4, zd16, zd16, zd16)


def _mix_kernel(ya_ref, yb_ref, ga_ref, gb_ref, x_ref, woa_ref, wob_ref, wout_ref,
                g_ref, o_ref):
    pa = _dot(ya_ref[...], woa_ref[...])
    pb = _dot(yb_ref[...], wob_ref[...])
    mixed = (jax.nn.sigmoid(ga_ref[...].astype(F32)) * pa
             + jax.nn.sigmoid(gb_ref[...].astype(F32)) * pb)
    y = _dot(mixed.astype(BF), wout_ref[...])
    o_ref[...] = x_ref[...] + _rms(y, g_ref[...])


def _mix(ya, yb, z, x2d, woa, wob, wout, g, *, tm):
    tokens = x2d.shape[0]
    gate_blk = lambda off: pl.BlockSpec((tm, D_MODEL), lambda i: (i, off))
    const = lambda i: (0, 0)
    return pl.pallas_call(
        _mix_kernel,
        grid=(tokens // tm,),
        in_specs=[
            pl.BlockSpec((tm, ATT_WIDTH), lambda i: (i, 0)),
            pl.BlockSpec((tm, ATT_WIDTH), lambda i: (i, 0)),
            gate_blk(3),
            gate_blk(4),
            pl.BlockSpec((tm, D_MODEL), lambda i: (i, 0)),
            pl.BlockSpec((ATT_WIDTH, D_MODEL), const),
            pl.BlockSpec((ATT_WIDTH, D_MODEL), const),
            pl.BlockSpec((D_MODEL, D_MODEL), const),
            pl.BlockSpec((1, D_MODEL), const),
        ],
        out_specs=pl.BlockSpec((tm, D_MODEL), lambda i: (i, 0)),
        out_shape=jax.ShapeDtypeStruct((tokens, D_MODEL), F32),
        compiler_params=_cparams(("arbitrary",)),
        name="mix",
    )(ya, yb, z, z, x2d, woa, wob, wout, g)


def _ffn_kernel(x_ref, halo_ref, gpre_ref, wup_ref, cw_ref, cb_ref, wd_ref, gpost_ref, o_ref,
                h_scr, ua0_scr, ub0_scr, ua1_scr, ub1_scr, acc_scr, rows_scr,
                *, tm, tf, tiles_per_seq, n_chunks):
    i = pl.program_id(0)
    halo = BF16_ROWS
    n_rows = tm + halo
    n_groups = n_rows // SUBLANES
    chunk = n_rows // n_chunks
    assert n_groups * SUBLANES == n_rows and chunk * n_chunks == n_rows
    assert chunk % BF16_ROWS == 0 and n_groups % 2 == 0
    n_tiles = D_FF // tf
    u_sets = ((ua0_scr, ub0_scr), (ua1_scr, ub1_scr))

    def group_rows(g):
        return pl.ds(g, SUBLANES, stride=n_groups)

    def cols(f, gate):
        return pl.ds(pl.multiple_of(f * tf + (D_FF if gate else 0), LANES), tf)

    def up_proj(f, dst, lo, hi):
        h = h_scr[lo:hi, :]
        dst[0][lo:hi, :] = _dot(h, wup_ref[:, cols(f, False)])
        dst[1][lo:hi, :] = _dot(h, wup_ref[:, cols(f, True)])

    def tap_rows(u_scr, back, first, n):
        lo = first - back * SUBLANES
        if lo >= 0:
            return u_scr[lo:lo + n, :]
        wrapped = [pltpu.roll(u_scr[n_rows + k * SUBLANES:n_rows + (k + 1) * SUBLANES, :], 1, 0)
                   for k in range(lo // SUBLANES, 0)]
        return jnp.concatenate(wrapped + [u_scr[0:n + lo, :]], axis=0)

    def conv(u_scr, window, post_scale, first, n):
        out = cb_ref[:, window] * post_scale
        for tap in range(CONV_WIDTH):
            back = CONV_WIDTH - 1 - tap
            out = out + (cw_ref[tap:tap + 1, window] * post_scale) * tap_rows(u_scr, back, first, n)
        return out

    def down_proj(f, src, first, n):
        a = conv(src[0], cols(f, False), 1.0, first, n)
        half_b = conv(src[1], cols(f, True), 0.5, first, n)
        c0 = math.sqrt(2.0 / math.pi)
        inner = a * (c0 + (c0 * 0.044715) * (a * a))
        hidden = ((a * half_b) * (1.0 + jnp.tanh(inner))).astype(BF)
        return _dot(hidden, wd_ref[pl.ds(pl.multiple_of(f * tf, tf), tf), :])

    def stage(f, parity):
        src, dst = u_sets[1 - parity], u_sets[parity]
        for c in range(n_chunks):
            up_proj(f, dst, c * chunk, (c + 1) * chunk)
            out_rows = slice(c * chunk, (c + 1) * chunk)
            acc_scr[out_rows, :] += down_proj(f - 1, src, c * chunk, chunk)

    n_slabs = D_MODEL // LANES
    g = gpre_ref[...]
    hh = _rms(halo_ref[...], g)
    hh = jnp.where(i % tiles_per_seq == 0, jnp.zeros_like(hh), hh)
    hx = _rms(x_ref[...], g)
    for s in range(n_slabs):
        rows_scr[s, 0:halo, :] = hh[:, s * LANES:(s + 1) * LANES]
        rows_scr[s, halo:, :] = hx[:, s * LANES:(s + 1) * LANES]
    for gg in range(0, n_groups, 2):
        pair = jnp.concatenate(
            [jnp.concatenate([rows_scr[s, group_rows(gg + d), :] for s in range(n_slabs)], axis=1)
             for d in range(2)], axis=0)
        h_scr[gg * SUBLANES:(gg + 2) * SUBLANES, :] = pair.astype(BF)
    acc_scr[...] = jnp.zeros_like(acc_scr)
    up_proj(0, u_sets[0], 0, n_rows)

    def pair_of_tiles(k, carry):
        stage(2 * k + 1, 1)
        stage(2 * k + 2, 0)
        return carry

    assert n_tiles % 2 == 1
    for k in range((n_tiles - 1) // 2):
        pair_of_tiles(k, 0)
    gpost = gpost_ref[...]
    for c in range(n_chunks):
        y = acc_scr[c * chunk:(c + 1) * chunk, :] + down_proj(n_tiles - 1, u_sets[0], c * chunk, chunk)
        normed = _rms(y, gpost)
        for gg in range(chunk // SUBLANES):
            g_abs = c * (chunk // SUBLANES) + gg
            for s in range(n_slabs):
                rows_scr[s, group_rows(g_abs), :] = normed[gg * SUBLANES:(gg + 1) * SUBLANES,
                                                           s * LANES:(s + 1) * LANES]
    for s in range(n_slabs):
        lanes = slice(s * LANES, (s + 1) * LANES)
        o_ref[:, lanes] = x_ref[:, lanes] + rows_scr[s, halo:, :]


def _ffn(x1, g_pre, w_up, conv_w, conv_b, w_down, g_post, *, seq, tm, tf):
    tokens = x1.shape[0]
    halo = BF16_ROWS
    per = tm // halo
    resident = lambda shape: pl.BlockSpec(shape, lambda i: (0, 0),
                                          pipeline_mode=pl.Buffered(1))
    u_buf = pltpu.VMEM((tm + halo, tf), F32)
    return pl.pallas_call(
        functools.partial(_ffn_kernel, tm=tm, tf=tf, tiles_per_seq=seq // tm,
                          n_chunks=FFN_ROW_CHUNKS),
        grid=(tokens // tm,),
        in_specs=[
            pl.BlockSpec((tm, D_MODEL), lambda i: (i, 0)),
            pl.BlockSpec((halo, D_MODEL), lambda i: (jnp.maximum(i * per - 1, 0), 0)),
            resident((1, D_MODEL)),
            resident((D_MODEL, 2 * D_FF)),
            resident((CONV_WIDTH, 2 * D_FF)),
            resident((1, 2 * D_FF)),
            resident((D_FF, D_MODEL)),
            resident((1, D_MODEL)),
        ],
        out_specs=pl.BlockSpec((tm, D_MODEL), lambda i: (i, 0)),
        out_shape=jax.ShapeDtypeStruct((tokens, D_MODEL), F32),
        scratch_shapes=[
            pltpu.VMEM((tm + halo, D_MODEL), BF),
            u_buf, u_buf, u_buf, u_buf,
            pltpu.VMEM((tm + halo, D_MODEL), F32),
            pltpu.VMEM((D_MODEL // LANES, tm + halo, LANES), F32),
        ],
        compiler_params=_cparams(("arbitrary",)),
        name="ffn",
    )(x1, x1, g_pre, w_up, conv_w, conv_b, w_down, g_post)


def _rope_freq_lanes():
    inv_freq = ROPE_THETA ** (-jnp.arange(ROPE_HALF, dtype=F32) * 2.0 / ROPE_DIM)
    return jnp.tile(inv_freq, LANES // ROPE_HALF).reshape(1, LANES)


def kernel(x, g_pre_mix, w_in, b_forget, w_o_fox, w_o_dil, w_out, g_post_mix,
           g_pre_ffn, w_up, conv_w, conv_b, w_down, g_post_ffn):
    batch, seq, d_model = x.shape
    assert d_model == D_MODEL and seq % max(MIX_TOKENS, FFN_TOKENS, FOX_TILE, DIL_CHUNK) == 0
    depth = w_in.shape[0]
    freq_lanes = _rope_freq_lanes()
    x2d = x.reshape(batch * seq, D_MODEL)
    row = lambda v: v.reshape(1, -1)
    for l in range(depth):
        bias_lanes = jnp.pad(b_forget[l], (0, LANES - N_HEADS)).reshape(1, LANES)

        z, fa, zd4, zd16 = _in_proj(x2d, row(g_pre_mix[l]), w_in[l].astype(BF), freq_lanes,
                                    batch=batch, seq=seq, tm=DIL_CHUNK)
        gq, gk = _forget_scan(fa, bias_lanes, batch=batch, seq=seq, chunk=SCAN_CHUNK)
        ya = _fox_attention(z, gq, gk, batch=batch, seq=seq, tq=FOX_TILE)
        yb = _dilated_attention(z, zd4, zd16, batch=batch, seq=seq)
        x2d = _mix(ya, yb, z, x2d, w_o_fox[l].astype(BF), w_o_dil[l].astype(BF),
                   w_out[l].astype(BF), row(g_post_mix[l]), tm=MIX_TOKENS)
        x2d = _ffn(x2d, row(g_pre_ffn[l]), w_up[l].astype(BF), conv_w[l],
                   row(conv_b[l]), w_down[l].astype(BF), row(g_post_ffn[l]),
                   seq=seq, tm=FFN_TOKENS, tf=FFN_HIDDEN_TILE)
    return x2d.reshape(batch, seq, D_MODEL)
```

```python
import functools
import math

import numpy as np
import jax
import jax.numpy as jnp
from jax import lax
from jax.experimental import pallas as pl
from jax.experimental.pallas import tpu as pltpu

D_MODEL = 1024
HEAD_DIM = 64
N_HEADS = 8
ATT_WIDTH = N_HEADS * HEAD_DIM
N_HEAD_PAIRS = N_HEADS // 2
DIL_PATTERNS = ((128, 1), (512, 4), (2048, 16))
ROPE_DIM = HEAD_DIM // 4
ROPE_HALF = ROPE_DIM // 2
ROPE_THETA = 500000.0
D_FF = 2816
CONV_WIDTH = 3
RMS_EPS = 1e-6
NEG_INF = -1e30
Q_SCALE = 1.0 / math.sqrt(HEAD_DIM)
LOG2_E = math.log2(math.e)
UNDERFLOW_BITS = 150.0
SCORE_MARGIN = 1.0
NORM_SLACK = 1.0 + 2.0 ** -7

LANES = 128
SUBLANES = 8
BF16_ROWS = 16
Z_WIDTH = 3 * ATT_WIDTH * 2 + 2 * D_MODEL
Z_BLK = 512
DIL_SECTION = 3
DIL_CHUNK = 512
N_PIECES = 3
VMEM_LIMIT = 56 * 1024 * 1024

SCAN_CHUNK = 512
FOX_TILE = 512
DIL_MERGE_ROWS = 256
MIX_TOKENS = 1024
FFN_TOKENS = 1024
FFN_HIDDEN_TILE = 256
FFN_ROW_CHUNKS = 5

BF = jnp.bfloat16
F32 = jnp.float32


def _cparams(sem, flags=None):
    return pltpu.CompilerParams(dimension_semantics=sem, vmem_limit_bytes=VMEM_LIMIT,
                                flags=flags)


def _rms(xf, g):
    inv = lax.rsqrt(jnp.mean(xf * xf, axis=-1, keepdims=True) + RMS_EPS)
    return xf * inv * g


def _split3(x):
    hi = x.astype(BF)
    r1 = x - hi.astype(F32)
    mid = r1.astype(BF)
    lo = (r1 - mid.astype(F32)).astype(BF)
    return hi, mid, lo


def _dot(a, b):
    return jnp.dot(a, b, preferred_element_type=F32)


def _dot_nt(a, b):
    return lax.dot_general(a, b, (((1,), (1,)), ((), ())), preferred_element_type=F32)


def _in_proj_kernel(x_ref, g_ref, w_raw_ref, freq_ref, z_ref, fa_ref, zd4_ref, zd16_ref,
                    w_ref, h_scr, cos_scr, sneg_scr, spos_scr, stage_scr, stage4_scr, *, tm):
    p = pl.program_id(0)
    b = pl.program_id(1)
    zd_refs = {4: zd4_ref, 16: zd16_ref}
    n_slabs = Z_BLK // LANES

    @pl.when((p == 0) & (b == 0))
    def _regroup_weight():
        fox_end = 3 * ATT_WIDTH
        raw_width = w_raw_ref.shape[1]
        first = fox_end // LANES

        def group(i):
            lo = i * LANES
            return w_raw_ref[:, lo:min(lo + LANES, raw_width)].astype(F32)

        w_ref[:, 0:fox_end] = w_raw_ref[:, 0:fox_end]
        for k in range((Z_WIDTH - fox_end) // LANES):
            shifted = jnp.concatenate(
                [group(first + k)[:, N_HEADS:], group(first + k + 1)[:, :N_HEADS]], axis=1)
            w_ref[:, fox_end + k * LANES:fox_end + (k + 1) * LANES] = shifted.astype(BF)
        w_ref[:, Z_WIDTH:Z_WIDTH + LANES] = jnp.concatenate(
            [group(first)[:, :N_HEADS], jnp.zeros((D_MODEL, LANES - N_HEADS), F32)],
            axis=1).astype(BF)

    @pl.when(b == 0)
    def _tables():
        pos = (p * tm + lax.broadcasted_iota(jnp.int32, (tm, LANES), 0)).astype(F32)
        lane = lax.broadcasted_iota(jnp.int32, (tm, LANES), 1)
        c = lane % HEAD_DIM
        ang = pos * freq_ref[...]
        cs = jnp.cos(ang)
        sn = jnp.sin(ang)
        cos_scr[...] = jnp.where(c < ROPE_DIM, cs, 1.0)
        sneg_scr[...] = jnp.where(c < ROPE_HALF, -sn, 0.0)
        spos_scr[...] = jnp.where((c >= ROPE_HALF) & (c < ROPE_DIM), sn, 0.0)

    def rope(t):
        up = pltpu.roll(t, LANES - ROPE_HALF, 1)
        dn = pltpu.roll(t, ROPE_HALF, 1)
        return t * cos_scr[...] + up * sneg_scr[...] + dn * spos_scr[...]

    def emit_dilated(section, slab, val):
        out_lanes = slice(section * Z_BLK + slab * LANES, section * Z_BLK + (slab + 1) * LANES)
        z_ref[:, (DIL_SECTION + section) * Z_BLK + slab * LANES:
              (DIL_SECTION + section) * Z_BLK + (slab + 1) * LANES] = val.astype(BF)
        stage, stage4 = stage_scr.at[section, slab], stage4_scr.at[section, slab]
        stage[...] = val
        seg4 = tm // 4
        for c in range(4):
            part = stage[pl.ds(c, seg4, stride=4), :]
            stage4[c * seg4:(c + 1) * seg4, :] = part
            zd_refs[4][c * seg4:(c + 1) * seg4, out_lanes] = part.astype(BF)
        seg16 = tm // 16
        for jj in range(16):
            c, a = jj % 4, jj // 4
            part = stage4[pl.ds(c * seg4 + a, seg16, stride=4), :]
            zd_refs[16][jj * seg16:(jj + 1) * seg16, out_lanes] = part.astype(BF)

    h = _rms(x_ref[...], g_ref[...]).astype(BF)
    h_scr[...] = h
    fa_ref[...] = _dot(h, w_ref[:, Z_WIDTH:Z_WIDTH + LANES])

    for j in range(Z_WIDTH // Z_BLK):
        cols = slice(j * Z_BLK, (j + 1) * Z_BLK)
        acc = _dot(h_scr[...], w_ref[:, cols])
        section = j - DIL_SECTION
        if j == 0:
            z_ref[:, cols] = (acc * (Q_SCALE * LOG2_E)).astype(BF)
        elif 0 <= section < 3:
            for slab in range(n_slabs):
                t = acc[:, slab * LANES:(slab + 1) * LANES]
                if section == 0:
                    t = rope(t) * (Q_SCALE * LOG2_E)
                elif section == 1:
                    t = rope(t)
                emit_dilated(section, slab, t)
        else:
            z_ref[:, cols] = acc.astype(BF)


def _in_proj(x2d, g, w_raw, freq_lanes, *, batch, seq, tm):
    n_p = seq // tm
    tokens = batch * seq
    row = lambda p, b: (b * n_p + p, 0)
    resident = lambda shape: pl.BlockSpec(shape, lambda p, b: (0, 0),
                                          pipeline_mode=pl.Buffered(1))
    dil_shape = jax.ShapeDtypeStruct((tokens, 3 * ATT_WIDTH), BF)
    stage_buf = pltpu.VMEM((3, Z_BLK // LANES, tm, LANES), F32)
    return pl.pallas_call(
        functools.partial(_in_proj_kernel, tm=tm),
        grid=(n_p, batch),
        in_specs=[
            pl.BlockSpec((tm, D_MODEL), row),
            resident((1, D_MODEL)),
            resident((D_MODEL, Z_WIDTH + N_HEADS)),
            resident((1, LANES)),
        ],
        out_specs=[
            pl.BlockSpec((tm, Z_WIDTH), row),
            pl.BlockSpec((tm, LANES), row),
            pl.BlockSpec((tm, 3 * ATT_WIDTH), row),
            pl.BlockSpec((tm, 3 * ATT_WIDTH), row),
        ],
        out_shape=[
            jax.ShapeDtypeStruct((tokens, Z_WIDTH), BF),
            jax.ShapeDtypeStruct((tokens, LANES), F32),
            dil_shape,
            dil_shape,
        ],
        scratch_shapes=[
            pltpu.VMEM((D_MODEL, Z_WIDTH + LANES), BF),
            pltpu.VMEM((tm, D_MODEL), BF),
            pltpu.VMEM((tm, LANES), F32),
            pltpu.VMEM((tm, LANES), F32),
            pltpu.VMEM((tm, LANES), F32),
            stage_buf,
            stage_buf,
        ],
        compiler_params=_cparams(("arbitrary", "arbitrary")),
        name="in_proj",
    )(x2d, g, w_raw, freq_lanes)


def _extras_base(head):
    return (HEAD_DIM if head % 2 == 0 else 0) + 2 * N_PIECES * (head // 2)


def _forget_scan_kernel(fa_ref, bias_ref, tri_ref, pq_ref, pk_ref, oq_ref, ok_ref,
                        gq_ref, gk_ref):
    chunk = tri_ref.shape[0]
    n_chunks = fa_ref.shape[0] // chunk
    t = fa_ref[...] + bias_ref[...]
    log_f = jnp.minimum(t, 0.0) - jnp.log1p(jnp.exp(-jnp.abs(t)))
    wide = jnp.concatenate(
        [log_f[c * chunk:(c + 1) * chunk, :] for c in range(n_chunks)], axis=1)
    tri = tri_ref[...]
    local = sum(_dot(tri, piece) for piece in _split3(wide))
    carry = jnp.zeros((1, LANES), F32)
    parts = []
    for c in range(n_chunks):
        part = local[:, c * LANES:(c + 1) * LANES] + carry
        parts.append(part)
        carry = part[chunk - 1:chunk, :]
    run = jnp.concatenate(parts, axis=0)
    gq = oq_ref[...].astype(F32)
    gk = ok_ref[...].astype(F32)
    for i, piece in enumerate(_split3(run * LOG2_E)):
        gq = gq + _dot(piece, pq_ref[i])
        gk = gk - _dot(piece, pk_ref[i])
    gq_ref[0] = gq.astype(BF)
    gk_ref[0] = gk.astype(BF)


def _forget_scan_constants(chunk):
    tri = np.tril(np.ones((chunk, chunk), np.float32))
    pq = np.zeros((N_PIECES, LANES, LANES), np.float32)
    pk = np.zeros((N_PIECES, LANES, LANES), np.float32)
    oq = np.zeros((1, LANES), np.float32)
    ok = np.zeros((1, LANES), np.float32)
    for h in range(N_HEADS):
        base = _extras_base(h)
        for i in range(N_PIECES):
            pq[i, h, base + i] = 1.0
            oq[0, base + N_PIECES + i] = 1.0
            ok[0, base + i] = 1.0
            pk[i, h, base + N_PIECES + i] = 1.0
    as_bf = lambda a: jnp.asarray(a, BF)
    return as_bf(tri), as_bf(pq), as_bf(pk), as_bf(oq), as_bf(ok)


def _forget_scan(fa, bias_lanes, *, batch, seq, chunk):
    tri, pq, pk, oq, ok = _forget_scan_constants(chunk)
    const2 = lambda b: (0, 0)
    const3 = lambda b: (0, 0, 0)
    return pl.pallas_call(
        _forget_scan_kernel,
        grid=(batch,),
        in_specs=[
            pl.BlockSpec((seq, LANES), lambda b: (b, 0)),
            pl.BlockSpec((1, LANES), const2),
            pl.BlockSpec((chunk, chunk), const2),
            pl.BlockSpec((N_PIECES, LANES, LANES), const3),
            pl.BlockSpec((N_PIECES, LANES, LANES), const3),
            pl.BlockSpec((1, LANES), const2),
            pl.BlockSpec((1, LANES), const2),
        ],
        out_specs=[
            pl.BlockSpec((1, seq, LANES), lambda b: (b, 0, 0)),
            pl.BlockSpec((1, seq, LANES), lambda b: (b, 0, 0)),
        ],
        out_shape=[
            jax.ShapeDtypeStruct((batch, seq, LANES), BF),
            jax.ShapeDtypeStruct((batch, seq, LANES), BF),
        ],
        compiler_params=_cparams(("arbitrary",)),
        name="forget_scan",
    )(fa, bias_lanes, tri, pq, pk, oq, ok)


def _with_bias_lanes(slab, extras, hp, parity):
    lane = lax.broadcasted_iota(jnp.int32, slab.shape, 1)
    own = (lane < HEAD_DIM) if parity == 0 else (lane >= HEAD_DIM)
    base = (HEAD_DIM if parity == 0 else 0) + 2 * N_PIECES * hp
    in_extras = (lane >= base) & (lane < base + 2 * N_PIECES)
    return jnp.where(own, slab, jnp.where(in_extras, extras, jnp.zeros_like(extras)))


def _fox_kernel(q_ref, k_ref, v_ref, gq_ref, gk_ref, o_ref,
                ka_scr, kb_scr, vta_scr, vtb_scr, qp_scr, bias_scr, s_scr, acc_scr, ot_scr,
                bound_smem, *, tq):
    hp = pl.program_id(1)
    k_scrs = (ka_scr, kb_scr)
    vt_scrs = (vta_scr, vtb_scr)
    seq = k_ref.shape[0]
    n_q = seq // tq
    lane_row = lax.broadcasted_iota(jnp.int32, (1, LANES), 1)

    sel_row = lax.broadcasted_iota(jnp.int32, (LANES, LANES), 0)
    sel_col = lax.broadcasted_iota(jnp.int32, (LANES, LANES), 1)
    head_sum = jnp.where(sel_row // HEAD_DIM == sel_col, 1.0, 0.0).astype(BF)

    def max_row_norm(slab, head):
        lane = lax.broadcasted_iota(jnp.int32, slab.shape, 1)
        own = (lane < HEAD_DIM) if head == 0 else (lane >= HEAD_DIM)
        x = jnp.where(own, slab.astype(F32), 0.0)
        sq = jnp.sum(x * x, axis=1, keepdims=True)
        return jnp.sqrt(jnp.max(sq, axis=0, keepdims=True))

    def max_row_norms(slab):
        x = slab.astype(F32)
        per_row = _dot((x * x).astype(BF), head_sum)
        return jnp.sqrt(jnp.max(per_row, axis=0, keepdims=True) * NORM_SLACK)

    def bias_lane_sum(row, head, first):
        base = (HEAD_DIM if head == 0 else 0) + 2 * N_PIECES * hp + first
        picked = (lane_row >= base) & (lane_row < base + N_PIECES)
        return jnp.sum(jnp.where(picked, row, 0.0), axis=1, keepdims=True)

    k2 = k_ref[...]
    gk = gk_ref[0]
    k_norms = max_row_norms(k2)
    for head in range(2):
        k_scrs[head][...] = _with_bias_lanes(k2, gk, hp, head)
        bound_smem[head, n_q] = k_norms[0, head]
        for j in range(n_q):
            rows16 = gk_ref[0, (j + 1) * tq - BF16_ROWS:(j + 1) * tq, :].astype(F32)
            last = rows16[BF16_ROWS - 1:BF16_ROWS, :]
            bound_smem[head, j] = -bias_lane_sum(last, head, N_PIECES)[0, 0]
    vt = v_ref[...].T
    ones = jnp.ones((BF16_ROWS, seq), BF)
    for head, vt_scr in enumerate(vt_scrs):
        vt_scr[0:HEAD_DIM, :] = vt[head * HEAD_DIM:(head + 1) * HEAD_DIM, :]
        vt_scr[HEAD_DIM:, :] = ones
    row = lax.broadcasted_iota(jnp.int32, (tq, tq), 0)
    col = lax.broadcasted_iota(jnp.int32, (tq, tq), 1)
    bias_scr[...] = jnp.where(row <= col, 0.0, NEG_INF)
    acc_scr[...] = jnp.zeros_like(acc_scr)

    def scores(head, tile, masked):
        kt = k_scrs[head][pl.ds(pl.multiple_of(tile * tq, tq), tq), :]
        st = _dot(kt, qp_scr[head])
        if masked:
            st = st + bias_scr[...]
        s_scr[head] = st
        return jnp.max(st, axis=0, keepdims=True)

    def softmax_pv(head, st, tile, tile_max, m):
        start = pl.multiple_of(tile * tq, tq)
        m_new = jnp.maximum(m, tile_max)
        alpha = jnp.exp2(m - m_new)
        pt = jnp.exp2(st - m_new).astype(BF)
        vt_tile = vt_scrs[head][:, pl.ds(start, tq)]
        acc_scr[head] = alpha * acc_scr[head] + _dot(vt_tile, pt)
        return m_new

    def start_tile(qi):
        rows = pl.ds(pl.multiple_of(qi * tq, tq), tq)
        q2 = q_ref[rows, :]
        gq = gq_ref[0, rows, :]
        first_row = gq[0:BF16_ROWS, :].astype(F32)[0:1, :]
        maxes, slack = [], []
        for head in range(2):
            qp_scr[head] = _with_bias_lanes(q2, gq, hp, head).T
            maxes.append(scores(head, qi, True))
            reach = (max_row_norm(q2, head) * bound_smem[head, n_q]
                     + bias_lane_sum(first_row, head, 0) + SCORE_MARGIN)
            lowest_max = jnp.min(maxes[head], axis=1, keepdims=True)
            slack.append(reach - (lowest_max - UNDERFLOW_BITS))
        count = jnp.zeros((1, 1), jnp.int32)
        for j in range(n_q - 1):
            needed = (((slack[0] >= bound_smem[0, j]) | (slack[1] >= bound_smem[1, j]))
                      & (j < qi))
            count = count + jnp.where(needed, 1, 0)
        return maxes[0], maxes[1], count[0, 0]

    def finish_tile(qi):
        for head in range(2):
            acc = acc_scr[head]
            ot_scr[head * HEAD_DIM:(head + 1) * HEAD_DIM, :] = (
                acc[0:HEAD_DIM, :] / acc[HEAD_DIM:HEAD_DIM + 1, :])
        o_ref[pl.ds(pl.multiple_of(qi * tq, tq), tq), :] = ot_scr[...].astype(BF).T
        acc_scr[...] = jnp.zeros_like(acc_scr)

    m_init = jnp.full((1, tq), NEG_INF, F32)

    def query_tile(qi, carry):
        first_a, first_b, n_needed = carry

        def key_tile(t, inner):
            max_a, max_b, m_a, m_b, prev_tile = inner
            prev = [s_scr[head] for head in range(2)]
            tile = qi - 1 - t
            next_max = [scores(head, tile, False) for head in range(2)]
            m_a = softmax_pv(0, prev[0], prev_tile, max_a, m_a)
            m_b = softmax_pv(1, prev[1], prev_tile, max_b, m_b)
            return next_max[0], next_max[1], m_a, m_b, tile

        def two_key_tiles(u, inner):
            return key_tile(2 * u + 1, key_tile(2 * u, inner))

        inner = lax.fori_loop(0, n_needed // 2, two_key_tiles,
                              (first_a, first_b, m_init, m_init, qi))
        max_a, max_b, m_a, m_b, prev_tile = lax.cond(
            n_needed % 2 == 1, lambda c: key_tile(n_needed - 1, c), lambda c: c, inner)
        last = [s_scr[head] for head in range(2)]
        softmax_pv(0, last[0], prev_tile, max_a, m_a)
        softmax_pv(1, last[1], prev_tile, max_b, m_b)
        following = start_tile(jnp.minimum(qi + 1, n_q - 1))
        finish_tile(qi)
        return following

    lax.fori_loop(0, n_q, query_tile, start_tile(0))


def _fox_attention(z, gq, gk, *, batch, seq, tq):
    tokens = batch * seq
    col = lambda section: pl.BlockSpec(
        (seq, LANES), lambda b, hp: (b, section * N_HEAD_PAIRS + hp))
    whole = pl.BlockSpec((1, seq, LANES), lambda b, hp: (b, 0, 0))
    ext = HEAD_DIM + BF16_ROWS
    return pl.pallas_call(
        functools.partial(_fox_kernel, tq=tq),
        grid=(batch, N_HEAD_PAIRS),
        in_specs=[col(0), col(1), col(2), whole, whole],
        out_specs=pl.BlockSpec((seq, LANES), lambda b, hp: (b, hp)),
        out_shape=jax.ShapeDtypeStruct((tokens, ATT_WIDTH), BF),
        scratch_shapes=[
            pltpu.VMEM((seq, LANES), BF),
            pltpu.VMEM((seq, LANES), BF),
            pltpu.VMEM((ext, seq), BF),
            pltpu.VMEM((ext, seq), BF),
            pltpu.VMEM((2, LANES, tq), BF),
            pltpu.VMEM((tq, tq), F32),
            pltpu.VMEM((2, tq, tq), F32),
            pltpu.VMEM((2, ext, tq), F32),
            pltpu.VMEM((LANES, tq), F32),
            pltpu.SMEM((2, seq // tq + 1), F32),
        ],
        compiler_params=_cparams(("arbitrary", "arbitrary")),
        name="fox_attn",
    )(z, z, z, gq, gk)


def _log2(n):
    assert n > 0 and n & (n - 1) == 0, n
    return n.bit_length() - 1


def _dilated_kernel(q1, k1, v1, q4, k4, v4, q16, k16, v16, o_ref,
                    num1, num4, num16, den1, den4, den16, max1, max4, max16, bias_scr,
                    *, seq, blk, chunk, group, merge_rows):
    sources = ((q1, k1, v1), (q4, k4, v4), (q16, k16, v16))
    num_scrs = (num1, num4, num16)
    den_scrs = (den1, den4, den16)
    max_scrs = (max1, max4, max16)
    n_blocks = seq // blk

    low = lax.broadcasted_iota(jnp.int32, (blk, LANES), 1) < HEAD_DIM
    qrow = lax.broadcasted_iota(jnp.int32, (2 * blk, 2 * blk), 0) % blk
    kcol = lax.broadcasted_iota(jnp.int32, (2 * blk, 2 * blk), 1)
    dist = qrow + blk - kcol
    band = (dist >= 0) & (dist <= blk)
    bias_scr[0] = jnp.where(band & (kcol >= blk), 0.0, NEG_INF)
    bias_scr[1] = jnp.where(band, 0.0, NEG_INF)
    ones = jnp.ones((2 * blk, LANES), BF)

    for idx, (window, dilation) in enumerate(DIL_PATTERNS):
        assert window // dilation == blk
        q_ref, k_ref, v_ref = sources[idx]
        num_scr, den_scr, max_scr = num_scrs[idx], den_scrs[idx], max_scrs[idx]
        sub_shift = _log2(seq // dilation // blk)
        seg_rows = chunk // dilation
        seg = min(blk, seg_rows)

        def load(ref, j, l0):
            parts = []
            for s in range(blk // seg):
                l = l0 + s * seg
                p = lax.shift_right_logical(l, _log2(seg_rows))
                i = l & (seg_rows - 1)
                start = pl.multiple_of(p * chunk + j * seg_rows + i, seg)
                parts.append(ref[pl.ds(start, seg), :])
            return parts[0] if len(parts) == 1 else jnp.concatenate(parts, axis=0)

        def one_block(g):
            j = lax.shift_right_logical(g, sub_shift)
            gs = g & ((1 << sub_shift) - 1)
            l0 = gs * blk
            lp = jnp.maximum(l0 - blk, 0)
            q2 = load(q_ref, j, l0)
            zero = jnp.zeros_like(q2)
            qq = jnp.concatenate([jnp.where(low, q2, zero), jnp.where(low, zero, q2)], axis=0)
            kwin = jnp.concatenate([load(k_ref, j, lp), load(k_ref, j, l0)], axis=0)
            vwin = jnp.concatenate([load(v_ref, j, lp), load(v_ref, j, l0)], axis=0)
            s = _dot_nt(qq, kwin) + bias_scr[jnp.minimum(gs, 1)]
            m = jnp.max(s, axis=1, keepdims=True)
            p = jnp.exp2(s - m).astype(BF)
            pv = _dot(p, jnp.concatenate([vwin, ones], axis=1))
            if dilation == 1:
                dst = pl.ds(pl.multiple_of(l0, blk), blk)
            else:
                dst = pl.ds(l0 * dilation + j, blk, stride=dilation)
            num_scr[dst, :] = jnp.where(low, pv[:blk, :LANES], pv[blk:, :LANES])
            den_scr[dst, :] = jnp.where(low, pv[:blk, LANES:], pv[blk:, LANES:])
            max_scr[dst, :] = jnp.where(low, jnp.broadcast_to(m[:blk], (blk, LANES)),
                                        jnp.broadcast_to(m[blk:], (blk, LANES)))

        def blocks(it, carry):
            for u in range(group):
                one_block(it * group + u)
            return carry

        lax.fori_loop(0, n_blocks // group, blocks, 0)

    def merge(c, carry):
        rows = pl.ds(pl.multiple_of(c * merge_rows, merge_rows), merge_rows)
        maxes = [max_scr[rows, :] for max_scr in max_scrs]
        top = functools.reduce(jnp.maximum, maxes)
        weights = [jnp.exp2(m - top) for m in maxes]
        num = sum(w * num_scr[rows, :] for w, num_scr in zip(weights, num_scrs))
        den = sum(w * den_scr[rows, :] for w, den_scr in zip(weights, den_scrs))
        o_ref[rows, :] = (num / den).astype(BF)
        return carry

    lax.fori_loop(0, seq // merge_rows, merge, 0)


def _dilated_attention(z, zd4, zd16, *, batch, seq):
    blk = DIL_PATTERNS[0][0] // DIL_PATTERNS[0][1]
    tokens = batch * seq
    per_section = ATT_WIDTH // LANES
    col = lambda section: pl.BlockSpec(
        (seq, LANES), lambda b, hp: (b, section * per_section + hp))
    f32_buf = pltpu.VMEM((seq, LANES), F32)
    return pl.pallas_call(
        functools.partial(_dilated_kernel, seq=seq, blk=blk, chunk=DIL_CHUNK,
                          group=seq // blk, merge_rows=DIL_MERGE_ROWS),
        grid=(batch, N_HEAD_PAIRS),
        in_specs=[col(DIL_SECTION), col(DIL_SECTION + 1), col(DIL_SECTION + 2),
                  col(0), col(1), col(2), col(0), col(1), col(2)],
        out_specs=pl.BlockSpec((seq, LANES), lambda b, hp: (b, hp)),
        out_shape=jax.ShapeDtypeStruct((tokens, ATT_WIDTH), BF),
        scratch_shapes=[f32_buf] * 9 + [pltpu.VMEM((2, 2 * blk, 2 * blk), F32)],
        compiler_params=_cparams(("arbitrary", "arbitrary")),
        name="dilated_attn",
    )(z, z, z, zd4, zd4, zd4, zd16, zd16, zd16)


def _mix_kernel(ya_ref, yb_ref, ga_ref, gb_ref, x_ref, woa_ref, wob_ref, wout_ref,
                g_ref, o_ref):
    pa = _dot(ya_ref[...], woa_ref[...])
    pb = _dot(yb_ref[...], wob_ref[...])
    mixed = (jax.nn.sigmoid(ga_ref[...].astype(F32)) * pa
             + jax.nn.sigmoid(gb_ref[...].astype(F32)) * pb)
    y = _dot(mixed.astype(BF), wout_ref[...])
    o_ref[...] = x_ref[...] + _rms(y, g_ref[...])


def _mix(ya, yb, z, x2d, woa, wob, wout, g, *, tm):
    tokens = x2d.shape[0]
    gate_blk = lambda off: pl.BlockSpec((tm, D_MODEL), lambda i: (i, off))
    const = lambda i: (0, 0)
    return pl.pallas_call(
        _mix_kernel,
        grid=(tokens // tm,),
        in_specs=[
            pl.BlockSpec((tm, ATT_WIDTH), lambda i: (i, 0)),
            pl.BlockSpec((tm, ATT_WIDTH), lambda i: (i, 0)),
            gate_blk(3),
            gate_blk(4),
            pl.BlockSpec((tm, D_MODEL), lambda i: (i, 0)),
            pl.BlockSpec((ATT_WIDTH, D_MODEL), const),
            pl.BlockSpec((ATT_WIDTH, D_MODEL), const),
            pl.BlockSpec((D_MODEL, D_MODEL), const),
            pl.BlockSpec((1, D_MODEL), const),
        ],
        out_specs=pl.BlockSpec((tm, D_MODEL), lambda i: (i, 0)),
        out_shape=jax.ShapeDtypeStruct((tokens, D_MODEL), F32),
        compiler_params=_cparams(("arbitrary",)),
        name="mix",
    )(ya, yb, z, z, x2d, woa, wob, wout, g)


def _ffn_kernel(x_ref, halo_ref, gpre_ref, wup_ref, cw_ref, cb_ref, wd_ref, gpost_ref, o_ref,
                h_scr, ua0_scr, ub0_scr, ua1_scr, ub1_scr, acc_scr, rows_scr,
                *, tm, tf, tiles_per_seq, n_chunks):
    i = pl.program_id(0)
    halo = BF16_ROWS
    n_rows = tm + halo
    n_groups = n_rows // SUBLANES
    chunk = n_rows // n_chunks
    assert n_groups * SUBLANES == n_rows and chunk * n_chunks == n_rows
    assert chunk % BF16_ROWS == 0 and n_groups % 2 == 0
    n_tiles = D_FF // tf
    u_sets = ((ua0_scr, ub0_scr), (ua1_scr, ub1_scr))

    def group_rows(g):
        return pl.ds(g, SUBLANES, stride=n_groups)

    def cols(f, gate):
        return pl.ds(pl.multiple_of(f * tf + (D_FF if gate else 0), LANES), tf)

    def up_proj(f, dst, lo, hi):
        h = h_scr[lo:hi, :]
        dst[0][lo:hi, :] = _dot(h, wup_ref[:, cols(f, False)])
        dst[1][lo:hi, :] = _dot(h, wup_ref[:, cols(f, True)])

    def tap_rows(u_scr, back, first, n):
        lo = first - back * SUBLANES
        if lo >= 0:
            return u_scr[lo:lo + n, :]
        wrapped = [pltpu.roll(u_scr[n_rows + k * SUBLANES:n_rows + (k + 1) * SUBLANES, :], 1, 0)
                   for k in range(lo // SUBLANES, 0)]
        return jnp.concatenate(wrapped + [u_scr[0:n + lo, :]], axis=0)

    def conv(u_scr, window, post_scale, first, n):
        out = cb_ref[:, window] * post_scale
        for tap in range(CONV_WIDTH):
            back = CONV_WIDTH - 1 - tap
            out = out + (cw_ref[tap:tap + 1, window] * post_scale) * tap_rows(u_scr, back, first, n)
        return out

    def down_proj(f, src, first, n):
        a = conv(src[0], cols(f, False), 1.0, first, n)
        half_b = conv(src[1], cols(f, True), 0.5, first, n)
        c0 = math.sqrt(2.0 / math.pi)
        inner = a * (c0 + (c0 * 0.044715) * (a * a))
        hidden = ((a * half_b) * (1.0 + jnp.tanh(inner))).astype(BF)
        return _dot(hidden, wd_ref[pl.ds(pl.multiple_of(f * tf, tf), tf), :])

    def stage(f, parity):
        src, dst = u_sets[1 - parity], u_sets[parity]
        for c in range(n_chunks):
            up_proj(f, dst, c * chunk, (c + 1) * chunk)
            out_rows = slice(c * chunk, (c + 1) * chunk)
            acc_scr[out_rows, :] += down_proj(f - 1, src, c * chunk, chunk)

    n_slabs = D_MODEL // LANES
    g = gpre_ref[...]
    hh = _rms(halo_ref[...], g)
    hh = jnp.where(i % tiles_per_seq == 0, jnp.zeros_like(hh), hh)
    hx = _rms(x_ref[...], g)
    for s in range(n_slabs):
        rows_scr[s, 0:halo, :] = hh[:, s * LANES:(s + 1) * LANES]
        rows_scr[s, halo:, :] = hx[:, s * LANES:(s + 1) * LANES]
    for gg in range(0, n_groups, 2):
        pair = jnp.concatenate(
            [jnp.concatenate([rows_scr[s, group_rows(gg + d), :] for s in range(n_slabs)], axis=1)
             for d in range(2)], axis=0)
        h_scr[gg * SUBLANES:(gg + 2) * SUBLANES, :] = pair.astype(BF)
    acc_scr[...] = jnp.zeros_like(acc_scr)
    up_proj(0, u_sets[0], 0, n_rows)

    def pair_of_tiles(k, carry):
        stage(2 * k + 1, 1)
        stage(2 * k + 2, 0)
        return carry

    assert n_tiles % 2 == 1
    for k in range((n_tiles - 1) // 2):
        pair_of_tiles(k, 0)
    gpost = gpost_ref[...]
    for c in range(n_chunks):
        y = acc_scr[c * chunk:(c + 1) * chunk, :] + down_proj(n_tiles - 1, u_sets[0], c * chunk, chunk)
        normed = _rms(y, gpost)
        for gg in range(chunk // SUBLANES):
            g_abs = c * (chunk // SUBLANES) + gg
            for s in range(n_slabs):
                rows_scr[s, group_rows(g_abs), :] = normed[gg * SUBLANES:(gg + 1) * SUBLANES,
                                                           s * LANES:(s + 1) * LANES]
    for s in range(n_slabs):
        lanes = slice(s * LANES, (s + 1) * LANES)
        o_ref[:, lanes] = x_ref[:, lanes] + rows_scr[s, halo:, :]


def _ffn(x1, g_pre, w_up, conv_w, conv_b, w_down, g_post, *, seq, tm, tf):
    tokens = x1.shape[0]
    halo = BF16_ROWS
    per = tm // halo
    resident = lambda shape: pl.BlockSpec(shape, lambda i: (0, 0),
                                          pipeline_mode=pl.Buffered(1))
    u_buf = pltpu.VMEM((tm + halo, tf), F32)
    return pl.pallas_call(
        functools.partial(_ffn_kernel, tm=tm, tf=tf, tiles_per_seq=seq // tm,
                          n_chunks=FFN_ROW_CHUNKS),
        grid=(tokens // tm,),
        in_specs=[
            pl.BlockSpec((tm, D_MODEL), lambda i: (i, 0)),
            pl.BlockSpec((halo, D_MODEL), lambda i: (jnp.maximum(i * per - 1, 0), 0)),
            resident((1, D_MODEL)),
            resident((D_MODEL, 2 * D_FF)),
            resident((CONV_WIDTH, 2 * D_FF)),
            resident((1, 2 * D_FF)),
            resident((D_FF, D_MODEL)),
            resident((1, D_MODEL)),
        ],
        out_specs=pl.BlockSpec((tm, D_MODEL), lambda i: (i, 0)),
        out_shape=jax.ShapeDtypeStruct((tokens, D_MODEL), F32),
        scratch_shapes=[
            pltpu.VMEM((tm + halo, D_MODEL), BF),
            u_buf, u_buf, u_buf, u_buf,
            pltpu.VMEM((tm + halo, D_MODEL), F32),
            pltpu.VMEM((D_MODEL // LANES, tm + halo, LANES), F32),
        ],
        compiler_params=_cparams(("arbitrary",)),
        name="ffn",
    )(x1, x1, g_pre, w_up, conv_w, conv_b, w_down, g_post)


def _rope_freq_lanes():
    inv_freq = ROPE_THETA ** (-jnp.arange(ROPE_HALF, dtype=F32) * 2.0 / ROPE_DIM)
    return jnp.tile(inv_freq, LANES // ROPE_HALF).reshape(1, LANES)


def kernel(x, g_pre_mix, w_in, b_forget, w_o_fox, w_o_dil, w_out, g_post_mix,
           g_pre_ffn, w_up, conv_w, conv_b, w_down, g_post_ffn):
    batch, seq, d_model = x.shape
    assert d_model == D_MODEL and seq % max(MIX_TOKENS, FFN_TOKENS, FOX_TILE, DIL_CHUNK) == 0
    depth = w_in.shape[0]
    freq_lanes = _rope_freq_lanes()
    x2d = x.reshape(batch * seq, D_MODEL)
    row = lambda v: v.reshape(1, -1)
    for l in range(depth):
        bias_lanes = jnp.pad(b_forget[l], (0, LANES - N_HEADS)).reshape(1, LANES)

        z, fa, zd4, zd16 = _in_proj(x2d, row(g_pre_mix[l]), w_in[l].astype(BF), freq_lanes,
                                    batch=batch, seq=seq, tm=DIL_CHUNK)
        gq, gk = _forget_scan(fa, bias_lanes, batch=batch, seq=seq, chunk=SCAN_CHUNK)
        ya = _fox_attention(z, gq, gk, batch=batch, seq=seq, tq=FOX_TILE)
        yb = _dilated_attention(z, zd4, zd16, batch=batch, seq=seq)
        x2d = _mix(ya, yb, z, x2d, w_o_fox[l].astype(BF), w_o_dil[l].astype(BF),
                   w_out[l].astype(BF), row(g_post_mix[l]), tm=MIX_TOKENS)
        x2d = _ffn(x2d, row(g_pre_ffn[l]), w_up[l].astype(BF), conv_w[l],
                   row(conv_b[l]), w_down[l].astype(BF), row(g_post_ffn[l]),
                   seq=seq, tm=FFN_TOKENS, tf=FFN_HIDDEN_TILE)
    return x2d.reshape(batch, seq, D_MODEL)
```

```python
import functools
import math

import numpy as np
import jax
import jax.numpy as jnp
from jax import lax
from jax.experimental import pallas as pl
from jax.experimental.pallas import tpu as pltpu

D_MODEL = 1024
HEAD_DIM = 64
N_HEADS = 8
ATT_WIDTH = N_HEADS * HEAD_DIM
N_HEAD_PAIRS = N_HEADS // 2
DIL_PATTERNS = ((128, 1), (512, 4), (2048, 16))
ROPE_DIM = HEAD_DIM // 4
ROPE_HALF = ROPE_DIM // 2
ROPE_THETA = 500000.0
D_FF = 2816
CONV_WIDTH = 3
RMS_EPS = 1e-6
NEG_INF = -1e30
Q_SCALE = 1.0 / math.sqrt(HEAD_DIM)
LOG2_E = math.log2(math.e)
UNDERFLOW_BITS = 150.0
SCORE_MARGIN = 1.0
NORM_SLACK = 1.0 + 2.0 ** -7

LANES = 128
SUBLANES = 8
BF16_ROWS = 16
Z_WIDTH = 3 * ATT_WIDTH * 2 + 2 * D_MODEL
Z_BLK = 512
DIL_SECTION = 3
DIL_CHUNK = 512
N_PIECES = 3
VMEM_LIMIT = 56 * 1024 * 1024

SCAN_CHUNK = 512
FOX_TILE = 512
DIL_MERGE_ROWS = 256
MIX_TOKENS = 1024
MIX_ROW_CHUNKS = 4
FFN_TOKENS = 1024
FFN_HIDDEN_TILE = 256
FFN_ROW_CHUNKS = 5

BF = jnp.bfloat16
F32 = jnp.float32


def _cparams(sem, flags=None):
    return pltpu.CompilerParams(dimension_semantics=sem, vmem_limit_bytes=VMEM_LIMIT,
                                flags=flags)


def _rms(xf, g):
    inv = lax.rsqrt(jnp.mean(xf * xf, axis=-1, keepdims=True) + RMS_EPS)
    return xf * inv * g


def _split3(x):
    hi = x.astype(BF)
    r1 = x - hi.astype(F32)
    mid = r1.astype(BF)
    lo = (r1 - mid.astype(F32)).astype(BF)
    return hi, mid, lo


def _dot(a, b):
    return jnp.dot(a, b, preferred_element_type=F32)


def _dot_nt(a, b):
    return lax.dot_general(a, b, (((1,), (1,)), ((), ())), preferred_element_type=F32)


def _in_proj_kernel(x_ref, g_ref, w_raw_ref, freq_ref, z_ref, fa_ref, zd4_ref, zd16_ref,
                    w_ref, h_scr, cos_scr, sneg_scr, spos_scr, stage_scr, stage4_scr, *, tm):
    p = pl.program_id(0)
    b = pl.program_id(1)
    zd_refs = {4: zd4_ref, 16: zd16_ref}
    n_slabs = Z_BLK // LANES

    @pl.when((p == 0) & (b == 0))
    def _regroup_weight():
        fox_end = 3 * ATT_WIDTH
        raw_width = w_raw_ref.shape[1]
        first = fox_end // LANES

        def group(i):
            lo = i * LANES
            return w_raw_ref[:, lo:min(lo + LANES, raw_width)].astype(F32)

        w_ref[:, 0:fox_end] = w_raw_ref[:, 0:fox_end]
        for k in range((Z_WIDTH - fox_end) // LANES):
            shifted = jnp.concatenate(
                [group(first + k)[:, N_HEADS:], group(first + k + 1)[:, :N_HEADS]], axis=1)
            w_ref[:, fox_end + k * LANES:fox_end + (k + 1) * LANES] = shifted.astype(BF)
        w_ref[:, Z_WIDTH:Z_WIDTH + LANES] = jnp.concatenate(
            [group(first)[:, :N_HEADS], jnp.zeros((D_MODEL, LANES - N_HEADS), F32)],
            axis=1).astype(BF)

    @pl.when(b == 0)
    def _tables():
        pos = (p * tm + lax.broadcasted_iota(jnp.int32, (tm, LANES), 0)).astype(F32)
        lane = lax.broadcasted_iota(jnp.int32, (tm, LANES), 1)
        c = lane % HEAD_DIM
        ang = pos * freq_ref[...]
        cs = jnp.cos(ang)
        sn = jnp.sin(ang)
        cos_scr[...] = jnp.where(c < ROPE_DIM, cs, 1.0)
        sneg_scr[...] = jnp.where(c < ROPE_HALF, -sn, 0.0)
        spos_scr[...] = jnp.where((c >= ROPE_HALF) & (c < ROPE_DIM), sn, 0.0)

    def rope(t):
        up = pltpu.roll(t, LANES - ROPE_HALF, 1)
        dn = pltpu.roll(t, ROPE_HALF, 1)
        return t * cos_scr[...] + up * sneg_scr[...] + dn * spos_scr[...]

    def emit_dilated(section, slab, val):
        out_lanes = slice(section * Z_BLK + slab * LANES, section * Z_BLK + (slab + 1) * LANES)
        z_ref[:, (DIL_SECTION + section) * Z_BLK + slab * LANES:
              (DIL_SECTION + section) * Z_BLK + (slab + 1) * LANES] = val.astype(BF)
        stage, stage4 = stage_scr.at[section, slab], stage4_scr.at[section, slab]
        stage[...] = val
        seg4 = tm // 4
        for c in range(4):
            part = stage[pl.ds(c, seg4, stride=4), :]
            stage4[c * seg4:(c + 1) * seg4, :] = part
            zd_refs[4][c * seg4:(c + 1) * seg4, out_lanes] = part.astype(BF)
        seg16 = tm // 16
        for jj in range(16):
            c, a = jj % 4, jj // 4
            part = stage4[pl.ds(c * seg4 + a, seg16, stride=4), :]
            zd_refs[16][jj * seg16:(jj + 1) * seg16, out_lanes] = part.astype(BF)

    h = _rms(x_ref[...], g_ref[...]).astype(BF)
    h_scr[...] = h
    fa_ref[...] = _dot(h, w_ref[:, Z_WIDTH:Z_WIDTH + LANES])

    for j in range(Z_WIDTH // Z_BLK):
        cols = slice(j * Z_BLK, (j + 1) * Z_BLK)
        acc = _dot(h_scr[...], w_ref[:, cols])
        section = j - DIL_SECTION
        if j == 0:
            z_ref[:, cols] = (acc * (Q_SCALE * LOG2_E)).astype(BF)
        elif 0 <= section < 3:
            for slab in range(n_slabs):
                t = acc[:, slab * LANES:(slab + 1) * LANES]
                if section == 0:
                    t = rope(t) * (Q_SCALE * LOG2_E)
                elif section == 1:
                    t = rope(t)
                emit_dilated(section, slab, t)
        else:
            z_ref[:, cols] = acc.astype(BF)


def _in_proj(x2d, g, w_raw, freq_lanes, *, batch, seq, tm):
    n_p = seq // tm
    tokens = batch * seq
    row = lambda p, b: (b * n_p + p, 0)
    resident = lambda shape: pl.BlockSpec(shape, lambda p, b: (0, 0),
                                          pipeline_mode=pl.Buffered(1))
    dil_shape = jax.ShapeDtypeStruct((tokens, 3 * ATT_WIDTH), BF)
    stage_buf = pltpu.VMEM((3, Z_BLK // LANES, tm, LANES), F32)
    return pl.pallas_call(
        functools.partial(_in_proj_kernel, tm=tm),
        grid=(n_p, batch),
        in_specs=[
            pl.BlockSpec((tm, D_MODEL), row),
            resident((1, D_MODEL)),
            resident((D_MODEL, Z_WIDTH + N_HEADS)),
            resident((1, LANES)),
        ],
        out_specs=[
            pl.BlockSpec((tm, Z_WIDTH), row),
            pl.BlockSpec((tm, LANES), row),
            pl.BlockSpec((tm, 3 * ATT_WIDTH), row),
            pl.BlockSpec((tm, 3 * ATT_WIDTH), row),
        ],
        out_shape=[
            jax.ShapeDtypeStruct((tokens, Z_WIDTH), BF),
            jax.ShapeDtypeStruct((tokens, LANES), F32),
            dil_shape,
            dil_shape,
        ],
        scratch_shapes=[
            pltpu.VMEM((D_MODEL, Z_WIDTH + LANES), BF),
            pltpu.VMEM((tm, D_MODEL), BF),
            pltpu.VMEM((tm, LANES), F32),
            pltpu.VMEM((tm, LANES), F32),
            pltpu.VMEM((tm, LANES), F32),
            stage_buf,
            stage_buf,
        ],
        compiler_params=_cparams(("arbitrary", "arbitrary")),
        name="in_proj",
    )(x2d, g, w_raw, freq_lanes)


def _extras_base(head):
    return (HEAD_DIM if head % 2 == 0 else 0) + 2 * N_PIECES * (head // 2)


def _forget_scan_kernel(fa_ref, bias_ref, tri_ref, pq_ref, pk_ref, oq_ref, ok_ref,
                        gq_ref, gk_ref):
    chunk = tri_ref.shape[0]
    n_chunks = fa_ref.shape[0] // chunk
    t = fa_ref[...] + bias_ref[...]
    log_f = jnp.minimum(t, 0.0) - jnp.log1p(jnp.exp(-jnp.abs(t)))
    wide = jnp.concatenate(
        [log_f[c * chunk:(c + 1) * chunk, :] for c in range(n_chunks)], axis=1)
    tri = tri_ref[...]
    local = sum(_dot(tri, piece) for piece in _split3(wide))
    carry = jnp.zeros((1, LANES), F32)
    parts = []
    for c in range(n_chunks):
        part = local[:, c * LANES:(c + 1) * LANES] + carry
        parts.append(part)
        carry = part[chunk - 1:chunk, :]
    run = jnp.concatenate(parts, axis=0)
    gq = oq_ref[...].astype(F32)
    gk = ok_ref[...].astype(F32)
    for i, piece in enumerate(_split3(run * LOG2_E)):
        gq = gq + _dot(piece, pq_ref[i])
        gk = gk - _dot(piece, pk_ref[i])
    gq_ref[0] = gq.astype(BF)
    gk_ref[0] = gk.astype(BF)


def _forget_scan_constants(chunk):
    tri = np.tril(np.ones((chunk, chunk), np.float32))
    pq = np.zeros((N_PIECES, LANES, LANES), np.float32)
    pk = np.zeros((N_PIECES, LANES, LANES), np.float32)
    oq = np.zeros((1, LANES), np.float32)
    ok = np.zeros((1, LANES), np.float32)
    for h in range(N_HEADS):
        base = _extras_base(h)
        for i in range(N_PIECES):
            pq[i, h, base + i] = 1.0
            oq[0, base + N_PIECES + i] = 1.0
            ok[0, base + i] = 1.0
            pk[i, h, base + N_PIECES + i] = 1.0
    as_bf = lambda a: jnp.asarray(a, BF)
    return as_bf(tri), as_bf(pq), as_bf(pk), as_bf(oq), as_bf(ok)


def _forget_scan(fa, bias_lanes, *, batch, seq, chunk):
    tri, pq, pk, oq, ok = _forget_scan_constants(chunk)
    const2 = lambda b: (0, 0)
    const3 = lambda b: (0, 0, 0)
    return pl.pallas_call(
        _forget_scan_kernel,
        grid=(batch,),
        in_specs=[
            pl.BlockSpec((seq, LANES), lambda b: (b, 0)),
            pl.BlockSpec((1, LANES), const2),
            pl.BlockSpec((chunk, chunk), const2),
            pl.BlockSpec((N_PIECES, LANES, LANES), const3),
            pl.BlockSpec((N_PIECES, LANES, LANES), const3),
            pl.BlockSpec((1, LANES), const2),
            pl.BlockSpec((1, LANES), const2),
        ],
        out_specs=[
            pl.BlockSpec((1, seq, LANES), lambda b: (b, 0, 0)),
            pl.BlockSpec((1, seq, LANES), lambda b: (b, 0, 0)),
        ],
        out_shape=[
            jax.ShapeDtypeStruct((batch, seq, LANES), BF),
            jax.ShapeDtypeStruct((batch, seq, LANES), BF),
        ],
        compiler_params=_cparams(("arbitrary",)),
        name="forget_scan",
    )(fa, bias_lanes, tri, pq, pk, oq, ok)


def _with_bias_lanes(slab, extras, hp, parity):
    lane = lax.broadcasted_iota(jnp.int32, slab.shape, 1)
    own = (lane < HEAD_DIM) if parity == 0 else (lane >= HEAD_DIM)
    base = (HEAD_DIM if parity == 0 else 0) + 2 * N_PIECES * hp
    in_extras = (lane >= base) & (lane < base + 2 * N_PIECES)
    return jnp.where(own, slab, jnp.where(in_extras, extras, jnp.zeros_like(extras)))


def _fox_kernel(q_ref, k_ref, v_ref, gq_ref, gk_ref, o_ref,
                ka_scr, kb_scr, vta_scr, vtb_scr, qp_scr, bias_scr, s_scr, acc_scr, ot_scr,
                bound_smem, *, tq):
    hp = pl.program_id(1)
    k_scrs = (ka_scr, kb_scr)
    vt_scrs = (vta_scr, vtb_scr)
    seq = k_ref.shape[0]
    n_q = seq // tq
    lane_row = lax.broadcasted_iota(jnp.int32, (1, LANES), 1)

    sel_row = lax.broadcasted_iota(jnp.int32, (LANES, LANES), 0)
    sel_col = lax.broadcasted_iota(jnp.int32, (LANES, LANES), 1)
    head_sum = jnp.where(sel_row // HEAD_DIM == sel_col, 1.0, 0.0).astype(BF)

    def max_row_norm(slab, head):
        lane = lax.broadcasted_iota(jnp.int32, slab.shape, 1)
        own = (lane < HEAD_DIM) if head == 0 else (lane >= HEAD_DIM)
        x = jnp.where(own, slab.astype(F32), 0.0)
        sq = jnp.sum(x * x, axis=1, keepdims=True)
        return jnp.sqrt(jnp.max(sq, axis=0, keepdims=True))

    def max_row_norms(slab):
        x = slab.astype(F32)
        per_row = _dot((x * x).astype(BF), head_sum)
        return jnp.sqrt(jnp.max(per_row, axis=0, keepdims=True) * NORM_SLACK)

    def bias_lane_sum(row, head, first):
        base = (HEAD_DIM if head == 0 else 0) + 2 * N_PIECES * hp + first
        picked = (lane_row >= base) & (lane_row < base + N_PIECES)
        return jnp.sum(jnp.where(picked, row, 0.0), axis=1, keepdims=True)

    k2 = k_ref[...]
    gk = gk_ref[0]
    k_norms = max_row_norms(k2)
    for head in range(2):
        k_scrs[head][...] = _with_bias_lanes(k2, gk, hp, head)
        bound_smem[head, n_q] = k_norms[0, head]
        for j in range(n_q):
            rows16 = gk_ref[0, (j + 1) * tq - BF16_ROWS:(j + 1) * tq, :].astype(F32)
            last = rows16[BF16_ROWS - 1:BF16_ROWS, :]
            bound_smem[head, j] = -bias_lane_sum(last, head, N_PIECES)[0, 0]
    vt = v_ref[...].T
    ones = jnp.ones((BF16_ROWS, seq), BF)
    for head, vt_scr in enumerate(vt_scrs):
        vt_scr[0:HEAD_DIM, :] = vt[head * HEAD_DIM:(head + 1) * HEAD_DIM, :]
        vt_scr[HEAD_DIM:, :] = ones
    row = lax.broadcasted_iota(jnp.int32, (tq, tq), 0)
    col = lax.broadcasted_iota(jnp.int32, (tq, tq), 1)
    bias_scr[...] = jnp.where(row <= col, 0.0, NEG_INF)
    acc_scr[...] = jnp.zeros_like(acc_scr)

    def scores(head, tile, masked):
        kt = k_scrs[head][pl.ds(pl.multiple_of(tile * tq, tq), tq), :]
        st = _dot(kt, qp_scr[head])
        if masked:
            st = st + bias_scr[...]
        s_scr[head] = st
        return jnp.max(st, axis=0, keepdims=True)

    def softmax_pv(head, st, tile, tile_max, m):
        start = pl.multiple_of(tile * tq, tq)
        m_new = jnp.maximum(m, tile_max)
        alpha = jnp.exp2(m - m_new)
        pt = jnp.exp2(st - m_new).astype(BF)
        vt_tile = vt_scrs[head][:, pl.ds(start, tq)]
        acc_scr[head] = alpha * acc_scr[head] + _dot(vt_tile, pt)
        return m_new

    def start_tile(qi):
        rows = pl.ds(pl.multiple_of(qi * tq, tq), tq)
        q2 = q_ref[rows, :]
        gq = gq_ref[0, rows, :]
        first_row = gq[0:BF16_ROWS, :].astype(F32)[0:1, :]
        maxes, slack = [], []
        for head in range(2):
            qp_scr[head] = _with_bias_lanes(q2, gq, hp, head).T
            maxes.append(scores(head, qi, True))
            reach = (max_row_norm(q2, head) * bound_smem[head, n_q]
                     + bias_lane_sum(first_row, head, 0) + SCORE_MARGIN)
            lowest_max = jnp.min(maxes[head], axis=1, keepdims=True)
            slack.append(reach - (lowest_max - UNDERFLOW_BITS))
        count = jnp.zeros((1, 1), jnp.int32)
        for j in range(n_q - 1):
            needed = (((slack[0] >= bound_smem[0, j]) | (slack[1] >= bound_smem[1, j]))
                      & (j < qi))
            count = count + jnp.where(needed, 1, 0)
        return maxes[0], maxes[1], count[0, 0]

    def finish_tile(qi):
        for head in range(2):
            acc = acc_scr[head]
            ot_scr[head * HEAD_DIM:(head + 1) * HEAD_DIM, :] = (
                acc[0:HEAD_DIM, :] / acc[HEAD_DIM:HEAD_DIM + 1, :])
        o_ref[pl.ds(pl.multiple_of(qi * tq, tq), tq), :] = ot_scr[...].T.astype(BF)
        acc_scr[...] = jnp.zeros_like(acc_scr)

    m_init = jnp.full((1, tq), NEG_INF, F32)

    def query_tile(qi, carry):
        first_a, first_b, n_needed = carry

        def key_tile(t, inner):
            max_a, max_b, m_a, m_b, prev_tile = inner
            prev = [s_scr[head] for head in range(2)]
            tile = qi - 1 - t
            next_max = [scores(head, tile, False) for head in range(2)]
            m_a = softmax_pv(0, prev[0], prev_tile, max_a, m_a)
            m_b = softmax_pv(1, prev[1], prev_tile, max_b, m_b)
            return next_max[0], next_max[1], m_a, m_b, tile

        def two_key_tiles(u, inner):
            return key_tile(2 * u + 1, key_tile(2 * u, inner))

        inner = lax.fori_loop(0, n_needed // 2, two_key_tiles,
                              (first_a, first_b, m_init, m_init, qi))
        max_a, max_b, m_a, m_b, prev_tile = lax.cond(
            n_needed % 2 == 1, lambda c: key_tile(n_needed - 1, c), lambda c: c, inner)
        last = [s_scr[head] for head in range(2)]
        softmax_pv(0, last[0], prev_tile, max_a, m_a)
        softmax_pv(1, last[1], prev_tile, max_b, m_b)
        following = start_tile(jnp.minimum(qi + 1, n_q - 1))
        finish_tile(qi)
        return following

    lax.fori_loop(0, n_q, query_tile, start_tile(0))


def _fox_attention(z, gq, gk, *, batch, seq, tq):
    tokens = batch * seq
    col = lambda section: pl.BlockSpec(
        (seq, LANES), lambda b, hp: (b, section * N_HEAD_PAIRS + hp))
    whole = pl.BlockSpec((1, seq, LANES), lambda b, hp: (b, 0, 0))
    ext = HEAD_DIM + BF16_ROWS
    return pl.pallas_call(
        functools.partial(_fox_kernel, tq=tq),
        grid=(batch, N_HEAD_PAIRS),
        in_specs=[col(0), col(1), col(2), whole, whole],
        out_specs=pl.BlockSpec((seq, LANES), lambda b, hp: (b, hp)),
        out_shape=jax.ShapeDtypeStruct((tokens, ATT_WIDTH), BF),
        scratch_shapes=[
            pltpu.VMEM((seq, LANES), BF),
            pltpu.VMEM((seq, LANES), BF),
            pltpu.VMEM((ext, seq), BF),
            pltpu.VMEM((ext, seq), BF),
            pltpu.VMEM((2, LANES, tq), BF),
            pltpu.VMEM((tq, tq), F32),
            pltpu.VMEM((2, tq, tq), F32),
            pltpu.VMEM((2, ext, tq), F32),
            pltpu.VMEM((LANES, tq), F32),
            pltpu.SMEM((2, seq // tq + 1), F32),
        ],
        compiler_params=_cparams(("arbitrary", "arbitrary")),
        name="fox_attn",
    )(z, z, z, gq, gk)


def _log2(n):
    assert n > 0 and n & (n - 1) == 0, n
    return n.bit_length() - 1


def _dilated_kernel(q1, k1, v1, q4, k4, v4, q16, k16, v16, o_ref,
                    num1, num4, num16, den1, den4, den16, max1, max4, max16, bias_scr,
                    *, seq, blk, chunk, group, merge_rows):
    sources = ((q1, k1, v1), (q4, k4, v4), (q16, k16, v16))
    num_scrs = (num1, num4, num16)
    den_scrs = (den1, den4, den16)
    max_scrs = (max1, max4, max16)
    n_blocks = seq // blk

    low = lax.broadcasted_iota(jnp.int32, (blk, LANES), 1) < HEAD_DIM
    qrow = lax.broadcasted_iota(jnp.int32, (2 * blk, 2 * blk), 0) % blk
    kcol = lax.broadcasted_iota(jnp.int32, (2 * blk, 2 * blk), 1)
    dist = qrow + blk - kcol
    band = (dist >= 0) & (dist <= blk)
    bias_scr[0] = jnp.where(band & (kcol >= blk), 0.0, NEG_INF)
    bias_scr[1] = jnp.where(band, 0.0, NEG_INF)
    ones = jnp.ones((2 * blk, LANES), BF)

    for idx, (window, dilation) in enumerate(DIL_PATTERNS):
        assert window // dilation == blk
        q_ref, k_ref, v_ref = sources[idx]
        num_scr, den_scr, max_scr = num_scrs[idx], den_scrs[idx], max_scrs[idx]
        sub_shift = _log2(seq // dilation // blk)
        seg_rows = chunk // dilation
        seg = min(blk, seg_rows)

        def load(ref, j, l0):
            parts = []
            for s in range(blk // seg):
                l = l0 + s * seg
                p = lax.shift_right_logical(l, _log2(seg_rows))
                i = l & (seg_rows - 1)
                start = pl.multiple_of(p * chunk + j * seg_rows + i, seg)
                parts.append(ref[pl.ds(start, seg), :])
            return parts[0] if len(parts) == 1 else jnp.concatenate(parts, axis=0)

        def one_block(g):
            j = lax.shift_right_logical(g, sub_shift)
            gs = g & ((1 << sub_shift) - 1)
            l0 = gs * blk
            lp = jnp.maximum(l0 - blk, 0)
            q2 = load(q_ref, j, l0)
            zero = jnp.zeros_like(q2)
            qq = jnp.concatenate([jnp.where(low, q2, zero), jnp.where(low, zero, q2)], axis=0)
            kwin = jnp.concatenate([load(k_ref, j, lp), load(k_ref, j, l0)], axis=0)
            vwin = jnp.concatenate([load(v_ref, j, lp), load(v_ref, j, l0)], axis=0)
            s = _dot_nt(qq, kwin) + bias_scr[jnp.minimum(gs, 1)]
            m = jnp.max(s, axis=1, keepdims=True)
            p = jnp.exp2(s - m).astype(BF)
            pv = _dot(p, jnp.concatenate([vwin, ones], axis=1))
            if dilation == 1:
                dst = pl.ds(pl.multiple_of(l0, blk), blk)
            else:
                dst = pl.ds(l0 * dilation + j, blk, stride=dilation)
            num_scr[dst, :] = jnp.where(low, pv[:blk, :LANES], pv[blk:, :LANES])
            den_scr[dst, :] = jnp.where(low, pv[:blk, LANES:], pv[blk:, LANES:])
            max_scr[dst, :] = jnp.where(low, jnp.broadcast_to(m[:blk], (blk, LANES)),
                                        jnp.broadcast_to(m[blk:], (blk, LANES)))

        def blocks(it, carry):
            for u in range(group):
                one_block(it * group + u)
            return carry

        lax.fori_loop(0, n_blocks // group, blocks, 0)

    def merge(c, carry):
        rows = pl.ds(pl.multiple_of(c * merge_rows, merge_rows), merge_rows)
        maxes = [max_scr[rows, :] for max_scr in max_scrs]
        top = functools.reduce(jnp.maximum, maxes)
        weights = [jnp.exp2(m - top) for m in maxes]
        num = sum(w * num_scr[rows, :] for w, num_scr in zip(weights, num_scrs))
        den = sum(w * den_scr[rows, :] for w, den_scr in zip(weights, den_scrs))
        o_ref[rows, :] = (num / den).astype(BF)
        return carry

    lax.fori_loop(0, seq // merge_rows, merge, 0)


def _dilated_attention(z, zd4, zd16, *, batch, seq):
    blk = DIL_PATTERNS[0][0] // DIL_PATTERNS[0][1]
    tokens = batch * seq
    per_section = ATT_WIDTH // LANES
    col = lambda section: pl.BlockSpec(
        (seq, LANES), lambda b, hp: (b, section * per_section + hp))
    f32_buf = pltpu.VMEM((seq, LANES), F32)
    return pl.pallas_call(
        functools.partial(_dilated_kernel, seq=seq, blk=blk, chunk=DIL_CHUNK,
                          group=seq // blk, merge_rows=DIL_MERGE_ROWS),
        grid=(batch, N_HEAD_PAIRS),
        in_specs=[col(DIL_SECTION), col(DIL_SECTION + 1), col(DIL_SECTION + 2),
                  col(0), col(1), col(2), col(0), col(1), col(2)],
        out_specs=pl.BlockSpec((seq, LANES), lambda b, hp: (b, hp)),
        out_shape=jax.ShapeDtypeStruct((tokens, ATT_WIDTH), BF),
        scratch_shapes=[f32_buf] * 9 + [pltpu.VMEM((2, 2 * blk, 2 * blk), F32)],
        compiler_params=_cparams(("arbitrary", "arbitrary")),
        name="dilated_attn",
    )(z, z, z, zd4, zd4, zd4, zd16, zd16, zd16)


def _mix_kernel(ya_ref, yb_ref, ga_ref, gb_ref, x_ref, woa_ref, wob_ref, wout_ref,
                g_ref, o_ref):
    rows = ya_ref.shape[0] // MIX_ROW_CHUNKS
    for c in range(MIX_ROW_CHUNKS):
        r = slice(c * rows, (c + 1) * rows)
        pa = _dot(ya_ref[r, :], woa_ref[...])
        pb = _dot(yb_ref[r, :], wob_ref[...])
        mixed = (jax.nn.sigmoid(ga_ref[r, :].astype(F32)) * pa
                 + jax.nn.sigmoid(gb_ref[r, :].astype(F32)) * pb)
        y = _dot(mixed.astype(BF), wout_ref[...])
        o_ref[r, :] = x_ref[r, :] + _rms(y, g_ref[...])


def _mix(ya, yb, z, x2d, woa, wob, wout, g, *, tm):
    tokens = x2d.shape[0]
    gate_blk = lambda off: pl.BlockSpec((tm, D_MODEL), lambda i: (i, off))
    const = lambda i: (0, 0)
    return pl.pallas_call(
        _mix_kernel,
        grid=(tokens // tm,),
        in_specs=[
            pl.BlockSpec((tm, ATT_WIDTH), lambda i: (i, 0)),
            pl.BlockSpec((tm, ATT_WIDTH), lambda i: (i, 0)),
            gate_blk(3),
            gate_blk(4),
            pl.BlockSpec((tm, D_MODEL), lambda i: (i, 0)),
            pl.BlockSpec((ATT_WIDTH, D_MODEL), const),
            pl.BlockSpec((ATT_WIDTH, D_MODEL), const),
            pl.BlockSpec((D_MODEL, D_MODEL), const),
            pl.BlockSpec((1, D_MODEL), const),
        ],
        out_specs=pl.BlockSpec((tm, D_MODEL), lambda i: (i, 0)),
        out_shape=jax.ShapeDtypeStruct((tokens, D_MODEL), F32),
        compiler_params=_cparams(("arbitrary",)),
        name="mix",
    )(ya, yb, z, z, x2d, woa, wob, wout, g)


def _ffn_kernel(x_ref, halo_ref, gpre_ref, wup_ref, cw_ref, cb_ref, wd_ref, gpost_ref, o_ref,
                h_scr, ua0_scr, ub0_scr, ua1_scr, ub1_scr, acc_scr, rows_scr,
                *, tm, tf, tiles_per_seq, n_chunks):
    i = pl.program_id(0)
    halo = BF16_ROWS
    n_rows = tm + halo
    n_groups = n_rows // SUBLANES
    chunk = n_rows // n_chunks
    assert n_groups * SUBLANES == n_rows and chunk * n_chunks == n_rows
    assert chunk % BF16_ROWS == 0 and n_groups % 2 == 0
    n_tiles = D_FF // tf
    u_sets = ((ua0_scr, ub0_scr), (ua1_scr, ub1_scr))

    def group_rows(g):
        return pl.ds(g, SUBLANES, stride=n_groups)

    def cols(f, gate):
        return pl.ds(pl.multiple_of(f * tf + (D_FF if gate else 0), LANES), tf)

    def up_proj(f, dst, lo, hi):
        h = h_scr[lo:hi, :]
        dst[0][lo:hi, :] = _dot(h, wup_ref[:, cols(f, False)])
        dst[1][lo:hi, :] = _dot(h, wup_ref[:, cols(f, True)])

    def tap_rows(u_scr, back, first, n):
        lo = first - back * SUBLANES
        if lo >= 0:
            return u_scr[lo:lo + n, :]
        wrapped = [pltpu.roll(u_scr[n_rows + k * SUBLANES:n_rows + (k + 1) * SUBLANES, :], 1, 0)
                   for k in range(lo // SUBLANES, 0)]
        return jnp.concatenate(wrapped + [u_scr[0:n + lo, :]], axis=0)

    def conv(u_scr, window, post_scale, first, n):
        out = cb_ref[:, window] * post_scale
        for tap in range(CONV_WIDTH):
            back = CONV_WIDTH - 1 - tap
            out = out + (cw_ref[tap:tap + 1, window] * post_scale) * tap_rows(u_scr, back, first, n)
        return out

    def down_proj(f, src, first, n):
        a = conv(src[0], cols(f, False), 1.0, first, n)
        half_b = conv(src[1], cols(f, True), 0.5, first, n)
        c0 = math.sqrt(2.0 / math.pi)
        inner = a * (c0 + (c0 * 0.044715) * (a * a))
        hidden = ((a * half_b) * (1.0 + jnp.tanh(inner))).astype(BF)
        return _dot(hidden, wd_ref[pl.ds(pl.multiple_of(f * tf, tf), tf), :])

    def stage(f, parity):
        src, dst = u_sets[1 - parity], u_sets[parity]
        for c in range(n_chunks):
            up_proj(f, dst, c * chunk, (c + 1) * chunk)
            out_rows = slice(c * chunk, (c + 1) * chunk)
            acc_scr[out_rows, :] += down_proj(f - 1, src, c * chunk, chunk)

    n_slabs = D_MODEL // LANES
    g = gpre_ref[...]
    hh = _rms(halo_ref[...], g)
    hh = jnp.where(i % tiles_per_seq == 0, jnp.zeros_like(hh), hh)
    hx = _rms(x_ref[...], g)
    for s in range(n_slabs):
        rows_scr[s, 0:halo, :] = hh[:, s * LANES:(s + 1) * LANES]
        rows_scr[s, halo:, :] = hx[:, s * LANES:(s + 1) * LANES]
    for gg in range(0, n_groups, 2):
        pair = jnp.concatenate(
            [jnp.concatenate([rows_scr[s, group_rows(gg + d), :] for s in range(n_slabs)], axis=1)
             for d in range(2)], axis=0)
        h_scr[gg * SUBLANES:(gg + 2) * SUBLANES, :] = pair.astype(BF)
    acc_scr[...] = jnp.zeros_like(acc_scr)
    up_proj(0, u_sets[0], 0, n_rows)

    def pair_of_tiles(k, carry):
        stage(2 * k + 1, 1)
        stage(2 * k + 2, 0)
        return carry

    assert n_tiles % 2 == 1
    for k in range((n_tiles - 1) // 2):
        pair_of_tiles(k, 0)
    gpost = gpost_ref[...]
    for c in range(n_chunks):
        y = acc_scr[c * chunk:(c + 1) * chunk, :] + down_proj(n_tiles - 1, u_sets[0], c * chunk, chunk)
        normed = _rms(y, gpost)
        for gg in range(chunk // SUBLANES):
            g_abs = c * (chunk // SUBLANES) + gg
            for s in range(n_slabs):
                rows_scr[s, group_rows(g_abs), :] = normed[gg * SUBLANES:(gg + 1) * SUBLANES,
                                                           s * LANES:(s + 1) * LANES]
    for s in range(n_slabs):
        lanes = slice(s * LANES, (s + 1) * LANES)
        o_ref[:, lanes] = x_ref[:, lanes] + rows_scr[s, halo:, :]


def _ffn(x1, g_pre, w_up, conv_w, conv_b, w_down, g_post, *, seq, tm, tf):
    tokens = x1.shape[0]
    halo = BF16_ROWS
    per = tm // halo
    resident = lambda shape: pl.BlockSpec(shape, lambda i: (0, 0),
                                          pipeline_mode=pl.Buffered(1))
    u_buf = pltpu.VMEM((tm + halo, tf), F32)
    return pl.pallas_call(
        functools.partial(_ffn_kernel, tm=tm, tf=tf, tiles_per_seq=seq // tm,
                          n_chunks=FFN_ROW_CHUNKS),
        grid=(tokens // tm,),
        in_specs=[
            pl.BlockSpec((tm, D_MODEL), lambda i: (i, 0)),
            pl.BlockSpec((halo, D_MODEL), lambda i: (jnp.maximum(i * per - 1, 0), 0)),
            resident((1, D_MODEL)),
            resident((D_MODEL, 2 * D_FF)),
            resident((CONV_WIDTH, 2 * D_FF)),
            resident((1, 2 * D_FF)),
            resident((D_FF, D_MODEL)),
            resident((1, D_MODEL)),
        ],
        out_specs=pl.BlockSpec((tm, D_MODEL), lambda i: (i, 0)),
        out_shape=jax.ShapeDtypeStruct((tokens, D_MODEL), F32),
        scratch_shapes=[
            pltpu.VMEM((tm + halo, D_MODEL), BF),
            u_buf, u_buf, u_buf, u_buf,
            pltpu.VMEM((tm + halo, D_MODEL), F32),
            pltpu.VMEM((D_MODEL // LANES, tm + halo, LANES), F32),
        ],
        compiler_params=_cparams(("arbitrary",)),
        name="ffn",
    )(x1, x1, g_pre, w_up, conv_w, conv_b, w_down, g_post)


def _rope_freq_lanes():
    inv_freq = ROPE_THETA ** (-jnp.arange(ROPE_HALF, dtype=F32) * 2.0 / ROPE_DIM)
    return jnp.tile(inv_freq, LANES // ROPE_HALF).reshape(1, LANES)


def kernel(x, g_pre_mix, w_in, b_forget, w_o_fox, w_o_dil, w_out, g_post_mix,
           g_pre_ffn, w_up, conv_w, conv_b, w_down, g_post_ffn):
    batch, seq, d_model = x.shape
    assert d_model == D_MODEL and seq % max(MIX_TOKENS, FFN_TOKENS, FOX_TILE, DIL_CHUNK) == 0
    depth = w_in.shape[0]
    freq_lanes = _rope_freq_lanes()
    x2d = x.reshape(batch * seq, D_MODEL)
    row = lambda v: v.reshape(1, -1)
    for l in range(depth):
        bias_lanes = jnp.pad(b_forget[l], (0, LANES - N_HEADS)).reshape(1, LANES)

        z, fa, zd4, zd16 = _in_proj(x2d, row(g_pre_mix[l]), w_in[l].astype(BF), freq_lanes,
                                    batch=batch, seq=seq, tm=DIL_CHUNK)
        gq, gk = _forget_scan(fa, bias_lanes, batch=batch, seq=seq, chunk=SCAN_CHUNK)
        ya = _fox_attention(z, gq, gk, batch=batch, seq=seq, tq=FOX_TILE)
        yb = _dilated_attention(z, zd4, zd16, batch=batch, seq=seq)
        x2d = _mix(ya, yb, z, x2d, w_o_fox[l].astype(BF), w_o_dil[l].astype(BF),
                   w_out[l].astype(BF), row(g_post_mix[l]), tm=MIX_TOKENS)
        x2d = _ffn(x2d, row(g_pre_ffn[l]), w_up[l].astype(BF), conv_w[l],
                   row(conv_b[l]), w_down[l].astype(BF), row(g_post_ffn[l]),
                   seq=seq, tm=FFN_TOKENS, tf=FFN_HIDDEN_TILE)
    return x2d.reshape(batch, seq, D_MODEL)
```
